```python
import jax, jax.numpy as jnp
from jax import lax
import numpy as np

D_MODEL = 1024
BATCH = 4
SEQ = 4096
DEPTH = 1
DEC_BATCH = 32
DEC_SEQ = 8
PAST_LEN = 16384
PAGE_SIZE = 128

POOL_WIDTH = D_MODEL // 4
POOL_WINDOWS = (2, 4, 8, 16)
POOL_GROUP = POOL_WIDTH // len(POOL_WINDOWS)
POOL_HIST = max(POOL_WINDOWS) - 1
HEAD_DIM = 64
NSA_WIDTH = D_MODEL - POOL_WIDTH
N_HEADS = NSA_WIDTH // HEAD_DIM
GQA_GROUP = 4
KV_HEADS = N_HEADS // GQA_GROUP
KV_WIDTH = KV_HEADS * HEAD_DIM
CMP_BLOCK = 32
CMP_STRIDE = 16
CMP_HID = 2 * HEAD_DIM
SEL_BLOCK = 64
N_SEL = 16
WINDOW = 512
N_BRANCH = 3
Q_BLOCK = 128
IN_WIDTH = 2 * POOL_WIDTH + 2 * NSA_WIDTH + N_BRANCH * 2 * KV_WIDTH + N_BRANCH * N_HEADS
EPS = 1e-6
NEG = -1e30
FORCE = 1e4

kernel_name = 'hymba_pool_nsa_decoder_step'


def _rmsnorm(x, g):
    xf = x.astype(jnp.float32)
    y = xf * lax.rsqrt(jnp.mean(xf * xf, axis=-1, keepdims=True) + EPS)
    return (y * g.astype(jnp.float32)).astype(x.dtype)


def _masked_softmax(s, mask):
    s = jnp.where(mask, s.astype(jnp.float32), NEG)
    m = jnp.max(s, axis=-1, keepdims=True)
    e = jnp.where(mask, jnp.exp(s - m), 0.0)
    return e / jnp.maximum(jnp.sum(e, axis=-1, keepdims=True), 1e-30)


def _in_proj(x, c, norm_w, w_ada, b_ada, w_in):
    B, T = x.shape[0], x.shape[1]
    ada = jax.nn.silu(c) @ w_ada + b_ada
    shift, scale, gate = jnp.split(ada, 3, axis=-1)
    h = _rmsnorm(x, norm_w) * (1.0 + scale[:, None, :]) + shift[:, None, :]
    z = h @ w_in
    sizes = [POOL_WIDTH, POOL_WIDTH, NSA_WIDTH, NSA_WIDTH, 2 * KV_WIDTH, 2 * KV_WIDTH, 2 * KV_WIDTH]
    cuts = [sum(sizes[:i + 1]) for i in range(len(sizes))]
    u, zp, q, zn, kvc, kvs, kvw, gl = jnp.split(z, cuts, axis=-1)
    q = q.reshape(B, T, KV_HEADS, GQA_GROUP, HEAD_DIM)
    kv_shape = (B, T, 2, KV_HEADS, HEAD_DIM)
    g = jax.nn.sigmoid(gl.astype(jnp.float32)).astype(x.dtype)
    g = g.reshape(B, T, KV_HEADS, GQA_GROUP, N_BRANCH)
    return u, zp, q, zn, kvc.reshape(kv_shape), kvs.reshape(kv_shape), kvw.reshape(kv_shape), g, gate


def _pool_mix(u_hist, u, pos0, pool_w, pool_scale):
    B, P, C = u_hist.shape
    T = u.shape[1]
    ue = jnp.concatenate([u_hist, u], axis=1)
    uf = ue.astype(jnp.float32)
    cs = jnp.concatenate([jnp.zeros((B, 1, C), jnp.float32), jnp.cumsum(uf, axis=1)], axis=1)
    e = P + jnp.arange(T)
    pos = pos0 + jnp.arange(T)
    hi = cs[:, e + 1]
    means = []
    for gi, w in enumerate(POOL_WINDOWS):
        sl = slice(gi * POOL_GROUP, (gi + 1) * POOL_GROUP)
        lo = cs[:, jnp.maximum(e + 1 - w, 0), sl]
        cnt = jnp.minimum(w, pos + 1).astype(jnp.float32)[None, :, None]
        means.append((hi[..., sl] - lo) / cnt)
    d = (jnp.concatenate(means, axis=-1) - uf[:, P:]).astype(u.dtype)
    d = d.reshape(B, T, len(POOL_WINDOWS), POOL_GROUP)
    y = jnp.einsum('btgc,gcd->btgd', d, pool_w).reshape(B, T, POOL_WIDTH) * pool_scale
    return y, ue[:, -POOL_HIST:]


def _compress(kv, w1, pe, w2):
    B, L = kv.shape[0], kv.shape[1]
    n_str = -(-L // CMP_STRIDE)
    span = CMP_BLOCK // CMP_STRIDE
    n_cmp = n_str - span + 1
    kv = jnp.pad(kv, ((0, 0), (0, n_str * CMP_STRIDE - L), (0, 0), (0, 0), (0, 0)))
    ch = kv.reshape(B, n_str, CMP_STRIDE, 2, KV_HEADS, HEAD_DIM)
    pre = 0.0
    for m in range(span):
        sl = slice(m * CMP_STRIDE, (m + 1) * CMP_STRIDE)
        a = jnp.einsum('bnjkgd,jkdh->bnkgh', ch, w1[sl]) + jnp.einsum('jkd,jkdh->kh', pe[sl], w1[sl])[:, None, :]
        pre = pre + a[:, m:m + n_cmp]
    comp = jnp.einsum('bnkgh,khd->bnkgd', jax.nn.gelu(pre), w2)
    cend = jnp.arange(n_cmp) * CMP_STRIDE + (CMP_BLOCK - 1)
    return comp, cend


def _nsa_core(q, qpos, comp, cend, gather_sel, n_sel, kvw, wpos, g):
    B, T, G, R = q.shape[0], q.shape[1], q.shape[2], q.shape[3]
    scale = HEAD_DIM ** -0.5
    s = jnp.einsum('btgrd,bngd->btgrn', q, comp[:, :, 0]) * scale
    p = _masked_softmax(s, (cend[None, :] <= qpos[:, None])[None, :, None, None, :])
    o_c = jnp.einsum('btgrn,bngd->btgrd', p.astype(q.dtype), comp[:, :, 1])
    pg = jnp.sum(p, axis=3)
    span = CMP_BLOCK // CMP_STRIDE
    n_str = cend.shape[0] + span - 1
    pp = jnp.pad(pg, ((0, 0), (0, 0), (0, 0), (span - 1, span - 1)))
    ps = sum(pp[..., span - 1 - m:span - 1 - m + n_str] for m in range(span))
    ratio = SEL_BLOCK // CMP_STRIDE
    ps = jnp.pad(ps, ((0, 0), (0, 0), (0, 0), (0, n_sel * ratio - n_str)))
    imp = ps.reshape(B, T, G, n_sel, ratio).sum(-1)
    blk = jnp.arange(n_sel)[None, :]
    cur = (qpos // SEL_BLOCK)[:, None]
    forced = (blk == 0) | (blk == cur) | (blk == cur - 1)
    score = jnp.where((blk <= cur)[None, :, None, :], imp, -1.0)
    score = jnp.where(forced[None, :, None, :], FORCE, score)
    k_top = min(N_SEL, n_sel)
    idx = lax.top_k(score, k_top)[1].transpose(0, 2, 1, 3)
    kvg = gather_sel(idx)
    kpos = idx[..., None] * SEL_BLOCK + jnp.arange(SEL_BLOCK)
    smask = (kpos <= qpos[None, None, :, None, None]).reshape(B, G, T, 1, k_top * SEL_BLOCK)
    s = jnp.einsum('btgrd,bgtnjd->bgtrnj', q, kvg[..., 0, :]) * scale
    p = _masked_softmax(s.reshape(B, G, T, R, k_top * SEL_BLOCK), smask).reshape(B, G, T, R, k_top, SEL_BLOCK)
    o_s = jnp.einsum('bgtrnj,bgtnjd->btgrd', p.astype(q.dtype), kvg[..., 1, :])
    dq = qpos[:, None] - wpos[None, :]
    wmask = (dq >= 0) & (dq < WINDOW) & (wpos[None, :] >= 0)
    s = jnp.einsum('btgrd,blgd->btgrl', q, kvw[:, :, 0]) * scale
    p = _masked_softmax(s, wmask[None, :, None, None, :])
    o_w = jnp.einsum('btgrl,blgd->btgrd', p.astype(q.dtype), kvw[:, :, 1])
    o = g[..., 0:1] * o_c + g[..., 1:2] * o_s + g[..., 2:3] * o_w
    return o.reshape(B, T, NSA_WIDTH)


def _nsa_prompt(q, kvc, kvs, kvw, g, w1, pe, w2):
    B, S = q.shape[0], q.shape[1]
    comp, cend = _compress(kvc, w1, pe, w2)
    n_sel = -(-S // SEL_BLOCK)
    kvs_b = jnp.pad(kvs, ((0, 0), (0, n_sel * SEL_BLOCK - S), (0, 0), (0, 0), (0, 0)))
    kvs_b = kvs_b.reshape(B, n_sel, SEL_BLOCK, 2, KV_HEADS, HEAD_DIM)
    bidx = jnp.arange(B)[:, None, None, None]
    gidx = jnp.arange(KV_HEADS)[None, :, None, None]

    def gather_sel(idx):
        return kvs_b[bidx, idx, :, :, gidx]

    kvw_pad = jnp.pad(kvw, ((0, 0), (WINDOW, 0), (0, 0), (0, 0), (0, 0)))

    def body(i):
        qs = i * Q_BLOCK
        qc = lax.dynamic_slice_in_dim(q, qs, Q_BLOCK, axis=1)
        gc = lax.dynamic_slice_in_dim(g, qs, Q_BLOCK, axis=1)
        kw = lax.dynamic_slice_in_dim(kvw_pad, qs, WINDOW + Q_BLOCK, axis=1)
        qpos = qs + jnp.arange(Q_BLOCK)
        wpos = qs - WINDOW + jnp.arange(WINDOW + Q_BLOCK)
        return _nsa_core(qc, qpos, comp, cend, gather_sel, n_sel, kw, wpos, gc)

    o = lax.map(body, jnp.arange(S // Q_BLOCK))
    return o.transpose(1, 0, 2, 3).reshape(B, S, NSA_WIDTH)


def _nsa_sample(q, kvc, kvs, kvw, g, cache_cmp, cache_slc, win_buf, page_table, w1, pe, w2):
    DB, T = q.shape[0], q.shape[1]
    n_pages = page_table.shape[1]
    past = n_pages * PAGE_SIZE
    qpos = past + jnp.arange(T)
    past_cmp = cache_cmp[page_table].reshape(DB, past, 2, KV_HEADS, HEAD_DIM)
    comp, cend = _compress(jnp.concatenate([past_cmp, kvc], axis=1), w1, pe, w2)
    n_sel = -(-(past + T) // SEL_BLOCK)
    n_past_blk = past // SEL_BLOCK
    n_new_blk = n_sel - n_past_blk
    new_b = jnp.pad(kvs, ((0, 0), (0, n_new_blk * SEL_BLOCK - T), (0, 0), (0, 0), (0, 0)))
    new_b = new_b.reshape(DB, n_new_blk, SEL_BLOCK, 2, KV_HEADS, HEAD_DIM)
    bpp = PAGE_SIZE // SEL_BLOCK
    pool_b = cache_slc.reshape(-1, SEL_BLOCK, 2, KV_HEADS, HEAD_DIM)
    bidx = jnp.arange(DB)[:, None, None, None]
    gidx = jnp.arange(KV_HEADS)[None, :, None, None]

    def gather_sel(idx):
        page = page_table[bidx, jnp.clip(idx // bpp, 0, n_pages - 1)]
        old = pool_b[page * bpp + idx % bpp, :, :, gidx]
        new = new_b[bidx, jnp.clip(idx - n_past_blk, 0, n_new_blk - 1), :, :, gidx]
        return jnp.where((idx >= n_past_blk)[..., None, None, None], new, old)

    buf = win_buf.shape[1]
    kw = jnp.concatenate([win_buf, kvw], axis=1)
    wpos = past - buf + jnp.arange(buf + T)
    o = _nsa_core(q, qpos, comp, cend, gather_sel, n_sel, kw, wpos, g)
    return o, kw[:, -buf:]


def _out(x, pool_o, zp, nsa_o, zn, gate, gn_pool, gn_nsa, w_out):
    m = jnp.concatenate([_rmsnorm(pool_o, gn_pool) * jax.nn.silu(zp),
                         _rmsnorm(nsa_o, gn_nsa) * jax.nn.silu(zn)], axis=-1)
    return x + gate[:, None, :] * (m @ w_out)


def setup_inputs(seed: int = 0) -> dict:
    key = jax.random.key(seed)
    ks = jax.random.split(key, 24)
    n_pages = PAST_LEN // PAGE_SIZE
    n_used = DEC_BATCH * n_pages
    n_phys = n_used + n_used // 4
    win_len = min(WINDOW, PAST_LEN)
    f32 = jnp.float32
    nrm = lambda k, shape, s=1.0: jax.random.normal(k, shape, f32) * s
    page_table = jax.random.permutation(ks[0], n_phys)[:n_used].reshape(DEC_BATCH, n_pages).astype(jnp.int32)
    return {
        'x_prompt': nrm(ks[1], (BATCH, SEQ, D_MODEL)),
        'x_sample': nrm(ks[2], (DEC_BATCH, DEC_SEQ, D_MODEL)),
        'c_prompt': nrm(ks[3], (BATCH, D_MODEL)),
        'c_sample': nrm(ks[4], (DEC_BATCH, D_MODEL)),
        'cache_cmp_kv': nrm(ks[5], (DEPTH, n_phys, PAGE_SIZE, 2, KV_HEADS, HEAD_DIM)),
        'cache_slc_kv': nrm(ks[6], (DEPTH, n_phys, PAGE_SIZE, 2, KV_HEADS, HEAD_DIM)),
        'state_win_kv': nrm(ks[7], (DEPTH, DEC_BATCH, win_len, 2, KV_HEADS, HEAD_DIM)),
        'state_pool': nrm(ks[8], (DEPTH, DEC_BATCH, POOL_HIST, POOL_WIDTH)),
        'page_table': page_table,
        'norm_w': 1.0 + nrm(ks[9], (DEPTH, D_MODEL), 0.05),
        'w_ada': nrm(ks[10], (DEPTH, D_MODEL, 3 * D_MODEL), 0.5 * D_MODEL ** -0.5),
        'b_ada': nrm(ks[11], (DEPTH, 3 * D_MODEL), 0.02),
        'w_in': nrm(ks[12], (DEPTH, D_MODEL, IN_WIDTH), D_MODEL ** -0.5),
        'pool_w': nrm(ks[13], (DEPTH, len(POOL_WINDOWS), POOL_GROUP, POOL_GROUP), POOL_GROUP ** -0.5),
        'pool_scale': 1.0 + nrm(ks[14], (DEPTH, POOL_WIDTH), 0.1),
        'phi_w1': nrm(ks[15], (DEPTH, CMP_BLOCK, 2, HEAD_DIM, CMP_HID), (CMP_BLOCK * HEAD_DIM) ** -0.5),
        'phi_pe': nrm(ks[16], (DEPTH, CMP_BLOCK, 2, HEAD_DIM), 0.1),
        'phi_w2': nrm(ks[17], (DEPTH, 2, CMP_HID, HEAD_DIM), 1.4 * CMP_HID ** -0.5),
        'gn_pool': 1.0 + nrm(ks[18], (DEPTH, POOL_WIDTH), 0.05),
        'gn_nsa': 1.0 + nrm(ks[19], (DEPTH, NSA_WIDTH), 0.05),
        'w_out': nrm(ks[20], (DEPTH, D_MODEL, D_MODEL), D_MODEL ** -0.5),
        'final_norm': 1.0 + nrm(ks[21], (D_MODEL,), 0.05),
    }


def reference(x_prompt, x_sample, c_prompt, c_sample, cache_cmp_kv, cache_slc_kv, state_win_kv, state_pool,
              page_table, norm_w, w_ada, b_ada, w_in, pool_w, pool_scale, phi_w1, phi_pe, phi_w2,
              gn_pool, gn_nsa, w_out, final_norm):
    past = page_table.shape[1] * PAGE_SIZE
    xp, xs = x_prompt, x_sample
    cmp_p, cmp_s, slc_p, slc_s = [], [], [], []
    win_p, win_s, pool_p, pool_s = [], [], [], []
    for l in range(DEPTH):
        u, zp, q, zn, kvc, kvs, kvw, g, gate = _in_proj(xp, c_prompt, norm_w[l], w_ada[l], b_ada[l], w_in[l])
        pool_o, hist = _pool_mix(u[:, :0], u, 0, pool_w[l], pool_scale[l])
        nsa_o = _nsa_prompt(q, kvc, kvs, kvw, g, phi_w1[l], phi_pe[l], phi_w2[l])
        xp = _out(xp, pool_o, zp, nsa_o, zn, gate, gn_pool[l], gn_nsa[l], w_out[l])
        cmp_p.append(kvc)
        slc_p.append(kvs)
        win_p.append(kvw[:, -min(WINDOW, kvw.shape[1]):])
        pool_p.append(hist)
        u, zp, q, zn, kvc, kvs, kvw, g, gate = _in_proj(xs, c_sample, norm_w[l], w_ada[l], b_ada[l], w_in[l])
        pool_o, hist = _pool_mix(state_pool[l], u, past, pool_w[l], pool_scale[l])
        nsa_o, win_new = _nsa_sample(q, kvc, kvs, kvw, g, cache_cmp_kv[l], cache_slc_kv[l], state_win_kv[l],
                                     page_table, phi_w1[l], phi_pe[l], phi_w2[l])
        xs = _out(xs, pool_o, zp, nsa_o, zn, gate, gn_pool[l], gn_nsa[l], w_out[l])
        cmp_s.append(kvc)
        slc_s.append(kvs)
        win_s.append(win_new)
        pool_s.append(hist)
    y_prompt = _rmsnorm(xp, final_norm)
    y_sample = _rmsnorm(xs, final_norm)
    return (y_prompt, y_sample, jnp.stack(cmp_p), jnp.stack(cmp_s), jnp.stack(slc_p), jnp.stack(slc_s),
            jnp.stack(win_p), jnp.stack(win_s), jnp.stack(pool_p), jnp.stack(pool_s))
```

```python
import functools

import numpy as np
import jax
import jax.numpy as jnp
from jax import lax
from jax.experimental import pallas as pl
from jax.experimental.pallas import tpu as pltpu

F32 = jnp.float32
BF16 = jnp.bfloat16

HEAD_DIM = 64
GQA_GROUP = 4
KV_HEADS = 3
N_HEADS = KV_HEADS * GQA_GROUP
N_KV_SLABS = 2 * KV_HEADS
KV_ROW = N_KV_SLABS * HEAD_DIM
N_BRANCH = 3
POOL_WINDOWS = (2, 4, 8, 16)
POOL_HALO = 16
CMP_BLOCK = 32
CMP_STRIDE = 16
CMP_SPAN = CMP_BLOCK // CMP_STRIDE
CMP_HID = 2 * HEAD_DIM
SEL_BLOCK = 64
N_SEL = 16
WINDOW = 512
PAGE_SIZE = 128
CHUNKS_PER_PAGE = PAGE_SIZE // CMP_STRIDE
EPS = 1e-6
NEG = -1e30
FORCE = 1e4
PAD_SCORE = -1e38
TAKEN_SCORE = -3e38

LANES = 128
SUBLANES = 8
VMEM_LIMIT_BYTES = 56 * 1024 * 1024


def _dot(a, b):
    return jnp.dot(a, b, preferred_element_type=F32)


def _dot_nt(a, b):
    return lax.dot_general(a, b, (((1,), (1,)), ((), ())), preferred_element_type=F32)


def _dot_tn(a, b):
    return lax.dot_general(a, b, (((0,), (0,)), ((), ())), preferred_element_type=F32)


def _sigmoid(v):
    return 1.0 / (1.0 + jnp.exp(-v))


def _rms(v, g):
    return v * lax.rsqrt(jnp.mean(v * v, axis=-1, keepdims=True) + EPS) * g


def _split3(v):
    hi = v.astype(BF16)
    r1 = v - hi.astype(F32)
    mid = r1.astype(BF16)
    lo = (r1 - mid.astype(F32)).astype(BF16)
    return hi, mid, lo


def _params(*sem):
    return pltpu.CompilerParams(dimension_semantics=sem, vmem_limit_bytes=VMEM_LIMIT_BYTES)


def _ada_kernel(c_ref, w_ref, b_ref, o_ref):
    c = c_ref[...]
    a = (c * _sigmoid(c)).astype(BF16)
    o_ref[...] = _dot(a, w_ref[...].astype(BF16)) + b_ref[...]


def _ada(c, w_ada, b_ada):
    m, d = c.shape
    n = w_ada.shape[1]
    tn = 512
    return pl.pallas_call(
        _ada_kernel,
        grid=(n // tn,),
        in_specs=[pl.BlockSpec((m, d), lambda j: (0, 0)),
                  pl.BlockSpec((d, tn), lambda j: (0, j)),
                  pl.BlockSpec((1, tn), lambda j: (0, j))],
        out_specs=pl.BlockSpec((m, tn), lambda j: (0, j)),
        out_shape=jax.ShapeDtypeStruct((m, n), F32),
        compiler_params=_params("arbitrary"),
        name="ada",
    )(c, w_ada, b_ada.reshape(1, n))


POOL_WIDTH = 256
NSA_WIDTH = N_HEADS * HEAD_DIM
GATE_COLS = N_BRANCH * N_HEADS
_SEG = {}
_off = 0
for _name, _w in (("u", POOL_WIDTH), ("zp", POOL_WIDTH), ("q", NSA_WIDTH), ("zn", NSA_WIDTH),
                  ("kvc", KV_ROW), ("kvs", KV_ROW), ("kvw", KV_ROW), ("gl", LANES)):
    _SEG[_name] = (_off, _w)
    _off += _w
IN_PAD = _off
IN_WIDTH = IN_PAD - LANES + GATE_COLS


def _in_proj_kernel(x_ref, sc_ref, sh_ref, nw_ref, w_ref,
                    u_ref, zp_ref, zn_ref, q_ref, kvc_ref, kvs_ref, kvw_ref, ks_hm_ref, kw_hm_ref, g_ref):
    h = (_rms(x_ref[0], nw_ref[...]) * (1.0 + sc_ref[0]) + sh_ref[0]).astype(BF16)

    def seg(name):
        s, w = _SEG[name]
        return _dot(h, w_ref[:, s:s + w])

    u_ref[0] = seg("u")
    zp_ref[0] = seg("zp")
    zn_ref[0] = seg("zn")
    q = seg("q") * (HEAD_DIM ** -0.5)
    for hh in range(N_HEADS):
        q_ref[0, hh] = q[:, hh * HEAD_DIM:(hh + 1) * HEAD_DIM].astype(BF16)
    kvc_ref[0] = seg("kvc")
    kvs = seg("kvs")
    kvs_ref[0] = kvs
    kvw = seg("kvw")
    kvw_ref[0] = kvw
    for kg in range(N_KV_SLABS):
        ks_hm_ref[0, kg] = kvs[:, kg * HEAD_DIM:(kg + 1) * HEAD_DIM].astype(BF16)
        kw_hm_ref[0, kg] = kvw[:, kg * HEAD_DIM:(kg + 1) * HEAD_DIM].astype(BF16)
    sig = _sigmoid(seg("gl"))
    per_group = GQA_GROUP * N_BRANCH
    for gi in range(KV_HEADS):
        g_ref[0, :, gi * LANES:(gi + 1) * LANES] = sig if gi == 0 else pltpu.roll(sig, LANES - per_group * gi, 1)


def _in_proj(x, scale, shift, norm_w, w_pad, tm):
    nb, t, d = x.shape
    r = scale.shape[1]
    assert t % tm == 0 and r in (1, t)
    if r == 1:
        mod_spec = pl.BlockSpec((1, 1, d), lambda b, i: (b, 0, 0))
    else:
        mod_spec = pl.BlockSpec((1, tm, d), lambda b, i: (b, i, 0))

    def tok(width):
        return pl.BlockSpec((1, tm, width), lambda b, i: (b, i, 0))

    def hm(n):
        return pl.BlockSpec((1, n, tm, HEAD_DIM), lambda b, i: (b, 0, i, 0))

    def sd(*shape, dtype=F32):
        return jax.ShapeDtypeStruct(shape, dtype)

    return pl.pallas_call(
        _in_proj_kernel,
        grid=(nb, t // tm),
        in_specs=[tok(d), mod_spec, mod_spec,
                  pl.BlockSpec((1, d), lambda b, i: (0, 0)),
                  pl.BlockSpec((d, IN_PAD), lambda b, i: (0, 0))],
        out_specs=[tok(POOL_WIDTH), tok(POOL_WIDTH), tok(NSA_WIDTH), hm(N_HEADS),
                   tok(KV_ROW), tok(KV_ROW), tok(KV_ROW), hm(N_KV_SLABS), hm(N_KV_SLABS),
                   tok(KV_HEADS * LANES)],
        out_shape=[sd(nb, t, POOL_WIDTH), sd(nb, t, POOL_WIDTH), sd(nb, t, NSA_WIDTH),
                   sd(nb, N_HEADS, t, HEAD_DIM, dtype=BF16),
                   sd(nb, t, KV_ROW), sd(nb, t, KV_ROW), sd(nb, t, KV_ROW),
                   sd(nb, N_KV_SLABS, t, HEAD_DIM, dtype=BF16), sd(nb, N_KV_SLABS, t, HEAD_DIM, dtype=BF16),
                   sd(nb, t, KV_HEADS * LANES)],
        compiler_params=_params("arbitrary", "arbitrary"),
        name="in_proj",
    )(x, scale, shift, norm_w.reshape(1, d), w_pad)


def _pool_kernel(hist_ref, uprev_ref, u_ref, wbd_ref, ps_ref, o_ref, *, pos0, tp):
    i = pl.program_id(1)
    u = u_ref[0]
    halo = jnp.where(i == 0, hist_ref[0], uprev_ref[0])
    ue = jnp.concatenate([halo, u], axis=0)
    sums = []
    s = ue
    for w in POOL_WINDOWS:
        s = s + pltpu.roll(s, w // 2, 0)
        sums.append(s[POOL_HALO:])
    c = u.shape[-1]
    grp = lax.broadcasted_iota(jnp.int32, (1, c), 1) // (c // len(POOL_WINDOWS))
    pos = pos0 + i * tp + lax.broadcasted_iota(jnp.int32, (tp, 1), 0)
    tot = sums[-1]
    win = jnp.full((1, c), float(POOL_WINDOWS[-1]), F32)
    for gi in range(len(POOL_WINDOWS) - 2, -1, -1):
        tot = jnp.where(grp == gi, sums[gi], tot)
        win = jnp.where(grp == gi, float(POOL_WINDOWS[gi]), win)
    cnt = jnp.minimum(win, (pos + 1).astype(F32))
    dlt = tot / cnt - u
    o_ref[0] = _dot(dlt.astype(BF16), wbd_ref[...]) * ps_ref[...]


def _pool(hist, u, wbd, pool_scale, pos0, tp):
    nb, t, c = u.shape
    assert t % tp == 0
    if t >= POOL_HALO:
        assert tp % POOL_HALO == 0
        uprev, ratio = u, tp // POOL_HALO
        prev_spec = pl.BlockSpec((1, POOL_HALO, c), lambda b, i: (b, jnp.maximum(i * ratio - 1, 0), 0))
    else:
        assert t == tp
        uprev = hist
        prev_spec = pl.BlockSpec((1, POOL_HALO, c), lambda b, i: (b, 0, 0))
    return pl.pallas_call(
        functools.partial(_pool_kernel, pos0=pos0, tp=tp),
        grid=(nb, t // tp),
        in_specs=[pl.BlockSpec((1, POOL_HALO, c), lambda b, i: (b, 0, 0)), prev_spec,
                  pl.BlockSpec((1, tp, c), lambda b, i: (b, i, 0)),
                  pl.BlockSpec((c, c), lambda b, i: (0, 0)),
                  pl.BlockSpec((1, c), lambda b, i: (0, 0))],
        out_specs=pl.BlockSpec((1, tp, c), lambda b, i: (b, i, 0)),
        out_shape=jax.ShapeDtypeStruct((nb, t, c), F32),
        compiler_params=_params("arbitrary", "arbitrary"),
        name="pool",
    )(hist, uprev, u, wbd, pool_scale.reshape(1, c))


def _gelu_tanh(v):
    return 0.5 * v * (1.0 + jnp.tanh(np.sqrt(2.0 / np.pi).astype(np.float32) * (v + 0.044715 * (v * v * v))))


def _compress_kernel(pt_ref, *refs, n_pages_step, n_steps, has_new, token_major):
    del pt_ref
    p_cnt = n_pages_step
    page_refs = refs[:p_cnt]
    w1_ref, pe_ref, w2_ref = refs[p_cnt:p_cnt + 3]
    k = p_cnt + 3
    xnew_ref = None
    if has_new:
        xnew_ref = refs[k]
        k += 1
    out_ref, slab_scr, x_scr, carry_scr, bias_scr = refs[k:k + 5]
    i = pl.program_id(1)
    rows = CHUNKS_PER_PAGE * p_cnt
    row_id = lax.broadcasted_iota(jnp.int32, (rows, 1), 0)

    @pl.when(i == 0)
    def _():
        carry_scr[...] = jnp.zeros_like(carry_scr)
        for kk in range(2):
            pb = _dot(pe_ref[kk].astype(BF16), w1_ref[kk])
            bias_scr[kk] = pb[0:1, :CMP_HID] + pb[1:2, CMP_HID:]

    def store(kg, comp):
        if token_major:
            out_ref[0, :, kg * HEAD_DIM:(kg + 1) * HEAD_DIM] = comp.astype(BF16)
        else:
            out_ref[0, kg] = comp.astype(BF16)

    def finish(kg, a0_prev, a1):
        kk = kg // KV_HEADS
        pre = a0_prev + a1 + bias_scr[kk]
        return _dot(_gelu_tanh(pre).astype(BF16), w2_ref[kk])

    def main():
        low = lax.broadcasted_iota(jnp.int32, (rows, LANES), 1) < HEAD_DIM
        for p in range(p_cnt):
            for cb in range(KV_ROW // LANES):
                slab_scr[cb, p * PAGE_SIZE:(p + 1) * PAGE_SIZE, :] = page_refs[p][0, :, cb * LANES:(cb + 1) * LANES]
        for cb in range(KV_ROW // LANES):
            for c in range(CMP_STRIDE // 2):
                va = slab_scr[cb, pl.ds(2 * c, rows, stride=CMP_STRIDE), :]
                vb = slab_scr[cb, pl.ds(2 * c + 1, rows, stride=CMP_STRIDE), :]
                x_scr[2 * cb, :, c * LANES:(c + 1) * LANES] = jnp.where(low, va, pltpu.roll(vb, HEAD_DIM, 1))
                x_scr[2 * cb + 1, :, c * LANES:(c + 1) * LANES] = jnp.where(low, pltpu.roll(va, HEAD_DIM, 1), vb)
        for kg in range(N_KV_SLABS):
            a = _dot(x_scr[kg].astype(BF16), w1_ref[kg // KV_HEADS])
            a0 = a[:, :CMP_HID]
            a0_prev = jnp.where(row_id == 0, carry_scr[kg], pltpu.roll(a0, 1, 0))
            carry_scr[kg] = a0[rows - 1:rows, :]
            store(kg, finish(kg, a0_prev, a[:, CMP_HID:]))

    if not has_new:
        main()
    else:
        pl.when(i < n_steps)(main)

        @pl.when(i == n_steps)
        def _():
            for kg in range(N_KV_SLABS):
                xk = jnp.broadcast_to(xnew_ref[0, kg:kg + 1, :], (SUBLANES, CMP_STRIDE * HEAD_DIM)).astype(BF16)
                a1 = _dot(xk, w1_ref[kg // KV_HEADS])[0:1, CMP_HID:]
                comp = finish(kg, carry_scr[kg], a1)
                store(kg, jnp.where(row_id == 0, jnp.broadcast_to(comp, (rows, HEAD_DIM)), 0.0))


def _compress(pages, page_table, w1cat, pe2, w2, xnew, token_major):
    nb, n_pages = page_table.shape
    p_cnt = min(32, n_pages)
    assert n_pages % p_cnt == 0
    n_steps = n_pages // p_cnt
    has_new = xnew is not None
    rows = CHUNKS_PER_PAGE * p_cnt
    grid_steps = n_steps + (1 if has_new else 0)
    tot_rows = rows * grid_steps
    xw = CMP_STRIDE * HEAD_DIM

    def page_spec(p):
        def imap(b, i, pt):
            step = jnp.minimum(i, n_steps - 1)
            return (pt[b * n_pages + step * p_cnt + p], 0, 0)
        return pl.BlockSpec((1, PAGE_SIZE, KV_ROW), imap)

    in_specs = [page_spec(p) for p in range(p_cnt)]
    in_specs += [pl.BlockSpec((2, xw, 2 * CMP_HID), lambda b, i, pt: (0, 0, 0)),
                 pl.BlockSpec((2, SUBLANES, xw), lambda b, i, pt: (0, 0, 0)),
                 pl.BlockSpec((2, CMP_HID, HEAD_DIM), lambda b, i, pt: (0, 0, 0))]
    args = [pages] * p_cnt + [w1cat, pe2, w2]
    if has_new:
        in_specs.append(pl.BlockSpec((1, N_KV_SLABS, xw), lambda b, i, pt: (b, 0, 0)))
        args.append(xnew)
    if token_major:
        out_spec = pl.BlockSpec((1, rows, KV_ROW), lambda b, i, pt: (b, i, 0))
        out_shape = jax.ShapeDtypeStruct((nb, tot_rows, KV_ROW), BF16)
    else:
        out_spec = pl.BlockSpec((1, N_KV_SLABS, rows, HEAD_DIM), lambda b, i, pt: (b, 0, i, 0))
        out_shape = jax.ShapeDtypeStruct((nb, N_KV_SLABS, tot_rows, HEAD_DIM), BF16)
    return pl.pallas_call(
        functools.partial(_compress_kernel, n_pages_step=p_cnt, n_steps=n_steps, has_new=has_new,
                          token_major=token_major),
        grid_spec=pltpu.PrefetchScalarGridSpec(
            num_scalar_prefetch=1, grid=(nb, grid_steps), in_specs=in_specs, out_specs=out_spec,
            scratch_shapes=[pltpu.VMEM((KV_ROW // LANES, p_cnt * PAGE_SIZE, LANES), F32),
                            pltpu.VMEM((N_KV_SLABS, rows, xw), F32),
                            pltpu.VMEM((N_KV_SLABS, 1, CMP_HID), F32),
                            pltpu.VMEM((2, 1, CMP_HID), F32)]),
        out_shape=out_shape,
        compiler_params=_params("arbitrary", "arbitrary"),
        name="compress_new" if has_new else "compress",
    )(page_table.reshape(-1), *args)


def _softmax_parts(s, valid, axis):
    s = jnp.where(valid, s, NEG)
    m = jnp.max(s, axis=axis, keepdims=True)
    e = jnp.where(valid, jnp.exp(s - m), 0.0)
    return e, jnp.sum(e, axis=axis, keepdims=True)


def _nsa_prompt_kernel(q_ref, kc_ref, vc_ref, ks_ref, vs_ref, kw_ref, vw_ref, g_ref, mt_ref, e_ref, o_ref, *,
                       tq, kb, n_cmp, k_top):
    qi = pl.program_id(2)
    qs = qi * tq
    r4 = GQA_GROUP
    q = q_ref[0].reshape(r4 * tq, HEAD_DIM)
    qpos = qs + lax.broadcasted_iota(jnp.int32, (1, tq, 1), 1)

    nc = kc_ref.shape[2]
    s = _dot_nt(q, kc_ref[0, 0]).reshape(r4, tq, nc)
    cid = lax.broadcasted_iota(jnp.int32, (1, 1, nc), 2)
    valid = (cid >= 1) & (cid <= n_cmp) & (cid * CMP_STRIDE + (CMP_BLOCK - CMP_STRIDE - 1) <= qpos)
    e, den = _softmax_parts(s, valid, 2)
    p = e * (1.0 / jnp.maximum(den, 1e-30))
    o_c = _dot(p.reshape(r4 * tq, nc).astype(BF16), vc_ref[0, 0]).reshape(r4, tq, HEAD_DIM)
    pg = p[0] + p[1] + p[2] + p[3]

    mt = mt_ref[...]
    imp = sum(_dot_nt(mt, piece) for piece in _split3(pg))
    n_blk = imp.shape[0]
    blk = lax.broadcasted_iota(jnp.int32, (n_blk, 1), 0)
    cur = (qs + lax.broadcasted_iota(jnp.int32, (1, tq), 1)) // SEL_BLOCK
    score = jnp.where(blk <= cur, imp, -1.0)
    score = jnp.where((blk == 0) | (blk == cur) | (blk == cur - 1), FORCE, score)
    rank = jnp.zeros((n_blk, tq), F32)
    for j in range(n_blk):
        sj = score[j:j + 1, :]
        ahead = (sj > score) | ((sj == score) & (blk > j))
        rank = rank + jnp.where(ahead, 1.0, 0.0)
    sel = jnp.where(rank < float(k_top), 1.0, 0.0).T.astype(BF16)

    n_kt = (qs + tq + kb - 1) // kb

    def body(kt, carry):
        m_run, l_run, acc = carry
        k0 = pl.multiple_of(kt * kb, kb)
        sk = _dot_nt(q, ks_ref[0, 0, pl.ds(k0, kb), :]).reshape(r4, tq, kb)
        in_sel = _dot(sel, e_ref[kt]) > 0.5
        kpos = k0 + lax.broadcasted_iota(jnp.int32, (1, 1, kb), 2)
        ok = in_sel[None] & (kpos <= qpos)
        sk = jnp.where(ok, sk, NEG)
        m_new = jnp.maximum(m_run, jnp.max(sk, axis=2, keepdims=True))
        alpha = jnp.exp(m_run - m_new)
        ek = jnp.where(ok, jnp.exp(sk - m_new), 0.0)
        l_new = alpha * l_run + jnp.sum(ek, axis=2, keepdims=True)
        pv = _dot(ek.reshape(r4 * tq, kb).astype(BF16), vs_ref[0, 0, pl.ds(k0, kb), :])
        return m_new, l_new, alpha * acc + pv.reshape(r4, tq, HEAD_DIM)

    init = (jnp.full((r4, tq, 1), NEG, F32), jnp.zeros((r4, tq, 1), F32), jnp.zeros((r4, tq, HEAD_DIM), F32))
    _, l_s, acc_s = lax.fori_loop(0, n_kt, body, init)
    o_s = acc_s * (1.0 / l_s)

    wl = kw_ref.shape[2] if kw_ref.shape[2] < WINDOW + tq else WINDOW + tq
    w0 = pl.multiple_of(jnp.maximum(qs + tq - wl, 0), tq)
    sw = _dot_nt(q, kw_ref[0, 0, pl.ds(w0, wl), :]).reshape(r4, tq, wl)
    dq = qpos - (w0 + lax.broadcasted_iota(jnp.int32, (1, 1, wl), 2))
    ew, denw = _softmax_parts(sw, (dq >= 0) & (dq < WINDOW), 2)
    o_w = _dot(ew.reshape(r4 * tq, wl).astype(BF16), vw_ref[0, 0, pl.ds(w0, wl), :]).reshape(r4, tq, HEAD_DIM)
    o_w = o_w * (1.0 / denw)

    gt = g_ref[0]
    for r in range(r4):
        c0 = r * N_BRANCH
        o_ref[0, :, r * HEAD_DIM:(r + 1) * HEAD_DIM] = (
            gt[:, c0:c0 + 1] * o_c[r] + gt[:, c0 + 1:c0 + 2] * o_s[r] + gt[:, c0 + 2:c0 + 3] * o_w[r])


def _importance_matrix(n_blk_pad, n_col):
    mt = np.zeros((n_blk_pad, n_col), np.float32)
    ratio = SEL_BLOCK // CMP_STRIDE
    for b in range(n_blk_pad):
        for c, w in zip(range(ratio * b, ratio * b + ratio + 1), (1.0,) + (2.0,) * (ratio - 1) + (1.0,)):
            if c < n_col:
                mt[b, c] = w
    return mt


def _nsa_prompt(q_hm, comp_hm, ks_hm, kw_hm, gates, tq, kb):
    b_sz, _, s_len, _ = q_hm.shape
    nc = comp_hm.shape[2]
    assert s_len % SEL_BLOCK == 0 and s_len % kb == 0 and kb % tq == 0 and s_len % tq == 0 and CMP_SPAN == 2
    n_blk = s_len // SEL_BLOCK
    n_cmp = s_len // CMP_STRIDE - CMP_SPAN + 1
    mt = jnp.asarray(_importance_matrix(n_blk, nc), BF16)
    key_blk = np.arange(s_len).reshape(s_len // kb, 1, kb) // SEL_BLOCK
    expand = jnp.asarray(key_blk == np.arange(n_blk).reshape(1, n_blk, 1), BF16)

    def hm_spec(rows, off):
        return pl.BlockSpec((1, 1, rows, HEAD_DIM), lambda b, g, i: (b, off + g, 0, 0))

    return pl.pallas_call(
        functools.partial(_nsa_prompt_kernel, tq=tq, kb=kb, n_cmp=n_cmp, k_top=min(N_SEL, n_blk)),
        grid=(b_sz, KV_HEADS, s_len // tq),
        in_specs=[pl.BlockSpec((1, GQA_GROUP, tq, HEAD_DIM), lambda b, g, i: (b, g, i, 0)),
                  hm_spec(nc, 0), hm_spec(nc, KV_HEADS),
                  hm_spec(s_len, 0), hm_spec(s_len, KV_HEADS),
                  hm_spec(s_len, 0), hm_spec(s_len, KV_HEADS),
                  pl.BlockSpec((1, tq, LANES), lambda b, g, i: (b, i, g)),
                  pl.BlockSpec((n_blk, nc), lambda b, g, i: (0, 0)),
                  pl.BlockSpec((s_len // kb, n_blk, kb), lambda b, g, i: (0, 0, 0))],
        out_specs=pl.BlockSpec((1, tq, GQA_GROUP * HEAD_DIM), lambda b, g, i: (b, i, g)),
        out_shape=jax.ShapeDtypeStruct((b_sz, s_len, NSA_WIDTH), F32),
        compiler_params=_params("arbitrary", "arbitrary", "arbitrary"),
        name="nsa_prompt",
    )(q_hm, comp_hm, comp_hm, ks_hm, ks_hm, kw_hm, kw_hm, gates, mt, expand)


Q_COLS = LANES
COLS_PER_HEAD = Q_COLS // GQA_GROUP
V_OFF = LANES
V_COL0 = KV_HEADS * HEAD_DIM - V_OFF


def _nsa_sample_kernel(pt_ref, *refs, n_pages_step, n_steps, t_new, past, n_cmp, n_sel, k_top):
    del pt_ref
    p_cnt = n_pages_step
    page_refs = refs[:p_cnt]
    (qbd_ref, comp_ref, mt_ref, et_ref, knew_ref, win_ref, wnew_ref, g_ref,
     o_ref, sel_scr, m_scr, l_scr, acc_scr, oc_scr) = refs[p_cnt:]
    i = pl.program_id(1)
    qbd = qbd_ref[0]
    col = lax.broadcasted_iota(jnp.int32, (1, Q_COLS), 1)
    tok = col % SUBLANES
    qpos = past + tok
    kwin = slice(0, 2 * LANES)
    vwin = slice(V_OFF, V_OFF + 2 * LANES)

    def attend(rows_bf16, valid):
        s = _dot(rows_bf16[:, kwin], qbd)
        s = jnp.where(valid, s, NEG)
        return s

    @pl.when(i == 0)
    def _():
        comp = comp_ref[0]
        ncp = comp.shape[0]
        cid = lax.broadcasted_iota(jnp.int32, (ncp, 1), 0)
        valid = (cid >= 1) & (cid <= n_cmp) & (cid * CMP_STRIDE + (CMP_BLOCK - CMP_STRIDE - 1) <= qpos)
        s = attend(comp, valid)
        e, den = _softmax_parts(s, valid, 0)
        p = e * (1.0 / jnp.maximum(den, 1e-30))
        oc_scr[...] = _dot_tn(p.astype(BF16), comp[:, vwin])
        mt = mt_ref[...]
        imp = sum(_dot(mt, piece) for piece in _split3(p))
        imp = imp + pltpu.roll(imp, COLS_PER_HEAD, 1)
        imp = imp + pltpu.roll(imp, 2 * COLS_PER_HEAD, 1)
        nbp = imp.shape[0]
        blk = lax.broadcasted_iota(jnp.int32, (nbp, 1), 0)
        cur = qpos // SEL_BLOCK
        score = jnp.where(blk <= cur, imp, -1.0)
        score = jnp.where((blk == 0) | (blk == cur) | (blk == cur - 1), FORCE, score)
        score = jnp.where(blk < n_sel, score, PAD_SCORE)
        blk_f = blk.astype(F32)
        sel = jnp.zeros((nbp, Q_COLS), F32)
        for _ in range(k_top):
            best = jnp.max(score, axis=0, keepdims=True)
            first = jnp.min(jnp.where(score == best, blk_f, float(nbp)), axis=0, keepdims=True)
            hit = blk_f == first
            sel = jnp.where(hit, 1.0, sel)
            score = jnp.where(hit, TAKEN_SCORE, score)
        sel_scr[...] = sel
        m_scr[...] = jnp.full_like(m_scr, NEG)
        l_scr[...] = jnp.zeros_like(l_scr)
        acc_scr[...] = jnp.zeros_like(acc_scr)

    def accumulate(s, v_rows):
        m_run = m_scr[0:1, :]
        m_new = jnp.maximum(m_run, jnp.max(s, axis=0, keepdims=True))
        alpha = jnp.exp(m_run - m_new)
        e = jnp.where(s > 0.5 * NEG, jnp.exp(s - m_new), 0.0)
        l_scr[0:1, :] = alpha * l_scr[0:1, :] + jnp.sum(e, axis=0, keepdims=True)
        m_scr[0:1, :] = m_new
        alpha_col = jnp.broadcast_to(alpha, (SUBLANES, Q_COLS)).T[:, 0:1]
        acc_scr[...] = alpha_col * acc_scr[...] + _dot_tn(e.astype(BF16), v_rows)

    blocks_per_page = PAGE_SIZE // SEL_BLOCK
    nb_step = blocks_per_page * p_cnt
    kv = jnp.concatenate([page_refs[p][0].astype(BF16) for p in range(p_cnt)], axis=0)
    sel_step = sel_scr[pl.ds(pl.multiple_of(i * nb_step, nb_step), nb_step), :].astype(BF16)
    in_sel = _dot(et_ref[...], sel_step) > 0.5
    accumulate(attend(kv, in_sel), kv[:, vwin])

    @pl.when(i == n_steps - 1)
    def _():
        knew = knew_ref[0].astype(BF16)
        krow = lax.broadcasted_iota(jnp.int32, (t_new, 1), 0)
        new_blk = past // SEL_BLOCK
        ok = (sel_scr[new_blk:new_blk + 1, :] > 0.5) & (krow <= tok)
        accumulate(attend(knew, ok), knew[:, vwin])
        o_s = acc_scr[...] * jnp.broadcast_to(1.0 / l_scr[0:1, :], (SUBLANES, Q_COLS)).T[:, 0:1]

        wbuf = win_ref[0].astype(BF16)
        wnew = wnew_ref[0].astype(BF16)
        buf = wbuf.shape[0]
        pos_b = past - buf + lax.broadcasted_iota(jnp.int32, (buf, 1), 0)
        pos_n = past + krow
        ok_b = (qpos - pos_b >= 0) & (qpos - pos_b < WINDOW) & (pos_b >= 0)
        ok_n = (qpos - pos_n >= 0) & (qpos - pos_n < WINDOW)
        s_b = attend(wbuf, ok_b)
        s_n = attend(wnew, ok_n)
        m_w = jnp.maximum(jnp.max(s_b, axis=0, keepdims=True), jnp.max(s_n, axis=0, keepdims=True))
        e_b = jnp.where(ok_b, jnp.exp(s_b - m_w), 0.0)
        e_n = jnp.where(ok_n, jnp.exp(s_n - m_w), 0.0)
        den = jnp.sum(e_b, axis=0, keepdims=True) + jnp.sum(e_n, axis=0, keepdims=True)
        o_w = _dot_tn(e_b.astype(BF16), wbuf[:, vwin]) + _dot_tn(e_n.astype(BF16), wnew[:, vwin])
        o_w = o_w * jnp.broadcast_to(1.0 / jnp.maximum(den, 1e-30), (SUBLANES, Q_COLS)).T[:, 0:1]
        o_c = oc_scr[...]

        gt = g_ref[0]
        for g in range(KV_HEADS):
            for r in range(GQA_GROUP):
                r0 = r * COLS_PER_HEAD + g * SUBLANES
                c0 = V_COL0 + g * HEAD_DIM
                gc = (g * GQA_GROUP + r) * N_BRANCH
                blk_o = [o[r0:r0 + t_new, c0:c0 + HEAD_DIM] for o in (o_c, o_s, o_w)]
                h0 = (g * GQA_GROUP + r) * HEAD_DIM
                o_ref[0, :, h0:h0 + HEAD_DIM] = (gt[:, gc:gc + 1] * blk_o[0] + gt[:, gc + 1:gc + 2] * blk_o[1]
                                                 + gt[:, gc + 2:gc + 3] * blk_o[2])


def _nsa_sample(cache_slc, page_table, qbd, comp_tm, knew, win_buf, wnew, gates, past):
    db, n_pages = page_table.shape
    t_new = knew.shape[1]
    assert t_new == SUBLANES and KV_HEADS * t_new <= COLS_PER_HEAD and past % SEL_BLOCK == 0 and t_new <= SEL_BLOCK
    p_cnt = min(32, n_pages)
    assert n_pages % p_cnt == 0
    n_steps = n_pages // p_cnt
    ncp = comp_tm.shape[1]
    n_str = -(-(past + t_new) // CMP_STRIDE)
    n_cmp = n_str - CMP_SPAN + 1
    n_sel = -(-(past + t_new) // SEL_BLOCK)
    blocks_step = (PAGE_SIZE // SEL_BLOCK) * p_cnt
    nbp = -(-(n_sel + 1) // blocks_step) * blocks_step
    mt = jnp.asarray(_importance_matrix(nbp, ncp), BF16)
    keys_step = PAGE_SIZE * p_cnt
    et = jnp.asarray((np.arange(keys_step)[:, None] // SEL_BLOCK) == np.arange(blocks_step)[None, :], BF16)
    buf = win_buf.shape[1]

    def page_spec(p):
        return pl.BlockSpec((1, PAGE_SIZE, KV_ROW), lambda b, i, pt: (pt[b * n_pages + i * p_cnt + p], 0, 0))

    def per_b(*shape):
        return pl.BlockSpec((1,) + shape, lambda b, i, pt: (b,) + (0,) * len(shape))

    def const(*shape):
        return pl.BlockSpec(shape, lambda b, i, pt: (0,) * len(shape))

    in_specs = [page_spec(p) for p in range(p_cnt)]
    in_specs += [per_b(2 * LANES, Q_COLS), per_b(ncp, KV_ROW), const(nbp, ncp), const(keys_step, blocks_step),
                 per_b(t_new, KV_ROW), per_b(buf, KV_ROW), per_b(t_new, KV_ROW), per_b(t_new, LANES)]
    return pl.pallas_call(
        functools.partial(_nsa_sample_kernel, n_pages_step=p_cnt, n_steps=n_steps, t_new=t_new, past=past,
                          n_cmp=n_cmp, n_sel=n_sel, k_top=min(N_SEL, n_sel)),
        grid_spec=pltpu.PrefetchScalarGridSpec(
            num_scalar_prefetch=1, grid=(db, n_steps), in_specs=in_specs,
            out_specs=per_b(t_new, NSA_WIDTH),
            scratch_shapes=[pltpu.VMEM((nbp, Q_COLS), F32),
                            pltpu.VMEM((SUBLANES, Q_COLS), F32), pltpu.VMEM((SUBLANES, Q_COLS), F32),
                            pltpu.VMEM((Q_COLS, 2 * LANES), F32), pltpu.VMEM((Q_COLS, 2 * LANES), F32)]),
        out_shape=jax.ShapeDtypeStruct((db, t_new, NSA_WIDTH), F32),
        compiler_params=_params("arbitrary", "arbitrary"),
        name="nsa_sample",
    )(page_table.reshape(-1), *([cache_slc] * p_cnt), qbd, comp_tm, mt, et, knew, win_buf, wnew, gates)


def _out_kernel(x_ref, po_ref, zp_ref, no_ref, zn_ref, gate_ref, gnp_ref, gnn_ref, w_ref, fn_ref, y_ref):
    zp = zp_ref[0]
    zn = zn_ref[0]
    mp = _rms(po_ref[0], gnp_ref[...]) * (zp * _sigmoid(zp))
    mn = _rms(no_ref[0], gnn_ref[...]) * (zn * _sigmoid(zn))
    m = jnp.concatenate([mp, mn], axis=-1).astype(BF16)
    xo = x_ref[0] + gate_ref[0] * _dot(m, w_ref[...])
    y_ref[0] = _rms(xo, fn_ref[...])


def _out(x, pool_o, zp, nsa_o, zn, gate, gn_pool, gn_nsa, w_out_bf16, final_norm, tm):
    nb, t, d = x.shape
    r = gate.shape[1]
    assert t % tm == 0 and r in (1, t)
    if r == 1:
        gate_spec = pl.BlockSpec((1, 1, d), lambda b, i: (b, 0, 0))
    else:
        gate_spec = pl.BlockSpec((1, tm, d), lambda b, i: (b, i, 0))

    def tok(width):
        return pl.BlockSpec((1, tm, width), lambda b, i: (b, i, 0))

    def const(*shape):
        return pl.BlockSpec(shape, lambda b, i: (0,) * len(shape))

    return pl.pallas_call(
        _out_kernel,
        grid=(nb, t // tm),
        in_specs=[tok(d), tok(POOL_WIDTH), tok(POOL_WIDTH), tok(NSA_WIDTH), tok(NSA_WIDTH), gate_spec,
                  const(1, POOL_WIDTH), const(1, NSA_WIDTH), const(d, d), const(1, d)],
        out_specs=tok(d),
        out_shape=jax.ShapeDtypeStruct((nb, t, d), F32),
        compiler_params=_params("arbitrary", "arbitrary"),
        name="out",
    )(x, pool_o, zp, nsa_o, zn, gate, gn_pool.reshape(1, -1), gn_nsa.reshape(1, -1), w_out_bf16,
      final_norm.reshape(1, d))


def _block_diag(pool_w):
    n, c, _ = pool_w.shape
    eye = jnp.eye(n, dtype=pool_w.dtype)
    return (eye[:, None, :, None] * pool_w[:, :, None, :]).reshape(n * c, n * c)


def _first_layer(phi_w1, phi_pe):
    w = phi_w1.reshape(CMP_SPAN, CMP_STRIDE, 2, HEAD_DIM, CMP_HID)
    w1cat = w.transpose(2, 1, 3, 0, 4).reshape(2, CMP_STRIDE * HEAD_DIM, CMP_SPAN * CMP_HID)
    pe = phi_pe.reshape(CMP_SPAN, CMP_STRIDE, 2, HEAD_DIM).transpose(2, 0, 1, 3).reshape(2, CMP_SPAN, -1)
    pe2 = jnp.concatenate([pe, jnp.zeros((2, SUBLANES - CMP_SPAN, pe.shape[-1]), pe.dtype)], axis=1)
    return w1cat.astype(BF16), pe2


def kernel(x_prompt, x_sample, c_prompt, c_sample, cache_cmp_kv, cache_slc_kv, state_win_kv, state_pool, page_table, norm_w, w_ada, b_ada, w_in, pool_w, pool_scale, phi_w1, phi_pe, phi_w2, gn_pool, gn_nsa, w_out, final_norm):
    b_sz, s_len, d = x_prompt.shape
    db, t_new, _ = x_sample.shape
    depth = norm_w.shape[0]
    assert depth == 1 and w_in.shape[-1] == IN_WIDTH and d // 4 == POOL_WIDTH
    n_phys = cache_cmp_kv.shape[1]
    n_pages = page_table.shape[1]
    past = n_pages * PAGE_SIZE
    lyr = 0

    n_c = b_sz + db
    c_all = jnp.concatenate([c_prompt, c_sample, jnp.zeros((-n_c % SUBLANES, d), F32)], axis=0)
    ada = _ada(c_all, w_ada[lyr], b_ada[lyr])
    shift, scale, gate = ada[:, :d], ada[:, d:2 * d], ada[:, 2 * d:]

    w_pad = jnp.pad(w_in[lyr].astype(BF16), ((0, 0), (0, IN_PAD - IN_WIDTH)))
    w_out_bf16 = w_out[lyr].astype(BF16)
    wbd = _block_diag(pool_w[lyr]).astype(BF16)
    w1cat, pe2 = _first_layer(phi_w1[lyr], phi_pe[lyr])
    w2 = phi_w2[lyr].astype(BF16)
    n_tok_s = db * t_new

    def per_token(v):
        return jnp.repeat(v[b_sz:n_c], t_new, axis=0)[None]

    tm = min(512, s_len)
    (u_p, zp_p, zn_p, q_p, kvc_p, kvs_p, kvw_p, ks_hm, kw_hm, g_p) = _in_proj(
        x_prompt, scale[:b_sz, None], shift[:b_sz, None], norm_w[lyr], w_pad, tm)
    pool_p = _pool(jnp.zeros((b_sz, POOL_HALO, POOL_WIDTH), F32), u_p, wbd, pool_scale[lyr], 0, tm)
    pages_p = s_len // PAGE_SIZE
    ident = jnp.arange(b_sz * pages_p, dtype=jnp.int32).reshape(b_sz, pages_p)
    comp_p = _compress(kvc_p.reshape(b_sz * pages_p, PAGE_SIZE, KV_ROW), ident, w1cat, pe2, w2, None, False)
    nsa_p = _nsa_prompt(q_p, comp_p, ks_hm, kw_hm, g_p, tq=128, kb=min(512, s_len))
    y_p = _out(x_prompt, pool_p, zp_p, nsa_p, zn_p, gate[:b_sz, None], gn_pool[lyr], gn_nsa[lyr], w_out_bf16,
               final_norm, tm)

    (u_s, zp_s, zn_s, q_s, kvc_s, kvs_s, kvw_s, _, _, g_s) = _in_proj(
        x_sample.reshape(1, n_tok_s, d), per_token(scale), per_token(shift), norm_w[lyr], w_pad, n_tok_s)
    u_s3 = u_s.reshape(db, t_new, POOL_WIDTH)
    hist = jnp.concatenate([jnp.zeros((db, POOL_HALO - state_pool.shape[2], POOL_WIDTH), F32), state_pool[lyr]], axis=1)
    pool_s = _pool(hist, u_s3, wbd, pool_scale[lyr], past, t_new)
    kvc_s3 = kvc_s.reshape(db, t_new, KV_ROW)
    kvs_s3 = kvs_s.reshape(db, t_new, KV_ROW)
    kvw_s3 = kvw_s.reshape(db, t_new, KV_ROW)
    xnew = jnp.pad(kvc_s3, ((0, 0), (0, CMP_STRIDE - t_new), (0, 0)))
    xnew = xnew.reshape(db, CMP_STRIDE, N_KV_SLABS, HEAD_DIM).transpose(0, 2, 1, 3).reshape(db, N_KV_SLABS, -1)
    comp_s = _compress(cache_cmp_kv[lyr].reshape(n_phys, PAGE_SIZE, KV_ROW), page_table, w1cat, pe2, w2, xnew, True)
    q5 = q_s[0].reshape(KV_HEADS, GQA_GROUP, db, t_new, HEAD_DIM)
    eye = jnp.eye(KV_HEADS, dtype=BF16)
    qbd = q5.transpose(2, 0, 4, 1, 3)[:, :, :, :, None, :] * eye[None, :, None, None, :, None]
    qbd = jnp.pad(qbd.reshape(db, KV_HEADS * HEAD_DIM, GQA_GROUP, KV_HEADS * t_new),
                  ((0, 0), (0, 2 * LANES - KV_HEADS * HEAD_DIM), (0, 0), (0, COLS_PER_HEAD - KV_HEADS * t_new)))
    qbd = qbd.reshape(db, 2 * LANES, Q_COLS)
    g_s3 = g_s[0, :, :LANES].reshape(db, t_new, LANES)
    win_buf = state_win_kv[lyr].reshape(db, -1, KV_ROW)
    nsa_s = _nsa_sample(cache_slc_kv[lyr].reshape(n_phys, PAGE_SIZE, KV_ROW), page_table, qbd, comp_s, kvs_s3,
                        win_buf, kvw_s3, g_s3, past)
    y_s = _out(x_sample.reshape(1, n_tok_s, d), pool_s.reshape(1, n_tok_s, POOL_WIDTH), zp_s,
               nsa_s.reshape(1, n_tok_s, NSA_WIDTH), zn_s, per_token(gate), gn_pool[lyr], gn_nsa[lyr], w_out_bf16,
               final_norm, n_tok_s).reshape(db, t_new, d)

    kv_shape = (2, KV_HEADS, HEAD_DIM)
    win_len = min(WINDOW, s_len)
    buf = win_buf.shape[1]
    new_win_s = jnp.concatenate([win_buf, kvw_s3], axis=1)[:, -buf:]
    hist_len = state_pool.shape[2]
    new_pool_s = jnp.concatenate([state_pool[lyr], u_s3], axis=1)[:, -hist_len:]
    return (y_p, y_s,
            kvc_p.reshape((1, b_sz, s_len) + kv_shape), kvc_s3.reshape((1, db, t_new) + kv_shape),
            kvs_p.reshape((1, b_sz, s_len) + kv_shape), kvs_s3.reshape((1, db, t_new) + kv_shape),
            kvw_p[:, s_len - win_len:].reshape((1, b_sz, win_len) + kv_shape),
            new_win_s.reshape((1, db, buf) + kv_shape),
            u_p[:, s_len - hist_len:][None], new_pool_s[None])
```

```python
import functools

import numpy as np
import jax
import jax.numpy as jnp
from jax import lax
from jax.experimental import pallas as pl
from jax.experimental.pallas import tpu as pltpu

F32 = jnp.float32
BF16 = jnp.bfloat16

HEAD_DIM = 64
GQA_GROUP = 4
KV_HEADS = 3
N_HEADS = KV_HEADS * GQA_GROUP
N_KV_SLABS = 2 * KV_HEADS
KV_ROW = N_KV_SLABS * HEAD_DIM
N_BRANCH = 3
POOL_WINDOWS = (2, 4, 8, 16)
POOL_HALO = 16
CMP_BLOCK = 32
CMP_STRIDE = 16
CMP_SPAN = CMP_BLOCK // CMP_STRIDE
CMP_HID = 2 * HEAD_DIM
SEL_BLOCK = 64
N_SEL = 16
WINDOW = 512
PAGE_SIZE = 128
CHUNKS_PER_PAGE = PAGE_SIZE // CMP_STRIDE
CHUNK_PITCH = 24
EPS = 1e-6
NEG = -1e30
FORCE = 1e4
PAD_SCORE = -1e38
TAKEN_SCORE = -3e38

LANES = 128
SUBLANES = 8
VMEM_LIMIT_BYTES = 56 * 1024 * 1024


def _dot(a, b):
    return jnp.dot(a, b, preferred_element_type=F32)


def _dot_nt(a, b):
    return lax.dot_general(a, b, (((1,), (1,)), ((), ())), preferred_element_type=F32)


def _dot_tn(a, b):
    return lax.dot_general(a, b, (((0,), (0,)), ((), ())), preferred_element_type=F32)


def _sigmoid(v):
    return 1.0 / (1.0 + jnp.exp(-v))


def _rms(v, g):
    return v * lax.rsqrt(jnp.mean(v * v, axis=-1, keepdims=True) + EPS) * g


def _split3(v):
    hi = v.astype(BF16)
    r1 = v - hi.astype(F32)
    mid = r1.astype(BF16)
    lo = (r1 - mid.astype(F32)).astype(BF16)
    return hi, mid, lo


def _params(*sem):
    return pltpu.CompilerParams(dimension_semantics=sem, vmem_limit_bytes=VMEM_LIMIT_BYTES)


def _ada_kernel(c_ref, w_ref, b_ref, o_ref):
    c = c_ref[...]
    a = (c * _sigmoid(c)).astype(BF16)
    o_ref[...] = _dot(a, w_ref[...].astype(BF16)) + b_ref[...]


def _ada(c, w_ada, b_ada):
    m, d = c.shape
    n = w_ada.shape[1]
    tn = 512
    return pl.pallas_call(
        _ada_kernel,
        grid=(n // tn,),
        in_specs=[pl.BlockSpec((m, d), lambda j: (0, 0)),
                  pl.BlockSpec((d, tn), lambda j: (0, j)),
                  pl.BlockSpec((1, tn), lambda j: (0, j))],
        out_specs=pl.BlockSpec((m, tn), lambda j: (0, j)),
        out_shape=jax.ShapeDtypeStruct((m, n), F32),
        compiler_params=_params("arbitrary"),
        name="ada",
    )(c, w_ada, b_ada.reshape(1, n))


POOL_WIDTH = 256
NSA_WIDTH = N_HEADS * HEAD_DIM
GATE_COLS = N_BRANCH * N_HEADS
MAX_SEL_BLOCKS = LANES - HEAD_DIM
_SEG = {}
_off = 0
for _name, _w in (("u", POOL_WIDTH), ("zp", POOL_WIDTH), ("q", NSA_WIDTH), ("zn", NSA_WIDTH),
                  ("kvc", KV_ROW), ("kvs", KV_ROW), ("kvw", KV_ROW), ("gl", LANES)):
    _SEG[_name] = (_off, _w)
    _off += _w
IN_PAD = _off
IN_WIDTH = IN_PAD - LANES + GATE_COLS


def _in_proj_kernel(x_ref, sc_ref, sh_ref, nw_ref, w_ref,
                    u_ref, zp_ref, zn_ref, q_ref, kvc_ref, kvs_ref, kvw_ref, ks_aug_ref, vs_hm_ref, kw_hm_ref, g_ref):
    tm = x_ref.shape[1]
    h = (_rms(x_ref[0], nw_ref[...]) * (1.0 + sc_ref[0]) + sh_ref[0]).astype(BF16)
    pos = pl.program_id(1) * tm + lax.broadcasted_iota(jnp.int32, (tm, 1), 0)
    onehot = (lax.broadcasted_iota(jnp.int32, (1, MAX_SEL_BLOCKS), 1) == pos // SEL_BLOCK).astype(BF16)

    def seg(name):
        s, w = _SEG[name]
        return _dot(h, w_ref[:, s:s + w])

    u_ref[0] = seg("u")
    zp_ref[0] = seg("zp")
    zn_ref[0] = seg("zn")
    q = seg("q") * (HEAD_DIM ** -0.5)
    for hh in range(N_HEADS):
        q_ref[0, hh] = q[:, hh * HEAD_DIM:(hh + 1) * HEAD_DIM].astype(BF16)
    kvc_ref[0] = seg("kvc")
    kvs = seg("kvs")
    kvs_ref[0] = kvs
    kvw = seg("kvw")
    kvw_ref[0] = kvw
    for kg in range(N_KV_SLABS):
        slab = kvs[:, kg * HEAD_DIM:(kg + 1) * HEAD_DIM].astype(BF16)
        if kg < KV_HEADS:
            ks_aug_ref[0, kg] = jnp.concatenate([slab, onehot], axis=1)
        else:
            vs_hm_ref[0, kg - KV_HEADS] = slab
        kw_hm_ref[0, kg] = kvw[:, kg * HEAD_DIM:(kg + 1) * HEAD_DIM].astype(BF16)
    sig = _sigmoid(seg("gl"))
    per_group = GQA_GROUP * N_BRANCH
    for gi in range(KV_HEADS):
        g_ref[0, :, gi * LANES:(gi + 1) * LANES] = sig if gi == 0 else pltpu.roll(sig, LANES - per_group * gi, 1)


def _in_proj(x, scale, shift, norm_w, w_pad, tm):
    nb, t, d = x.shape
    r = scale.shape[1]
    assert t % tm == 0 and r in (1, t)
    if r == 1:
        mod_spec = pl.BlockSpec((1, 1, d), lambda b, i: (b, 0, 0))
    else:
        mod_spec = pl.BlockSpec((1, tm, d), lambda b, i: (b, i, 0))

    def tok(width):
        return pl.BlockSpec((1, tm, width), lambda b, i: (b, i, 0))

    def hm(n, width=HEAD_DIM):
        return pl.BlockSpec((1, n, tm, width), lambda b, i: (b, 0, i, 0))

    def sd(*shape, dtype=F32):
        return jax.ShapeDtypeStruct(shape, dtype)

    return pl.pallas_call(
        _in_proj_kernel,
        grid=(nb, t // tm),
        in_specs=[tok(d), mod_spec, mod_spec,
                  pl.BlockSpec((1, d), lambda b, i: (0, 0)),
                  pl.BlockSpec((d, IN_PAD), lambda b, i: (0, 0))],
        out_specs=[tok(POOL_WIDTH), tok(POOL_WIDTH), tok(NSA_WIDTH), hm(N_HEADS),
                   tok(KV_ROW), tok(KV_ROW), tok(KV_ROW),
                   hm(KV_HEADS, HEAD_DIM + MAX_SEL_BLOCKS), hm(KV_HEADS), hm(N_KV_SLABS),
                   tok(KV_HEADS * LANES)],
        out_shape=[sd(nb, t, POOL_WIDTH), sd(nb, t, POOL_WIDTH), sd(nb, t, NSA_WIDTH),
                   sd(nb, N_HEADS, t, HEAD_DIM, dtype=BF16),
                   sd(nb, t, KV_ROW), sd(nb, t, KV_ROW), sd(nb, t, KV_ROW),
                   sd(nb, KV_HEADS, t, HEAD_DIM + MAX_SEL_BLOCKS, dtype=BF16),
                   sd(nb, KV_HEADS, t, HEAD_DIM, dtype=BF16), sd(nb, N_KV_SLABS, t, HEAD_DIM, dtype=BF16),
                   sd(nb, t, KV_HEADS * LANES)],
        compiler_params=_params("arbitrary", "arbitrary"),
        name="in_proj",
    )(x, scale, shift, norm_w.reshape(1, d), w_pad)


def _pool_kernel(hist_ref, uprev_ref, u_ref, wbd_ref, ps_ref, o_ref, *, pos0, tp):
    i = pl.program_id(1)
    u = u_ref[0]
    halo = jnp.where(i == 0, hist_ref[0], uprev_ref[0])
    ue = jnp.concatenate([halo, u], axis=0)
    sums = []
    s = ue
    for w in POOL_WINDOWS:
        s = s + pltpu.roll(s, w // 2, 0)
        sums.append(s[POOL_HALO:])
    c = u.shape[-1]
    grp = lax.broadcasted_iota(jnp.int32, (1, c), 1) // (c // len(POOL_WINDOWS))
    pos = pos0 + i * tp + lax.broadcasted_iota(jnp.int32, (tp, 1), 0)
    tot = sums[-1]
    win = jnp.full((1, c), float(POOL_WINDOWS[-1]), F32)
    for gi in range(len(POOL_WINDOWS) - 2, -1, -1):
        tot = jnp.where(grp == gi, sums[gi], tot)
        win = jnp.where(grp == gi, float(POOL_WINDOWS[gi]), win)
    cnt = jnp.minimum(win, (pos + 1).astype(F32))
    dlt = tot / cnt - u
    o_ref[0] = _dot(dlt.astype(BF16), wbd_ref[...]) * ps_ref[...]


def _pool(hist, u, wbd, pool_scale, pos0, tp):
    nb, t, c = u.shape
    assert t % tp == 0
    if t >= POOL_HALO:
        assert tp % POOL_HALO == 0
        uprev, ratio = u, tp // POOL_HALO
        prev_spec = pl.BlockSpec((1, POOL_HALO, c), lambda b, i: (b, jnp.maximum(i * ratio - 1, 0), 0))
    else:
        assert t == tp
        uprev = hist
        prev_spec = pl.BlockSpec((1, POOL_HALO, c), lambda b, i: (b, 0, 0))
    return pl.pallas_call(
        functools.partial(_pool_kernel, pos0=pos0, tp=tp),
        grid=(nb, t // tp),
        in_specs=[pl.BlockSpec((1, POOL_HALO, c), lambda b, i: (b, 0, 0)), prev_spec,
                  pl.BlockSpec((1, tp, c), lambda b, i: (b, i, 0)),
                  pl.BlockSpec((c, c), lambda b, i: (0, 0)),
                  pl.BlockSpec((1, c), lambda b, i: (0, 0))],
        out_specs=pl.BlockSpec((1, tp, c), lambda b, i: (b, i, 0)),
        out_shape=jax.ShapeDtypeStruct((nb, t, c), F32),
        compiler_params=_params("arbitrary", "arbitrary"),
        name="pool",
    )(hist, uprev, u, wbd, pool_scale.reshape(1, c))


def _gelu_tanh(v):
    return 0.5 * v * (1.0 + jnp.tanh(np.sqrt(2.0 / np.pi).astype(np.float32) * (v + 0.044715 * (v * v * v))))


def _compress_kernel(pt_ref, *refs, n_pages_step, n_steps, has_new, token_major):
    del pt_ref
    p_cnt = n_pages_step
    page_refs = refs[:p_cnt]
    w1_ref, pe_ref, w2_ref = refs[p_cnt:p_cnt + 3]
    k = p_cnt + 3
    xnew_ref = None
    if has_new:
        xnew_ref = refs[k]
        k += 1
    out_ref, slab_scr, x_scr, carry_scr, bias_scr = refs[k:k + 5]
    i = pl.program_id(1)
    rows = CHUNKS_PER_PAGE * p_cnt
    row_id = lax.broadcasted_iota(jnp.int32, (rows, 1), 0)

    @pl.when(i == 0)
    def _():
        carry_scr[...] = jnp.zeros_like(carry_scr)
        for kk in range(2):
            pb = _dot(pe_ref[kk].astype(BF16), w1_ref[kk])
            bias_scr[kk] = pb[0:1, :CMP_HID] + pb[1:2, CMP_HID:]

    def store(kg, comp):
        if token_major:
            out_ref[0, :, kg * HEAD_DIM:(kg + 1) * HEAD_DIM] = comp.astype(BF16)
        else:
            out_ref[0, kg] = comp.astype(BF16)

    def finish(kg, a0_prev, a1):
        kk = kg // KV_HEADS
        pre = a0_prev + a1 + bias_scr[kk]
        return _dot(_gelu_tanh(pre).astype(BF16), w2_ref[kk])

    def main():
        low = lax.broadcasted_iota(jnp.int32, (rows, LANES), 1) < HEAD_DIM
        for p in range(p_cnt):
            for cb in range(KV_ROW // LANES):
                rows_t = page_refs[p][0, cb].T
                for ch in range(CHUNKS_PER_PAGE):
                    r0 = (p * CHUNKS_PER_PAGE + ch) * CHUNK_PITCH
                    slab_scr[cb, r0:r0 + CMP_STRIDE, :] = rows_t[ch * CMP_STRIDE:(ch + 1) * CMP_STRIDE, :]
        for cb in range(KV_ROW // LANES):
            for c in range(CMP_STRIDE // 2):
                va = slab_scr[cb, pl.ds(2 * c, rows, stride=CHUNK_PITCH), :]
                vb = slab_scr[cb, pl.ds(2 * c + 1, rows, stride=CHUNK_PITCH), :]
                x_scr[2 * cb, :, c * LANES:(c + 1) * LANES] = jnp.where(low, va, pltpu.roll(vb, HEAD_DIM, 1))
                x_scr[2 * cb + 1, :, c * LANES:(c + 1) * LANES] = jnp.where(low, pltpu.roll(va, HEAD_DIM, 1), vb)
        for kg in range(N_KV_SLABS):
            a = _dot(x_scr[kg].astype(BF16), w1_ref[kg // KV_HEADS])
            a0 = a[:, :CMP_HID]
            a0_prev = jnp.where(row_id == 0, carry_scr[kg], pltpu.roll(a0, 1, 0))
            carry_scr[kg] = a0[rows - 1:rows, :]
            store(kg, finish(kg, a0_prev, a[:, CMP_HID:]))

    if not has_new:
        main()
    else:
        pl.when(i < n_steps)(main)

        @pl.when(i == n_steps)
        def _():
            for kg in range(N_KV_SLABS):
                xk = jnp.broadcast_to(xnew_ref[0, kg:kg + 1, :], (SUBLANES, CMP_STRIDE * HEAD_DIM)).astype(BF16)
                a1 = _dot(xk, w1_ref[kg // KV_HEADS])[0:1, CMP_HID:]
                comp = finish(kg, carry_scr[kg], a1)
                store(kg, jnp.where(row_id == 0, jnp.broadcast_to(comp, (rows, HEAD_DIM)), 0.0))


def _compress(pages, page_table, w1cat, pe2, w2, xnew, token_major):
    nb, n_pages = page_table.shape
    p_cnt = min(32, n_pages)
    assert n_pages % p_cnt == 0
    n_steps = n_pages // p_cnt
    has_new = xnew is not None
    rows = CHUNKS_PER_PAGE * p_cnt
    grid_steps = n_steps + (1 if has_new else 0)
    tot_rows = rows * grid_steps
    xw = CMP_STRIDE * HEAD_DIM

    def page_spec(p):
        def imap(b, i, pt):
            step = jnp.minimum(i, n_steps - 1)
            return (pt[b * n_pages + step * p_cnt + p], 0, 0, 0)
        return pl.BlockSpec((1, KV_ROW // LANES, LANES, PAGE_SIZE), imap)

    in_specs = [page_spec(p) for p in range(p_cnt)]
    in_specs += [pl.BlockSpec((2, xw, 2 * CMP_HID), lambda b, i, pt: (0, 0, 0)),
                 pl.BlockSpec((2, SUBLANES, xw), lambda b, i, pt: (0, 0, 0)),
                 pl.BlockSpec((2, CMP_HID, HEAD_DIM), lambda b, i, pt: (0, 0, 0))]
    args = [pages] * p_cnt + [w1cat, pe2, w2]
    if has_new:
        in_specs.append(pl.BlockSpec((1, N_KV_SLABS, xw), lambda b, i, pt: (b, 0, 0)))
        args.append(xnew)
    if token_major:
        out_spec = pl.BlockSpec((1, rows, KV_ROW), lambda b, i, pt: (b, i, 0))
        out_shape = jax.ShapeDtypeStruct((nb, tot_rows, KV_ROW), BF16)
    else:
        out_spec = pl.BlockSpec((1, N_KV_SLABS, rows, HEAD_DIM), lambda b, i, pt: (b, 0, i, 0))
        out_shape = jax.ShapeDtypeStruct((nb, N_KV_SLABS, tot_rows, HEAD_DIM), BF16)
    return pl.pallas_call(
        functools.partial(_compress_kernel, n_pages_step=p_cnt, n_steps=n_steps, has_new=has_new,
                          token_major=token_major),
        grid_spec=pltpu.PrefetchScalarGridSpec(
            num_scalar_prefetch=1, grid=(nb, grid_steps), in_specs=in_specs, out_specs=out_spec,
            scratch_shapes=[pltpu.VMEM((KV_ROW // LANES, rows * CHUNK_PITCH, LANES), F32),
                            pltpu.VMEM((N_KV_SLABS, rows, xw), F32),
                            pltpu.VMEM((N_KV_SLABS, 1, CMP_HID), F32),
                            pltpu.VMEM((2, 1, CMP_HID), F32)]),
        out_shape=out_shape,
        compiler_params=_params("arbitrary", "arbitrary"),
        name="compress_new" if has_new else "compress",
    )(page_table.reshape(-1), *args)


def _softmax_parts(s, valid, axis):
    s = jnp.where(valid, s, NEG)
    m = jnp.max(s, axis=axis, keepdims=True)
    e = jnp.where(valid, jnp.exp(s - m), 0.0)
    return e, jnp.sum(e, axis=axis, keepdims=True)


def _nsa_prompt_kernel(q_ref, kc_ref, vc_ref, ks_ref, vs_ref, kw_ref, vw_ref, g_ref, mt_ref, o_ref, *,
                       tq, kb, n_cmp, k_top):
    qi = pl.program_id(2)
    qs = qi * tq
    r4 = GQA_GROUP
    q3 = q_ref[0]
    q = q3.reshape(r4 * tq, HEAD_DIM)
    qpos = qs + lax.broadcasted_iota(jnp.int32, (1, tq, 1), 1)

    nc = kc_ref.shape[2]
    s = _dot_nt(q, kc_ref[0, 0]).reshape(r4, tq, nc)
    cid = lax.broadcasted_iota(jnp.int32, (1, 1, nc), 2)
    valid = (cid >= 1) & (cid <= n_cmp) & (cid * CMP_STRIDE + (CMP_BLOCK - CMP_STRIDE - 1) <= qpos)
    e, den = _softmax_parts(s, valid, 2)
    p = e * (1.0 / jnp.maximum(den, 1e-30))
    o_c = _dot(p.reshape(r4 * tq, nc).astype(BF16), vc_ref[0, 0]).reshape(r4, tq, HEAD_DIM)
    pg = p[0] + p[1] + p[2] + p[3]

    mt = mt_ref[...]
    imp = sum(_dot_nt(mt, piece) for piece in _split3(pg))
    n_blk = imp.shape[0]
    blk = lax.broadcasted_iota(jnp.int32, (n_blk, 1), 0)
    cur = (qs + lax.broadcasted_iota(jnp.int32, (1, tq), 1)) // SEL_BLOCK
    score = jnp.where(blk <= cur, imp, -1.0)
    score = jnp.where((blk == 0) | (blk == cur) | (blk == cur - 1), FORCE, score)
    sub = lax.broadcasted_iota(jnp.int32, (SUBLANES, 1), 0)
    tiles = [score[v * SUBLANES:(v + 1) * SUBLANES, :] for v in range(n_blk // SUBLANES)]
    ranks = [jnp.zeros((SUBLANES, tq), F32) for _ in tiles]
    for j in range(n_blk):
        sj = score[j:j + 1, :]
        for v, sc in enumerate(tiles):
            if v > j // SUBLANES:
                ahead = sj >= sc
            elif v < j // SUBLANES:
                ahead = sj > sc
            else:
                ahead = (sj > sc) | ((sj == sc) & (sub > j % SUBLANES))
            ranks[v] = ranks[v] + jnp.where(ahead, 1.0, 0.0)
    rank = jnp.concatenate(ranks, axis=0)
    sel_bias = jnp.where((rank < float(k_top)) & (blk <= cur), 0.0, NEG).T.astype(BF16)
    q_aug = jnp.concatenate([q3, jnp.broadcast_to(sel_bias[None], (r4, tq, n_blk))], axis=2)
    q_aug = q_aug.reshape(r4 * tq, HEAD_DIM + n_blk)

    def tile(k0, carry, diagonal):
        m_run, l_run, acc = carry
        sk = _dot_nt(q_aug, ks_ref[0, 0, pl.ds(k0, kb), :]).reshape(r4, tq, kb)
        if diagonal:
            kpos = k0 + lax.broadcasted_iota(jnp.int32, (1, 1, kb), 2)
            sk = jnp.where(kpos <= qpos, sk, NEG)
        m_new = jnp.maximum(m_run, jnp.max(sk, axis=2, keepdims=True))
        alpha = jnp.exp(m_run - m_new)
        ek = jnp.exp(sk - m_new)
        l_new = alpha * l_run + jnp.sum(ek, axis=2, keepdims=True)
        pv = _dot(ek.reshape(r4 * tq, kb).astype(BF16), vs_ref[0, 0, pl.ds(k0, kb), :])
        return m_new, l_new, alpha * acc + pv.reshape(r4, tq, HEAD_DIM)

    init = (jnp.full((r4, tq, 1), NEG, F32), jnp.zeros((r4, tq, 1), F32), jnp.zeros((r4, tq, HEAD_DIM), F32))
    n_full = qs // kb
    carry = lax.fori_loop(0, n_full, lambda kt, c: tile(pl.multiple_of(kt * kb, kb), c, False), init)
    _, l_s, acc_s = tile(pl.multiple_of(n_full * kb, kb), carry, True)
    o_s = acc_s * (1.0 / l_s)

    wl = kw_ref.shape[2] if kw_ref.shape[2] < WINDOW + tq else WINDOW + tq
    w0 = pl.multiple_of(jnp.maximum(qs + tq - wl, 0), tq)
    sw = _dot_nt(q, kw_ref[0, 0, pl.ds(w0, wl), :]).reshape(r4, tq, wl)
    dq = qpos - (w0 + lax.broadcasted_iota(jnp.int32, (1, 1, wl), 2))
    sw = sw + jnp.where((dq >= 0) & (dq < WINDOW), 0.0, NEG)
    ew = jnp.exp(sw - jnp.max(sw, axis=2, keepdims=True))
    denw = jnp.sum(ew, axis=2, keepdims=True)
    o_w = _dot(ew.reshape(r4 * tq, wl).astype(BF16), vw_ref[0, 0, pl.ds(w0, wl), :]).reshape(r4, tq, HEAD_DIM)
    o_w = o_w * (1.0 / denw)

    gt = g_ref[0]
    for r in range(r4):
        c0 = r * N_BRANCH
        o_ref[0, :, r * HEAD_DIM:(r + 1) * HEAD_DIM] = (
            gt[:, c0:c0 + 1] * o_c[r] + gt[:, c0 + 1:c0 + 2] * o_s[r] + gt[:, c0 + 2:c0 + 3] * o_w[r])


def _importance_matrix(n_blk_pad, n_col):
    mt = np.zeros((n_blk_pad, n_col), np.float32)
    ratio = SEL_BLOCK // CMP_STRIDE
    for b in range(n_blk_pad):
        for c, w in zip(range(ratio * b, ratio * b + ratio + 1), (1.0,) + (2.0,) * (ratio - 1) + (1.0,)):
            if c < n_col:
                mt[b, c] = w
    return mt


def _nsa_prompt(q_hm, comp_hm, ks_aug, vs_hm, kw_hm, gates, tq, kb):
    b_sz, _, s_len, _ = q_hm.shape
    nc = comp_hm.shape[2]
    assert s_len % SEL_BLOCK == 0 and s_len % kb == 0 and kb % tq == 0 and s_len % tq == 0 and CMP_SPAN == 2
    n_blk = s_len // SEL_BLOCK
    assert n_blk <= MAX_SEL_BLOCKS
    n_cmp = s_len // CMP_STRIDE - CMP_SPAN + 1
    mt = jnp.asarray(_importance_matrix(MAX_SEL_BLOCKS, nc), BF16)

    def hm_spec(rows, off, width=HEAD_DIM):
        return pl.BlockSpec((1, 1, rows, width), lambda b, g, i: (b, off + g, 0, 0))

    return pl.pallas_call(
        functools.partial(_nsa_prompt_kernel, tq=tq, kb=kb, n_cmp=n_cmp, k_top=min(N_SEL, n_blk)),
        grid=(b_sz, KV_HEADS, s_len // tq),
        in_specs=[pl.BlockSpec((1, GQA_GROUP, tq, HEAD_DIM), lambda b, g, i: (b, g, i, 0)),
                  hm_spec(nc, 0), hm_spec(nc, KV_HEADS),
                  hm_spec(s_len, 0, HEAD_DIM + MAX_SEL_BLOCKS), hm_spec(s_len, 0),
                  hm_spec(s_len, 0), hm_spec(s_len, KV_HEADS),
                  pl.BlockSpec((1, tq, LANES), lambda b, g, i: (b, i, g)),
                  pl.BlockSpec((MAX_SEL_BLOCKS, nc), lambda b, g, i: (0, 0))],
        out_specs=pl.BlockSpec((1, tq, GQA_GROUP * HEAD_DIM), lambda b, g, i: (b, i, g)),
        out_shape=jax.ShapeDtypeStruct((b_sz, s_len, NSA_WIDTH), F32),
        compiler_params=_params("arbitrary", "arbitrary", "arbitrary"),
        name="nsa_prompt",
    )(q_hm, comp_hm, comp_hm, ks_aug, vs_hm, kw_hm, kw_hm, gates, mt)


Q_COLS = LANES
COLS_PER_HEAD = Q_COLS // GQA_GROUP
V_OFF = LANES
V_COL0 = KV_HEADS * HEAD_DIM - V_OFF


def _nsa_sample_kernel(pt_ref, *refs, n_pages_step, n_steps, t_new, past, n_cmp, n_sel, k_top):
    del pt_ref
    p_cnt = n_pages_step
    page_refs = refs[:p_cnt]
    (qbd_ref, comp_ref, mt_ref, et_ref, knew_ref, win_ref, wnew_ref, g_ref,
     o_ref, kv_scr, sel_scr, m_scr, l_scr, acc_scr, oc_scr) = refs[p_cnt:]
    i = pl.program_id(1)
    qbd = qbd_ref[0]
    col = lax.broadcasted_iota(jnp.int32, (1, Q_COLS), 1)
    tok = col % SUBLANES
    qpos = past + tok
    kwin = slice(0, 2 * LANES)
    vwin = slice(V_OFF, V_OFF + 2 * LANES)

    def attend(rows_bf16, valid):
        s = _dot(rows_bf16[:, kwin], qbd)
        s = jnp.where(valid, s, NEG)
        return s

    @pl.when(i == 0)
    def _():
        comp = comp_ref[0]
        ncp = comp.shape[0]
        cid = lax.broadcasted_iota(jnp.int32, (ncp, 1), 0)
        valid = (cid >= 1) & (cid <= n_cmp) & (cid * CMP_STRIDE + (CMP_BLOCK - CMP_STRIDE - 1) <= qpos)
        s = attend(comp, valid)
        e, den = _softmax_parts(s, valid, 0)
        p = e * (1.0 / jnp.maximum(den, 1e-30))
        oc_scr[...] = _dot_tn(p.astype(BF16), comp[:, vwin])
        mt = mt_ref[...]
        imp = sum(_dot(mt, piece) for piece in _split3(p))
        imp = imp + pltpu.roll(imp, COLS_PER_HEAD, 1)
        imp = imp + pltpu.roll(imp, 2 * COLS_PER_HEAD, 1)
        nbp = imp.shape[0]
        blk = lax.broadcasted_iota(jnp.int32, (nbp, 1), 0)
        cur = qpos // SEL_BLOCK
        score = jnp.where(blk <= cur, imp, -1.0)
        score = jnp.where((blk == 0) | (blk == cur) | (blk == cur - 1), FORCE, score)
        score = jnp.where(blk < n_sel, score, PAD_SCORE)
        blk_f = blk.astype(F32)
        sel = jnp.zeros((nbp, Q_COLS), F32)
        for _ in range(k_top):
            best = jnp.max(score, axis=0, keepdims=True)
            first = jnp.min(jnp.where(score == best, blk_f, float(nbp)), axis=0, keepdims=True)
            hit = blk_f == first
            sel = jnp.where(hit, 1.0, sel)
            score = jnp.where(hit, TAKEN_SCORE, score)
        sel_scr[...] = sel
        m_scr[...] = jnp.full_like(m_scr, NEG)
        l_scr[...] = jnp.zeros_like(l_scr)
        acc_scr[...] = jnp.zeros_like(acc_scr)

    def accumulate(s, v_rows):
        m_run = m_scr[0:1, :]
        m_new = jnp.maximum(m_run, jnp.max(s, axis=0, keepdims=True))
        alpha = jnp.exp(m_run - m_new)
        e = jnp.where(s > 0.5 * NEG, jnp.exp(s - m_new), 0.0)
        l_scr[0:1, :] = alpha * l_scr[0:1, :] + jnp.sum(e, axis=0, keepdims=True)
        m_scr[0:1, :] = m_new
        alpha_col = jnp.broadcast_to(alpha, (SUBLANES, Q_COLS)).T[:, 0:1]
        acc_scr[...] = alpha_col * acc_scr[...] + _dot_tn(e.astype(BF16), v_rows)

    blocks_per_page = PAGE_SIZE // SEL_BLOCK
    nb_step = blocks_per_page * p_cnt
    for p in range(p_cnt):
        for cb in range(KV_ROW // LANES):
            kv_scr[p * PAGE_SIZE:(p + 1) * PAGE_SIZE, cb * LANES:(cb + 1) * LANES] = page_refs[p][0, cb].T.astype(BF16)
    kv = kv_scr[...]
    sel_step = sel_scr[pl.ds(pl.multiple_of(i * nb_step, nb_step), nb_step), :].astype(BF16)
    in_sel = _dot(et_ref[...], sel_step) > 0.5
    accumulate(attend(kv, in_sel), kv[:, vwin])

    @pl.when(i == n_steps - 1)
    def _():
        knew = knew_ref[0].astype(BF16)
        krow = lax.broadcasted_iota(jnp.int32, (t_new, 1), 0)
        new_blk = past // SEL_BLOCK
        ok = (sel_scr[new_blk:new_blk + 1, :] > 0.5) & (krow <= tok)
        accumulate(attend(knew, ok), knew[:, vwin])
        o_s = acc_scr[...] * jnp.broadcast_to(1.0 / l_scr[0:1, :], (SUBLANES, Q_COLS)).T[:, 0:1]

        wbuf = win_ref[0].astype(BF16)
        wnew = wnew_ref[0].astype(BF16)
        buf = wbuf.shape[0]
        pos_b = past - buf + lax.broadcasted_iota(jnp.int32, (buf, 1), 0)
        pos_n = past + krow
        ok_b = (qpos - pos_b >= 0) & (qpos - pos_b < WINDOW) & (pos_b >= 0)
        ok_n = (qpos - pos_n >= 0) & (qpos - pos_n < WINDOW)
        s_b = attend(wbuf, ok_b)
        s_n = attend(wnew, ok_n)
        m_w = jnp.maximum(jnp.max(s_b, axis=0, keepdims=True), jnp.max(s_n, axis=0, keepdims=True))
        e_b = jnp.where(ok_b, jnp.exp(s_b - m_w), 0.0)
        e_n = jnp.where(ok_n, jnp.exp(s_n - m_w), 0.0)
        den = jnp.sum(e_b, axis=0, keepdims=True) + jnp.sum(e_n, axis=0, keepdims=True)
        o_w = _dot_tn(e_b.astype(BF16), wbuf[:, vwin]) + _dot_tn(e_n.astype(BF16), wnew[:, vwin])
        o_w = o_w * jnp.broadcast_to(1.0 / jnp.maximum(den, 1e-30), (SUBLANES, Q_COLS)).T[:, 0:1]
        o_c = oc_scr[...]

        gt = g_ref[0]
        for g in range(KV_HEADS):
            for r in range(GQA_GROUP):
                r0 = r * COLS_PER_HEAD + g * SUBLANES
                c0 = V_COL0 + g * HEAD_DIM
                gc = (g * GQA_GROUP + r) * N_BRANCH
                blk_o = [o[r0:r0 + t_new, c0:c0 + HEAD_DIM] for o in (o_c, o_s, o_w)]
                h0 = (g * GQA_GROUP + r) * HEAD_DIM
                o_ref[0, :, h0:h0 + HEAD_DIM] = (gt[:, gc:gc + 1] * blk_o[0] + gt[:, gc + 1:gc + 2] * blk_o[1]
                                                 + gt[:, gc + 2:gc + 3] * blk_o[2])


def _nsa_sample(cache_slc, page_table, qbd, comp_tm, knew, win_buf, wnew, gates, past):
    db, n_pages = page_table.shape
    t_new = knew.shape[1]
    assert t_new == SUBLANES and KV_HEADS * t_new <= COLS_PER_HEAD and past % SEL_BLOCK == 0 and t_new <= SEL_BLOCK
    p_cnt = min(32, n_pages)
    assert n_pages % p_cnt == 0
    n_steps = n_pages // p_cnt
    ncp = comp_tm.shape[1]
    n_str = -(-(past + t_new) // CMP_STRIDE)
    n_cmp = n_str - CMP_SPAN + 1
    n_sel = -(-(past + t_new) // SEL_BLOCK)
    blocks_step = (PAGE_SIZE // SEL_BLOCK) * p_cnt
    nbp = -(-(n_sel + 1) // blocks_step) * blocks_step
    mt = jnp.asarray(_importance_matrix(nbp, ncp), BF16)
    keys_step = PAGE_SIZE * p_cnt
    et = jnp.asarray((np.arange(keys_step)[:, None] // SEL_BLOCK) == np.arange(blocks_step)[None, :], BF16)
    buf = win_buf.shape[1]

    def page_spec(p):
        return pl.BlockSpec((1, KV_ROW // LANES, LANES, PAGE_SIZE),
                            lambda b, i, pt: (pt[b * n_pages + i * p_cnt + p], 0, 0, 0))

    def per_b(*shape):
        return pl.BlockSpec((1,) + shape, lambda b, i, pt: (b,) + (0,) * len(shape))

    def const(*shape):
        return pl.BlockSpec(shape, lambda b, i, pt: (0,) * len(shape))

    in_specs = [page_spec(p) for p in range(p_cnt)]
    in_specs += [per_b(2 * LANES, Q_COLS), per_b(ncp, KV_ROW), const(nbp, ncp), const(keys_step, blocks_step),
                 per_b(t_new, KV_ROW), per_b(buf, KV_ROW), per_b(t_new, KV_ROW), per_b(t_new, LANES)]
    return pl.pallas_call(
        functools.partial(_nsa_sample_kernel, n_pages_step=p_cnt, n_steps=n_steps, t_new=t_new, past=past,
                          n_cmp=n_cmp, n_sel=n_sel, k_top=min(N_SEL, n_sel)),
        grid_spec=pltpu.PrefetchScalarGridSpec(
            num_scalar_prefetch=1, grid=(db, n_steps), in_specs=in_specs,
            out_specs=per_b(t_new, NSA_WIDTH),
            scratch_shapes=[pltpu.VMEM((keys_step, KV_ROW), BF16),
                            pltpu.VMEM((nbp, Q_COLS), F32),
                            pltpu.VMEM((SUBLANES, Q_COLS), F32), pltpu.VMEM((SUBLANES, Q_COLS), F32),
                            pltpu.VMEM((Q_COLS, 2 * LANES), F32), pltpu.VMEM((Q_COLS, 2 * LANES), F32)]),
        out_shape=jax.ShapeDtypeStruct((db, t_new, NSA_WIDTH), F32),
        compiler_params=_params("arbitrary", "arbitrary"),
        name="nsa_sample",
    )(page_table.reshape(-1), *([cache_slc] * p_cnt), qbd, comp_tm, mt, et, knew, win_buf, wnew, gates)


def _out_kernel(x_ref, po_ref, zp_ref, no_ref, zn_ref, gate_ref, gnp_ref, gnn_ref, w_ref, fn_ref, y_ref):
    zp = zp_ref[0]
    zn = zn_ref[0]
    mp = _rms(po_ref[0], gnp_ref[...]) * (zp * _sigmoid(zp))
    mn = _rms(no_ref[0], gnn_ref[...]) * (zn * _sigmoid(zn))
    m = jnp.concatenate([mp, mn], axis=-1).astype(BF16)
    xo = x_ref[0] + gate_ref[0] * _dot(m, w_ref[...])
    y_ref[0] = _rms(xo, fn_ref[...])


def _out(x, pool_o, zp, nsa_o, zn, gate, gn_pool, gn_nsa, w_out_bf16, final_norm, tm):
    nb, t, d = x.shape
    r = gate.shape[1]
    assert t % tm == 0 and r in (1, t)
    if r == 1:
        gate_spec = pl.BlockSpec((1, 1, d), lambda b, i: (b, 0, 0))
    else:
        gate_spec = pl.BlockSpec((1, tm, d), lambda b, i: (b, i, 0))

    def tok(width):
        return pl.BlockSpec((1, tm, width), lambda b, i: (b, i, 0))

    def const(*shape):
        return pl.BlockSpec(shape, lambda b, i: (0,) * len(shape))

    return pl.pallas_call(
        _out_kernel,
        grid=(nb, t // tm),
        in_specs=[tok(d), tok(POOL_WIDTH), tok(POOL_WIDTH), tok(NSA_WIDTH), tok(NSA_WIDTH), gate_spec,
                  const(1, POOL_WIDTH), const(1, NSA_WIDTH), const(d, d), const(1, d)],
        out_specs=tok(d),
        out_shape=jax.ShapeDtypeStruct((nb, t, d), F32),
        compiler_params=_params("arbitrary", "arbitrary"),
        name="out",
    )(x, pool_o, zp, nsa_o, zn, gate, gn_pool.reshape(1, -1), gn_nsa.reshape(1, -1), w_out_bf16,
      final_norm.reshape(1, d))


def _pages_t(pages):
    n = pages.shape[0]
    return pages.transpose(0, 2, 3, 4, 1).reshape(n, KV_ROW // LANES, LANES, PAGE_SIZE)


def _block_diag(pool_w):
    n, c, _ = pool_w.shape
    eye = jnp.eye(n, dtype=pool_w.dtype)
    return (eye[:, None, :, None] * pool_w[:, :, None, :]).reshape(n * c, n * c)


def _first_layer(phi_w1, phi_pe):
    w = phi_w1.reshape(CMP_SPAN, CMP_STRIDE, 2, HEAD_DIM, CMP_HID)
    w1cat = w.transpose(2, 1, 3, 0, 4).reshape(2, CMP_STRIDE * HEAD_DIM, CMP_SPAN * CMP_HID)
    pe = phi_pe.reshape(CMP_SPAN, CMP_STRIDE, 2, HEAD_DIM).transpose(2, 0, 1, 3).reshape(2, CMP_SPAN, -1)
    pe2 = jnp.concatenate([pe, jnp.zeros((2, SUBLANES - CMP_SPAN, pe.shape[-1]), pe.dtype)], axis=1)
    return w1cat.astype(BF16), pe2


def kernel(x_prompt, x_sample, c_prompt, c_sample, cache_cmp_kv, cache_slc_kv, state_win_kv, state_pool, page_table, norm_w, w_ada, b_ada, w_in, pool_w, pool_scale, phi_w1, phi_pe, phi_w2, gn_pool, gn_nsa, w_out, final_norm):
    b_sz, s_len, d = x_prompt.shape
    db, t_new, _ = x_sample.shape
    depth = norm_w.shape[0]
    assert depth == 1 and w_in.shape[-1] == IN_WIDTH and d // 4 == POOL_WIDTH
    n_phys = cache_cmp_kv.shape[1]
    n_pages = page_table.shape[1]
    past = n_pages * PAGE_SIZE
    lyr = 0

    n_c = b_sz + db
    c_all = jnp.concatenate([c_prompt, c_sample, jnp.zeros((-n_c % SUBLANES, d), F32)], axis=0)
    ada = _ada(c_all, w_ada[lyr], b_ada[lyr])
    shift, scale, gate = ada[:, :d], ada[:, d:2 * d], ada[:, 2 * d:]

    w_pad = jnp.pad(w_in[lyr].astype(BF16), ((0, 0), (0, IN_PAD - IN_WIDTH)))
    w_out_bf16 = w_out[lyr].astype(BF16)
    wbd = _block_diag(pool_w[lyr]).astype(BF16)
    w1cat, pe2 = _first_layer(phi_w1[lyr], phi_pe[lyr])
    w2 = phi_w2[lyr].astype(BF16)
    n_tok_s = db * t_new

    def per_token(v):
        return jnp.repeat(v[b_sz:n_c], t_new, axis=0)[None]

    tm = min(512, s_len)
    (u_p, zp_p, zn_p, q_p, kvc_p, kvs_p, kvw_p, ks_aug, vs_hm, kw_hm, g_p) = _in_proj(
        x_prompt, scale[:b_sz, None], shift[:b_sz, None], norm_w[lyr], w_pad, tm)
    pool_p = _pool(jnp.zeros((b_sz, POOL_HALO, POOL_WIDTH), F32), u_p, wbd, pool_scale[lyr], 0, tm)
    pages_p = s_len // PAGE_SIZE
    ident = jnp.arange(b_sz * pages_p, dtype=jnp.int32).reshape(b_sz, pages_p)
    comp_p = _compress(_pages_t(kvc_p.reshape(b_sz * pages_p, PAGE_SIZE, 2, KV_HEADS, HEAD_DIM)), ident, w1cat, pe2, w2,
                       None, False)
    nsa_p = _nsa_prompt(q_p, comp_p, ks_aug, vs_hm, kw_hm, g_p, tq=128, kb=min(512, s_len))
    y_p = _out(x_prompt, pool_p, zp_p, nsa_p, zn_p, gate[:b_sz, None], gn_pool[lyr], gn_nsa[lyr], w_out_bf16,
               final_norm, tm)

    (u_s, zp_s, zn_s, q_s, kvc_s, kvs_s, kvw_s, _, _, _, g_s) = _in_proj(
        x_sample.reshape(1, n_tok_s, d), per_token(scale), per_token(shift), norm_w[lyr], w_pad, n_tok_s)
    u_s3 = u_s.reshape(db, t_new, POOL_WIDTH)
    hist = jnp.concatenate([jnp.zeros((db, POOL_HALO - state_pool.shape[2], POOL_WIDTH), F32), state_pool[lyr]], axis=1)
    pool_s = _pool(hist, u_s3, wbd, pool_scale[lyr], past, t_new)
    kvc_s3 = kvc_s.reshape(db, t_new, KV_ROW)
    kvs_s3 = kvs_s.reshape(db, t_new, KV_ROW)
    kvw_s3 = kvw_s.reshape(db, t_new, KV_ROW)
    xnew = jnp.pad(kvc_s3, ((0, 0), (0, CMP_STRIDE - t_new), (0, 0)))
    xnew = xnew.reshape(db, CMP_STRIDE, N_KV_SLABS, HEAD_DIM).transpose(0, 2, 1, 3).reshape(db, N_KV_SLABS, -1)
    comp_s = _compress(_pages_t(cache_cmp_kv[lyr]), page_table, w1cat, pe2, w2, xnew, True)
    q5 = q_s[0].reshape(KV_HEADS, GQA_GROUP, db, t_new, HEAD_DIM)
    eye = jnp.eye(KV_HEADS, dtype=BF16)
    qbd = q5.transpose(2, 0, 4, 1, 3)[:, :, :, :, None, :] * eye[None, :, None, None, :, None]
    qbd = jnp.pad(qbd.reshape(db, KV_HEADS * HEAD_DIM, GQA_GROUP, KV_HEADS * t_new),
                  ((0, 0), (0, 2 * LANES - KV_HEADS * HEAD_DIM), (0, 0), (0, COLS_PER_HEAD - KV_HEADS * t_new)))
    qbd = qbd.reshape(db, 2 * LANES, Q_COLS)
    g_s3 = g_s[0, :, :LANES].reshape(db, t_new, LANES)
    win_buf = state_win_kv[lyr].reshape(db, -1, KV_ROW)
    nsa_s = _nsa_sample(_pages_t(cache_slc_kv[lyr]), page_table, qbd, comp_s, kvs_s3,
                        win_buf, kvw_s3, g_s3, past)
    y_s = _out(x_sample.reshape(1, n_tok_s, d), pool_s.reshape(1, n_tok_s, POOL_WIDTH), zp_s,
               nsa_s.reshape(1, n_tok_s, NSA_WIDTH), zn_s, per_token(gate), gn_pool[lyr], gn_nsa[lyr], w_out_bf16,
               final_norm, n_tok_s).reshape(db, t_new, d)

    kv_shape = (2, KV_HEADS, HEAD_DIM)
    win_len = min(WINDOW, s_len)
    buf = win_buf.shape[1]
    new_win_s = jnp.concatenate([win_buf, kvw_s3], axis=1)[:, -buf:]
    hist_len = state_pool.shape[2]
    new_pool_s = jnp.concatenate([state_pool[lyr], u_s3], axis=1)[:, -hist_len:]
    return (y_p, y_s,
            kvc_p.reshape((1, b_sz, s_len) + kv_shape), kvc_s3.reshape((1, db, t_new) + kv_shape),
            kvs_p.reshape((1, b_sz, s_len) + kv_shape), kvs_s3.reshape((1, db, t_new) + kv_shape),
            kvw_p[:, s_len - win_len:].reshape((1, b_sz, win_len) + kv_shape),
            new_win_s.reshape((1, db, buf) + kv_shape),
            u_p[:, s_len - hist_len:][None], new_pool_s[None])
```

```python
import functools

import numpy as np
import jax
import jax.numpy as jnp
from jax import lax
from jax.experimental import pallas as pl
from jax.experimental.pallas import tpu as pltpu

F32 = jnp.float32
BF16 = jnp.bfloat16

HEAD_DIM = 64
GQA_GROUP = 4
KV_HEADS = 3
N_HEADS = KV_HEADS * GQA_GROUP
N_KV_SLABS = 2 * KV_HEADS
KV_ROW = N_KV_SLABS * HEAD_DIM
N_BRANCH = 3
POOL_WINDOWS = (2, 4, 8, 16)
POOL_HALO = 16
CMP_BLOCK = 32
CMP_STRIDE = 16
CMP_SPAN = CMP_BLOCK // CMP_STRIDE
CMP_HID = 2 * HEAD_DIM
SEL_BLOCK = 64
N_SEL = 16
WINDOW = 512
PAGE_SIZE = 128
CHUNKS_PER_PAGE = PAGE_SIZE // CMP_STRIDE
CHUNK_PITCH = 24
EPS = 1e-6
NEG = -1e30
FORCE = 1e4
PAD_SCORE = -1e38
TAKEN_SCORE = -3e38

LANES = 128
SUBLANES = 8
VMEM_LIMIT_BYTES = 56 * 1024 * 1024


def _dot(a, b):
    return jnp.dot(a, b, preferred_element_type=F32)


def _dot_nt(a, b):
    return lax.dot_general(a, b, (((1,), (1,)), ((), ())), preferred_element_type=F32)


def _dot_tn(a, b):
    return lax.dot_general(a, b, (((0,), (0,)), ((), ())), preferred_element_type=F32)


def _sigmoid(v):
    return 1.0 / (1.0 + jnp.exp(-v))


def _rms(v, g):
    return v * lax.rsqrt(jnp.mean(v * v, axis=-1, keepdims=True) + EPS) * g


def _split3(v):
    hi = v.astype(BF16)
    r1 = v - hi.astype(F32)
    mid = r1.astype(BF16)
    lo = (r1 - mid.astype(F32)).astype(BF16)
    return hi, mid, lo


def _params(*sem):
    return pltpu.CompilerParams(dimension_semantics=sem, vmem_limit_bytes=VMEM_LIMIT_BYTES)


def _ada_kernel(c_ref, w_ref, b_ref, o_ref):
    c = c_ref[...]
    a = (c * _sigmoid(c)).astype(BF16)
    o_ref[...] = _dot(a, w_ref[...].astype(BF16)) + b_ref[...]


def _ada(c, w_ada, b_ada):
    m, d = c.shape
    n = w_ada.shape[1]
    tn = 512
    return pl.pallas_call(
        _ada_kernel,
        grid=(n // tn,),
        in_specs=[pl.BlockSpec((m, d), lambda j: (0, 0)),
                  pl.BlockSpec((d, tn), lambda j: (0, j)),
                  pl.BlockSpec((1, tn), lambda j: (0, j))],
        out_specs=pl.BlockSpec((m, tn), lambda j: (0, j)),
        out_shape=jax.ShapeDtypeStruct((m, n), F32),
        compiler_params=_params("arbitrary"),
        name="ada",
    )(c, w_ada, b_ada.reshape(1, n))


POOL_WIDTH = 256
NSA_WIDTH = N_HEADS * HEAD_DIM
GATE_COLS = N_BRANCH * N_HEADS
MAX_SEL_BLOCKS = LANES - HEAD_DIM
_SEG = {}
_off = 0
for _name, _w in (("u", POOL_WIDTH), ("zp", POOL_WIDTH), ("q", NSA_WIDTH), ("zn", NSA_WIDTH),
                  ("kvc", KV_ROW), ("kvs", KV_ROW), ("kvw", KV_ROW), ("gl", LANES)):
    _SEG[_name] = (_off, _w)
    _off += _w
IN_PAD = _off
IN_WIDTH = IN_PAD - LANES + GATE_COLS


GATE_ROWS = 16


def _in_proj_kernel(x_ref, sc_ref, sh_ref, nw_ref, w_ref, u_ref, zp_ref, zn_ref, kvc_ref, kvs_ref, kvw_ref, *rest,
                    attn_layouts):
    tm = x_ref.shape[1]
    h = (_rms(x_ref[0], nw_ref[...]) * (1.0 + sc_ref[0]) + sh_ref[0]).astype(BF16)

    def seg(name):
        s, w = _SEG[name]
        return _dot(h, w_ref[:, s:s + w])

    u_ref[0] = seg("u")
    zp_ref[0] = seg("zp")
    zn_ref[0] = seg("zn")
    q = seg("q") * (HEAD_DIM ** -0.5)
    kvc_ref[0] = seg("kvc")
    kvs = seg("kvs")
    kvs_ref[0] = kvs
    kvw = seg("kvw")
    kvw_ref[0] = kvw
    sig = _sigmoid(seg("gl"))
    if not attn_layouts:
        q_ref, g_ref = rest
        q_ref[0] = q.astype(BF16)
        g_ref[0] = sig
        return

    qt_ref, ks_aug_ref, kw_ref, vst_ref, vwt_ref, gt_ref = rest
    qt_ref[0] = q.T.astype(BF16).reshape(N_HEADS, HEAD_DIM, tm)
    pos = pl.program_id(1) * tm + lax.broadcasted_iota(jnp.int32, (tm, 1), 0)
    onehot = (lax.broadcasted_iota(jnp.int32, (1, MAX_SEL_BLOCKS), 1) == pos // SEL_BLOCK).astype(BF16)
    kvs_t = kvs.T.astype(BF16)
    kvw_t = kvw.T.astype(BF16)
    v0 = KV_HEADS * HEAD_DIM
    for g in range(KV_HEADS):
        ks_aug_ref[0, g] = jnp.concatenate([kvs[:, g * HEAD_DIM:(g + 1) * HEAD_DIM].astype(BF16), onehot], axis=1)
        kw_ref[0, g] = kvw[:, g * HEAD_DIM:(g + 1) * HEAD_DIM].astype(BF16)
        for c in range(tm // LANES):
            vst_ref[0, g, c] = kvs_t[v0 + g * HEAD_DIM:v0 + (g + 1) * HEAD_DIM, c * LANES:(c + 1) * LANES]
            vwt_ref[0, g, c] = kvw_t[v0 + g * HEAD_DIM:v0 + (g + 1) * HEAD_DIM, c * LANES:(c + 1) * LANES]
        per_group = GQA_GROUP * N_BRANCH
        rolled = sig if g == 0 else pltpu.roll(sig, LANES - per_group * g, 1)
        gt_ref[0, g] = rolled.T[:GATE_ROWS, :]


def _in_proj(x, scale, shift, norm_w, w_pad, tm, attn_layouts):
    nb, t, d = x.shape
    r = scale.shape[1]
    assert t % tm == 0 and r in (1, t) and (tm % LANES == 0 or not attn_layouts)
    if r == 1:
        mod_spec = pl.BlockSpec((1, 1, d), lambda b, i: (b, 0, 0))
    else:
        mod_spec = pl.BlockSpec((1, tm, d), lambda b, i: (b, i, 0))

    def tok(width):
        return pl.BlockSpec((1, tm, width), lambda b, i: (b, i, 0))

    def hm(n, width=HEAD_DIM):
        return pl.BlockSpec((1, n, tm, width), lambda b, i: (b, 0, i, 0))

    def sd(*shape, dtype=F32):
        return jax.ShapeDtypeStruct(shape, dtype)

    out_specs = [tok(POOL_WIDTH), tok(POOL_WIDTH), tok(NSA_WIDTH), tok(KV_ROW), tok(KV_ROW), tok(KV_ROW)]
    out_shape = [sd(nb, t, POOL_WIDTH), sd(nb, t, POOL_WIDTH), sd(nb, t, NSA_WIDTH),
                 sd(nb, t, KV_ROW), sd(nb, t, KV_ROW), sd(nb, t, KV_ROW)]
    if attn_layouts:
        lane_tiles = tm // LANES
        vt_spec = pl.BlockSpec((1, KV_HEADS, lane_tiles, HEAD_DIM, LANES), lambda b, i: (b, 0, i, 0, 0))
        vt_shape = sd(nb, KV_HEADS, t // LANES, HEAD_DIM, LANES, dtype=BF16)
        out_specs += [pl.BlockSpec((1, N_HEADS, HEAD_DIM, tm), lambda b, i: (b, 0, 0, i)),
                      hm(KV_HEADS, HEAD_DIM + MAX_SEL_BLOCKS), hm(KV_HEADS), vt_spec, vt_spec,
                      pl.BlockSpec((1, KV_HEADS, GATE_ROWS, tm), lambda b, i: (b, 0, 0, i))]
        out_shape += [sd(nb, N_HEADS, HEAD_DIM, t, dtype=BF16),
                      sd(nb, KV_HEADS, t, HEAD_DIM + MAX_SEL_BLOCKS, dtype=BF16),
                      sd(nb, KV_HEADS, t, HEAD_DIM, dtype=BF16), vt_shape, vt_shape,
                      sd(nb, KV_HEADS, GATE_ROWS, t)]
    else:
        out_specs += [tok(NSA_WIDTH), tok(LANES)]
        out_shape += [sd(nb, t, NSA_WIDTH, dtype=BF16), sd(nb, t, LANES)]
    return pl.pallas_call(
        functools.partial(_in_proj_kernel, attn_layouts=attn_layouts),
        grid=(nb, t // tm),
        in_specs=[tok(d), mod_spec, mod_spec,
                  pl.BlockSpec((1, d), lambda b, i: (0, 0)),
                  pl.BlockSpec((d, IN_PAD), lambda b, i: (0, 0))],
        out_specs=out_specs,
        out_shape=out_shape,
        compiler_params=_params("arbitrary", "arbitrary"),
        name="in_proj",
    )(x, scale, shift, norm_w.reshape(1, d), w_pad)


def _pool_kernel(hist_ref, uprev_ref, u_ref, wbd_ref, ps_ref, o_ref, *, pos0, tp):
    i = pl.program_id(1)
    u = u_ref[0]
    halo = jnp.where(i == 0, hist_ref[0], uprev_ref[0])
    ue = jnp.concatenate([halo, u], axis=0)
    sums = []
    s = ue
    for w in POOL_WINDOWS:
        s = s + pltpu.roll(s, w // 2, 0)
        sums.append(s[POOL_HALO:])
    c = u.shape[-1]
    grp = lax.broadcasted_iota(jnp.int32, (1, c), 1) // (c // len(POOL_WINDOWS))
    pos = pos0 + i * tp + lax.broadcasted_iota(jnp.int32, (tp, 1), 0)
    tot = sums[-1]
    win = jnp.full((1, c), float(POOL_WINDOWS[-1]), F32)
    for gi in range(len(POOL_WINDOWS) - 2, -1, -1):
        tot = jnp.where(grp == gi, sums[gi], tot)
        win = jnp.where(grp == gi, float(POOL_WINDOWS[gi]), win)
    cnt = jnp.minimum(win, (pos + 1).astype(F32))
    dlt = tot / cnt - u
    o_ref[0] = _dot(dlt.astype(BF16), wbd_ref[...]) * ps_ref[...]


def _pool(hist, u, wbd, pool_scale, pos0, tp):
    nb, t, c = u.shape
    assert t % tp == 0
    if t >= POOL_HALO:
        assert tp % POOL_HALO == 0
        uprev, ratio = u, tp // POOL_HALO
        prev_spec = pl.BlockSpec((1, POOL_HALO, c), lambda b, i: (b, jnp.maximum(i * ratio - 1, 0), 0))
    else:
        assert t == tp
        uprev = hist
        prev_spec = pl.BlockSpec((1, POOL_HALO, c), lambda b, i: (b, 0, 0))
    return pl.pallas_call(
        functools.partial(_pool_kernel, pos0=pos0, tp=tp),
        grid=(nb, t // tp),
        in_specs=[pl.BlockSpec((1, POOL_HALO, c), lambda b, i: (b, 0, 0)), prev_spec,
                  pl.BlockSpec((1, tp, c), lambda b, i: (b, i, 0)),
                  pl.BlockSpec((c, c), lambda b, i: (0, 0)),
                  pl.BlockSpec((1, c), lambda b, i: (0, 0))],
        out_specs=pl.BlockSpec((1, tp, c), lambda b, i: (b, i, 0)),
        out_shape=jax.ShapeDtypeStruct((nb, t, c), F32),
        compiler_params=_params("arbitrary", "arbitrary"),
        name="pool",
    )(hist, uprev, u, wbd, pool_scale.reshape(1, c))


def _gelu_tanh(v):
    return 0.5 * v * (1.0 + jnp.tanh(np.sqrt(2.0 / np.pi).astype(np.float32) * (v + 0.044715 * (v * v * v))))


def _compress_kernel(pt_ref, *refs, n_pages_step, n_steps, has_new, token_major):
    del pt_ref
    p_cnt = n_pages_step
    page_refs = refs[:p_cnt]
    w1p_ref, w1_ref, pe_ref, w2_ref = refs[p_cnt:p_cnt + 4]
    k = p_cnt + 4
    xnew_ref = None
    if has_new:
        xnew_ref = refs[k]
        k += 1
    if token_major:
        out_ref = refs[k]
        k += 1
    else:
        outk_ref, outvt_ref = refs[k:k + 2]
        k += 2
    slab_scr, carry_scr, bias_scr = refs[k:k + 3]
    i = pl.program_id(1)
    rows = CHUNKS_PER_PAGE * p_cnt
    row_id = lax.broadcasted_iota(jnp.int32, (rows, 1), 0)

    @pl.when(i == 0)
    def _():
        carry_scr[...] = jnp.zeros_like(carry_scr)
        for kk in range(2):
            pb = _dot(pe_ref[kk].astype(BF16), w1_ref[kk])
            bias_scr[kk] = pb[0:1, :CMP_HID] + pb[1:2, CMP_HID:]

    def store(kg, comp):
        if token_major:
            out_ref[0, :, kg * HEAD_DIM:(kg + 1) * HEAD_DIM] = comp.astype(BF16)
        elif kg < KV_HEADS:
            outk_ref[0, kg] = comp.astype(BF16)
        else:
            wide = jnp.concatenate([comp, jnp.zeros((rows, LANES - HEAD_DIM), F32)], axis=1)
            outvt_ref[0, kg - KV_HEADS] = wide.T[:HEAD_DIM, :].astype(BF16)

    def finish(kg, a0_prev, a1):
        kk = kg // KV_HEADS
        pre = a0_prev + a1 + bias_scr[kk]
        return _dot(_gelu_tanh(pre).astype(BF16), w2_ref[kk])

    def main():
        for p in range(p_cnt):
            for cb in range(KV_ROW // LANES):
                rows_t = page_refs[p][0, cb].astype(BF16).T.astype(F32)
                for ch in range(CHUNKS_PER_PAGE):
                    r0 = (p * CHUNKS_PER_PAGE + ch) * CHUNK_PITCH
                    slab_scr[cb, r0:r0 + CMP_STRIDE, :] = rows_t[ch * CMP_STRIDE:(ch + 1) * CMP_STRIDE, :]
        for cb in range(KV_ROW // LANES):
            x = jnp.concatenate([slab_scr[cb, pl.ds(j, rows, stride=CHUNK_PITCH), :] for j in range(CMP_STRIDE)],
                                axis=1).astype(BF16)
            a_both = _dot(x, w1p_ref[cb])
            for half in range(2):
                kg = 2 * cb + half
                a = a_both[:, half * 2 * CMP_HID:(half + 1) * 2 * CMP_HID]
                a0 = a[:, :CMP_HID]
                a0_prev = jnp.where(row_id == 0, carry_scr[kg], pltpu.roll(a0, 1, 0))
                carry_scr[kg] = a0[rows - 1:rows, :]
                store(kg, finish(kg, a0_prev, a[:, CMP_HID:]))

    if not has_new:
        main()
    else:
        pl.when(i < n_steps)(main)

        @pl.when(i == n_steps)
        def _():
            for kg in range(N_KV_SLABS):
                xk = jnp.broadcast_to(xnew_ref[0, kg:kg + 1, :], (SUBLANES, CMP_STRIDE * HEAD_DIM)).astype(BF16)
                a1 = _dot(xk, w1_ref[kg // KV_HEADS])[0:1, CMP_HID:]
                comp = finish(kg, carry_scr[kg], a1)
                store(kg, jnp.where(row_id == 0, jnp.broadcast_to(comp, (rows, HEAD_DIM)), 0.0))


def _compress(pages, page_table, w1cat, pe2, w2, xnew, token_major):
    nb, n_pages = page_table.shape
    p_cnt = min(32, n_pages)
    assert n_pages % p_cnt == 0
    n_steps = n_pages // p_cnt
    has_new = xnew is not None
    rows = CHUNKS_PER_PAGE * p_cnt
    grid_steps = n_steps + (1 if has_new else 0)
    tot_rows = rows * grid_steps
    xw = CMP_STRIDE * HEAD_DIM

    def page_spec(p):
        def imap(b, i, pt):
            step = jnp.minimum(i, n_steps - 1)
            return (pt[b * n_pages + step * p_cnt + p], 0, 0, 0)
        return pl.BlockSpec((1, KV_ROW // LANES, LANES, PAGE_SIZE), imap)

    n_cb = KV_ROW // LANES
    w4 = w1cat.reshape(2, CMP_STRIDE, HEAD_DIM, 2 * CMP_HID)
    zero = jnp.zeros_like(w4[0])
    w1p = jnp.stack([jnp.concatenate([jnp.concatenate([w4[(2 * cb) // KV_HEADS], zero], axis=2),
                                      jnp.concatenate([zero, w4[(2 * cb + 1) // KV_HEADS]], axis=2)], axis=1)
                     for cb in range(n_cb)]).reshape(n_cb, CMP_STRIDE * LANES, 4 * CMP_HID)
    in_specs = [page_spec(p) for p in range(p_cnt)]
    in_specs += [pl.BlockSpec((n_cb, CMP_STRIDE * LANES, 4 * CMP_HID), lambda b, i, pt: (0, 0, 0)),
                 pl.BlockSpec((2, xw, 2 * CMP_HID), lambda b, i, pt: (0, 0, 0)),
                 pl.BlockSpec((2, SUBLANES, xw), lambda b, i, pt: (0, 0, 0)),
                 pl.BlockSpec((2, CMP_HID, HEAD_DIM), lambda b, i, pt: (0, 0, 0))]
    args = [pages] * p_cnt + [w1p, w1cat, pe2, w2]
    if has_new:
        in_specs.append(pl.BlockSpec((1, N_KV_SLABS, xw), lambda b, i, pt: (b, 0, 0)))
        args.append(xnew)
    if token_major:
        out_spec = pl.BlockSpec((1, rows, KV_ROW), lambda b, i, pt: (b, i, 0))
        out_shape = jax.ShapeDtypeStruct((nb, tot_rows, KV_ROW), BF16)
    else:
        assert rows % LANES == 0
        out_spec = [pl.BlockSpec((1, KV_HEADS, rows, HEAD_DIM), lambda b, i, pt: (b, 0, i, 0)),
                    pl.BlockSpec((1, KV_HEADS, HEAD_DIM, rows), lambda b, i, pt: (b, 0, 0, i))]
        out_shape = [jax.ShapeDtypeStruct((nb, KV_HEADS, tot_rows, HEAD_DIM), BF16),
                     jax.ShapeDtypeStruct((nb, KV_HEADS, HEAD_DIM, tot_rows), BF16)]
    return pl.pallas_call(
        functools.partial(_compress_kernel, n_pages_step=p_cnt, n_steps=n_steps, has_new=has_new,
                          token_major=token_major),
        grid_spec=pltpu.PrefetchScalarGridSpec(
            num_scalar_prefetch=1, grid=(nb, grid_steps), in_specs=in_specs, out_specs=out_spec,
            scratch_shapes=[pltpu.VMEM((n_cb, rows * CHUNK_PITCH, LANES), F32),
                            pltpu.VMEM((N_KV_SLABS, 1, CMP_HID), F32),
                            pltpu.VMEM((2, 1, CMP_HID), F32)]),
        out_shape=out_shape,
        compiler_params=_params("arbitrary", "arbitrary"),
        name="compress_new" if has_new else "compress",
    )(page_table.reshape(-1), *args)


def _softmax_parts(s, valid, axis):
    s = jnp.where(valid, s, NEG)
    m = jnp.max(s, axis=axis, keepdims=True)
    e = jnp.where(valid, jnp.exp(s - m), 0.0)
    return e, jnp.sum(e, axis=axis, keepdims=True)


def _nsa_prompt_kernel(qt_ref, kc_ref, vct_ref, ks_ref, vst_ref, kw_ref, vwt_ref, gt_ref, mt_ref, o_ref, *,
                       tq, kb, n_cmp, k_top):
    qi = pl.program_id(2)
    qs = qi * tq
    r4 = GQA_GROUP
    cols = r4 * tq
    qt = jnp.concatenate([qt_ref[0, r] for r in range(r4)], axis=1)
    qpos = qs + (lax.broadcasted_iota(jnp.int32, (1, cols), 1) & (tq - 1))

    def value_tiles(ref, k0, width):
        j0 = k0 // LANES
        return jnp.concatenate([ref[0, 0, j0 + j] for j in range(width // LANES)], axis=1)

    nc = kc_ref.shape[2]
    s = _dot(kc_ref[0, 0], qt)
    cid = lax.broadcasted_iota(jnp.int32, (nc, 1), 0)
    valid = (cid >= 1) & (cid <= n_cmp) & (cid * CMP_STRIDE + (CMP_BLOCK - CMP_STRIDE - 1) <= qpos)
    e, den = _softmax_parts(s, valid, 0)
    p = e * (1.0 / jnp.maximum(den, 1e-30))
    o_c = _dot(vct_ref[0, 0], p.astype(BF16))
    pg = p[:, 0:tq]
    for r in range(1, r4):
        pg = pg + p[:, r * tq:(r + 1) * tq]

    mt = mt_ref[...]
    imp = sum(_dot(mt, piece) for piece in _split3(pg))
    n_blk = imp.shape[0]
    blk = lax.broadcasted_iota(jnp.int32, (n_blk, 1), 0)
    cur = (qs + lax.broadcasted_iota(jnp.int32, (1, tq), 1)) // SEL_BLOCK
    score = jnp.where(blk <= cur, imp, -1.0)
    score = jnp.where((blk == 0) | (blk == cur) | (blk == cur - 1), FORCE, score)
    sub = lax.broadcasted_iota(jnp.int32, (SUBLANES, 1), 0)
    tiles = [score[v * SUBLANES:(v + 1) * SUBLANES, :] for v in range(n_blk // SUBLANES)]
    ranks = [jnp.zeros((SUBLANES, tq), F32) for _ in tiles]
    for j in range(n_blk):
        sj = score[j:j + 1, :]
        for v, sc in enumerate(tiles):
            if v > j // SUBLANES:
                ahead = sj >= sc
            elif v < j // SUBLANES:
                ahead = sj > sc
            else:
                ahead = (sj > sc) | ((sj == sc) & (sub > j % SUBLANES))
            ranks[v] = ranks[v] + jnp.where(ahead, 1.0, 0.0)
    rank = jnp.concatenate(ranks, axis=0)
    sel_bias = jnp.where((rank < float(k_top)) & (blk <= cur), 0.0, NEG).astype(BF16)
    q_aug = jnp.concatenate([qt, jnp.concatenate([sel_bias] * r4, axis=1)], axis=0)

    def scores(k0):
        return _dot(ks_ref[0, 0, pl.ds(k0, kb), :], q_aug)

    def absorb(sk, k0, state, diagonal):
        m_run, l_run, acc = state
        if diagonal:
            kpos = k0 + lax.broadcasted_iota(jnp.int32, (kb, 1), 0)
            sk = jnp.where(kpos <= qpos, sk, NEG)
        m_new = jnp.maximum(m_run, jnp.max(sk, axis=0, keepdims=True))
        alpha = jnp.exp(m_run - m_new)
        ek = jnp.exp(sk - m_new)
        l_new = alpha * l_run + jnp.sum(ek, axis=0, keepdims=True)
        pv = _dot(value_tiles(vst_ref, k0, kb), ek.astype(BF16))
        return m_new, l_new, alpha * acc + pv

    def tile(k0, state, diagonal):
        return absorb(scores(k0), k0, state, diagonal)

    init = (jnp.full((1, cols), NEG, F32), jnp.zeros((1, cols), F32), jnp.zeros((HEAD_DIM, cols), F32))
    n_full = qs // kb
    state = lax.fori_loop(0, n_full, lambda kt, st: tile(pl.multiple_of(kt * kb, kb), st, False), init)
    _, l_s, acc_s = tile(pl.multiple_of(n_full * kb, kb), state, True)
    o_s = acc_s * (1.0 / l_s)

    wl = kw_ref.shape[2] if kw_ref.shape[2] < WINDOW + tq else WINDOW + tq
    w0 = pl.multiple_of(jnp.maximum(qs + tq - wl, 0), tq)
    sw = _dot(kw_ref[0, 0, pl.ds(w0, wl), :], qt)
    dq = qpos - (w0 + lax.broadcasted_iota(jnp.int32, (wl, 1), 0))
    sw = sw + jnp.where((dq >= 0) & (dq < WINDOW), 0.0, NEG)
    ew = jnp.exp(sw - jnp.max(sw, axis=0, keepdims=True))
    denw = jnp.sum(ew, axis=0, keepdims=True)
    o_w = _dot(value_tiles(vwt_ref, w0, wl), ew.astype(BF16)) * (1.0 / denw)

    gt = gt_ref[0, 0]
    heads = []
    for r in range(r4):
        c0 = r * N_BRANCH
        sl = slice(r * tq, (r + 1) * tq)
        heads.append(gt[c0:c0 + 1] * o_c[:, sl] + gt[c0 + 1:c0 + 2] * o_s[:, sl] + gt[c0 + 2:c0 + 3] * o_w[:, sl])
    o_ref[0] = jnp.concatenate(heads, axis=0).T


def _importance_matrix(n_blk_pad, n_col):
    mt = np.zeros((n_blk_pad, n_col), np.float32)
    ratio = SEL_BLOCK // CMP_STRIDE
    for b in range(n_blk_pad):
        for c, w in zip(range(ratio * b, ratio * b + ratio + 1), (1.0,) + (2.0,) * (ratio - 1) + (1.0,)):
            if c < n_col:
                mt[b, c] = w
    return mt


def _nsa_prompt(q_t, comp_k, comp_vt, ks_aug, vs_t, kw, vw_t, gates_t, tq, kb):
    b_sz, _, _, s_len = q_t.shape
    nc = comp_k.shape[2]
    assert s_len % SEL_BLOCK == 0 and s_len % kb == 0 and kb % tq == 0 and s_len % tq == 0 and CMP_SPAN == 2
    assert tq % LANES == 0 and tq & (tq - 1) == 0
    n_blk = s_len // SEL_BLOCK
    assert n_blk <= MAX_SEL_BLOCKS
    n_cmp = s_len // CMP_STRIDE - CMP_SPAN + 1
    mt = jnp.asarray(_importance_matrix(MAX_SEL_BLOCKS, nc), BF16)

    def per_group(*shape):
        return pl.BlockSpec((1, 1) + shape, lambda b, g, i: (b, g) + (0,) * len(shape))

    return pl.pallas_call(
        functools.partial(_nsa_prompt_kernel, tq=tq, kb=kb, n_cmp=n_cmp, k_top=min(N_SEL, n_blk)),
        grid=(b_sz, KV_HEADS, s_len // tq),
        in_specs=[pl.BlockSpec((1, GQA_GROUP, HEAD_DIM, tq), lambda b, g, i: (b, g, 0, i)),
                  per_group(nc, HEAD_DIM), per_group(HEAD_DIM, nc),
                  per_group(s_len, HEAD_DIM + MAX_SEL_BLOCKS), per_group(s_len // LANES, HEAD_DIM, LANES),
                  per_group(s_len, HEAD_DIM), per_group(s_len // LANES, HEAD_DIM, LANES),
                  pl.BlockSpec((1, 1, GATE_ROWS, tq), lambda b, g, i: (b, g, 0, i)),
                  pl.BlockSpec((MAX_SEL_BLOCKS, nc), lambda b, g, i: (0, 0))],
        out_specs=pl.BlockSpec((1, tq, GQA_GROUP * HEAD_DIM), lambda b, g, i: (b, i, g)),
        out_shape=jax.ShapeDtypeStruct((b_sz, s_len, NSA_WIDTH), F32),
        compiler_params=_params("arbitrary", "arbitrary", "arbitrary"),
        name="nsa_prompt",
    )(q_t, comp_k, comp_vt, ks_aug, vs_t, kw, vw_t, gates_t, mt)


Q_COLS = LANES
COLS_PER_HEAD = Q_COLS // GQA_GROUP
V_OFF = LANES
V_COL0 = KV_HEADS * HEAD_DIM - V_OFF


def _nsa_sample_kernel(pt_ref, *refs, n_pages_step, n_steps, t_new, past, n_cmp, n_sel, k_top):
    del pt_ref
    p_cnt = n_pages_step
    page_refs = refs[:p_cnt]
    (qbd_ref, comp_ref, mt_ref, knew_ref, win_ref, wnew_ref, g_ref,
     o_ref, kv_scr, bias_scr, m_scr, l_scr, acc_scr, oc_scr) = refs[p_cnt:]
    i = pl.program_id(1)
    qbd = qbd_ref[0]
    col = lax.broadcasted_iota(jnp.int32, (1, Q_COLS), 1)
    tok = col & (SUBLANES - 1)
    qpos = past + tok
    kwin = slice(0, 2 * LANES)
    vwin = slice(V_OFF, V_OFF + 2 * LANES)
    k_lanes = KV_HEADS * HEAD_DIM
    blocks_per_page = PAGE_SIZE // SEL_BLOCK
    nb_step = blocks_per_page * p_cnt
    keys_step = PAGE_SIZE * p_cnt

    def attend(rows_bf16, valid):
        s = _dot(rows_bf16[:, kwin], qbd)
        return jnp.where(valid, s, NEG)

    @pl.when((pl.program_id(0) == 0) & (i == 0))
    def _():
        key_blk = lax.broadcasted_iota(jnp.int32, (keys_step, 1), 0) // SEL_BLOCK
        lane_blk = lax.broadcasted_iota(jnp.int32, (1, 2 * LANES - k_lanes), 1)
        kv_scr[:, k_lanes:2 * LANES] = (key_blk == lane_blk).astype(BF16)

    @pl.when(i == 0)
    def _():
        comp = comp_ref[0]
        ncp = comp.shape[0]
        cid = lax.broadcasted_iota(jnp.int32, (ncp, 1), 0)
        valid = (cid >= 1) & (cid <= n_cmp) & (cid * CMP_STRIDE + (CMP_BLOCK - CMP_STRIDE - 1) <= qpos)
        s = attend(comp, valid)
        e, den = _softmax_parts(s, valid, 0)
        p = e * (1.0 / jnp.maximum(den, 1e-30))
        oc_scr[...] = _dot_tn(p.astype(BF16), comp[:, vwin])
        mt = mt_ref[...]
        imp = sum(_dot(mt, piece) for piece in _split3(p))
        imp = imp + pltpu.roll(imp, COLS_PER_HEAD, 1)
        imp = imp + pltpu.roll(imp, 2 * COLS_PER_HEAD, 1)
        nbp = imp.shape[0]
        blk = lax.broadcasted_iota(jnp.int32, (nbp, 1), 0)
        cur = qpos // SEL_BLOCK
        score = jnp.where(blk <= cur, imp, -1.0)
        score = jnp.where((blk == 0) | (blk == cur) | (blk == cur - 1), FORCE, score)
        score = jnp.where(blk < n_sel, score, PAD_SCORE)
        blk_f = blk.astype(F32)
        bias = jnp.full((nbp, Q_COLS), NEG, F32)
        for _ in range(k_top):
            best = jnp.max(score, axis=0, keepdims=True)
            first = jnp.min(jnp.where(score == best, blk_f, float(nbp)), axis=0, keepdims=True)
            hit = blk_f == first
            bias = jnp.where(hit, 0.0, bias)
            score = jnp.where(hit, TAKEN_SCORE, score)
        bias_scr[...] = bias
        m_scr[...] = jnp.full_like(m_scr, NEG)
        l_scr[...] = jnp.zeros_like(l_scr)
        acc_scr[...] = jnp.zeros_like(acc_scr)

    def accumulate(s, v_rows):
        m_run = m_scr[0:1, :]
        m_new = jnp.maximum(m_run, jnp.max(s, axis=0, keepdims=True))
        alpha = jnp.exp(m_run - m_new)
        e = jnp.exp(s - m_new)
        l_scr[0:1, :] = alpha * l_scr[0:1, :] + jnp.sum(e, axis=0, keepdims=True)
        m_scr[0:1, :] = m_new
        alpha_col = jnp.broadcast_to(alpha, (SUBLANES, Q_COLS)).T[:, 0:1]
        acc_scr[...] = alpha_col * acc_scr[...] + _dot_tn(e.astype(BF16), v_rows)

    for p in range(p_cnt):
        rows_p = slice(p * PAGE_SIZE, (p + 1) * PAGE_SIZE)
        tiles = [page_refs[p][0, cb].astype(BF16).T for cb in range(KV_ROW // LANES)]
        kv_scr[rows_p, 0:LANES] = tiles[0]
        kv_scr[rows_p, LANES:k_lanes] = tiles[1][:, :k_lanes - LANES]
        kv_scr[rows_p, 2 * LANES:3 * LANES] = tiles[1]
        kv_scr[rows_p, 3 * LANES:4 * LANES] = tiles[2]
    bias_step = bias_scr[pl.ds(pl.multiple_of(i * nb_step, nb_step), nb_step), :].astype(BF16)
    pad = 2 * LANES - k_lanes - nb_step
    q_parts = [qbd[:k_lanes], bias_step] + ([jnp.zeros((pad, Q_COLS), BF16)] if pad else [])
    q_step = jnp.concatenate(q_parts, axis=0)
    accumulate(_dot(kv_scr[:, 0:2 * LANES], q_step), kv_scr[:, 2 * LANES:4 * LANES])

    @pl.when(i == n_steps - 1)
    def _():
        knew = knew_ref[0].astype(BF16)
        krow = lax.broadcasted_iota(jnp.int32, (t_new, 1), 0)
        new_blk = past // SEL_BLOCK
        s_new = _dot(knew[:, kwin], qbd) + bias_scr[new_blk:new_blk + 1, :]
        accumulate(jnp.where(krow <= tok, s_new, NEG), knew[:, vwin])
        o_s = acc_scr[...] * jnp.broadcast_to(1.0 / l_scr[0:1, :], (SUBLANES, Q_COLS)).T[:, 0:1]

        wbuf = win_ref[0].astype(BF16)
        wnew = wnew_ref[0].astype(BF16)
        buf = wbuf.shape[0]
        pos_b = past - buf + lax.broadcasted_iota(jnp.int32, (buf, 1), 0)
        pos_n = past + krow
        ok_b = (qpos - pos_b >= 0) & (qpos - pos_b < WINDOW) & (pos_b >= 0)
        ok_n = (qpos - pos_n >= 0) & (qpos - pos_n < WINDOW)
        s_b = attend(wbuf, ok_b)
        s_n = attend(wnew, ok_n)
        m_w = jnp.maximum(jnp.max(s_b, axis=0, keepdims=True), jnp.max(s_n, axis=0, keepdims=True))
        e_b = jnp.where(ok_b, jnp.exp(s_b - m_w), 0.0)
        e_n = jnp.where(ok_n, jnp.exp(s_n - m_w), 0.0)
        den = jnp.sum(e_b, axis=0, keepdims=True) + jnp.sum(e_n, axis=0, keepdims=True)
        o_w = _dot_tn(e_b.astype(BF16), wbuf[:, vwin]) + _dot_tn(e_n.astype(BF16), wnew[:, vwin])
        o_w = o_w * jnp.broadcast_to(1.0 / jnp.maximum(den, 1e-30), (SUBLANES, Q_COLS)).T[:, 0:1]
        o_c = oc_scr[...]

        gt = g_ref[0]
        for g in range(KV_HEADS):
            for r in range(GQA_GROUP):
                r0 = r * COLS_PER_HEAD + g * SUBLANES
                c0 = V_COL0 + g * HEAD_DIM
                gc = (g * GQA_GROUP + r) * N_BRANCH
                blk_o = [o[r0:r0 + t_new, c0:c0 + HEAD_DIM] for o in (o_c, o_s, o_w)]
                h0 = (g * GQA_GROUP + r) * HEAD_DIM
                o_ref[0, :, h0:h0 + HEAD_DIM] = (gt[:, gc:gc + 1] * blk_o[0] + gt[:, gc + 1:gc + 2] * blk_o[1]
                                                 + gt[:, gc + 2:gc + 3] * blk_o[2])


def _nsa_sample(cache_slc, page_table, qbd, comp_tm, knew, win_buf, wnew, gates, past):
    db, n_pages = page_table.shape
    t_new = knew.shape[1]
    assert t_new == SUBLANES and KV_HEADS * t_new <= COLS_PER_HEAD and past % SEL_BLOCK == 0 and t_new <= SEL_BLOCK
    p_cnt = min(32, n_pages)
    assert n_pages % p_cnt == 0
    n_steps = n_pages // p_cnt
    ncp = comp_tm.shape[1]
    n_str = -(-(past + t_new) // CMP_STRIDE)
    n_cmp = n_str - CMP_SPAN + 1
    n_sel = -(-(past + t_new) // SEL_BLOCK)
    blocks_step = (PAGE_SIZE // SEL_BLOCK) * p_cnt
    nbp = -(-(n_sel + 1) // blocks_step) * blocks_step
    mt = jnp.asarray(_importance_matrix(nbp, ncp), BF16)
    keys_step = PAGE_SIZE * p_cnt
    assert blocks_step <= 2 * LANES - KV_HEADS * HEAD_DIM
    buf = win_buf.shape[1]

    def page_spec(p):
        return pl.BlockSpec((1, KV_ROW // LANES, LANES, PAGE_SIZE),
                            lambda b, i, pt: (pt[b * n_pages + i * p_cnt + p], 0, 0, 0))

    def per_b(*shape):
        return pl.BlockSpec((1,) + shape, lambda b, i, pt: (b,) + (0,) * len(shape))

    def const(*shape):
        return pl.BlockSpec(shape, lambda b, i, pt: (0,) * len(shape))

    in_specs = [page_spec(p) for p in range(p_cnt)]
    in_specs += [per_b(2 * LANES, Q_COLS), per_b(ncp, KV_ROW), const(nbp, ncp),
                 per_b(t_new, KV_ROW), per_b(buf, KV_ROW), per_b(t_new, KV_ROW), per_b(t_new, LANES)]
    return pl.pallas_call(
        functools.partial(_nsa_sample_kernel, n_pages_step=p_cnt, n_steps=n_steps, t_new=t_new, past=past,
                          n_cmp=n_cmp, n_sel=n_sel, k_top=min(N_SEL, n_sel)),
        grid_spec=pltpu.PrefetchScalarGridSpec(
            num_scalar_prefetch=1, grid=(db, n_steps), in_specs=in_specs,
            out_specs=per_b(t_new, NSA_WIDTH),
            scratch_shapes=[pltpu.VMEM((keys_step, 4 * LANES), BF16),
                            pltpu.VMEM((nbp, Q_COLS), F32),
                            pltpu.VMEM((SUBLANES, Q_COLS), F32), pltpu.VMEM((SUBLANES, Q_COLS), F32),
                            pltpu.VMEM((Q_COLS, 2 * LANES), F32), pltpu.VMEM((Q_COLS, 2 * LANES), F32)]),
        out_shape=jax.ShapeDtypeStruct((db, t_new, NSA_WIDTH), F32),
        compiler_params=_params("arbitrary", "arbitrary"),
        name="nsa_sample",
    )(page_table.reshape(-1), *([cache_slc] * p_cnt), qbd, comp_tm, mt, knew, win_buf, wnew, gates)


def _out_kernel(x_ref, po_ref, zp_ref, no_ref, zn_ref, gate_ref, gnp_ref, gnn_ref, w_ref, fn_ref, y_ref):
    zp = zp_ref[0]
    zn = zn_ref[0]
    mp = _rms(po_ref[0], gnp_ref[...]) * (zp * _sigmoid(zp))
    mn = _rms(no_ref[0], gnn_ref[...]) * (zn * _sigmoid(zn))
    m = jnp.concatenate([mp, mn], axis=-1).astype(BF16)
    xo = x_ref[0] + gate_ref[0] * _dot(m, w_ref[...])
    y_ref[0] = _rms(xo, fn_ref[...])


def _out(x, pool_o, zp, nsa_o, zn, gate, gn_pool, gn_nsa, w_out_bf16, final_norm, tm):
    nb, t, d = x.shape
    r = gate.shape[1]
    assert t % tm == 0 and r in (1, t)
    if r == 1:
        gate_spec = pl.BlockSpec((1, 1, d), lambda b, i: (b, 0, 0))
    else:
        gate_spec = pl.BlockSpec((1, tm, d), lambda b, i: (b, i, 0))

    def tok(width):
        return pl.BlockSpec((1, tm, width), lambda b, i: (b, i, 0))

    def const(*shape):
        return pl.BlockSpec(shape, lambda b, i: (0,) * len(shape))

    return pl.pallas_call(
        _out_kernel,
        grid=(nb, t // tm),
        in_specs=[tok(d), tok(POOL_WIDTH), tok(POOL_WIDTH), tok(NSA_WIDTH), tok(NSA_WIDTH), gate_spec,
                  const(1, POOL_WIDTH), const(1, NSA_WIDTH), const(d, d), const(1, d)],
        out_specs=tok(d),
        out_shape=jax.ShapeDtypeStruct((nb, t, d), F32),
        compiler_params=_params("arbitrary", "arbitrary"),
        name="out",
    )(x, pool_o, zp, nsa_o, zn, gate, gn_pool.reshape(1, -1), gn_nsa.reshape(1, -1), w_out_bf16,
      final_norm.reshape(1, d))


def _pages_t(pages):
    n = pages.shape[0]
    return pages.transpose(0, 2, 3, 4, 1).reshape(n, KV_ROW // LANES, LANES, PAGE_SIZE)


def _block_diag(pool_w):
    n, c, _ = pool_w.shape
    eye = jnp.eye(n, dtype=pool_w.dtype)
    return (eye[:, None, :, None] * pool_w[:, :, None, :]).reshape(n * c, n * c)


def _first_layer(phi_w1, phi_pe):
    w = phi_w1.reshape(CMP_SPAN, CMP_STRIDE, 2, HEAD_DIM, CMP_HID)
    w1cat = w.transpose(2, 1, 3, 0, 4).reshape(2, CMP_STRIDE * HEAD_DIM, CMP_SPAN * CMP_HID)
    pe = phi_pe.reshape(CMP_SPAN, CMP_STRIDE, 2, HEAD_DIM).transpose(2, 0, 1, 3).reshape(2, CMP_SPAN, -1)
    pe2 = jnp.concatenate([pe, jnp.zeros((2, SUBLANES - CMP_SPAN, pe.shape[-1]), pe.dtype)], axis=1)
    return w1cat.astype(BF16), pe2


def kernel(x_prompt, x_sample, c_prompt, c_sample, cache_cmp_kv, cache_slc_kv, state_win_kv, state_pool, page_table, norm_w, w_ada, b_ada, w_in, pool_w, pool_scale, phi_w1, phi_pe, phi_w2, gn_pool, gn_nsa, w_out, final_norm):
    b_sz, s_len, d = x_prompt.shape
    db, t_new, _ = x_sample.shape
    depth = norm_w.shape[0]
    assert depth == 1 and w_in.shape[-1] == IN_WIDTH and d // 4 == POOL_WIDTH
    n_phys = cache_cmp_kv.shape[1]
    n_pages = page_table.shape[1]
    past = n_pages * PAGE_SIZE
    lyr = 0

    n_c = b_sz + db
    c_all = jnp.concatenate([c_prompt, c_sample, jnp.zeros((-n_c % SUBLANES, d), F32)], axis=0)
    ada = _ada(c_all, w_ada[lyr], b_ada[lyr])
    shift, scale, gate = ada[:, :d], ada[:, d:2 * d], ada[:, 2 * d:]

    w_pad = jnp.pad(w_in[lyr].astype(BF16), ((0, 0), (0, IN_PAD - IN_WIDTH)))
    w_out_bf16 = w_out[lyr].astype(BF16)
    wbd = _block_diag(pool_w[lyr]).astype(BF16)
    w1cat, pe2 = _first_layer(phi_w1[lyr], phi_pe[lyr])
    w2 = phi_w2[lyr].astype(BF16)
    n_tok_s = db * t_new

    def per_token(v):
        return jnp.repeat(v[b_sz:n_c], t_new, axis=0)[None]

    tm = min(512, s_len)
    (u_p, zp_p, zn_p, kvc_p, kvs_p, kvw_p, q_t, ks_aug, kw_p, vs_t, vw_t, g_t) = _in_proj(
        x_prompt, scale[:b_sz, None], shift[:b_sz, None], norm_w[lyr], w_pad, tm, True)
    pool_p = _pool(jnp.zeros((b_sz, POOL_HALO, POOL_WIDTH), F32), u_p, wbd, pool_scale[lyr], 0, tm)
    pages_p = s_len // PAGE_SIZE
    ident = jnp.arange(b_sz * pages_p, dtype=jnp.int32).reshape(b_sz, pages_p)
    comp_k, comp_vt = _compress(_pages_t(kvc_p.reshape(b_sz * pages_p, PAGE_SIZE, 2, KV_HEADS, HEAD_DIM)), ident,
                                w1cat, pe2, w2, None, False)
    nsa_p = _nsa_prompt(q_t, comp_k, comp_vt, ks_aug, vs_t, kw_p, vw_t, g_t, tq=128, kb=min(512, s_len))
    y_p = _out(x_prompt, pool_p, zp_p, nsa_p, zn_p, gate[:b_sz, None], gn_pool[lyr], gn_nsa[lyr], w_out_bf16,
               final_norm, tm)

    (u_s, zp_s, zn_s, kvc_s, kvs_s, kvw_s, q_s, g_s) = _in_proj(
        x_sample.reshape(1, n_tok_s, d), per_token(scale), per_token(shift), norm_w[lyr], w_pad, n_tok_s, False)
    u_s3 = u_s.reshape(db, t_new, POOL_WIDTH)
    hist = jnp.concatenate([jnp.zeros((db, POOL_HALO - state_pool.shape[2], POOL_WIDTH), F32), state_pool[lyr]], axis=1)
    pool_s = _pool(hist, u_s3, wbd, pool_scale[lyr], past, t_new)
    kvc_s3 = kvc_s.reshape(db, t_new, KV_ROW)
    kvs_s3 = kvs_s.reshape(db, t_new, KV_ROW)
    kvw_s3 = kvw_s.reshape(db, t_new, KV_ROW)
    xnew = jnp.pad(kvc_s3, ((0, 0), (0, CMP_STRIDE - t_new), (0, 0)))
    xnew = xnew.reshape(db, CMP_STRIDE, N_KV_SLABS, HEAD_DIM).transpose(0, 2, 1, 3).reshape(db, N_KV_SLABS, -1)
    comp_s = _compress(_pages_t(cache_cmp_kv[lyr]), page_table, w1cat, pe2, w2, xnew, True)
    q5 = q_s.reshape(db, t_new, KV_HEADS, GQA_GROUP, HEAD_DIM)
    eye = jnp.eye(KV_HEADS, dtype=BF16)
    qbd = q5.transpose(0, 2, 4, 3, 1)[:, :, :, :, None, :] * eye[None, :, None, None, :, None]
    qbd = jnp.pad(qbd.reshape(db, KV_HEADS * HEAD_DIM, GQA_GROUP, KV_HEADS * t_new),
                  ((0, 0), (0, 2 * LANES - KV_HEADS * HEAD_DIM), (0, 0), (0, COLS_PER_HEAD - KV_HEADS * t_new)))
    qbd = qbd.reshape(db, 2 * LANES, Q_COLS)
    g_s3 = g_s.reshape(db, t_new, LANES)
    win_buf = state_win_kv[lyr].reshape(db, -1, KV_ROW)
    nsa_s = _nsa_sample(_pages_t(cache_slc_kv[lyr]), page_table, qbd, comp_s, kvs_s3,
                        win_buf, kvw_s3, g_s3, past)
    y_s = _out(x_sample.reshape(1, n_tok_s, d), pool_s.reshape(1, n_tok_s, POOL_WIDTH), zp_s,
               nsa_s.reshape(1, n_tok_s, NSA_WIDTH), zn_s, per_token(gate), gn_pool[lyr], gn_nsa[lyr], w_out_bf16,
               final_norm, n_tok_s).reshape(db, t_new, d)

    kv_shape = (2, KV_HEADS, HEAD_DIM)
    win_len = min(WINDOW, s_len)
    buf = win_buf.shape[1]
    new_win_s = jnp.concatenate([win_buf, kvw_s3], axis=1)[:, -buf:]
    hist_len = state_pool.shape[2]
    new_pool_s = jnp.concatenate([state_pool[lyr], u_s3], axis=1)[:, -hist_len:]
    return (y_p, y_s,
            kvc_p.reshape((1, b_sz, s_len) + kv_shape), kvc_s3.reshape((1, db, t_new) + kv_shape),
            kvs_p.reshape((1, b_sz, s_len) + kv_shape), kvs_s3.reshape((1, db, t_new) + kv_shape),
            kvw_p[:, s_len - win_len:].reshape((1, b_sz, win_len) + kv_shape),
            new_win_s.reshape((1, db, buf) + kv_shape),
            u_p[:, s_len - hist_len:][None], new_pool_s[None])
```

```python
import functools

import numpy as np
import jax
import jax.numpy as jnp
from jax import lax
from jax.experimental import pallas as pl
from jax.experimental.pallas import tpu as pltpu

F32 = jnp.float32
BF16 = jnp.bfloat16

HEAD_DIM = 64
GQA_GROUP = 4
KV_HEADS = 3
N_HEADS = KV_HEADS * GQA_GROUP
N_KV_SLABS = 2 * KV_HEADS
KV_ROW = N_KV_SLABS * HEAD_DIM
N_BRANCH = 3
POOL_WINDOWS = (2, 4, 8, 16)
POOL_HALO = 16
CMP_BLOCK = 32
CMP_STRIDE = 16
CMP_SPAN = CMP_BLOCK // CMP_STRIDE
CMP_HID = 2 * HEAD_DIM
SEL_BLOCK = 64
N_SEL = 16
WINDOW = 512
PAGE_SIZE = 128
CHUNKS_PER_PAGE = PAGE_SIZE // CMP_STRIDE
CHUNK_PITCH = 24
EPS = 1e-6
NEG = -1e30
FORCE = 1e4
PAD_SCORE = -1e38
TAKEN_SCORE = -3e38

LANES = 128
SUBLANES = 8
VMEM_LIMIT_BYTES = 56 * 1024 * 1024


def _dot(a, b):
    return jnp.dot(a, b, preferred_element_type=F32)


def _dot_nt(a, b):
    return lax.dot_general(a, b, (((1,), (1,)), ((), ())), preferred_element_type=F32)


def _dot_tn(a, b):
    return lax.dot_general(a, b, (((0,), (0,)), ((), ())), preferred_element_type=F32)


def _sigmoid(v):
    return 1.0 / (1.0 + jnp.exp(-v))


def _rms(v, g):
    return v * lax.rsqrt(jnp.mean(v * v, axis=-1, keepdims=True) + EPS) * g


def _split3(v):
    hi = v.astype(BF16)
    r1 = v - hi.astype(F32)
    mid = r1.astype(BF16)
    lo = (r1 - mid.astype(F32)).astype(BF16)
    return hi, mid, lo


def _params(*sem):
    return pltpu.CompilerParams(dimension_semantics=sem, vmem_limit_bytes=VMEM_LIMIT_BYTES)


def _ada_kernel(c_ref, w_ref, b_ref, o_ref):
    c = c_ref[...]
    a = (c * _sigmoid(c)).astype(BF16)
    o_ref[...] = _dot(a, w_ref[...].astype(BF16)) + b_ref[...]


def _ada(c, w_ada, b_ada):
    m, d = c.shape
    n = w_ada.shape[1]
    tn = 512
    return pl.pallas_call(
        _ada_kernel,
        grid=(n // tn,),
        in_specs=[pl.BlockSpec((m, d), lambda j: (0, 0)),
                  pl.BlockSpec((d, tn), lambda j: (0, j)),
                  pl.BlockSpec((1, tn), lambda j: (0, j))],
        out_specs=pl.BlockSpec((m, tn), lambda j: (0, j)),
        out_shape=jax.ShapeDtypeStruct((m, n), F32),
        compiler_params=_params("arbitrary"),
        name="ada",
    )(c, w_ada, b_ada.reshape(1, n))


POOL_WIDTH = 256
NSA_WIDTH = N_HEADS * HEAD_DIM
GATE_COLS = N_BRANCH * N_HEADS
MAX_SEL_BLOCKS = LANES - HEAD_DIM
_SEG = {}
_off = 0
for _name, _w in (("u", POOL_WIDTH), ("zp", POOL_WIDTH), ("q", NSA_WIDTH), ("zn", NSA_WIDTH),
                  ("kvc", KV_ROW), ("kvs", KV_ROW), ("kvw", KV_ROW), ("gl", LANES)):
    _SEG[_name] = (_off, _w)
    _off += _w
IN_PAD = _off
IN_WIDTH = IN_PAD - LANES + GATE_COLS


GATE_ROWS = 16


def _in_proj_kernel(x_ref, sc_ref, sh_ref, nw_ref, w_ref, u_ref, zp_ref, zn_ref, kvc_ref, kvs_ref, kvw_ref, *rest,
                    attn_layouts):
    tm = x_ref.shape[1]
    h = (_rms(x_ref[0], nw_ref[...]) * (1.0 + sc_ref[0]) + sh_ref[0]).astype(BF16)

    def seg(name):
        s, w = _SEG[name]
        return _dot(h, w_ref[:, s:s + w])

    u_ref[0] = seg("u")
    zp_ref[0] = seg("zp")
    zn_ref[0] = seg("zn")
    q = seg("q") * (HEAD_DIM ** -0.5)
    kvc_ref[0] = seg("kvc")
    kvs = seg("kvs")
    kvs_ref[0] = kvs
    kvw = seg("kvw")
    kvw_ref[0] = kvw
    sig = _sigmoid(seg("gl"))
    if not attn_layouts:
        q_ref, g_ref = rest
        q_ref[0] = q.astype(BF16)
        g_ref[0] = sig
        return

    qt_ref, ks_aug_ref, kw_ref, vst_ref, vwt_ref, gt_ref = rest
    qt_ref[0] = q.T.astype(BF16).reshape(N_HEADS, HEAD_DIM, tm)
    pos = pl.program_id(1) * tm + lax.broadcasted_iota(jnp.int32, (tm, 1), 0)
    onehot = (lax.broadcasted_iota(jnp.int32, (1, MAX_SEL_BLOCKS), 1) == pos // SEL_BLOCK).astype(BF16)
    kvs_t = kvs.T.astype(BF16)
    kvw_t = kvw.T.astype(BF16)
    v0 = KV_HEADS * HEAD_DIM
    for g in range(KV_HEADS):
        ks_aug_ref[0, g] = jnp.concatenate([kvs[:, g * HEAD_DIM:(g + 1) * HEAD_DIM].astype(BF16), onehot], axis=1)
        kw_ref[0, g] = kvw[:, g * HEAD_DIM:(g + 1) * HEAD_DIM].astype(BF16)
        for c in range(tm // LANES):
            vst_ref[0, g, c] = kvs_t[v0 + g * HEAD_DIM:v0 + (g + 1) * HEAD_DIM, c * LANES:(c + 1) * LANES]
            vwt_ref[0, g, c] = kvw_t[v0 + g * HEAD_DIM:v0 + (g + 1) * HEAD_DIM, c * LANES:(c + 1) * LANES]
        per_group = GQA_GROUP * N_BRANCH
        rolled = sig if g == 0 else pltpu.roll(sig, LANES - per_group * g, 1)
        gt_ref[0, g] = rolled.T[:GATE_ROWS, :]


def _in_proj(x, scale, shift, norm_w, w_pad, tm, attn_layouts):
    nb, t, d = x.shape
    r = scale.shape[1]
    assert t % tm == 0 and r in (1, t) and (tm % LANES == 0 or not attn_layouts)
    if r == 1:
        mod_spec = pl.BlockSpec((1, 1, d), lambda b, i: (b, 0, 0))
    else:
        mod_spec = pl.BlockSpec((1, tm, d), lambda b, i: (b, i, 0))

    def tok(width):
        return pl.BlockSpec((1, tm, width), lambda b, i: (b, i, 0))

    def hm(n, width=HEAD_DIM):
        return pl.BlockSpec((1, n, tm, width), lambda b, i: (b, 0, i, 0))

    def sd(*shape, dtype=F32):
        return jax.ShapeDtypeStruct(shape, dtype)

    out_specs = [tok(POOL_WIDTH), tok(POOL_WIDTH), tok(NSA_WIDTH), tok(KV_ROW), tok(KV_ROW), tok(KV_ROW)]
    out_shape = [sd(nb, t, POOL_WIDTH), sd(nb, t, POOL_WIDTH), sd(nb, t, NSA_WIDTH),
                 sd(nb, t, KV_ROW), sd(nb, t, KV_ROW), sd(nb, t, KV_ROW)]
    if attn_layouts:
        lane_tiles = tm // LANES
        vt_spec = pl.BlockSpec((1, KV_HEADS, lane_tiles, HEAD_DIM, LANES), lambda b, i: (b, 0, i, 0, 0))
        vt_shape = sd(nb, KV_HEADS, t // LANES, HEAD_DIM, LANES, dtype=BF16)
        out_specs += [pl.BlockSpec((1, N_HEADS, HEAD_DIM, tm), lambda b, i: (b, 0, 0, i)),
                      hm(KV_HEADS, HEAD_DIM + MAX_SEL_BLOCKS), hm(KV_HEADS), vt_spec, vt_spec,
                      pl.BlockSpec((1, KV_HEADS, GATE_ROWS, tm), lambda b, i: (b, 0, 0, i))]
        out_shape += [sd(nb, N_HEADS, HEAD_DIM, t, dtype=BF16),
                      sd(nb, KV_HEADS, t, HEAD_DIM + MAX_SEL_BLOCKS, dtype=BF16),
                      sd(nb, KV_HEADS, t, HEAD_DIM, dtype=BF16), vt_shape, vt_shape,
                      sd(nb, KV_HEADS, GATE_ROWS, t)]
    else:
        out_specs += [tok(NSA_WIDTH), tok(LANES)]
        out_shape += [sd(nb, t, NSA_WIDTH, dtype=BF16), sd(nb, t, LANES)]
    return pl.pallas_call(
        functools.partial(_in_proj_kernel, attn_layouts=attn_layouts),
        grid=(nb, t // tm),
        in_specs=[tok(d), mod_spec, mod_spec,
                  pl.BlockSpec((1, d), lambda b, i: (0, 0)),
                  pl.BlockSpec((d, IN_PAD), lambda b, i: (0, 0))],
        out_specs=out_specs,
        out_shape=out_shape,
        compiler_params=_params("arbitrary", "arbitrary"),
        name="in_proj",
    )(x, scale, shift, norm_w.reshape(1, d), w_pad)


def _pool_kernel(hist_ref, uprev_ref, u_ref, wbd_ref, ps_ref, o_ref, *, pos0, tp):
    i = pl.program_id(1)
    u = u_ref[0]
    halo = jnp.where(i == 0, hist_ref[0], uprev_ref[0])
    ue = jnp.concatenate([halo, u], axis=0)
    sums = []
    s = ue
    for w in POOL_WINDOWS:
        s = s + pltpu.roll(s, w // 2, 0)
        sums.append(s[POOL_HALO:])
    c = u.shape[-1]
    grp = lax.broadcasted_iota(jnp.int32, (1, c), 1) // (c // len(POOL_WINDOWS))
    pos = pos0 + i * tp + lax.broadcasted_iota(jnp.int32, (tp, 1), 0)
    tot = sums[-1]
    win = jnp.full((1, c), float(POOL_WINDOWS[-1]), F32)
    for gi in range(len(POOL_WINDOWS) - 2, -1, -1):
        tot = jnp.where(grp == gi, sums[gi], tot)
        win = jnp.where(grp == gi, float(POOL_WINDOWS[gi]), win)
    cnt = jnp.minimum(win, (pos + 1).astype(F32))
    dlt = tot / cnt - u
    o_ref[0] = _dot(dlt.astype(BF16), wbd_ref[...]) * ps_ref[...]


def _pool(hist, u, wbd, pool_scale, pos0, tp):
    nb, t, c = u.shape
    assert t % tp == 0
    if t >= POOL_HALO:
        assert tp % POOL_HALO == 0
        uprev, ratio = u, tp // POOL_HALO
        prev_spec = pl.BlockSpec((1, POOL_HALO, c), lambda b, i: (b, jnp.maximum(i * ratio - 1, 0), 0))
    else:
        assert t == tp
        uprev = hist
        prev_spec = pl.BlockSpec((1, POOL_HALO, c), lambda b, i: (b, 0, 0))
    return pl.pallas_call(
        functools.partial(_pool_kernel, pos0=pos0, tp=tp),
        grid=(nb, t // tp),
        in_specs=[pl.BlockSpec((1, POOL_HALO, c), lambda b, i: (b, 0, 0)), prev_spec,
                  pl.BlockSpec((1, tp, c), lambda b, i: (b, i, 0)),
                  pl.BlockSpec((c, c), lambda b, i: (0, 0)),
                  pl.BlockSpec((1, c), lambda b, i: (0, 0))],
        out_specs=pl.BlockSpec((1, tp, c), lambda b, i: (b, i, 0)),
        out_shape=jax.ShapeDtypeStruct((nb, t, c), F32),
        compiler_params=_params("arbitrary", "arbitrary"),
        name="pool",
    )(hist, uprev, u, wbd, pool_scale.reshape(1, c))


def _gelu_tanh(v):
    return 0.5 * v * (1.0 + jnp.tanh(np.sqrt(2.0 / np.pi).astype(np.float32) * (v + 0.044715 * (v * v * v))))


def _compress_kernel(pt_ref, *refs, n_pages_step, n_steps, has_new, token_major):
    del pt_ref
    p_cnt = n_pages_step
    page_refs = refs[:p_cnt]
    w1p_ref, w1_ref, pe_ref, w2_ref = refs[p_cnt:p_cnt + 4]
    k = p_cnt + 4
    xnew_ref = None
    if has_new:
        xnew_ref = refs[k]
        k += 1
    if token_major:
        out_ref = refs[k]
        k += 1
    else:
        outk_ref, outvt_ref = refs[k:k + 2]
        k += 2
    slab_scr, carry_scr, bias_scr = refs[k:k + 3]
    i = pl.program_id(1)
    rows = CHUNKS_PER_PAGE * p_cnt
    row_id = lax.broadcasted_iota(jnp.int32, (rows, 1), 0)

    @pl.when(i == 0)
    def _():
        carry_scr[...] = jnp.zeros_like(carry_scr)
        for kk in range(2):
            pb = _dot(pe_ref[kk].astype(BF16), w1_ref[kk])
            bias_scr[kk] = pb[0:1, :CMP_HID] + pb[1:2, CMP_HID:]

    def store(kg, comp):
        if token_major:
            out_ref[0, :, kg * HEAD_DIM:(kg + 1) * HEAD_DIM] = comp.astype(BF16)
        elif kg < KV_HEADS:
            outk_ref[0, kg] = comp.astype(BF16)
        else:
            wide = jnp.concatenate([comp, jnp.zeros((rows, LANES - HEAD_DIM), F32)], axis=1)
            outvt_ref[0, kg - KV_HEADS] = wide.T[:HEAD_DIM, :].astype(BF16)

    def finish(kg, a0_prev, a1):
        kk = kg // KV_HEADS
        pre = a0_prev + a1 + bias_scr[kk]
        return _dot(_gelu_tanh(pre).astype(BF16), w2_ref[kk])

    def main():
        for p in range(p_cnt):
            for cb in range(KV_ROW // LANES):
                rows_t = page_refs[p][0, cb].astype(BF16).T.astype(F32)
                for ch in range(CHUNKS_PER_PAGE):
                    r0 = (p * CHUNKS_PER_PAGE + ch) * CHUNK_PITCH
                    slab_scr[cb, r0:r0 + CMP_STRIDE, :] = rows_t[ch * CMP_STRIDE:(ch + 1) * CMP_STRIDE, :]
        for cb in range(KV_ROW // LANES):
            x = jnp.concatenate([slab_scr[cb, pl.ds(j, rows, stride=CHUNK_PITCH), :] for j in range(CMP_STRIDE)],
                                axis=1).astype(BF16)
            a_both = _dot(x, w1p_ref[cb])
            for half in range(2):
                kg = 2 * cb + half
                a = a_both[:, half * 2 * CMP_HID:(half + 1) * 2 * CMP_HID]
                a0 = a[:, :CMP_HID]
                a0_prev = jnp.where(row_id == 0, carry_scr[kg], pltpu.roll(a0, 1, 0))
                carry_scr[kg] = a0[rows - 1:rows, :]
                store(kg, finish(kg, a0_prev, a[:, CMP_HID:]))

    if not has_new:
        main()
    else:
        pl.when(i < n_steps)(main)

        @pl.when(i == n_steps)
        def _():
            for kg in range(N_KV_SLABS):
                xk = jnp.broadcast_to(xnew_ref[0, kg:kg + 1, :], (SUBLANES, CMP_STRIDE * HEAD_DIM)).astype(BF16)
                a1 = _dot(xk, w1_ref[kg // KV_HEADS])[0:1, CMP_HID:]
                comp = finish(kg, carry_scr[kg], a1)
                store(kg, jnp.where(row_id == 0, jnp.broadcast_to(comp, (rows, HEAD_DIM)), 0.0))


def _compress(pages, page_table, w1cat, pe2, w2, xnew, token_major):
    nb, n_pages = page_table.shape
    p_cnt = min(32, n_pages)
    assert n_pages % p_cnt == 0
    n_steps = n_pages // p_cnt
    has_new = xnew is not None
    rows = CHUNKS_PER_PAGE * p_cnt
    grid_steps = n_steps + (1 if has_new else 0)
    tot_rows = rows * grid_steps
    xw = CMP_STRIDE * HEAD_DIM

    def page_spec(p):
        def imap(b, i, pt):
            step = jnp.minimum(i, n_steps - 1)
            return (pt[b * n_pages + step * p_cnt + p], 0, 0, 0)
        return pl.BlockSpec((1, KV_ROW // LANES, LANES, PAGE_SIZE), imap)

    n_cb = KV_ROW // LANES
    w4 = w1cat.reshape(2, CMP_STRIDE, HEAD_DIM, 2 * CMP_HID)
    zero = jnp.zeros_like(w4[0])
    w1p = jnp.stack([jnp.concatenate([jnp.concatenate([w4[(2 * cb) // KV_HEADS], zero], axis=2),
                                      jnp.concatenate([zero, w4[(2 * cb + 1) // KV_HEADS]], axis=2)], axis=1)
                     for cb in range(n_cb)]).reshape(n_cb, CMP_STRIDE * LANES, 4 * CMP_HID)
    in_specs = [page_spec(p) for p in range(p_cnt)]
    in_specs += [pl.BlockSpec((n_cb, CMP_STRIDE * LANES, 4 * CMP_HID), lambda b, i, pt: (0, 0, 0)),
                 pl.BlockSpec((2, xw, 2 * CMP_HID), lambda b, i, pt: (0, 0, 0)),
                 pl.BlockSpec((2, SUBLANES, xw), lambda b, i, pt: (0, 0, 0)),
                 pl.BlockSpec((2, CMP_HID, HEAD_DIM), lambda b, i, pt: (0, 0, 0))]
    args = [pages] * p_cnt + [w1p, w1cat, pe2, w2]
    if has_new:
        in_specs.append(pl.BlockSpec((1, N_KV_SLABS, xw), lambda b, i, pt: (b, 0, 0)))
        args.append(xnew)
    if token_major:
        out_spec = pl.BlockSpec((1, rows, KV_ROW), lambda b, i, pt: (b, i, 0))
        out_shape = jax.ShapeDtypeStruct((nb, tot_rows, KV_ROW), BF16)
    else:
        assert rows % LANES == 0
        out_spec = [pl.BlockSpec((1, KV_HEADS, rows, HEAD_DIM), lambda b, i, pt: (b, 0, i, 0)),
                    pl.BlockSpec((1, KV_HEADS, HEAD_DIM, rows), lambda b, i, pt: (b, 0, 0, i))]
        out_shape = [jax.ShapeDtypeStruct((nb, KV_HEADS, tot_rows, HEAD_DIM), BF16),
                     jax.ShapeDtypeStruct((nb, KV_HEADS, HEAD_DIM, tot_rows), BF16)]
    return pl.pallas_call(
        functools.partial(_compress_kernel, n_pages_step=p_cnt, n_steps=n_steps, has_new=has_new,
                          token_major=token_major),
        grid_spec=pltpu.PrefetchScalarGridSpec(
            num_scalar_prefetch=1, grid=(nb, grid_steps), in_specs=in_specs, out_specs=out_spec,
            scratch_shapes=[pltpu.VMEM((n_cb, rows * CHUNK_PITCH, LANES), F32),
                            pltpu.VMEM((N_KV_SLABS, 1, CMP_HID), F32),
                            pltpu.VMEM((2, 1, CMP_HID), F32)]),
        out_shape=out_shape,
        compiler_params=_params("arbitrary", "arbitrary"),
        name="compress_new" if has_new else "compress",
    )(page_table.reshape(-1), *args)


def _softmax_parts(s, valid, axis):
    s = jnp.where(valid, s, NEG)
    m = jnp.max(s, axis=axis, keepdims=True)
    e = jnp.where(valid, jnp.exp(s - m), 0.0)
    return e, jnp.sum(e, axis=axis, keepdims=True)


def _nsa_prompt_kernel(qt_ref, kc_ref, vct_ref, ks_ref, vst_ref, kw_ref, vwt_ref, gt_ref, mt_ref, o_ref, *,
                       tq, kb, n_cmp, k_top):
    qi = pl.program_id(2)
    qs = qi * tq
    r4 = GQA_GROUP
    cols = r4 * tq
    qt = jnp.concatenate([qt_ref[0, r] for r in range(r4)], axis=1)
    tpos = qs + lax.broadcasted_iota(jnp.int32, (1, tq), 1)

    def per_head(v):
        return jnp.concatenate([v] * r4, axis=1)

    def value_tiles(ref, k0, width):
        j0 = k0 // LANES
        return jnp.concatenate([ref[0, 0, j0 + j] for j in range(width // LANES)], axis=1)

    nc = kc_ref.shape[2]
    s = _dot(kc_ref[0, 0], qt)
    cid = lax.broadcasted_iota(jnp.int32, (nc, 1), 0)
    valid = (cid >= 1) & (cid <= n_cmp) & (cid * CMP_STRIDE + (CMP_BLOCK - CMP_STRIDE - 1) <= tpos)
    s = s + per_head(jnp.where(valid, 0.0, NEG))
    e = jnp.exp(s - jnp.max(s, axis=0, keepdims=True))
    den = jnp.sum(e, axis=0, keepdims=True)
    has_key = per_head(jnp.where(tpos >= CMP_BLOCK - 1, 1.0, 0.0)) if n_cmp >= 1 else 0.0
    p = e * (has_key / jnp.maximum(den, 1e-30))
    o_c = _dot(vct_ref[0, 0], p.astype(BF16))
    pg = p[:, 0:tq]
    for r in range(1, r4):
        pg = pg + p[:, r * tq:(r + 1) * tq]

    mt = mt_ref[...]
    imp = sum(_dot(mt, piece) for piece in _split3(pg))
    n_blk = imp.shape[0]
    blk = lax.broadcasted_iota(jnp.int32, (n_blk, 1), 0)
    cur = (qs + lax.broadcasted_iota(jnp.int32, (1, tq), 1)) // SEL_BLOCK
    score = jnp.where(blk <= cur, imp, -1.0)
    score = jnp.where((blk == 0) | (blk == cur) | (blk == cur - 1), FORCE, score)
    sub = lax.broadcasted_iota(jnp.int32, (SUBLANES, 1), 0)
    tiles = [score[v * SUBLANES:(v + 1) * SUBLANES, :] for v in range(n_blk // SUBLANES)]
    ranks = [jnp.zeros((SUBLANES, tq), F32) for _ in tiles]
    for j in range(n_blk):
        sj = score[j:j + 1, :]
        for v, sc in enumerate(tiles):
            if v > j // SUBLANES:
                ahead = sj >= sc
            elif v < j // SUBLANES:
                ahead = sj > sc
            else:
                ahead = (sj > sc) | ((sj == sc) & (sub > j % SUBLANES))
            ranks[v] = ranks[v] + jnp.where(ahead, 1.0, 0.0)
    rank = jnp.concatenate(ranks, axis=0)
    sel_bias = jnp.where((rank < float(k_top)) & (blk <= cur), 0.0, NEG).astype(BF16)
    q_aug = jnp.concatenate([qt, jnp.concatenate([sel_bias] * r4, axis=1)], axis=0)

    def scores(k0):
        return _dot(ks_ref[0, 0, pl.ds(k0, kb), :], q_aug)

    def absorb(sk, k0, state, diagonal):
        m_run, l_run, acc = state
        if diagonal:
            kpos = k0 + lax.broadcasted_iota(jnp.int32, (kb, 1), 0)
            sk = sk + per_head(jnp.where(kpos <= tpos, 0.0, NEG))
        m_new = jnp.maximum(m_run, jnp.max(sk, axis=0, keepdims=True))
        alpha = jnp.exp(m_run - m_new)
        ek = jnp.exp(sk - m_new)
        l_new = alpha * l_run + jnp.sum(ek, axis=0, keepdims=True)
        pv = _dot(value_tiles(vst_ref, k0, kb), ek.astype(BF16))
        return m_new, l_new, alpha * acc + pv

    def tile(k0, state, diagonal):
        return absorb(scores(k0), k0, state, diagonal)

    init = (jnp.full((1, cols), NEG, F32), jnp.zeros((1, cols), F32), jnp.zeros((HEAD_DIM, cols), F32))
    n_full = qs // kb
    state = lax.fori_loop(0, n_full, lambda kt, st: tile(pl.multiple_of(kt * kb, kb), st, False), init)
    _, l_s, acc_s = tile(pl.multiple_of(n_full * kb, kb), state, True)
    o_s = acc_s * (1.0 / l_s)

    wl = kw_ref.shape[2] if kw_ref.shape[2] < WINDOW + tq else WINDOW + tq
    w0 = pl.multiple_of(jnp.maximum(qs + tq - wl, 0), tq)
    sw = _dot(kw_ref[0, 0, pl.ds(w0, wl), :], qt)
    dq = tpos - (w0 + lax.broadcasted_iota(jnp.int32, (wl, 1), 0))
    sw = sw + per_head(jnp.where((dq >= 0) & (dq < WINDOW), 0.0, NEG))
    ew = jnp.exp(sw - jnp.max(sw, axis=0, keepdims=True))
    denw = jnp.sum(ew, axis=0, keepdims=True)
    o_w = _dot(value_tiles(vwt_ref, w0, wl), ew.astype(BF16)) * (1.0 / denw)

    gt = gt_ref[0, 0]
    heads = []
    for r in range(r4):
        c0 = r * N_BRANCH
        sl = slice(r * tq, (r + 1) * tq)
        heads.append(gt[c0:c0 + 1] * o_c[:, sl] + gt[c0 + 1:c0 + 2] * o_s[:, sl] + gt[c0 + 2:c0 + 3] * o_w[:, sl])
    o_ref[0] = jnp.concatenate(heads, axis=0).T


def _importance_matrix(n_blk_pad, n_col):
    mt = np.zeros((n_blk_pad, n_col), np.float32)
    ratio = SEL_BLOCK // CMP_STRIDE
    for b in range(n_blk_pad):
        for c, w in zip(range(ratio * b, ratio * b + ratio + 1), (1.0,) + (2.0,) * (ratio - 1) + (1.0,)):
            if c < n_col:
                mt[b, c] = w
    return mt


def _nsa_prompt(q_t, comp_k, comp_vt, ks_aug, vs_t, kw, vw_t, gates_t, tq, kb):
    b_sz, _, _, s_len = q_t.shape
    nc = comp_k.shape[2]
    assert s_len % SEL_BLOCK == 0 and s_len % kb == 0 and kb % tq == 0 and s_len % tq == 0 and CMP_SPAN == 2
    assert tq % LANES == 0 and tq & (tq - 1) == 0
    n_blk = s_len // SEL_BLOCK
    assert n_blk <= MAX_SEL_BLOCKS
    n_cmp = s_len // CMP_STRIDE - CMP_SPAN + 1
    mt = jnp.asarray(_importance_matrix(MAX_SEL_BLOCKS, nc), BF16)

    def per_group(*shape):
        return pl.BlockSpec((1, 1) + shape, lambda b, g, i: (b, g) + (0,) * len(shape))

    return pl.pallas_call(
        functools.partial(_nsa_prompt_kernel, tq=tq, kb=kb, n_cmp=n_cmp, k_top=min(N_SEL, n_blk)),
        grid=(b_sz, KV_HEADS, s_len // tq),
        in_specs=[pl.BlockSpec((1, GQA_GROUP, HEAD_DIM, tq), lambda b, g, i: (b, g, 0, i)),
                  per_group(nc, HEAD_DIM), per_group(HEAD_DIM, nc),
                  per_group(s_len, HEAD_DIM + MAX_SEL_BLOCKS), per_group(s_len // LANES, HEAD_DIM, LANES),
                  per_group(s_len, HEAD_DIM), per_group(s_len // LANES, HEAD_DIM, LANES),
                  pl.BlockSpec((1, 1, GATE_ROWS, tq), lambda b, g, i: (b, g, 0, i)),
                  pl.BlockSpec((MAX_SEL_BLOCKS, nc), lambda b, g, i: (0, 0))],
        out_specs=pl.BlockSpec((1, tq, GQA_GROUP * HEAD_DIM), lambda b, g, i: (b, i, g)),
        out_shape=jax.ShapeDtypeStruct((b_sz, s_len, NSA_WIDTH), F32),
        compiler_params=_params("arbitrary", "arbitrary", "arbitrary"),
        name="nsa_prompt",
    )(q_t, comp_k, comp_vt, ks_aug, vs_t, kw, vw_t, gates_t, mt)


Q_COLS = LANES
COLS_PER_HEAD = Q_COLS // GQA_GROUP
V_OFF = LANES
V_COL0 = KV_HEADS * HEAD_DIM - V_OFF


def _nsa_sample_kernel(pt_ref, *refs, n_pages_step, n_steps, t_new, past, n_cmp, n_sel, k_top):
    del pt_ref
    p_cnt = n_pages_step
    page_refs = refs[:p_cnt]
    (qbd_ref, comp_ref, mt_ref, knew_ref, win_ref, wnew_ref, g_ref,
     o_ref, kv_scr, bias_scr, m_scr, l_scr, acc_scr, oc_scr) = refs[p_cnt:]
    i = pl.program_id(1)
    qbd = qbd_ref[0]
    col = lax.broadcasted_iota(jnp.int32, (1, Q_COLS), 1)
    tok = col & (SUBLANES - 1)
    qpos = past + tok
    kwin = slice(0, 2 * LANES)
    vwin = slice(V_OFF, V_OFF + 2 * LANES)
    k_lanes = KV_HEADS * HEAD_DIM
    blocks_per_page = PAGE_SIZE // SEL_BLOCK
    nb_step = blocks_per_page * p_cnt
    keys_step = PAGE_SIZE * p_cnt

    def attend(rows_bf16, valid):
        s = _dot(rows_bf16[:, kwin], qbd)
        return jnp.where(valid, s, NEG)

    @pl.when((pl.program_id(0) == 0) & (i == 0))
    def _():
        key_blk = lax.broadcasted_iota(jnp.int32, (keys_step, 1), 0) // SEL_BLOCK
        lane_blk = lax.broadcasted_iota(jnp.int32, (1, 2 * LANES - k_lanes), 1)
        kv_scr[:, k_lanes:2 * LANES] = (key_blk == lane_blk).astype(BF16)

    @pl.when(i == 0)
    def _():
        comp = comp_ref[0]
        ncp = comp.shape[0]
        cid = lax.broadcasted_iota(jnp.int32, (ncp, 1), 0)
        valid = (cid >= 1) & (cid <= n_cmp) & (cid * CMP_STRIDE + (CMP_BLOCK - CMP_STRIDE - 1) <= qpos)
        s = attend(comp, valid)
        e, den = _softmax_parts(s, valid, 0)
        p = e * (1.0 / jnp.maximum(den, 1e-30))
        oc_scr[...] = _dot_tn(p.astype(BF16), comp[:, vwin])
        mt = mt_ref[...]
        imp = sum(_dot(mt, piece) for piece in _split3(p))
        imp = imp + pltpu.roll(imp, COLS_PER_HEAD, 1)
        imp = imp + pltpu.roll(imp, 2 * COLS_PER_HEAD, 1)
        nbp = imp.shape[0]
        blk = lax.broadcasted_iota(jnp.int32, (nbp, 1), 0)
        cur = qpos // SEL_BLOCK
        score = jnp.where(blk <= cur, imp, -1.0)
        score = jnp.where((blk == 0) | (blk == cur) | (blk == cur - 1), FORCE, score)
        score = jnp.where(blk < n_sel, score, PAD_SCORE)
        blk_f = blk.astype(F32)
        bias = jnp.full((nbp, Q_COLS), NEG, F32)
        for _ in range(k_top):
            best = jnp.max(score, axis=0, keepdims=True)
            first = jnp.min(jnp.where(score == best, blk_f, float(nbp)), axis=0, keepdims=True)
            hit = blk_f == first
            bias = jnp.where(hit, 0.0, bias)
            score = jnp.where(hit, TAKEN_SCORE, score)
        bias_scr[...] = bias
        m_scr[...] = jnp.full_like(m_scr, NEG)
        l_scr[...] = jnp.zeros_like(l_scr)
        acc_scr[...] = jnp.zeros_like(acc_scr)

    def accumulate(s, v_rows):
        m_run = m_scr[0:1, :]
        m_new = jnp.maximum(m_run, jnp.max(s, axis=0, keepdims=True))
        alpha = jnp.exp(m_run - m_new)
        e = jnp.exp(s - m_new)
        l_scr[0:1, :] = alpha * l_scr[0:1, :] + jnp.sum(e, axis=0, keepdims=True)
        m_scr[0:1, :] = m_new
        alpha_col = jnp.broadcast_to(alpha, (SUBLANES, Q_COLS)).T[:, 0:1]
        acc_scr[...] = alpha_col * acc_scr[...] + _dot_tn(e.astype(BF16), v_rows)

    for p in range(p_cnt):
        rows_p = slice(p * PAGE_SIZE, (p + 1) * PAGE_SIZE)
        tiles = [page_refs[p][0, cb].astype(BF16).T for cb in range(KV_ROW // LANES)]
        kv_scr[rows_p, 0:LANES] = tiles[0]
        kv_scr[rows_p, LANES:k_lanes] = tiles[1][:, :k_lanes - LANES]
        kv_scr[rows_p, 2 * LANES:3 * LANES] = tiles[1]
        kv_scr[rows_p, 3 * LANES:4 * LANES] = tiles[2]
    bias_step = bias_scr[pl.ds(pl.multiple_of(i * nb_step, nb_step), nb_step), :].astype(BF16)
    pad = 2 * LANES - k_lanes - nb_step
    q_parts = [qbd[:k_lanes], bias_step] + ([jnp.zeros((pad, Q_COLS), BF16)] if pad else [])
    q_step = jnp.concatenate(q_parts, axis=0)
    accumulate(_dot(kv_scr[:, 0:2 * LANES], q_step), kv_scr[:, 2 * LANES:4 * LANES])

    @pl.when(i == n_steps - 1)
    def _():
        knew = knew_ref[0].astype(BF16)
        krow = lax.broadcasted_iota(jnp.int32, (t_new, 1), 0)
        new_blk = past // SEL_BLOCK
        s_new = _dot(knew[:, kwin], qbd) + bias_scr[new_blk:new_blk + 1, :]
        accumulate(jnp.where(krow <= tok, s_new, NEG), knew[:, vwin])
        o_s = acc_scr[...] * jnp.broadcast_to(1.0 / l_scr[0:1, :], (SUBLANES, Q_COLS)).T[:, 0:1]

        wbuf = win_ref[0].astype(BF16)
        wnew = wnew_ref[0].astype(BF16)
        buf = wbuf.shape[0]
        pos_b = past - buf + lax.broadcasted_iota(jnp.int32, (buf, 1), 0)
        pos_n = past + krow
        ok_b = (qpos - pos_b >= 0) & (qpos - pos_b < WINDOW) & (pos_b >= 0)
        ok_n = (qpos - pos_n >= 0) & (qpos - pos_n < WINDOW)
        s_b = attend(wbuf, ok_b)
        s_n = attend(wnew, ok_n)
        m_w = jnp.maximum(jnp.max(s_b, axis=0, keepdims=True), jnp.max(s_n, axis=0, keepdims=True))
        e_b = jnp.where(ok_b, jnp.exp(s_b - m_w), 0.0)
        e_n = jnp.where(ok_n, jnp.exp(s_n - m_w), 0.0)
        den = jnp.sum(e_b, axis=0, keepdims=True) + jnp.sum(e_n, axis=0, keepdims=True)
        o_w = _dot_tn(e_b.astype(BF16), wbuf[:, vwin]) + _dot_tn(e_n.astype(BF16), wnew[:, vwin])
        o_w = o_w * jnp.broadcast_to(1.0 / jnp.maximum(den, 1e-30), (SUBLANES, Q_COLS)).T[:, 0:1]
        o_c = oc_scr[...]

        gt = g_ref[0]
        for g in range(KV_HEADS):
            for r in range(GQA_GROUP):
                r0 = r * COLS_PER_HEAD + g * SUBLANES
                c0 = V_COL0 + g * HEAD_DIM
                gc = (g * GQA_GROUP + r) * N_BRANCH
                blk_o = [o[r0:r0 + t_new, c0:c0 + HEAD_DIM] for o in (o_c, o_s, o_w)]
                h0 = (g * GQA_GROUP + r) * HEAD_DIM
                o_ref[0, :, h0:h0 + HEAD_DIM] = (gt[:, gc:gc + 1] * blk_o[0] + gt[:, gc + 1:gc + 2] * blk_o[1]
                                                 + gt[:, gc + 2:gc + 3] * blk_o[2])


def _nsa_sample(cache_slc, page_table, qbd, comp_tm, knew, win_buf, wnew, gates, past):
    db, n_pages = page_table.shape
    t_new = knew.shape[1]
    assert t_new == SUBLANES and KV_HEADS * t_new <= COLS_PER_HEAD and past % SEL_BLOCK == 0 and t_new <= SEL_BLOCK
    p_cnt = min(32, n_pages)
    assert n_pages % p_cnt == 0
    n_steps = n_pages // p_cnt
    ncp = comp_tm.shape[1]
    n_str = -(-(past + t_new) // CMP_STRIDE)
    n_cmp = n_str - CMP_SPAN + 1
    n_sel = -(-(past + t_new) // SEL_BLOCK)
    blocks_step = (PAGE_SIZE // SEL_BLOCK) * p_cnt
    nbp = -(-(n_sel + 1) // blocks_step) * blocks_step
    mt = jnp.asarray(_importance_matrix(nbp, ncp), BF16)
    keys_step = PAGE_SIZE * p_cnt
    assert blocks_step <= 2 * LANES - KV_HEADS * HEAD_DIM
    buf = win_buf.shape[1]

    def page_spec(p):
        return pl.BlockSpec((1, KV_ROW // LANES, LANES, PAGE_SIZE),
                            lambda b, i, pt: (pt[b * n_pages + i * p_cnt + p], 0, 0, 0))

    def per_b(*shape):
        return pl.BlockSpec((1,) + shape, lambda b, i, pt: (b,) + (0,) * len(shape))

    def const(*shape):
        return pl.BlockSpec(shape, lambda b, i, pt: (0,) * len(shape))

    in_specs = [page_spec(p) for p in range(p_cnt)]
    in_specs += [per_b(2 * LANES, Q_COLS), per_b(ncp, KV_ROW), const(nbp, ncp),
                 per_b(t_new, KV_ROW), per_b(buf, KV_ROW), per_b(t_new, KV_ROW), per_b(t_new, LANES)]
    return pl.pallas_call(
        functools.partial(_nsa_sample_kernel, n_pages_step=p_cnt, n_steps=n_steps, t_new=t_new, past=past,
                          n_cmp=n_cmp, n_sel=n_sel, k_top=min(N_SEL, n_sel)),
        grid_spec=pltpu.PrefetchScalarGridSpec(
            num_scalar_prefetch=1, grid=(db, n_steps), in_specs=in_specs,
            out_specs=per_b(t_new, NSA_WIDTH),
            scratch_shapes=[pltpu.VMEM((keys_step, 4 * LANES), BF16),
                            pltpu.VMEM((nbp, Q_COLS), F32),
                            pltpu.VMEM((SUBLANES, Q_COLS), F32), pltpu.VMEM((SUBLANES, Q_COLS), F32),
                            pltpu.VMEM((Q_COLS, 2 * LANES), F32), pltpu.VMEM((Q_COLS, 2 * LANES), F32)]),
        out_shape=jax.ShapeDtypeStruct((db, t_new, NSA_WIDTH), F32),
        compiler_params=_params("arbitrary", "arbitrary"),
        name="nsa_sample",
    )(page_table.reshape(-1), *([cache_slc] * p_cnt), qbd, comp_tm, mt, knew, win_buf, wnew, gates)


def _out_kernel(x_ref, po_ref, zp_ref, no_ref, zn_ref, gate_ref, gnp_ref, gnn_ref, w_ref, fn_ref, y_ref):
    zp = zp_ref[0]
    zn = zn_ref[0]
    mp = _rms(po_ref[0], gnp_ref[...]) * (zp * _sigmoid(zp))
    mn = _rms(no_ref[0], gnn_ref[...]) * (zn * _sigmoid(zn))
    m = jnp.concatenate([mp, mn], axis=-1).astype(BF16)
    xo = x_ref[0] + gate_ref[0] * _dot(m, w_ref[...])
    y_ref[0] = _rms(xo, fn_ref[...])


def _out(x, pool_o, zp, nsa_o, zn, gate, gn_pool, gn_nsa, w_out_bf16, final_norm, tm):
    nb, t, d = x.shape
    r = gate.shape[1]
    assert t % tm == 0 and r in (1, t)
    if r == 1:
        gate_spec = pl.BlockSpec((1, 1, d), lambda b, i: (b, 0, 0))
    else:
        gate_spec = pl.BlockSpec((1, tm, d), lambda b, i: (b, i, 0))

    def tok(width):
        return pl.BlockSpec((1, tm, width), lambda b, i: (b, i, 0))

    def const(*shape):
        return pl.BlockSpec(shape, lambda b, i: (0,) * len(shape))

    return pl.pallas_call(
        _out_kernel,
        grid=(nb, t // tm),
        in_specs=[tok(d), tok(POOL_WIDTH), tok(POOL_WIDTH), tok(NSA_WIDTH), tok(NSA_WIDTH), gate_spec,
                  const(1, POOL_WIDTH), const(1, NSA_WIDTH), const(d, d), const(1, d)],
        out_specs=tok(d),
        out_shape=jax.ShapeDtypeStruct((nb, t, d), F32),
        compiler_params=_params("arbitrary", "arbitrary"),
        name="out",
    )(x, pool_o, zp, nsa_o, zn, gate, gn_pool.reshape(1, -1), gn_nsa.reshape(1, -1), w_out_bf16,
      final_norm.reshape(1, d))


def _pages_t(pages):
    n = pages.shape[0]
    return pages.transpose(0, 2, 3, 4, 1).reshape(n, KV_ROW // LANES, LANES, PAGE_SIZE)


def _block_diag(pool_w):
    n, c, _ = pool_w.shape
    eye = jnp.eye(n, dtype=pool_w.dtype)
    return (eye[:, None, :, None] * pool_w[:, :, None, :]).reshape(n * c, n * c)


def _first_layer(phi_w1, phi_pe):
    w = phi_w1.reshape(CMP_SPAN, CMP_STRIDE, 2, HEAD_DIM, CMP_HID)
    w1cat = w.transpose(2, 1, 3, 0, 4).reshape(2, CMP_STRIDE * HEAD_DIM, CMP_SPAN * CMP_HID)
    pe = phi_pe.reshape(CMP_SPAN, CMP_STRIDE, 2, HEAD_DIM).transpose(2, 0, 1, 3).reshape(2, CMP_SPAN, -1)
    pe2 = jnp.concatenate([pe, jnp.zeros((2, SUBLANES - CMP_SPAN, pe.shape[-1]), pe.dtype)], axis=1)
    return w1cat.astype(BF16), pe2


def kernel(x_prompt, x_sample, c_prompt, c_sample, cache_cmp_kv, cache_slc_kv, state_win_kv, state_pool, page_table, norm_w, w_ada, b_ada, w_in, pool_w, pool_scale, phi_w1, phi_pe, phi_w2, gn_pool, gn_nsa, w_out, final_norm):
    b_sz, s_len, d = x_prompt.shape
    db, t_new, _ = x_sample.shape
    depth = norm_w.shape[0]
    assert depth == 1 and w_in.shape[-1] == IN_WIDTH and d // 4 == POOL_WIDTH
    n_phys = cache_cmp_kv.shape[1]
    n_pages = page_table.shape[1]
    past = n_pages * PAGE_SIZE
    lyr = 0

    n_c = b_sz + db
    c_all = jnp.concatenate([c_prompt, c_sample, jnp.zeros((-n_c % SUBLANES, d), F32)], axis=0)
    ada = _ada(c_all, w_ada[lyr], b_ada[lyr])
    shift, scale, gate = ada[:, :d], ada[:, d:2 * d], ada[:, 2 * d:]

    w_pad = jnp.pad(w_in[lyr].astype(BF16), ((0, 0), (0, IN_PAD - IN_WIDTH)))
    w_out_bf16 = w_out[lyr].astype(BF16)
    wbd = _block_diag(pool_w[lyr]).astype(BF16)
    w1cat, pe2 = _first_layer(phi_w1[lyr], phi_pe[lyr])
    w2 = phi_w2[lyr].astype(BF16)
    n_tok_s = db * t_new

    def per_token(v):
        return jnp.repeat(v[b_sz:n_c], t_new, axis=0)[None]

    tm = min(512, s_len)
    (u_p, zp_p, zn_p, kvc_p, kvs_p, kvw_p, q_t, ks_aug, kw_p, vs_t, vw_t, g_t) = _in_proj(
        x_prompt, scale[:b_sz, None], shift[:b_sz, None], norm_w[lyr], w_pad, tm, True)
    pool_p = _pool(jnp.zeros((b_sz, POOL_HALO, POOL_WIDTH), F32), u_p, wbd, pool_scale[lyr], 0, tm)
    pages_p = s_len // PAGE_SIZE
    ident = jnp.arange(b_sz * pages_p, dtype=jnp.int32).reshape(b_sz, pages_p)
    comp_k, comp_vt = _compress(_pages_t(kvc_p.reshape(b_sz * pages_p, PAGE_SIZE, 2, KV_HEADS, HEAD_DIM)), ident,
                                w1cat, pe2, w2, None, False)
    nsa_p = _nsa_prompt(q_t, comp_k, comp_vt, ks_aug, vs_t, kw_p, vw_t, g_t, tq=512, kb=min(512, s_len))
    y_p = _out(x_prompt, pool_p, zp_p, nsa_p, zn_p, gate[:b_sz, None], gn_pool[lyr], gn_nsa[lyr], w_out_bf16,
               final_norm, tm)

    (u_s, zp_s, zn_s, kvc_s, kvs_s, kvw_s, q_s, g_s) = _in_proj(
        x_sample.reshape(1, n_tok_s, d), per_token(scale), per_token(shift), norm_w[lyr], w_pad, n_tok_s, False)
    u_s3 = u_s.reshape(db, t_new, POOL_WIDTH)
    hist = jnp.concatenate([jnp.zeros((db, POOL_HALO - state_pool.shape[2], POOL_WIDTH), F32), state_pool[lyr]], axis=1)
    pool_s = _pool(hist, u_s3, wbd, pool_scale[lyr], past, t_new)
    kvc_s3 = kvc_s.reshape(db, t_new, KV_ROW)
    kvs_s3 = kvs_s.reshape(db, t_new, KV_ROW)
    kvw_s3 = kvw_s.reshape(db, t_new, KV_ROW)
    xnew = jnp.pad(kvc_s3, ((0, 0), (0, CMP_STRIDE - t_new), (0, 0)))
    xnew = xnew.reshape(db, CMP_STRIDE, N_KV_SLABS, HEAD_DIM).transpose(0, 2, 1, 3).reshape(db, N_KV_SLABS, -1)
    comp_s = _compress(_pages_t(cache_cmp_kv[lyr]), page_table, w1cat, pe2, w2, xnew, True)
    q5 = q_s.reshape(db, t_new, KV_HEADS, GQA_GROUP, HEAD_DIM)
    eye = jnp.eye(KV_HEADS, dtype=BF16)
    qbd = q5.transpose(0, 2, 4, 3, 1)[:, :, :, :, None, :] * eye[None, :, None, None, :, None]
    qbd = jnp.pad(qbd.reshape(db, KV_HEADS * HEAD_DIM, GQA_GROUP, KV_HEADS * t_new),
                  ((0, 0), (0, 2 * LANES - KV_HEADS * HEAD_DIM), (0, 0), (0, COLS_PER_HEAD - KV_HEADS * t_new)))
    qbd = qbd.reshape(db, 2 * LANES, Q_COLS)
    g_s3 = g_s.reshape(db, t_new, LANES)
    win_buf = state_win_kv[lyr].reshape(db, -1, KV_ROW)
    nsa_s = _nsa_sample(_pages_t(cache_slc_kv[lyr]), page_table, qbd, comp_s, kvs_s3,
                        win_buf, kvw_s3, g_s3, past)
    y_s = _out(x_sample.reshape(1, n_tok_s, d), pool_s.reshape(1, n_tok_s, POOL_WIDTH), zp_s,
               nsa_s.reshape(1, n_tok_s, NSA_WIDTH), zn_s, per_token(gate), gn_pool[lyr], gn_nsa[lyr], w_out_bf16,
               final_norm, n_tok_s).reshape(db, t_new, d)

    kv_shape = (2, KV_HEADS, HEAD_DIM)
    win_len = min(WINDOW, s_len)
    buf = win_buf.shape[1]
    new_win_s = jnp.concatenate([win_buf, kvw_s3], axis=1)[:, -buf:]
    hist_len = state_pool.shape[2]
    new_pool_s = jnp.concatenate([state_pool[lyr], u_s3], axis=1)[:, -hist_len:]
    return (y_p, y_s,
            kvc_p.reshape((1, b_sz, s_len) + kv_shape), kvc_s3.reshape((1, db, t_new) + kv_shape),
            kvs_p.reshape((1, b_sz, s_len) + kv_shape), kvs_s3.reshape((1, db, t_new) + kv_shape),
            kvw_p[:, s_len - win_len:].reshape((1, b_sz, win_len) + kv_shape),
            new_win_s.reshape((1, db, buf) + kv_shape),
            u_p[:, s_len - hist_len:][None], new_pool_s[None])
```

```python
import functools

import numpy as np
import jax
import jax.numpy as jnp
from jax import lax
from jax.experimental import pallas as pl
from jax.experimental.pallas import tpu as pltpu

F32 = jnp.float32
BF16 = jnp.bfloat16

HEAD_DIM = 64
GQA_GROUP = 4
KV_HEADS = 3
N_HEADS = KV_HEADS * GQA_GROUP
N_KV_SLABS = 2 * KV_HEADS
KV_ROW = N_KV_SLABS * HEAD_DIM
N_BRANCH = 3
POOL_WINDOWS = (2, 4, 8, 16)
POOL_HALO = 16
CMP_BLOCK = 32
CMP_STRIDE = 16
CMP_SPAN = CMP_BLOCK // CMP_STRIDE
CMP_HID = 2 * HEAD_DIM
SEL_BLOCK = 64
N_SEL = 16
WINDOW = 512
PAGE_SIZE = 128
CHUNKS_PER_PAGE = PAGE_SIZE // CMP_STRIDE
CHUNK_PITCH = 24
EPS = 1e-6
NEG = -1e30
FORCE = 1e4
LOG2_E = 1.4426950408889634
VT_ROWS = 80
PAD_SCORE = -1e38
TAKEN_SCORE = -3e38

LANES = 128
SUBLANES = 8
VMEM_LIMIT_BYTES = 56 * 1024 * 1024


def _dot(a, b):
    return jnp.dot(a, b, preferred_element_type=F32)


def _dot_nt(a, b):
    return lax.dot_general(a, b, (((1,), (1,)), ((), ())), preferred_element_type=F32)


def _dot_tn(a, b):
    return lax.dot_general(a, b, (((0,), (0,)), ((), ())), preferred_element_type=F32)


def _sigmoid(v):
    return 1.0 / (1.0 + jnp.exp(-v))


def _rms(v, g):
    return v * lax.rsqrt(jnp.mean(v * v, axis=-1, keepdims=True) + EPS) * g


def _split3(v):
    hi = v.astype(BF16)
    r1 = v - hi.astype(F32)
    mid = r1.astype(BF16)
    lo = (r1 - mid.astype(F32)).astype(BF16)
    return hi, mid, lo


def _params(*sem):
    return pltpu.CompilerParams(dimension_semantics=sem, vmem_limit_bytes=VMEM_LIMIT_BYTES)


def _ada_kernel(c_ref, w_ref, b_ref, o_ref):
    c = c_ref[...]
    a = (c * _sigmoid(c)).astype(BF16)
    o_ref[...] = _dot(a, w_ref[...].astype(BF16)) + b_ref[...]


def _ada(c, w_ada, b_ada):
    m, d = c.shape
    n = w_ada.shape[1]
    tn = 512
    return pl.pallas_call(
        _ada_kernel,
        grid=(n // tn,),
        in_specs=[pl.BlockSpec((m, d), lambda j: (0, 0)),
                  pl.BlockSpec((d, tn), lambda j: (0, j)),
                  pl.BlockSpec((1, tn), lambda j: (0, j))],
        out_specs=pl.BlockSpec((m, tn), lambda j: (0, j)),
        out_shape=jax.ShapeDtypeStruct((m, n), F32),
        compiler_params=_params("arbitrary"),
        name="ada",
    )(c, w_ada, b_ada.reshape(1, n))


POOL_WIDTH = 256
NSA_WIDTH = N_HEADS * HEAD_DIM
GATE_COLS = N_BRANCH * N_HEADS
MAX_SEL_BLOCKS = LANES - HEAD_DIM
_SEG = {}
_off = 0
for _name, _w in (("u", POOL_WIDTH), ("zp", POOL_WIDTH), ("q", NSA_WIDTH), ("zn", NSA_WIDTH),
                  ("kvc", KV_ROW), ("kvs", KV_ROW), ("kvw", KV_ROW), ("gl", LANES)):
    _SEG[_name] = (_off, _w)
    _off += _w
IN_PAD = _off
IN_WIDTH = IN_PAD - LANES + GATE_COLS


GATE_ROWS = 16


def _in_proj_kernel(x_ref, sc_ref, sh_ref, nw_ref, w_ref, u_ref, zp_ref, zn_ref, kvc_ref, kvs_ref, kvw_ref, *rest,
                    attn_layouts):
    tm = x_ref.shape[1]
    h = (_rms(x_ref[0], nw_ref[...]) * (1.0 + sc_ref[0]) + sh_ref[0]).astype(BF16)

    def seg(name):
        s, w = _SEG[name]
        return _dot(h, w_ref[:, s:s + w])

    u_ref[0] = seg("u")
    zp_ref[0] = seg("zp")
    zn_ref[0] = seg("zn")
    q = seg("q") * (HEAD_DIM ** -0.5 * LOG2_E)
    kvc = seg("kvc")
    kvc_ref[0] = kvc
    kvs = seg("kvs")
    kvs_ref[0] = kvs
    kvw = seg("kvw")
    kvw_ref[0] = kvw
    sig = _sigmoid(seg("gl"))
    if not attn_layouts:
        q_ref, g_ref = rest
        q_ref[0] = q.astype(BF16)
        g_ref[0] = sig
        return

    qt_ref, ks_aug_ref, kw_ref, vst_ref, vwt_ref, gt_ref, kvct_ref = rest
    qt_ref[0] = q.T.astype(BF16).reshape(N_HEADS, HEAD_DIM, tm)
    kvct_ref[0] = kvc.T.reshape(KV_ROW // LANES, LANES, tm)
    pos = pl.program_id(1) * tm + lax.broadcasted_iota(jnp.int32, (tm, 1), 0)
    onehot = (lax.broadcasted_iota(jnp.int32, (1, MAX_SEL_BLOCKS), 1) == pos // SEL_BLOCK).astype(BF16)
    kvs_t = kvs.T.astype(BF16)
    kvw_t = kvw.T.astype(BF16)
    v0 = KV_HEADS * HEAD_DIM
    tail = (lax.broadcasted_iota(jnp.int32, (VT_ROWS - HEAD_DIM, LANES), 0) == 0).astype(BF16)
    for g in range(KV_HEADS):
        ks_aug_ref[0, g] = jnp.concatenate([kvs[:, g * HEAD_DIM:(g + 1) * HEAD_DIM].astype(BF16), onehot], axis=1)
        kw_ref[0, g] = kvw[:, g * HEAD_DIM:(g + 1) * HEAD_DIM].astype(BF16)
        for c in range(tm // LANES):
            sl = (slice(v0 + g * HEAD_DIM, v0 + (g + 1) * HEAD_DIM), slice(c * LANES, (c + 1) * LANES))
            vst_ref[0, g, c] = jnp.concatenate([kvs_t[sl], tail], axis=0)
            vwt_ref[0, g, c] = jnp.concatenate([kvw_t[sl], tail], axis=0)
        per_group = GQA_GROUP * N_BRANCH
        rolled = sig if g == 0 else pltpu.roll(sig, LANES - per_group * g, 1)
        gt_ref[0, g] = rolled.T[:GATE_ROWS, :]


def _in_proj(x, scale, shift, norm_w, w_pad, tm, attn_layouts):
    nb, t, d = x.shape
    r = scale.shape[1]
    assert t % tm == 0 and r in (1, t) and (tm % LANES == 0 or not attn_layouts)
    if r == 1:
        mod_spec = pl.BlockSpec((1, 1, d), lambda b, i: (b, 0, 0))
    else:
        mod_spec = pl.BlockSpec((1, tm, d), lambda b, i: (b, i, 0))

    def tok(width):
        return pl.BlockSpec((1, tm, width), lambda b, i: (b, i, 0))

    def hm(n, width=HEAD_DIM):
        return pl.BlockSpec((1, n, tm, width), lambda b, i: (b, 0, i, 0))

    def sd(*shape, dtype=F32):
        return jax.ShapeDtypeStruct(shape, dtype)

    out_specs = [tok(POOL_WIDTH), tok(POOL_WIDTH), tok(NSA_WIDTH), tok(KV_ROW), tok(KV_ROW), tok(KV_ROW)]
    out_shape = [sd(nb, t, POOL_WIDTH), sd(nb, t, POOL_WIDTH), sd(nb, t, NSA_WIDTH),
                 sd(nb, t, KV_ROW), sd(nb, t, KV_ROW), sd(nb, t, KV_ROW)]
    if attn_layouts:
        lane_tiles = tm // LANES
        vt_spec = pl.BlockSpec((1, KV_HEADS, lane_tiles, VT_ROWS, LANES), lambda b, i: (b, 0, i, 0, 0))
        vt_shape = sd(nb, KV_HEADS, t // LANES, VT_ROWS, LANES, dtype=BF16)
        out_specs += [pl.BlockSpec((1, N_HEADS, HEAD_DIM, tm), lambda b, i: (b, 0, 0, i)),
                      hm(KV_HEADS, HEAD_DIM + MAX_SEL_BLOCKS), hm(KV_HEADS), vt_spec, vt_spec,
                      pl.BlockSpec((1, KV_HEADS, GATE_ROWS, tm), lambda b, i: (b, 0, 0, i)),
                      pl.BlockSpec((1, KV_ROW // LANES, LANES, tm), lambda b, i: (b, 0, 0, i))]
        out_shape += [sd(nb, N_HEADS, HEAD_DIM, t, dtype=BF16),
                      sd(nb, KV_HEADS, t, HEAD_DIM + MAX_SEL_BLOCKS, dtype=BF16),
                      sd(nb, KV_HEADS, t, HEAD_DIM, dtype=BF16), vt_shape, vt_shape,
                      sd(nb, KV_HEADS, GATE_ROWS, t), sd(nb, KV_ROW // LANES, LANES, t)]
    else:
        out_specs += [tok(NSA_WIDTH), tok(LANES)]
        out_shape += [sd(nb, t, NSA_WIDTH, dtype=BF16), sd(nb, t, LANES)]
    return pl.pallas_call(
        functools.partial(_in_proj_kernel, attn_layouts=attn_layouts),
        grid=(nb, t // tm),
        in_specs=[tok(d), mod_spec, mod_spec,
                  pl.BlockSpec((1, d), lambda b, i: (0, 0)),
                  pl.BlockSpec((d, IN_PAD), lambda b, i: (0, 0))],
        out_specs=out_specs,
        out_shape=out_shape,
        compiler_params=_params("arbitrary", "arbitrary"),
        name="in_proj",
    )(x, scale, shift, norm_w.reshape(1, d), w_pad)


def _pool_kernel(hist_ref, uprev_ref, u_ref, wbd_ref, ps_ref, o_ref, *, pos0, tp):
    i = pl.program_id(1)
    u = u_ref[0]
    halo = jnp.where(i == 0, hist_ref[0], uprev_ref[0])
    ue = jnp.concatenate([halo, u], axis=0)
    sums = []
    s = ue
    for w in POOL_WINDOWS:
        s = s + pltpu.roll(s, w // 2, 0)
        sums.append(s[POOL_HALO:])
    c = u.shape[-1]
    grp = lax.broadcasted_iota(jnp.int32, (1, c), 1) // (c // len(POOL_WINDOWS))
    pos = pos0 + i * tp + lax.broadcasted_iota(jnp.int32, (tp, 1), 0)
    tot = sums[-1]
    win = jnp.full((1, c), float(POOL_WINDOWS[-1]), F32)
    for gi in range(len(POOL_WINDOWS) - 2, -1, -1):
        tot = jnp.where(grp == gi, sums[gi], tot)
        win = jnp.where(grp == gi, float(POOL_WINDOWS[gi]), win)
    cnt = jnp.minimum(win, (pos + 1).astype(F32))
    dlt = tot / cnt - u
    o_ref[0] = _dot(dlt.astype(BF16), wbd_ref[...]) * ps_ref[...]


def _pool(hist, u, wbd, pool_scale, pos0, tp):
    nb, t, c = u.shape
    assert t % tp == 0
    if t >= POOL_HALO:
        assert tp % POOL_HALO == 0
        uprev, ratio = u, tp // POOL_HALO
        prev_spec = pl.BlockSpec((1, POOL_HALO, c), lambda b, i: (b, jnp.maximum(i * ratio - 1, 0), 0))
    else:
        assert t == tp
        uprev = hist
        prev_spec = pl.BlockSpec((1, POOL_HALO, c), lambda b, i: (b, 0, 0))
    return pl.pallas_call(
        functools.partial(_pool_kernel, pos0=pos0, tp=tp),
        grid=(nb, t // tp),
        in_specs=[pl.BlockSpec((1, POOL_HALO, c), lambda b, i: (b, 0, 0)), prev_spec,
                  pl.BlockSpec((1, tp, c), lambda b, i: (b, i, 0)),
                  pl.BlockSpec((c, c), lambda b, i: (0, 0)),
                  pl.BlockSpec((1, c), lambda b, i: (0, 0))],
        out_specs=pl.BlockSpec((1, tp, c), lambda b, i: (b, i, 0)),
        out_shape=jax.ShapeDtypeStruct((nb, t, c), F32),
        compiler_params=_params("arbitrary", "arbitrary"),
        name="pool",
    )(hist, uprev, u, wbd, pool_scale.reshape(1, c))


def _gelu_tanh(v):
    return 0.5 * v * (1.0 + jnp.tanh(np.sqrt(2.0 / np.pi).astype(np.float32) * (v + 0.044715 * (v * v * v))))


def _compress_kernel(pt_ref, *refs, n_pages_step, n_steps, has_new, token_major):
    del pt_ref
    p_cnt = n_pages_step
    page_refs = refs[:p_cnt]
    w1p_ref, w1_ref, pe_ref, w2_ref = refs[p_cnt:p_cnt + 4]
    k = p_cnt + 4
    xnew_ref = None
    if has_new:
        xnew_ref = refs[k]
        k += 1
    if token_major:
        out_ref = refs[k]
        k += 1
    else:
        outk_ref, outvt_ref = refs[k:k + 2]
        k += 2
    slab_scr, carry_scr, bias_scr = refs[k:k + 3]
    i = pl.program_id(1)
    rows = CHUNKS_PER_PAGE * p_cnt
    row_id = lax.broadcasted_iota(jnp.int32, (rows, 1), 0)

    @pl.when(i == 0)
    def _():
        carry_scr[...] = jnp.zeros_like(carry_scr)
        for kk in range(2):
            pb = _dot(pe_ref[kk].astype(BF16), w1_ref[kk])
            bias_scr[kk] = pb[0:1, :CMP_HID] + pb[1:2, CMP_HID:]

    def store(kg, comp):
        if token_major:
            out_ref[0, :, kg * HEAD_DIM:(kg + 1) * HEAD_DIM] = comp.astype(BF16)
        elif kg < KV_HEADS:
            outk_ref[0, kg] = comp.astype(BF16)
        else:
            wide = jnp.concatenate([comp, jnp.zeros((rows, LANES - HEAD_DIM), F32)], axis=1)
            outvt_ref[0, kg - KV_HEADS] = wide.T[:HEAD_DIM, :].astype(BF16)

    def finish(kg, a0_prev, a1):
        kk = kg // KV_HEADS
        pre = a0_prev + a1 + bias_scr[kk]
        return _dot(_gelu_tanh(pre).astype(BF16), w2_ref[kk])

    def main():
        for p in range(p_cnt):
            for cb in range(KV_ROW // LANES):
                rows_t = page_refs[p][0, cb].astype(BF16).T.astype(F32)
                for ch in range(CHUNKS_PER_PAGE):
                    r0 = (p * CHUNKS_PER_PAGE + ch) * CHUNK_PITCH
                    slab_scr[cb, r0:r0 + CMP_STRIDE, :] = rows_t[ch * CMP_STRIDE:(ch + 1) * CMP_STRIDE, :]
        for cb in range(KV_ROW // LANES):
            x = jnp.concatenate([slab_scr[cb, pl.ds(j, rows, stride=CHUNK_PITCH), :] for j in range(CMP_STRIDE)],
                                axis=1).astype(BF16)
            a_both = _dot(x, w1p_ref[cb])
            for half in range(2):
                kg = 2 * cb + half
                a = a_both[:, half * 2 * CMP_HID:(half + 1) * 2 * CMP_HID]
                a0 = a[:, :CMP_HID]
                a0_prev = jnp.where(row_id == 0, carry_scr[kg], pltpu.roll(a0, 1, 0))
                carry_scr[kg] = a0[rows - 1:rows, :]
                store(kg, finish(kg, a0_prev, a[:, CMP_HID:]))

    if not has_new:
        main()
    else:
        pl.when(i < n_steps)(main)

        @pl.when(i == n_steps)
        def _():
            for kg in range(N_KV_SLABS):
                xk = jnp.broadcast_to(xnew_ref[0, kg:kg + 1, :], (SUBLANES, CMP_STRIDE * HEAD_DIM)).astype(BF16)
                a1 = _dot(xk, w1_ref[kg // KV_HEADS])[0:1, CMP_HID:]
                comp = finish(kg, carry_scr[kg], a1)
                store(kg, jnp.where(row_id == 0, jnp.broadcast_to(comp, (rows, HEAD_DIM)), 0.0))


def _compress(pages, page_table, w1cat, pe2, w2, xnew, token_major, per_batch):
    nb, n_pages = page_table.shape
    p_cnt = min(32, n_pages)
    assert n_pages % p_cnt == 0
    n_steps = n_pages // p_cnt
    has_new = xnew is not None
    rows = CHUNKS_PER_PAGE * p_cnt
    grid_steps = n_steps + (1 if has_new else 0)
    tot_rows = rows * grid_steps
    xw = CMP_STRIDE * HEAD_DIM

    def page_spec(p):
        def imap(b, i, pt):
            step = jnp.minimum(i, n_steps - 1)
            page = pt[b * n_pages + step * p_cnt + p]
            return (b, 0, 0, page) if per_batch else (page, 0, 0, 0)
        return pl.BlockSpec((1, KV_ROW // LANES, LANES, PAGE_SIZE), imap)

    n_cb = KV_ROW // LANES
    w4 = w1cat.reshape(2, CMP_STRIDE, HEAD_DIM, 2 * CMP_HID)
    zero = jnp.zeros_like(w4[0])
    w1p = jnp.stack([jnp.concatenate([jnp.concatenate([w4[(2 * cb) // KV_HEADS], zero], axis=2),
                                      jnp.concatenate([zero, w4[(2 * cb + 1) // KV_HEADS]], axis=2)], axis=1)
                     for cb in range(n_cb)]).reshape(n_cb, CMP_STRIDE * LANES, 4 * CMP_HID)
    in_specs = [page_spec(p) for p in range(p_cnt)]
    in_specs += [pl.BlockSpec((n_cb, CMP_STRIDE * LANES, 4 * CMP_HID), lambda b, i, pt: (0, 0, 0)),
                 pl.BlockSpec((2, xw, 2 * CMP_HID), lambda b, i, pt: (0, 0, 0)),
                 pl.BlockSpec((2, SUBLANES, xw), lambda b, i, pt: (0, 0, 0)),
                 pl.BlockSpec((2, CMP_HID, HEAD_DIM), lambda b, i, pt: (0, 0, 0))]
    args = [pages] * p_cnt + [w1p, w1cat, pe2, w2]
    if has_new:
        in_specs.append(pl.BlockSpec((1, N_KV_SLABS, xw), lambda b, i, pt: (b, 0, 0)))
        args.append(xnew)
    if token_major:
        out_spec = pl.BlockSpec((1, rows, KV_ROW), lambda b, i, pt: (b, i, 0))
        out_shape = jax.ShapeDtypeStruct((nb, tot_rows, KV_ROW), BF16)
    else:
        assert rows % LANES == 0
        out_spec = [pl.BlockSpec((1, KV_HEADS, rows, HEAD_DIM), lambda b, i, pt: (b, 0, i, 0)),
                    pl.BlockSpec((1, KV_HEADS, HEAD_DIM, rows), lambda b, i, pt: (b, 0, 0, i))]
        out_shape = [jax.ShapeDtypeStruct((nb, KV_HEADS, tot_rows, HEAD_DIM), BF16),
                     jax.ShapeDtypeStruct((nb, KV_HEADS, HEAD_DIM, tot_rows), BF16)]
    return pl.pallas_call(
        functools.partial(_compress_kernel, n_pages_step=p_cnt, n_steps=n_steps, has_new=has_new,
                          token_major=token_major),
        grid_spec=pltpu.PrefetchScalarGridSpec(
            num_scalar_prefetch=1, grid=(nb, grid_steps), in_specs=in_specs, out_specs=out_spec,
            scratch_shapes=[pltpu.VMEM((n_cb, rows * CHUNK_PITCH, LANES), F32),
                            pltpu.VMEM((N_KV_SLABS, 1, CMP_HID), F32),
                            pltpu.VMEM((2, 1, CMP_HID), F32)]),
        out_shape=out_shape,
        compiler_params=_params("arbitrary", "arbitrary"),
        name="compress_new" if has_new else "compress",
    )(page_table.reshape(-1), *args)


def _softmax_parts(s, valid, axis):
    s = jnp.where(valid, s, NEG)
    m = jnp.max(s, axis=axis, keepdims=True)
    e = jnp.where(valid, jnp.exp2(s - m), 0.0)
    return e, jnp.sum(e, axis=axis, keepdims=True)


def _nsa_prompt_kernel(qt_ref, kc_ref, vct_ref, ks_ref, vst_ref, kw_ref, vwt_ref, gt_ref, mt_ref, o_ref, *,
                       tq, kb, n_cmp, k_top):
    qi = pl.program_id(2)
    qs = qi * tq
    r4 = GQA_GROUP
    cols = r4 * tq
    qt = jnp.concatenate([qt_ref[0, r] for r in range(r4)], axis=1)
    tpos = qs + lax.broadcasted_iota(jnp.int32, (1, tq), 1)

    def per_head(v):
        return jnp.concatenate([v] * r4, axis=1)

    def value_tiles(ref, k0, width):
        j0 = k0 // LANES
        return jnp.concatenate([ref[0, 0, j0 + j] for j in range(width // LANES)], axis=1)

    def normalise(acc):
        return acc[:HEAD_DIM] * (1.0 / acc[HEAD_DIM:HEAD_DIM + 1])

    nc = kc_ref.shape[2]
    s = _dot(kc_ref[0, 0], qt)
    cid = lax.broadcasted_iota(jnp.int32, (nc, 1), 0)
    valid = (cid >= 1) & (cid <= n_cmp) & (cid * CMP_STRIDE + (CMP_BLOCK - CMP_STRIDE - 1) <= tpos)
    s = s + per_head(jnp.where(valid, 0.0, NEG))
    e = jnp.exp2(s - jnp.max(s, axis=0, keepdims=True))
    den = jnp.sum(e, axis=0, keepdims=True)
    has_key = per_head(jnp.where(tpos >= CMP_BLOCK - 1, 1.0, 0.0)) if n_cmp >= 1 else 0.0
    p = e * (has_key / jnp.maximum(den, 1e-30))
    o_c = _dot(vct_ref[0, 0], p.astype(BF16))
    pg = p[:, 0:tq]
    for r in range(1, r4):
        pg = pg + p[:, r * tq:(r + 1) * tq]

    mt = mt_ref[...]
    imp = sum(_dot(mt, piece) for piece in _split3(pg))
    n_blk = imp.shape[0]
    blk = lax.broadcasted_iota(jnp.int32, (n_blk, 1), 0)
    cur = (qs + lax.broadcasted_iota(jnp.int32, (1, tq), 1)) // SEL_BLOCK
    score = jnp.where(blk <= cur, imp, -1.0)
    score = jnp.where((blk == 0) | (blk == cur) | (blk == cur - 1), FORCE, score)
    sub = lax.broadcasted_iota(jnp.int32, (SUBLANES, 1), 0)
    tiles = [score[v * SUBLANES:(v + 1) * SUBLANES, :] for v in range(n_blk // SUBLANES)]
    ranks = [jnp.zeros((SUBLANES, tq), F32) for _ in tiles]
    for j in range(n_blk):
        sj = score[j:j + 1, :]
        for v, sc in enumerate(tiles):
            if v > j // SUBLANES:
                ahead = sj >= sc
            elif v < j // SUBLANES:
                ahead = sj > sc
            else:
                ahead = (sj > sc) | ((sj == sc) & (sub > j % SUBLANES))
            ranks[v] = ranks[v] + jnp.where(ahead, 1.0, 0.0)
    rank = jnp.concatenate(ranks, axis=0)
    sel_bias = jnp.where((rank < float(k_top)) & (blk <= cur), 0.0, NEG).astype(BF16)
    q_aug = jnp.concatenate([qt, jnp.concatenate([sel_bias] * r4, axis=1)], axis=0)

    def scores(k0):
        return _dot(ks_ref[0, 0, pl.ds(k0, kb), :], q_aug)

    def absorb(sk, k0, state, diagonal):
        m_run, acc = state
        if diagonal:
            kpos = k0 + lax.broadcasted_iota(jnp.int32, (kb, 1), 0)
            sk = sk + per_head(jnp.where(kpos <= tpos, 0.0, NEG))
        m_new = jnp.maximum(m_run, jnp.max(sk, axis=0, keepdims=True))
        ek = jnp.exp2(sk - m_new)
        pv = _dot(value_tiles(vst_ref, k0, kb), ek.astype(BF16))
        return m_new, jnp.exp2(m_run - m_new) * acc + pv

    def tile(k0, state, diagonal):
        return absorb(scores(k0), k0, state, diagonal)

    init = (jnp.full((1, cols), NEG, F32), jnp.zeros((VT_ROWS, cols), F32))
    n_full = qs // kb
    state = lax.fori_loop(0, n_full, lambda kt, st: tile(pl.multiple_of(kt * kb, kb), st, False), init)
    o_s = normalise(tile(pl.multiple_of(n_full * kb, kb), state, True)[1])

    wl = kw_ref.shape[2] if kw_ref.shape[2] < WINDOW + tq else WINDOW + tq
    w0 = pl.multiple_of(jnp.maximum(qs + tq - wl, 0), tq)
    sw = _dot(kw_ref[0, 0, pl.ds(w0, wl), :], qt)
    dq = tpos - (w0 + lax.broadcasted_iota(jnp.int32, (wl, 1), 0))
    sw = sw + per_head(jnp.where((dq >= 0) & (dq < WINDOW), 0.0, NEG))
    ew = jnp.exp2(sw - jnp.max(sw, axis=0, keepdims=True))
    o_w = normalise(_dot(value_tiles(vwt_ref, w0, wl), ew.astype(BF16)))

    gt = gt_ref[0, 0]
    heads = []
    for r in range(r4):
        c0 = r * N_BRANCH
        sl = slice(r * tq, (r + 1) * tq)
        heads.append(gt[c0:c0 + 1] * o_c[:, sl] + gt[c0 + 1:c0 + 2] * o_s[:, sl] + gt[c0 + 2:c0 + 3] * o_w[:, sl])
    o_ref[0] = jnp.concatenate(heads, axis=0).T


def _importance_matrix(n_blk_pad, n_col):
    mt = np.zeros((n_blk_pad, n_col), np.float32)
    ratio = SEL_BLOCK // CMP_STRIDE
    for b in range(n_blk_pad):
        for c, w in zip(range(ratio * b, ratio * b + ratio + 1), (1.0,) + (2.0,) * (ratio - 1) + (1.0,)):
            if c < n_col:
                mt[b, c] = w
    return mt


def _nsa_prompt(q_t, comp_k, comp_vt, ks_aug, vs_t, kw, vw_t, gates_t, tq, kb):
    b_sz, _, _, s_len = q_t.shape
    nc = comp_k.shape[2]
    assert s_len % SEL_BLOCK == 0 and s_len % kb == 0 and kb % tq == 0 and s_len % tq == 0 and CMP_SPAN == 2
    assert tq % LANES == 0 and tq & (tq - 1) == 0
    n_blk = s_len // SEL_BLOCK
    assert n_blk <= MAX_SEL_BLOCKS
    n_cmp = s_len // CMP_STRIDE - CMP_SPAN + 1
    mt = jnp.asarray(_importance_matrix(MAX_SEL_BLOCKS, nc), BF16)

    def per_group(*shape):
        return pl.BlockSpec((1, 1) + shape, lambda b, g, i: (b, g) + (0,) * len(shape))

    return pl.pallas_call(
        functools.partial(_nsa_prompt_kernel, tq=tq, kb=kb, n_cmp=n_cmp, k_top=min(N_SEL, n_blk)),
        grid=(b_sz, KV_HEADS, s_len // tq),
        in_specs=[pl.BlockSpec((1, GQA_GROUP, HEAD_DIM, tq), lambda b, g, i: (b, g, 0, i)),
                  per_group(nc, HEAD_DIM), per_group(HEAD_DIM, nc),
                  per_group(s_len, HEAD_DIM + MAX_SEL_BLOCKS), per_group(s_len // LANES, VT_ROWS, LANES),
                  per_group(s_len, HEAD_DIM), per_group(s_len // LANES, VT_ROWS, LANES),
                  pl.BlockSpec((1, 1, GATE_ROWS, tq), lambda b, g, i: (b, g, 0, i)),
                  pl.BlockSpec((MAX_SEL_BLOCKS, nc), lambda b, g, i: (0, 0))],
        out_specs=pl.BlockSpec((1, tq, GQA_GROUP * HEAD_DIM), lambda b, g, i: (b, i, g)),
        out_shape=jax.ShapeDtypeStruct((b_sz, s_len, NSA_WIDTH), F32),
        compiler_params=_params("arbitrary", "arbitrary", "arbitrary"),
        name="nsa_prompt",
    )(q_t, comp_k, comp_vt, ks_aug, vs_t, kw, vw_t, gates_t, mt)


Q_COLS = LANES
COLS_PER_HEAD = Q_COLS // GQA_GROUP
V_OFF = LANES
V_COL0 = KV_HEADS * HEAD_DIM - V_OFF


def _nsa_sample_kernel(pt_ref, *refs, n_pages_step, n_steps, t_new, past, n_cmp, n_sel, k_top):
    del pt_ref
    p_cnt = n_pages_step
    page_refs = refs[:p_cnt]
    (qbd_ref, comp_ref, mt_ref, knew_ref, win_ref, wnew_ref, g_ref,
     o_ref, kv_scr, bias_scr, m_scr, l_scr, acc_scr, oc_scr) = refs[p_cnt:]
    i = pl.program_id(1)
    qbd = qbd_ref[0]
    col = lax.broadcasted_iota(jnp.int32, (1, Q_COLS), 1)
    tok = col & (SUBLANES - 1)
    qpos = past + tok
    kwin = slice(0, 2 * LANES)
    vwin = slice(V_OFF, V_OFF + 2 * LANES)
    k_lanes = KV_HEADS * HEAD_DIM
    blocks_per_page = PAGE_SIZE // SEL_BLOCK
    nb_step = blocks_per_page * p_cnt
    keys_step = PAGE_SIZE * p_cnt

    def attend(rows_bf16, valid):
        s = _dot(rows_bf16[:, kwin], qbd)
        return jnp.where(valid, s, NEG)

    @pl.when((pl.program_id(0) == 0) & (i == 0))
    def _():
        key_blk = lax.broadcasted_iota(jnp.int32, (keys_step, 1), 0) // SEL_BLOCK
        lane_blk = lax.broadcasted_iota(jnp.int32, (1, 2 * LANES - k_lanes), 1)
        kv_scr[:, k_lanes:2 * LANES] = (key_blk == lane_blk).astype(BF16)

    @pl.when(i == 0)
    def _():
        comp = comp_ref[0]
        ncp = comp.shape[0]
        cid = lax.broadcasted_iota(jnp.int32, (ncp, 1), 0)
        valid = (cid >= 1) & (cid <= n_cmp) & (cid * CMP_STRIDE + (CMP_BLOCK - CMP_STRIDE - 1) <= qpos)
        s = attend(comp, valid)
        e, den = _softmax_parts(s, valid, 0)
        p = e * (1.0 / jnp.maximum(den, 1e-30))
        oc_scr[...] = _dot_tn(p.astype(BF16), comp[:, vwin])
        mt = mt_ref[...]
        imp = sum(_dot(mt, piece) for piece in _split3(p))
        imp = imp + pltpu.roll(imp, COLS_PER_HEAD, 1)
        imp = imp + pltpu.roll(imp, 2 * COLS_PER_HEAD, 1)
        nbp = imp.shape[0]
        blk = lax.broadcasted_iota(jnp.int32, (nbp, 1), 0)
        cur = qpos // SEL_BLOCK
        score = jnp.where(blk <= cur, imp, -1.0)
        score = jnp.where((blk == 0) | (blk == cur) | (blk == cur - 1), FORCE, score)
        score = jnp.where(blk < n_sel, score, PAD_SCORE)
        blk_f = blk.astype(F32)
        bias = jnp.full((nbp, Q_COLS), NEG, F32)
        for _ in range(k_top):
            best = jnp.max(score, axis=0, keepdims=True)
            first = jnp.min(jnp.where(score == best, blk_f, float(nbp)), axis=0, keepdims=True)
            hit = blk_f == first
            bias = jnp.where(hit, 0.0, bias)
            score = jnp.where(hit, TAKEN_SCORE, score)
        bias_scr[...] = bias
        m_scr[...] = jnp.full_like(m_scr, NEG)
        l_scr[...] = jnp.zeros_like(l_scr)
        acc_scr[...] = jnp.zeros_like(acc_scr)

    def accumulate(s, v_rows):
        m_run = m_scr[0:1, :]
        m_new = jnp.maximum(m_run, jnp.max(s, axis=0, keepdims=True))
        alpha = jnp.exp2(m_run - m_new)
        e = jnp.exp2(s - m_new)
        l_scr[0:1, :] = alpha * l_scr[0:1, :] + jnp.sum(e, axis=0, keepdims=True)
        m_scr[0:1, :] = m_new
        alpha_col = jnp.broadcast_to(alpha, (SUBLANES, Q_COLS)).T[:, 0:1]
        acc_scr[...] = alpha_col * acc_scr[...] + _dot_tn(e.astype(BF16), v_rows)

    for p in range(p_cnt):
        rows_p = slice(p * PAGE_SIZE, (p + 1) * PAGE_SIZE)
        tiles = [page_refs[p][0, cb].astype(BF16).T for cb in range(KV_ROW // LANES)]
        kv_scr[rows_p, 0:LANES] = tiles[0]
        kv_scr[rows_p, LANES:k_lanes] = tiles[1][:, :k_lanes - LANES]
        kv_scr[rows_p, 2 * LANES:3 * LANES] = tiles[1]
        kv_scr[rows_p, 3 * LANES:4 * LANES] = tiles[2]
    bias_step = bias_scr[pl.ds(pl.multiple_of(i * nb_step, nb_step), nb_step), :].astype(BF16)
    pad = 2 * LANES - k_lanes - nb_step
    q_parts = [qbd[:k_lanes], bias_step] + ([jnp.zeros((pad, Q_COLS), BF16)] if pad else [])
    q_step = jnp.concatenate(q_parts, axis=0)
    accumulate(_dot(kv_scr[:, 0:2 * LANES], q_step), kv_scr[:, 2 * LANES:4 * LANES])

    @pl.when(i == n_steps - 1)
    def _():
        knew = knew_ref[0].astype(BF16)
        krow = lax.broadcasted_iota(jnp.int32, (t_new, 1), 0)
        new_blk = past // SEL_BLOCK
        s_new = _dot(knew[:, kwin], qbd) + bias_scr[new_blk:new_blk + 1, :]
        accumulate(jnp.where(krow <= tok, s_new, NEG), knew[:, vwin])
        o_s = acc_scr[...] * jnp.broadcast_to(1.0 / l_scr[0:1, :], (SUBLANES, Q_COLS)).T[:, 0:1]

        wbuf = win_ref[0].astype(BF16)
        wnew = wnew_ref[0].astype(BF16)
        buf = wbuf.shape[0]
        pos_b = past - buf + lax.broadcasted_iota(jnp.int32, (buf, 1), 0)
        pos_n = past + krow
        ok_b = (qpos - pos_b >= 0) & (qpos - pos_b < WINDOW) & (pos_b >= 0)
        ok_n = (qpos - pos_n >= 0) & (qpos - pos_n < WINDOW)
        s_b = attend(wbuf, ok_b)
        s_n = attend(wnew, ok_n)
        m_w = jnp.maximum(jnp.max(s_b, axis=0, keepdims=True), jnp.max(s_n, axis=0, keepdims=True))
        e_b = jnp.where(ok_b, jnp.exp2(s_b - m_w), 0.0)
        e_n = jnp.where(ok_n, jnp.exp2(s_n - m_w), 0.0)
        den = jnp.sum(e_b, axis=0, keepdims=True) + jnp.sum(e_n, axis=0, keepdims=True)
        o_w = _dot_tn(e_b.astype(BF16), wbuf[:, vwin]) + _dot_tn(e_n.astype(BF16), wnew[:, vwin])
        o_w = o_w * jnp.broadcast_to(1.0 / jnp.maximum(den, 1e-30), (SUBLANES, Q_COLS)).T[:, 0:1]
        o_c = oc_scr[...]

        gt = g_ref[0]
        for g in range(KV_HEADS):
            for r in range(GQA_GROUP):
                r0 = r * COLS_PER_HEAD + g * SUBLANES
                c0 = V_COL0 + g * HEAD_DIM
                gc = (g * GQA_GROUP + r) * N_BRANCH
                blk_o = [o[r0:r0 + t_new, c0:c0 + HEAD_DIM] for o in (o_c, o_s, o_w)]
                h0 = (g * GQA_GROUP + r) * HEAD_DIM
                o_ref[0, :, h0:h0 + HEAD_DIM] = (gt[:, gc:gc + 1] * blk_o[0] + gt[:, gc + 1:gc + 2] * blk_o[1]
                                                 + gt[:, gc + 2:gc + 3] * blk_o[2])


def _nsa_sample(cache_slc, page_table, qbd, comp_tm, knew, win_buf, wnew, gates, past):
    db, n_pages = page_table.shape
    t_new = knew.shape[1]
    assert t_new == SUBLANES and KV_HEADS * t_new <= COLS_PER_HEAD and past % SEL_BLOCK == 0 and t_new <= SEL_BLOCK
    p_cnt = min(32, n_pages)
    assert n_pages % p_cnt == 0
    n_steps = n_pages // p_cnt
    ncp = comp_tm.shape[1]
    n_str = -(-(past + t_new) // CMP_STRIDE)
    n_cmp = n_str - CMP_SPAN + 1
    n_sel = -(-(past + t_new) // SEL_BLOCK)
    blocks_step = (PAGE_SIZE // SEL_BLOCK) * p_cnt
    nbp = -(-(n_sel + 1) // blocks_step) * blocks_step
    mt = jnp.asarray(_importance_matrix(nbp, ncp), BF16)
    keys_step = PAGE_SIZE * p_cnt
    assert blocks_step <= 2 * LANES - KV_HEADS * HEAD_DIM
    buf = win_buf.shape[1]

    def page_spec(p):
        return pl.BlockSpec((1, KV_ROW // LANES, LANES, PAGE_SIZE),
                            lambda b, i, pt: (pt[b * n_pages + i * p_cnt + p], 0, 0, 0))

    def per_b(*shape):
        return pl.BlockSpec((1,) + shape, lambda b, i, pt: (b,) + (0,) * len(shape))

    def const(*shape):
        return pl.BlockSpec(shape, lambda b, i, pt: (0,) * len(shape))

    in_specs = [page_spec(p) for p in range(p_cnt)]
    in_specs += [per_b(2 * LANES, Q_COLS), per_b(ncp, KV_ROW), const(nbp, ncp),
                 per_b(t_new, KV_ROW), per_b(buf, KV_ROW), per_b(t_new, KV_ROW), per_b(t_new, LANES)]
    return pl.pallas_call(
        functools.partial(_nsa_sample_kernel, n_pages_step=p_cnt, n_steps=n_steps, t_new=t_new, past=past,
                          n_cmp=n_cmp, n_sel=n_sel, k_top=min(N_SEL, n_sel)),
        grid_spec=pltpu.PrefetchScalarGridSpec(
            num_scalar_prefetch=1, grid=(db, n_steps), in_specs=in_specs,
            out_specs=per_b(t_new, NSA_WIDTH),
            scratch_shapes=[pltpu.VMEM((keys_step, 4 * LANES), BF16),
                            pltpu.VMEM((nbp, Q_COLS), F32),
                            pltpu.VMEM((SUBLANES, Q_COLS), F32), pltpu.VMEM((SUBLANES, Q_COLS), F32),
                            pltpu.VMEM((Q_COLS, 2 * LANES), F32), pltpu.VMEM((Q_COLS, 2 * LANES), F32)]),
        out_shape=jax.ShapeDtypeStruct((db, t_new, NSA_WIDTH), F32),
        compiler_params=_params("arbitrary", "arbitrary"),
        name="nsa_sample",
    )(page_table.reshape(-1), *([cache_slc] * p_cnt), qbd, comp_tm, mt, knew, win_buf, wnew, gates)


def _out_kernel(x_ref, po_ref, zp_ref, no_ref, zn_ref, gate_ref, gnp_ref, gnn_ref, w_ref, fn_ref, y_ref):
    zp = zp_ref[0]
    zn = zn_ref[0]
    mp = _rms(po_ref[0], gnp_ref[...]) * (zp * _sigmoid(zp))
    mn = _rms(no_ref[0], gnn_ref[...]) * (zn * _sigmoid(zn))
    m = jnp.concatenate([mp, mn], axis=-1).astype(BF16)
    xo = x_ref[0] + gate_ref[0] * _dot(m, w_ref[...])
    y_ref[0] = _rms(xo, fn_ref[...])


def _out(x, pool_o, zp, nsa_o, zn, gate, gn_pool, gn_nsa, w_out_bf16, final_norm, tm):
    nb, t, d = x.shape
    r = gate.shape[1]
    assert t % tm == 0 and r in (1, t)
    if r == 1:
        gate_spec = pl.BlockSpec((1, 1, d), lambda b, i: (b, 0, 0))
    else:
        gate_spec = pl.BlockSpec((1, tm, d), lambda b, i: (b, i, 0))

    def tok(width):
        return pl.BlockSpec((1, tm, width), lambda b, i: (b, i, 0))

    def const(*shape):
        return pl.BlockSpec(shape, lambda b, i: (0,) * len(shape))

    return pl.pallas_call(
        _out_kernel,
        grid=(nb, t // tm),
        in_specs=[tok(d), tok(POOL_WIDTH), tok(POOL_WIDTH), tok(NSA_WIDTH), tok(NSA_WIDTH), gate_spec,
                  const(1, POOL_WIDTH), const(1, NSA_WIDTH), const(d, d), const(1, d)],
        out_specs=tok(d),
        out_shape=jax.ShapeDtypeStruct((nb, t, d), F32),
        compiler_params=_params("arbitrary", "arbitrary"),
        name="out",
    )(x, pool_o, zp, nsa_o, zn, gate, gn_pool.reshape(1, -1), gn_nsa.reshape(1, -1), w_out_bf16,
      final_norm.reshape(1, d))


def _pages_t(pages):
    n = pages.shape[0]
    return pages.transpose(0, 2, 3, 4, 1).reshape(n, KV_ROW // LANES, LANES, PAGE_SIZE)


def _block_diag(pool_w):
    n, c, _ = pool_w.shape
    eye = jnp.eye(n, dtype=pool_w.dtype)
    return (eye[:, None, :, None] * pool_w[:, :, None, :]).reshape(n * c, n * c)


def _first_layer(phi_w1, phi_pe):
    w = phi_w1.reshape(CMP_SPAN, CMP_STRIDE, 2, HEAD_DIM, CMP_HID)
    w1cat = w.transpose(2, 1, 3, 0, 4).reshape(2, CMP_STRIDE * HEAD_DIM, CMP_SPAN * CMP_HID)
    pe = phi_pe.reshape(CMP_SPAN, CMP_STRIDE, 2, HEAD_DIM).transpose(2, 0, 1, 3).reshape(2, CMP_SPAN, -1)
    pe2 = jnp.concatenate([pe, jnp.zeros((2, SUBLANES - CMP_SPAN, pe.shape[-1]), pe.dtype)], axis=1)
    return w1cat.astype(BF16), pe2


def kernel(x_prompt, x_sample, c_prompt, c_sample, cache_cmp_kv, cache_slc_kv, state_win_kv, state_pool, page_table, norm_w, w_ada, b_ada, w_in, pool_w, pool_scale, phi_w1, phi_pe, phi_w2, gn_pool, gn_nsa, w_out, final_norm):
    b_sz, s_len, d = x_prompt.shape
    db, t_new, _ = x_sample.shape
    depth = norm_w.shape[0]
    assert depth == 1 and w_in.shape[-1] == IN_WIDTH and d // 4 == POOL_WIDTH
    n_phys = cache_cmp_kv.shape[1]
    n_pages = page_table.shape[1]
    past = n_pages * PAGE_SIZE
    lyr = 0

    n_c = b_sz + db
    c_all = jnp.concatenate([c_prompt, c_sample, jnp.zeros((-n_c % SUBLANES, d), F32)], axis=0)
    ada = _ada(c_all, w_ada[lyr], b_ada[lyr])
    shift, scale, gate = ada[:, :d], ada[:, d:2 * d], ada[:, 2 * d:]

    w_pad = jnp.pad(w_in[lyr].astype(BF16), ((0, 0), (0, IN_PAD - IN_WIDTH)))
    w_out_bf16 = w_out[lyr].astype(BF16)
    wbd = _block_diag(pool_w[lyr]).astype(BF16)
    w1cat, pe2 = _first_layer(phi_w1[lyr], phi_pe[lyr])
    w2 = phi_w2[lyr].astype(BF16)
    n_tok_s = db * t_new

    def per_token(v):
        return jnp.repeat(v[b_sz:n_c], t_new, axis=0)[None]

    tm = min(512, s_len)
    (u_p, zp_p, zn_p, kvc_p, kvs_p, kvw_p, q_t, ks_aug, kw_p, vs_t, vw_t, g_t, kvc_t) = _in_proj(
        x_prompt, scale[:b_sz, None], shift[:b_sz, None], norm_w[lyr], w_pad, tm, True)
    pool_p = _pool(jnp.zeros((b_sz, POOL_HALO, POOL_WIDTH), F32), u_p, wbd, pool_scale[lyr], 0, tm)
    pages_p = s_len // PAGE_SIZE
    in_order = jnp.tile(jnp.arange(pages_p, dtype=jnp.int32), (b_sz, 1))
    comp_k, comp_vt = _compress(kvc_t, in_order, w1cat, pe2, w2, None, False, True)
    nsa_p = _nsa_prompt(q_t, comp_k, comp_vt, ks_aug, vs_t, kw_p, vw_t, g_t, tq=512, kb=min(512, s_len))
    y_p = _out(x_prompt, pool_p, zp_p, nsa_p, zn_p, gate[:b_sz, None], gn_pool[lyr], gn_nsa[lyr], w_out_bf16,
               final_norm, tm)

    (u_s, zp_s, zn_s, kvc_s, kvs_s, kvw_s, q_s, g_s) = _in_proj(
        x_sample.reshape(1, n_tok_s, d), per_token(scale), per_token(shift), norm_w[lyr], w_pad, n_tok_s, False)
    u_s3 = u_s.reshape(db, t_new, POOL_WIDTH)
    hist = jnp.concatenate([jnp.zeros((db, POOL_HALO - state_pool.shape[2], POOL_WIDTH), F32), state_pool[lyr]], axis=1)
    pool_s = _pool(hist, u_s3, wbd, pool_scale[lyr], past, t_new)
    kvc_s3 = kvc_s.reshape(db, t_new, KV_ROW)
    kvs_s3 = kvs_s.reshape(db, t_new, KV_ROW)
    kvw_s3 = kvw_s.reshape(db, t_new, KV_ROW)
    xnew = jnp.pad(kvc_s3, ((0, 0), (0, CMP_STRIDE - t_new), (0, 0)))
    xnew = xnew.reshape(db, CMP_STRIDE, N_KV_SLABS, HEAD_DIM).transpose(0, 2, 1, 3).reshape(db, N_KV_SLABS, -1)
    comp_s = _compress(_pages_t(cache_cmp_kv[lyr]), page_table, w1cat, pe2, w2, xnew, True, False)
    q5 = q_s.reshape(db, t_new, KV_HEADS, GQA_GROUP, HEAD_DIM)
    eye = jnp.eye(KV_HEADS, dtype=BF16)
    qbd = q5.transpose(0, 2, 4, 3, 1)[:, :, :, :, None, :] * eye[None, :, None, None, :, None]
    qbd = jnp.pad(qbd.reshape(db, KV_HEADS * HEAD_DIM, GQA_GROUP, KV_HEADS * t_new),
                  ((0, 0), (0, 2 * LANES - KV_HEADS * HEAD_DIM), (0, 0), (0, COLS_PER_HEAD - KV_HEADS * t_new)))
    qbd = qbd.reshape(db, 2 * LANES, Q_COLS)
    g_s3 = g_s.reshape(db, t_new, LANES)
    win_buf = state_win_kv[lyr].reshape(db, -1, KV_ROW)
    nsa_s = _nsa_sample(_pages_t(cache_slc_kv[lyr]), page_table, qbd, comp_s, kvs_s3,
                        win_buf, kvw_s3, g_s3, past)
    y_s = _out(x_sample.reshape(1, n_tok_s, d), pool_s.reshape(1, n_tok_s, POOL_WIDTH), zp_s,
               nsa_s.reshape(1, n_tok_s, NSA_WIDTH), zn_s, per_token(gate), gn_pool[lyr], gn_nsa[lyr], w_out_bf16,
               final_norm, n_tok_s).reshape(db, t_new, d)

    kv_shape = (2, KV_HEADS, HEAD_DIM)
    win_len = min(WINDOW, s_len)
    buf = win_buf.shape[1]
    new_win_s = jnp.concatenate([win_buf, kvw_s3], axis=1)[:, -buf:]
    hist_len = state_pool.shape[2]
    new_pool_s = jnp.concatenate([state_pool[lyr], u_s3], axis=1)[:, -hist_len:]
    return (y_p, y_s,
            kvc_p.reshape((1, b_sz, s_len) + kv_shape), kvc_s3.reshape((1, db, t_new) + kv_shape),
            kvs_p.reshape((1, b_sz, s_len) + kv_shape), kvs_s3.reshape((1, db, t_new) + kv_shape),
            kvw_p[:, s_len - win_len:].reshape((1, b_sz, win_len) + kv_shape),
            new_win_s.reshape((1, db, buf) + kv_shape),
            u_p[:, s_len - hist_len:][None], new_pool_s[None])
```

```python
import functools

import numpy as np
import jax
import jax.numpy as jnp
from jax import lax
from jax.experimental import pallas as pl
from jax.experimental.pallas import tpu as pltpu

F32 = jnp.float32
BF16 = jnp.bfloat16

HEAD_DIM = 64
GQA_GROUP = 4
KV_HEADS = 3
N_HEADS = KV_HEADS * GQA_GROUP
N_KV_SLABS = 2 * KV_HEADS
KV_ROW = N_KV_SLABS * HEAD_DIM
N_BRANCH = 3
POOL_WINDOWS = (2, 4, 8, 16)
POOL_HALO = 16
CMP_BLOCK = 32
CMP_STRIDE = 16
CMP_SPAN = CMP_BLOCK // CMP_STRIDE
CMP_HID = 2 * HEAD_DIM
SEL_BLOCK = 64
N_SEL = 16
WINDOW = 512
PAGE_SIZE = 128
CHUNKS_PER_PAGE = PAGE_SIZE // CMP_STRIDE
CHUNK_PITCH = 24
EPS = 1e-6
NEG = -1e30
FORCE = 1e4
LOG2_E = 1.4426950408889634
VT_ROWS = 80
PAD_SCORE = -1e38
TAKEN_SCORE = -3e38

LANES = 128
SUBLANES = 8
VMEM_LIMIT_BYTES = 56 * 1024 * 1024


def _dot(a, b):
    return jnp.dot(a, b, preferred_element_type=F32)


def _dot_nt(a, b):
    return lax.dot_general(a, b, (((1,), (1,)), ((), ())), preferred_element_type=F32)


def _dot_tn(a, b):
    return lax.dot_general(a, b, (((0,), (0,)), ((), ())), preferred_element_type=F32)


def _sigmoid(v):
    return 1.0 / (1.0 + jnp.exp(-v))


def _rms(v, g):
    return v * lax.rsqrt(jnp.mean(v * v, axis=-1, keepdims=True) + EPS) * g


def _split3(v):
    hi = v.astype(BF16)
    r1 = v - hi.astype(F32)
    mid = r1.astype(BF16)
    lo = (r1 - mid.astype(F32)).astype(BF16)
    return hi, mid, lo


def _params(*sem):
    return pltpu.CompilerParams(dimension_semantics=sem, vmem_limit_bytes=VMEM_LIMIT_BYTES)


def _ada_kernel(c_ref, w_ref, b_ref, o_ref):
    c = c_ref[...]
    a = (c * _sigmoid(c)).astype(BF16)
    o_ref[...] = _dot(a, w_ref[...].astype(BF16)) + b_ref[...]


def _ada(c, w_ada, b_ada):
    m, d = c.shape
    n = w_ada.shape[1]
    tn = 512
    return pl.pallas_call(
        _ada_kernel,
        grid=(n // tn,),
        in_specs=[pl.BlockSpec((m, d), lambda j: (0, 0)),
                  pl.BlockSpec((d, tn), lambda j: (0, j)),
                  pl.BlockSpec((1, tn), lambda j: (0, j))],
        out_specs=pl.BlockSpec((m, tn), lambda j: (0, j)),
        out_shape=jax.ShapeDtypeStruct((m, n), F32),
        compiler_params=_params("arbitrary"),
        name="ada",
    )(c, w_ada, b_ada.reshape(1, n))


POOL_WIDTH = 256
NSA_WIDTH = N_HEADS * HEAD_DIM
GATE_COLS = N_BRANCH * N_HEADS
MAX_SEL_BLOCKS = LANES - HEAD_DIM
_SEG = {}
_off = 0
for _name, _w in (("u", POOL_WIDTH), ("zp", POOL_WIDTH), ("q", NSA_WIDTH), ("zn", NSA_WIDTH),
                  ("kvc", KV_ROW), ("kvs", KV_ROW), ("kvw", KV_ROW)):
    _SEG[_name] = (_off, _w)
    _off += _w
IN_MAIN = _off
IN_WIDTH = IN_MAIN + GATE_COLS


GATE_ROWS = 16


def _in_proj_kernel(x_ref, sc_ref, sh_ref, nw_ref, w_ref, wg_ref, u_ref, zp_ref, zn_ref, *rest, attn_layouts):
    tm = x_ref.shape[1]
    h = (_rms(x_ref[0], nw_ref[...]) * (1.0 + sc_ref[0]) + sh_ref[0]).astype(BF16)

    def seg(name):
        s, w = _SEG[name]
        return _dot(h, w_ref[:, s:s + w])

    u_ref[0] = seg("u")
    zp_ref[0] = seg("zp")
    zn_ref[0] = seg("zn")
    q = seg("q") * (HEAD_DIM ** -0.5 * LOG2_E)
    kvc = seg("kvc")
    kvs = seg("kvs")
    kvw = seg("kvw")
    sig = _sigmoid(_dot(h, wg_ref[...]))
    if not attn_layouts:
        kvc_ref, kvs_ref, kvw_ref, q_ref, g_ref = rest
        kvc_ref[0] = kvc
        kvs_ref[0] = kvs
        kvw_ref[0] = kvw
        q_ref[0] = q.astype(BF16)
        g_ref[0] = sig
        return

    qt_ref, ks_aug_ref, kw_ref, vst_ref, vwt_ref, gt_ref, kvct_ref, kvst_ref, kvwt_ref = rest
    qt_ref[0] = q.T.astype(BF16).reshape(N_HEADS, HEAD_DIM, tm)
    kvs_t32 = kvs.T
    kvw_t32 = kvw.T
    kvct_ref[0] = kvc.T.reshape(KV_ROW // LANES, LANES, tm)
    kvst_ref[0] = kvs_t32.reshape(KV_ROW // LANES, LANES, tm)
    kvwt_ref[0] = kvw_t32.reshape(KV_ROW // LANES, LANES, tm)
    pos = pl.program_id(1) * tm + lax.broadcasted_iota(jnp.int32, (tm, 1), 0)
    onehot = (lax.broadcasted_iota(jnp.int32, (1, MAX_SEL_BLOCKS), 1) == pos // SEL_BLOCK).astype(BF16)
    kvs_t = kvs_t32.astype(BF16)
    kvw_t = kvw_t32.astype(BF16)
    v0 = KV_HEADS * HEAD_DIM
    tail = (lax.broadcasted_iota(jnp.int32, (VT_ROWS - HEAD_DIM, LANES), 0) == 0).astype(BF16)
    for g in range(KV_HEADS):
        ks_aug_ref[0, g] = jnp.concatenate([kvs[:, g * HEAD_DIM:(g + 1) * HEAD_DIM].astype(BF16), onehot], axis=1)
        kw_ref[0, g] = kvw[:, g * HEAD_DIM:(g + 1) * HEAD_DIM].astype(BF16)
        for c in range(tm // LANES):
            sl = (slice(v0 + g * HEAD_DIM, v0 + (g + 1) * HEAD_DIM), slice(c * LANES, (c + 1) * LANES))
            vst_ref[0, g, c] = jnp.concatenate([kvs_t[sl], tail], axis=0)
            vwt_ref[0, g, c] = jnp.concatenate([kvw_t[sl], tail], axis=0)
        per_group = GQA_GROUP * N_BRANCH
        rolled = sig if g == 0 else pltpu.roll(sig, LANES - per_group * g, 1)
        gt_ref[0, g] = rolled.T[:GATE_ROWS, :]


def _in_proj(x, scale, shift, norm_w, w_main, w_gate, tm, attn_layouts):
    nb, t, d = x.shape
    r = scale.shape[1]
    assert t % tm == 0 and r in (1, t) and (tm % LANES == 0 or not attn_layouts)
    if r == 1:
        mod_spec = pl.BlockSpec((1, 1, d), lambda b, i: (b, 0, 0))
    else:
        mod_spec = pl.BlockSpec((1, tm, d), lambda b, i: (b, i, 0))

    def tok(width):
        return pl.BlockSpec((1, tm, width), lambda b, i: (b, i, 0))

    def hm(n, width=HEAD_DIM):
        return pl.BlockSpec((1, n, tm, width), lambda b, i: (b, 0, i, 0))

    def sd(*shape, dtype=F32):
        return jax.ShapeDtypeStruct(shape, dtype)

    out_specs = [tok(POOL_WIDTH), tok(POOL_WIDTH), tok(NSA_WIDTH)]
    out_shape = [sd(nb, t, POOL_WIDTH), sd(nb, t, POOL_WIDTH), sd(nb, t, NSA_WIDTH)]
    if attn_layouts:
        lane_tiles = tm // LANES
        vt_spec = pl.BlockSpec((1, KV_HEADS, lane_tiles, VT_ROWS, LANES), lambda b, i: (b, 0, i, 0, 0))
        vt_shape = sd(nb, KV_HEADS, t // LANES, VT_ROWS, LANES, dtype=BF16)
        rows_t_spec = pl.BlockSpec((1, KV_ROW // LANES, LANES, tm), lambda b, i: (b, 0, 0, i))
        rows_t_shape = sd(nb, KV_ROW // LANES, LANES, t)
        out_specs += [pl.BlockSpec((1, N_HEADS, HEAD_DIM, tm), lambda b, i: (b, 0, 0, i)),
                      hm(KV_HEADS, HEAD_DIM + MAX_SEL_BLOCKS), hm(KV_HEADS), vt_spec, vt_spec,
                      pl.BlockSpec((1, KV_HEADS, GATE_ROWS, tm), lambda b, i: (b, 0, 0, i)),
                      rows_t_spec, rows_t_spec, rows_t_spec]
        out_shape += [sd(nb, N_HEADS, HEAD_DIM, t, dtype=BF16),
                      sd(nb, KV_HEADS, t, HEAD_DIM + MAX_SEL_BLOCKS, dtype=BF16),
                      sd(nb, KV_HEADS, t, HEAD_DIM, dtype=BF16), vt_shape, vt_shape,
                      sd(nb, KV_HEADS, GATE_ROWS, t), rows_t_shape, rows_t_shape, rows_t_shape]
    else:
        out_specs += [tok(KV_ROW), tok(KV_ROW), tok(KV_ROW), tok(NSA_WIDTH), tok(LANES)]
        out_shape += [sd(nb, t, KV_ROW), sd(nb, t, KV_ROW), sd(nb, t, KV_ROW),
                      sd(nb, t, NSA_WIDTH, dtype=BF16), sd(nb, t, LANES)]
    return pl.pallas_call(
        functools.partial(_in_proj_kernel, attn_layouts=attn_layouts),
        grid=(nb, t // tm),
        in_specs=[tok(d), mod_spec, mod_spec,
                  pl.BlockSpec((1, d), lambda b, i: (0, 0)),
                  pl.BlockSpec((d, IN_MAIN), lambda b, i: (0, 0)),
                  pl.BlockSpec((d, LANES), lambda b, i: (0, 0))],
        out_specs=out_specs,
        out_shape=out_shape,
        compiler_params=_params("arbitrary", "arbitrary"),
        name="in_proj",
    )(x, scale, shift, norm_w.reshape(1, d), w_main, w_gate)


def _pool_kernel(hist_ref, uprev_ref, u_ref, wbd_ref, ps_ref, o_ref, *, pos0, tp):
    i = pl.program_id(1)
    u = u_ref[0]
    halo = jnp.where(i == 0, hist_ref[0], uprev_ref[0])
    ue = jnp.concatenate([halo, u], axis=0)
    sums = []
    s = ue
    for w in POOL_WINDOWS:
        s = s + pltpu.roll(s, w // 2, 0)
        sums.append(s[POOL_HALO:])
    c = u.shape[-1]
    grp = lax.broadcasted_iota(jnp.int32, (1, c), 1) // (c // len(POOL_WINDOWS))
    pos = pos0 + i * tp + lax.broadcasted_iota(jnp.int32, (tp, 1), 0)
    tot = sums[-1]
    win = jnp.full((1, c), float(POOL_WINDOWS[-1]), F32)
    for gi in range(len(POOL_WINDOWS) - 2, -1, -1):
        tot = jnp.where(grp == gi, sums[gi], tot)
        win = jnp.where(grp == gi, float(POOL_WINDOWS[gi]), win)
    cnt = jnp.minimum(win, (pos + 1).astype(F32))
    dlt = tot / cnt - u
    o_ref[0] = _dot(dlt.astype(BF16), wbd_ref[...]) * ps_ref[...]


def _pool(hist, u, wbd, pool_scale, pos0, tp):
    nb, t, c = u.shape
    assert t % tp == 0
    if t >= POOL_HALO:
        assert tp % POOL_HALO == 0
        uprev, ratio = u, tp // POOL_HALO
        prev_spec = pl.BlockSpec((1, POOL_HALO, c), lambda b, i: (b, jnp.maximum(i * ratio - 1, 0), 0))
    else:
        assert t == tp
        uprev = hist
        prev_spec = pl.BlockSpec((1, POOL_HALO, c), lambda b, i: (b, 0, 0))
    return pl.pallas_call(
        functools.partial(_pool_kernel, pos0=pos0, tp=tp),
        grid=(nb, t // tp),
        in_specs=[pl.BlockSpec((1, POOL_HALO, c), lambda b, i: (b, 0, 0)), prev_spec,
                  pl.BlockSpec((1, tp, c), lambda b, i: (b, i, 0)),
                  pl.BlockSpec((c, c), lambda b, i: (0, 0)),
                  pl.BlockSpec((1, c), lambda b, i: (0, 0))],
        out_specs=pl.BlockSpec((1, tp, c), lambda b, i: (b, i, 0)),
        out_shape=jax.ShapeDtypeStruct((nb, t, c), F32),
        compiler_params=_params("arbitrary", "arbitrary"),
        name="pool",
    )(hist, uprev, u, wbd, pool_scale.reshape(1, c))


def _gelu_tanh(v):
    return 0.5 * v * (1.0 + jnp.tanh(np.sqrt(2.0 / np.pi).astype(np.float32) * (v + 0.044715 * (v * v * v))))


def _compress_kernel(pt_ref, *refs, n_pages_step, n_steps, has_new, token_major):
    del pt_ref
    p_cnt = n_pages_step
    page_refs = refs[:p_cnt]
    w1p_ref, w1_ref, pe_ref, w2_ref = refs[p_cnt:p_cnt + 4]
    k = p_cnt + 4
    xnew_ref = None
    if has_new:
        xnew_ref = refs[k]
        k += 1
    if token_major:
        out_ref = refs[k]
        k += 1
    else:
        outk_ref, outvt_ref = refs[k:k + 2]
        k += 2
    slab_scr, carry_scr, bias_scr = refs[k:k + 3]
    i = pl.program_id(1)
    rows = CHUNKS_PER_PAGE * p_cnt
    row_id = lax.broadcasted_iota(jnp.int32, (rows, 1), 0)

    @pl.when(i == 0)
    def _():
        carry_scr[...] = jnp.zeros_like(carry_scr)
        for kk in range(2):
            pb = _dot(pe_ref[kk].astype(BF16), w1_ref[kk])
            bias_scr[kk] = pb[0:1, :CMP_HID] + pb[1:2, CMP_HID:]

    def store(kg, comp):
        if token_major:
            out_ref[0, :, kg * HEAD_DIM:(kg + 1) * HEAD_DIM] = comp.astype(BF16)
        elif kg < KV_HEADS:
            outk_ref[0, kg] = comp.astype(BF16)
        else:
            wide = jnp.concatenate([comp, jnp.zeros((rows, LANES - HEAD_DIM), F32)], axis=1)
            outvt_ref[0, kg - KV_HEADS] = wide.T[:HEAD_DIM, :].astype(BF16)

    def finish(kg, a0_prev, a1):
        kk = kg // KV_HEADS
        pre = a0_prev + a1 + bias_scr[kk]
        return _dot(_gelu_tanh(pre).astype(BF16), w2_ref[kk])

    def main():
        for cb in range(KV_ROW // LANES):
            for p in range(p_cnt):
                rows_t = page_refs[p][0, cb].astype(BF16).T.astype(F32)
                for ch in range(CHUNKS_PER_PAGE):
                    r0 = (p * CHUNKS_PER_PAGE + ch) * CHUNK_PITCH
                    slab_scr[cb, r0:r0 + CMP_STRIDE, :] = rows_t[ch * CMP_STRIDE:(ch + 1) * CMP_STRIDE, :]
            x = jnp.concatenate([slab_scr[cb, pl.ds(j, rows, stride=CHUNK_PITCH), :] for j in range(CMP_STRIDE)],
                                axis=1).astype(BF16)
            a_both = _dot(x, w1p_ref[cb])
            for half in range(2):
                kg = 2 * cb + half
                a = a_both[:, half * 2 * CMP_HID:(half + 1) * 2 * CMP_HID]
                a0 = a[:, :CMP_HID]
                a0_prev = jnp.where(row_id == 0, carry_scr[kg], pltpu.roll(a0, 1, 0))
                carry_scr[kg] = a0[rows - 1:rows, :]
                store(kg, finish(kg, a0_prev, a[:, CMP_HID:]))

    if not has_new:
        main()
    else:
        pl.when(i < n_steps)(main)

        @pl.when(i == n_steps)
        def _():
            for kg in range(N_KV_SLABS):
                xk = jnp.broadcast_to(xnew_ref[0, kg:kg + 1, :], (SUBLANES, CMP_STRIDE * HEAD_DIM)).astype(BF16)
                a1 = _dot(xk, w1_ref[kg // KV_HEADS])[0:1, CMP_HID:]
                comp = finish(kg, carry_scr[kg], a1)
                store(kg, jnp.where(row_id == 0, jnp.broadcast_to(comp, (rows, HEAD_DIM)), 0.0))


def _compress(pages, page_table, w1cat, pe2, w2, xnew, token_major, per_batch):
    nb, n_pages = page_table.shape
    p_cnt = min(32, n_pages)
    assert n_pages % p_cnt == 0
    n_steps = n_pages // p_cnt
    has_new = xnew is not None
    rows = CHUNKS_PER_PAGE * p_cnt
    grid_steps = n_steps + (1 if has_new else 0)
    tot_rows = rows * grid_steps
    xw = CMP_STRIDE * HEAD_DIM

    def page_spec(p):
        def imap(b, i, pt):
            step = jnp.minimum(i, n_steps - 1)
            page = pt[b * n_pages + step * p_cnt + p]
            return (b, 0, 0, page) if per_batch else (page, 0, 0, 0)
        return pl.BlockSpec((1, KV_ROW // LANES, LANES, PAGE_SIZE), imap)

    n_cb = KV_ROW // LANES
    w4 = w1cat.reshape(2, CMP_STRIDE, HEAD_DIM, 2 * CMP_HID)
    zero = jnp.zeros_like(w4[0])
    w1p = jnp.stack([jnp.concatenate([jnp.concatenate([w4[(2 * cb) // KV_HEADS], zero], axis=2),
                                      jnp.concatenate([zero, w4[(2 * cb + 1) // KV_HEADS]], axis=2)], axis=1)
                     for cb in range(n_cb)]).reshape(n_cb, CMP_STRIDE * LANES, 4 * CMP_HID)
    in_specs = [page_spec(p) for p in range(p_cnt)]
    in_specs += [pl.BlockSpec((n_cb, CMP_STRIDE * LANES, 4 * CMP_HID), lambda b, i, pt: (0, 0, 0)),
                 pl.BlockSpec((2, xw, 2 * CMP_HID), lambda b, i, pt: (0, 0, 0)),
                 pl.BlockSpec((2, SUBLANES, xw), lambda b, i, pt: (0, 0, 0)),
                 pl.BlockSpec((2, CMP_HID, HEAD_DIM), lambda b, i, pt: (0, 0, 0))]
    args = [pages] * p_cnt + [w1p, w1cat, pe2, w2]
    if has_new:
        in_specs.append(pl.BlockSpec((1, N_KV_SLABS, xw), lambda b, i, pt: (b, 0, 0)))
        args.append(xnew)
    if token_major:
        out_spec = pl.BlockSpec((1, rows, KV_ROW), lambda b, i, pt: (b, i, 0))
        out_shape = jax.ShapeDtypeStruct((nb, tot_rows, KV_ROW), BF16)
    else:
        assert rows % LANES == 0
        out_spec = [pl.BlockSpec((1, KV_HEADS, rows, HEAD_DIM), lambda b, i, pt: (b, 0, i, 0)),
                    pl.BlockSpec((1, KV_HEADS, HEAD_DIM, rows), lambda b, i, pt: (b, 0, 0, i))]
        out_shape = [jax.ShapeDtypeStruct((nb, KV_HEADS, tot_rows, HEAD_DIM), BF16),
                     jax.ShapeDtypeStruct((nb, KV_HEADS, HEAD_DIM, tot_rows), BF16)]
    return pl.pallas_call(
        functools.partial(_compress_kernel, n_pages_step=p_cnt, n_steps=n_steps, has_new=has_new,
                          token_major=token_major),
        grid_spec=pltpu.PrefetchScalarGridSpec(
            num_scalar_prefetch=1, grid=(nb, grid_steps), in_specs=in_specs, out_specs=out_spec,
            scratch_shapes=[pltpu.VMEM((n_cb, rows * CHUNK_PITCH, LANES), F32),
                            pltpu.VMEM((N_KV_SLABS, 1, CMP_HID), F32),
                            pltpu.VMEM((2, 1, CMP_HID), F32)]),
        out_shape=out_shape,
        compiler_params=_params("arbitrary", "arbitrary"),
        name="compress_new" if has_new else "compress",
    )(page_table.reshape(-1), *args)


def _softmax_parts(s, valid, axis):
    s = jnp.where(valid, s, NEG)
    m = jnp.max(s, axis=axis, keepdims=True)
    e = jnp.where(valid, jnp.exp2(s - m), 0.0)
    return e, jnp.sum(e, axis=axis, keepdims=True)


def _nsa_prompt_kernel(qt_ref, kc_ref, vct_ref, ks_ref, vst_ref, kw_ref, vwt_ref, gt_ref, mt_ref, o_ref, *,
                       tq, kb, n_cmp, k_top):
    qi = pl.program_id(2)
    qs = qi * tq
    r4 = GQA_GROUP
    cols = r4 * tq
    qt = jnp.concatenate([qt_ref[0, r] for r in range(r4)], axis=1)
    tpos = qs + lax.broadcasted_iota(jnp.int32, (1, tq), 1)

    def per_head(v):
        return jnp.concatenate([v] * r4, axis=1)

    def value_tiles(ref, k0, width):
        j0 = k0 // LANES
        return jnp.concatenate([ref[0, 0, j0 + j] for j in range(width // LANES)], axis=1)

    def normalise(acc):
        return acc[:HEAD_DIM] * (1.0 / acc[HEAD_DIM:HEAD_DIM + 1])

    nc = kc_ref.shape[2]
    s = _dot(kc_ref[0, 0], qt)
    cid = lax.broadcasted_iota(jnp.int32, (nc, 1), 0)
    valid = (cid >= 1) & (cid <= n_cmp) & (cid * CMP_STRIDE + (CMP_BLOCK - CMP_STRIDE - 1) <= tpos)
    s = s + per_head(jnp.where(valid, 0.0, NEG))
    e = jnp.exp2(s - jnp.max(s, axis=0, keepdims=True))
    den = jnp.sum(e, axis=0, keepdims=True)
    has_key = per_head(jnp.where(tpos >= CMP_BLOCK - 1, 1.0, 0.0)) if n_cmp >= 1 else 0.0
    p = e * (has_key / jnp.maximum(den, 1e-30))
    o_c = _dot(vct_ref[0, 0], p.astype(BF16))
    pg = p[:, 0:tq]
    for r in range(1, r4):
        pg = pg + p[:, r * tq:(r + 1) * tq]

    mt = mt_ref[...]
    imp = sum(_dot(mt, piece) for piece in _split3(pg))
    n_blk = imp.shape[0]
    blk = lax.broadcasted_iota(jnp.int32, (n_blk, 1), 0)
    cur = (qs + lax.broadcasted_iota(jnp.int32, (1, tq), 1)) // SEL_BLOCK
    score = jnp.where(blk <= cur, imp, -1.0)
    score = jnp.where((blk == 0) | (blk == cur) | (blk == cur - 1), FORCE, score)
    sub = lax.broadcasted_iota(jnp.int32, (SUBLANES, 1), 0)
    tiles = [score[v * SUBLANES:(v + 1) * SUBLANES, :] for v in range(n_blk // SUBLANES)]
    ranks = [jnp.zeros((SUBLANES, tq), F32) for _ in tiles]
    for j in range(n_blk):
        sj = score[j:j + 1, :]
        for v, sc in enumerate(tiles):
            if v > j // SUBLANES:
                ahead = sj >= sc
            elif v < j // SUBLANES:
                ahead = sj > sc
            else:
                ahead = (sj > sc) | ((sj == sc) & (sub > j % SUBLANES))
            ranks[v] = ranks[v] + jnp.where(ahead, 1.0, 0.0)
    rank = jnp.concatenate(ranks, axis=0)
    sel_bias = jnp.where((rank < float(k_top)) & (blk <= cur), 0.0, NEG).astype(BF16)
    q_aug = jnp.concatenate([qt, jnp.concatenate([sel_bias] * r4, axis=1)], axis=0)

    def scores(k0):
        return _dot(ks_ref[0, 0, pl.ds(k0, kb), :], q_aug)

    def absorb(sk, k0, state, diagonal):
        m_run, acc = state
        if diagonal:
            kpos = k0 + lax.broadcasted_iota(jnp.int32, (kb, 1), 0)
            sk = sk + per_head(jnp.where(kpos <= tpos, 0.0, NEG))
        m_new = jnp.maximum(m_run, jnp.max(sk, axis=0, keepdims=True))
        ek = jnp.exp2(sk - m_new)
        pv = _dot(value_tiles(vst_ref, k0, kb), ek.astype(BF16))
        return m_new, jnp.exp2(m_run - m_new) * acc + pv

    def tile(k0, state, diagonal):
        return absorb(scores(k0), k0, state, diagonal)

    init = (jnp.full((1, cols), NEG, F32), jnp.zeros((VT_ROWS, cols), F32))
    n_full = qs // kb
    state = lax.fori_loop(0, n_full, lambda kt, st: tile(pl.multiple_of(kt * kb, kb), st, False), init)
    o_s = normalise(tile(pl.multiple_of(n_full * kb, kb), state, True)[1])

    wl = kw_ref.shape[2] if kw_ref.shape[2] < WINDOW + tq else WINDOW + tq
    w0 = pl.multiple_of(jnp.maximum(qs + tq - wl, 0), tq)
    sw = _dot(kw_ref[0, 0, pl.ds(w0, wl), :], qt)
    dq = tpos - (w0 + lax.broadcasted_iota(jnp.int32, (wl, 1), 0))
    sw = sw + per_head(jnp.where((dq >= 0) & (dq < WINDOW), 0.0, NEG))
    ew = jnp.exp2(sw - jnp.max(sw, axis=0, keepdims=True))
    o_w = normalise(_dot(value_tiles(vwt_ref, w0, wl), ew.astype(BF16)))

    gt = gt_ref[0, 0]
    heads = []
    for r in range(r4):
        c0 = r * N_BRANCH
        sl = slice(r * tq, (r + 1) * tq)
        heads.append(gt[c0:c0 + 1] * o_c[:, sl] + gt[c0 + 1:c0 + 2] * o_s[:, sl] + gt[c0 + 2:c0 + 3] * o_w[:, sl])
    o_ref[0] = jnp.concatenate(heads, axis=0).T


def _importance_matrix(n_blk_pad, n_col):
    mt = np.zeros((n_blk_pad, n_col), np.float32)
    ratio = SEL_BLOCK // CMP_STRIDE
    for b in range(n_blk_pad):
        for c, w in zip(range(ratio * b, ratio * b + ratio + 1), (1.0,) + (2.0,) * (ratio - 1) + (1.0,)):
            if c < n_col:
                mt[b, c] = w
    return mt


def _nsa_prompt(q_t, comp_k, comp_vt, ks_aug, vs_t, kw, vw_t, gates_t, tq, kb):
    b_sz, _, _, s_len = q_t.shape
    nc = comp_k.shape[2]
    assert s_len % SEL_BLOCK == 0 and s_len % kb == 0 and kb % tq == 0 and s_len % tq == 0 and CMP_SPAN == 2
    assert tq % LANES == 0 and tq & (tq - 1) == 0
    n_blk = s_len // SEL_BLOCK
    assert n_blk <= MAX_SEL_BLOCKS
    n_cmp = s_len // CMP_STRIDE - CMP_SPAN + 1
    mt = jnp.asarray(_importance_matrix(MAX_SEL_BLOCKS, nc), BF16)

    def per_group(*shape):
        return pl.BlockSpec((1, 1) + shape, lambda b, g, i: (b, g) + (0,) * len(shape))

    return pl.pallas_call(
        functools.partial(_nsa_prompt_kernel, tq=tq, kb=kb, n_cmp=n_cmp, k_top=min(N_SEL, n_blk)),
        grid=(b_sz, KV_HEADS, s_len // tq),
        in_specs=[pl.BlockSpec((1, GQA_GROUP, HEAD_DIM, tq), lambda b, g, i: (b, g, 0, i)),
                  per_group(nc, HEAD_DIM), per_group(HEAD_DIM, nc),
                  per_group(s_len, HEAD_DIM + MAX_SEL_BLOCKS), per_group(s_len // LANES, VT_ROWS, LANES),
                  per_group(s_len, HEAD_DIM), per_group(s_len // LANES, VT_ROWS, LANES),
                  pl.BlockSpec((1, 1, GATE_ROWS, tq), lambda b, g, i: (b, g, 0, i)),
                  pl.BlockSpec((MAX_SEL_BLOCKS, nc), lambda b, g, i: (0, 0))],
        out_specs=pl.BlockSpec((1, tq, GQA_GROUP * HEAD_DIM), lambda b, g, i: (b, i, g)),
        out_shape=jax.ShapeDtypeStruct((b_sz, s_len, NSA_WIDTH), F32),
        compiler_params=_params("arbitrary", "arbitrary", "arbitrary"),
        name="nsa_prompt",
    )(q_t, comp_k, comp_vt, ks_aug, vs_t, kw, vw_t, gates_t, mt)


Q_COLS = LANES
COLS_PER_HEAD = Q_COLS // GQA_GROUP
V_OFF = LANES
V_COL0 = KV_HEADS * HEAD_DIM - V_OFF

def _nsa_sample_kernel(pt_ref, *refs, n_pages_step, n_steps, t_new, past, n_cmp, n_sel, k_top):
    del pt_ref
    p_cnt = n_pages_step
    page_refs = refs[:p_cnt]
    (qbd_ref, comp_ref, knew_ref, win_ref, wnew_ref, g_ref,
     o_ref, kv_scr, bias_scr, p_scr, m_scr, l_scr, acc_scr, oc_scr) = refs[p_cnt:]
    i = pl.program_id(1)
    qbd = qbd_ref[0]
    col = lax.broadcasted_iota(jnp.int32, (1, Q_COLS), 1)
    tok = col & (SUBLANES - 1)
    qpos = past + tok
    kwin = slice(0, 2 * LANES)
    vwin = slice(V_OFF, V_OFF + 2 * LANES)
    k_lanes = KV_HEADS * HEAD_DIM
    blocks_per_page = PAGE_SIZE // SEL_BLOCK
    nb_step = blocks_per_page * p_cnt
    keys_step = PAGE_SIZE * p_cnt

    def attend(rows_bf16, valid):
        s = _dot(rows_bf16[:, kwin], qbd)
        return jnp.where(valid, s, NEG)

    @pl.when((pl.program_id(0) == 0) & (i == 0))
    def _():
        key_blk = lax.broadcasted_iota(jnp.int32, (keys_step, 1), 0) // SEL_BLOCK
        lane_blk = lax.broadcasted_iota(jnp.int32, (1, 2 * LANES - k_lanes), 1)
        kv_scr[:, k_lanes:2 * LANES] = (key_blk == lane_blk).astype(BF16)

    @pl.when(i == 0)
    def _():
        comp = comp_ref[0]
        ncp = comp.shape[0]
        cid = lax.broadcasted_iota(jnp.int32, (ncp, 1), 0)
        valid = (cid >= 1) & (cid <= n_cmp) & (cid * CMP_STRIDE + (CMP_BLOCK - CMP_STRIDE - 1) <= qpos)
        s = attend(comp, valid)
        e, den = _softmax_parts(s, valid, 0)
        p = e * (1.0 / jnp.maximum(den, 1e-30))
        oc_scr[...] = _dot_tn(p.astype(BF16), comp[:, vwin])
        nbp = bias_scr.shape[0]
        ratio = SEL_BLOCK // CMP_STRIDE
        p_scr[0:ncp, :] = p
        p_scr[ncp:, :] = jnp.zeros((p_scr.shape[0] - ncp, Q_COLS), F32)
        taps = [p_scr[pl.ds(k, nbp, stride=ratio), :] for k in range(ratio + 1)]
        imp = taps[0] + taps[ratio]
        for k in range(1, ratio):
            imp = imp + 2.0 * taps[k]
        imp = imp + pltpu.roll(imp, COLS_PER_HEAD, 1)
        imp = imp + pltpu.roll(imp, 2 * COLS_PER_HEAD, 1)
        blk = lax.broadcasted_iota(jnp.int32, (nbp, 1), 0)
        cur = qpos // SEL_BLOCK
        score = jnp.where(blk <= cur, imp, -1.0)
        score = jnp.where((blk == 0) | (blk == cur) | (blk == cur - 1), FORCE, score)
        score = jnp.where(blk < n_sel, score, PAD_SCORE)
        blk_f = blk.astype(F32)
        bias = jnp.full((nbp, Q_COLS), NEG, F32)
        for _ in range(k_top):
            best = jnp.max(score, axis=0, keepdims=True)
            first = jnp.min(jnp.where(score == best, blk_f, float(nbp)), axis=0, keepdims=True)
            hit = blk_f == first
            bias = jnp.where(hit, 0.0, bias)
            score = jnp.where(hit, TAKEN_SCORE, score)
        bias_scr[...] = bias
        m_scr[...] = jnp.full_like(m_scr, NEG)
        l_scr[...] = jnp.zeros_like(l_scr)
        acc_scr[...] = jnp.zeros_like(acc_scr)

    def as_col(v):
        return jnp.broadcast_to(v, (SUBLANES, Q_COLS)).T[:, 0:1]

    def partial_softmax(s, v_rows):
        m_g = jnp.max(s, axis=0, keepdims=True)
        e = jnp.exp2(s - m_g)
        return m_g, jnp.sum(e, axis=0, keepdims=True), _dot_tn(e.astype(BF16), v_rows)

    def accumulate(*parts):
        m_run = m_scr[0:1, :]
        m_new = m_run
        for m_g, _, _ in parts:
            m_new = jnp.maximum(m_new, m_g)
        w_run = jnp.exp2(m_run - m_new)
        l_new = w_run * l_scr[0:1, :]
        acc_new = as_col(w_run) * acc_scr[...]
        for m_g, l_g, acc_g in parts:
            w_g = jnp.exp2(m_g - m_new)
            l_new = l_new + w_g * l_g
            acc_new = acc_new + as_col(w_g) * acc_g
        m_scr[0:1, :] = m_new
        l_scr[0:1, :] = l_new
        acc_scr[...] = acc_new

    bias_step = bias_scr[pl.ds(pl.multiple_of(i * nb_step, nb_step), nb_step), :].astype(BF16)
    pad = 2 * LANES - k_lanes - nb_step
    q_parts = [qbd[:k_lanes], bias_step] + ([jnp.zeros((pad, Q_COLS), BF16)] if pad else [])
    q_step = jnp.concatenate(q_parts, axis=0)
    for p in range(p_cnt):
        rows_p = slice(p * PAGE_SIZE, (p + 1) * PAGE_SIZE)
        tiles = [page_refs[p][0, cb].astype(BF16).T for cb in range(KV_ROW // LANES)]
        kv_scr[rows_p, 0:LANES] = tiles[0]
        kv_scr[rows_p, LANES:k_lanes] = tiles[1][:, :k_lanes - LANES]
        kv_scr[rows_p, 2 * LANES:3 * LANES] = tiles[1]
        kv_scr[rows_p, 3 * LANES:4 * LANES] = tiles[2]
    accumulate(partial_softmax(_dot(kv_scr[:, 0:2 * LANES], q_step), kv_scr[:, 2 * LANES:4 * LANES]))

    @pl.when(i == n_steps - 1)
    def _():
        knew = knew_ref[0].astype(BF16)
        krow = lax.broadcasted_iota(jnp.int32, (t_new, 1), 0)
        new_blk = past // SEL_BLOCK
        s_new = _dot(knew[:, kwin], qbd) + bias_scr[new_blk:new_blk + 1, :]
        accumulate(partial_softmax(jnp.where(krow <= tok, s_new, NEG), knew[:, vwin]))
        o_s = acc_scr[...] * as_col(1.0 / l_scr[0:1, :])

        wbuf = jnp.concatenate([win_ref[0, cb].astype(BF16).T for cb in range(KV_ROW // LANES)], axis=1)
        wnew = wnew_ref[0].astype(BF16)
        buf = wbuf.shape[0]
        pos_b = past - buf + lax.broadcasted_iota(jnp.int32, (buf, 1), 0)
        pos_n = past + krow
        ok_b = (qpos - pos_b >= 0) & (qpos - pos_b < WINDOW) & (pos_b >= 0)
        ok_n = (qpos - pos_n >= 0) & (qpos - pos_n < WINDOW)
        s_b = attend(wbuf, ok_b)
        s_n = attend(wnew, ok_n)
        m_w = jnp.maximum(jnp.max(s_b, axis=0, keepdims=True), jnp.max(s_n, axis=0, keepdims=True))
        e_b = jnp.where(ok_b, jnp.exp2(s_b - m_w), 0.0)
        e_n = jnp.where(ok_n, jnp.exp2(s_n - m_w), 0.0)
        den = jnp.sum(e_b, axis=0, keepdims=True) + jnp.sum(e_n, axis=0, keepdims=True)
        o_w = _dot_tn(e_b.astype(BF16), wbuf[:, vwin]) + _dot_tn(e_n.astype(BF16), wnew[:, vwin])
        o_w = o_w * jnp.broadcast_to(1.0 / jnp.maximum(den, 1e-30), (SUBLANES, Q_COLS)).T[:, 0:1]
        o_c = oc_scr[...]

        gt = g_ref[0]
        for g in range(KV_HEADS):
            for r in range(GQA_GROUP):
                r0 = r * COLS_PER_HEAD + g * SUBLANES
                c0 = V_COL0 + g * HEAD_DIM
                gc = (g * GQA_GROUP + r) * N_BRANCH
                blk_o = [o[r0:r0 + t_new, c0:c0 + HEAD_DIM] for o in (o_c, o_s, o_w)]
                h0 = (g * GQA_GROUP + r) * HEAD_DIM
                o_ref[0, :, h0:h0 + HEAD_DIM] = (gt[:, gc:gc + 1] * blk_o[0] + gt[:, gc + 1:gc + 2] * blk_o[1]
                                                 + gt[:, gc + 2:gc + 3] * blk_o[2])


def _nsa_sample(cache_slc, page_table, qbd, comp_tm, knew, win_t, wnew, gates, past):
    db, n_pages = page_table.shape
    t_new = knew.shape[1]
    assert t_new == SUBLANES and KV_HEADS * t_new <= COLS_PER_HEAD and past % SEL_BLOCK == 0 and t_new <= SEL_BLOCK
    p_cnt = min(32, n_pages)
    assert n_pages % p_cnt == 0
    n_steps = n_pages // p_cnt
    ncp = comp_tm.shape[1]
    n_str = -(-(past + t_new) // CMP_STRIDE)
    n_cmp = n_str - CMP_SPAN + 1
    n_sel = -(-(past + t_new) // SEL_BLOCK)
    blocks_step = (PAGE_SIZE // SEL_BLOCK) * p_cnt
    nbp = -(-(n_sel + 1) // blocks_step) * blocks_step
    keys_step = PAGE_SIZE * p_cnt
    assert blocks_step <= 2 * LANES - KV_HEADS * HEAD_DIM
    assert CMP_SPAN == 2
    p_rows = max(ncp, (SEL_BLOCK // CMP_STRIDE) * nbp) + SUBLANES
    buf = win_t.shape[-1]
    assert buf % LANES == 0

    def page_spec(p):
        return pl.BlockSpec((1, KV_ROW // LANES, LANES, PAGE_SIZE),
                            lambda b, i, pt: (pt[b * n_pages + i * p_cnt + p], 0, 0, 0))

    def per_b(*shape):
        return pl.BlockSpec((1,) + shape, lambda b, i, pt: (b,) + (0,) * len(shape))

    in_specs = [page_spec(p) for p in range(p_cnt)]
    in_specs += [per_b(2 * LANES, Q_COLS), per_b(ncp, KV_ROW),
                 per_b(t_new, KV_ROW), per_b(KV_ROW // LANES, LANES, buf), per_b(t_new, KV_ROW), per_b(t_new, LANES)]
    return pl.pallas_call(
        functools.partial(_nsa_sample_kernel, n_pages_step=p_cnt, n_steps=n_steps, t_new=t_new, past=past,
                          n_cmp=n_cmp, n_sel=n_sel, k_top=min(N_SEL, n_sel)),
        grid_spec=pltpu.PrefetchScalarGridSpec(
            num_scalar_prefetch=1, grid=(db, n_steps), in_specs=in_specs,
            out_specs=per_b(t_new, NSA_WIDTH),
            scratch_shapes=[pltpu.VMEM((keys_step, 4 * LANES), BF16),
                            pltpu.VMEM((nbp, Q_COLS), F32), pltpu.VMEM((p_rows, Q_COLS), F32),
                            pltpu.VMEM((SUBLANES, Q_COLS), F32), pltpu.VMEM((SUBLANES, Q_COLS), F32),
                            pltpu.VMEM((Q_COLS, 2 * LANES), F32), pltpu.VMEM((Q_COLS, 2 * LANES), F32)]),
        out_shape=jax.ShapeDtypeStruct((db, t_new, NSA_WIDTH), F32),
        compiler_params=_params("arbitrary", "arbitrary"),
        name="nsa_sample",
    )(page_table.reshape(-1), *([cache_slc] * p_cnt), qbd, comp_tm, knew, win_t, wnew, gates)


def _out_kernel(x_ref, po_ref, zp_ref, no_ref, zn_ref, gate_ref, gnp_ref, gnn_ref, w_ref, fn_ref, y_ref):
    zp = zp_ref[0]
    zn = zn_ref[0]
    mp = _rms(po_ref[0], gnp_ref[...]) * (zp * _sigmoid(zp))
    mn = _rms(no_ref[0], gnn_ref[...]) * (zn * _sigmoid(zn))
    m = jnp.concatenate([mp, mn], axis=-1).astype(BF16)
    xo = x_ref[0] + gate_ref[0] * _dot(m, w_ref[...])
    y_ref[0] = _rms(xo, fn_ref[...])


def _out(x, pool_o, zp, nsa_o, zn, gate, gn_pool, gn_nsa, w_out_bf16, final_norm, tm):
    nb, t, d = x.shape
    r = gate.shape[1]
    assert t % tm == 0 and r in (1, t)
    if r == 1:
        gate_spec = pl.BlockSpec((1, 1, d), lambda b, i: (b, 0, 0))
    else:
        gate_spec = pl.BlockSpec((1, tm, d), lambda b, i: (b, i, 0))

    def tok(width):
        return pl.BlockSpec((1, tm, width), lambda b, i: (b, i, 0))

    def const(*shape):
        return pl.BlockSpec(shape, lambda b, i: (0,) * len(shape))

    return pl.pallas_call(
        _out_kernel,
        grid=(nb, t // tm),
        in_specs=[tok(d), tok(POOL_WIDTH), tok(POOL_WIDTH), tok(NSA_WIDTH), tok(NSA_WIDTH), gate_spec,
                  const(1, POOL_WIDTH), const(1, NSA_WIDTH), const(d, d), const(1, d)],
        out_specs=tok(d),
        out_shape=jax.ShapeDtypeStruct((nb, t, d), F32),
        compiler_params=_params("arbitrary", "arbitrary"),
        name="out",
    )(x, pool_o, zp, nsa_o, zn, gate, gn_pool.reshape(1, -1), gn_nsa.reshape(1, -1), w_out_bf16,
      final_norm.reshape(1, d))


def _pages_t(pages):
    n = pages.shape[0]
    return pages.transpose(0, 2, 3, 4, 1).reshape(n, KV_ROW // LANES, LANES, PAGE_SIZE)


def _block_diag(pool_w):
    n, c, _ = pool_w.shape
    eye = jnp.eye(n, dtype=pool_w.dtype)
    return (eye[:, None, :, None] * pool_w[:, :, None, :]).reshape(n * c, n * c)


def _first_layer(phi_w1, phi_pe):
    w = phi_w1.reshape(CMP_SPAN, CMP_STRIDE, 2, HEAD_DIM, CMP_HID)
    w1cat = w.transpose(2, 1, 3, 0, 4).reshape(2, CMP_STRIDE * HEAD_DIM, CMP_SPAN * CMP_HID)
    pe = phi_pe.reshape(CMP_SPAN, CMP_STRIDE, 2, HEAD_DIM).transpose(2, 0, 1, 3).reshape(2, CMP_SPAN, -1)
    pe2 = jnp.concatenate([pe, jnp.zeros((2, SUBLANES - CMP_SPAN, pe.shape[-1]), pe.dtype)], axis=1)
    return w1cat.astype(BF16), pe2


def kernel(x_prompt, x_sample, c_prompt, c_sample, cache_cmp_kv, cache_slc_kv, state_win_kv, state_pool, page_table, norm_w, w_ada, b_ada, w_in, pool_w, pool_scale, phi_w1, phi_pe, phi_w2, gn_pool, gn_nsa, w_out, final_norm):
    b_sz, s_len, d = x_prompt.shape
    db, t_new, _ = x_sample.shape
    depth = norm_w.shape[0]
    assert depth == 1 and w_in.shape[-1] == IN_WIDTH and d // 4 == POOL_WIDTH
    n_phys = cache_cmp_kv.shape[1]
    n_pages = page_table.shape[1]
    past = n_pages * PAGE_SIZE
    lyr = 0

    n_c = b_sz + db
    c_all = jnp.concatenate([c_prompt, c_sample, jnp.zeros((-n_c % SUBLANES, d), F32)], axis=0)
    ada = _ada(c_all, w_ada[lyr], b_ada[lyr])
    shift, scale, gate = ada[:, :d], ada[:, d:2 * d], ada[:, 2 * d:]

    w_main = w_in[lyr][:, :IN_MAIN].astype(BF16)
    w_gate = jnp.pad(w_in[lyr][:, IN_MAIN:].astype(BF16), ((0, 0), (0, LANES - GATE_COLS)))
    w_out_bf16 = w_out[lyr].astype(BF16)
    wbd = _block_diag(pool_w[lyr]).astype(BF16)
    w1cat, pe2 = _first_layer(phi_w1[lyr], phi_pe[lyr])
    w2 = phi_w2[lyr].astype(BF16)
    n_tok_s = db * t_new

    def per_token(v):
        return jnp.repeat(v[b_sz:n_c], t_new, axis=0)[None]

    tm = min(512, s_len)
    (u_p, zp_p, zn_p, q_t, ks_aug, kw_p, vs_t, vw_t, g_t, kvc_t, kvs_t, kvw_t) = _in_proj(
        x_prompt, scale[:b_sz, None], shift[:b_sz, None], norm_w[lyr], w_main, w_gate, tm, True)
    pool_p = _pool(jnp.zeros((b_sz, POOL_HALO, POOL_WIDTH), F32), u_p, wbd, pool_scale[lyr], 0, tm)
    pages_p = s_len // PAGE_SIZE
    in_order = jnp.tile(jnp.arange(pages_p, dtype=jnp.int32), (b_sz, 1))
    comp_k, comp_vt = _compress(kvc_t, in_order, w1cat, pe2, w2, None, False, True)
    nsa_p = _nsa_prompt(q_t, comp_k, comp_vt, ks_aug, vs_t, kw_p, vw_t, g_t, tq=512, kb=min(512, s_len))
    y_p = _out(x_prompt, pool_p, zp_p, nsa_p, zn_p, gate[:b_sz, None], gn_pool[lyr], gn_nsa[lyr], w_out_bf16,
               final_norm, tm)

    (u_s, zp_s, zn_s, kvc_s, kvs_s, kvw_s, q_s, g_s) = _in_proj(
        x_sample.reshape(1, n_tok_s, d), per_token(scale), per_token(shift), norm_w[lyr], w_main, w_gate, n_tok_s,
        False)
    u_s3 = u_s.reshape(db, t_new, POOL_WIDTH)
    hist = jnp.concatenate([jnp.zeros((db, POOL_HALO - state_pool.shape[2], POOL_WIDTH), F32), state_pool[lyr]], axis=1)
    pool_s = _pool(hist, u_s3, wbd, pool_scale[lyr], past, t_new)
    kvc_s3 = kvc_s.reshape(db, t_new, KV_ROW)
    kvs_s3 = kvs_s.reshape(db, t_new, KV_ROW)
    kvw_s3 = kvw_s.reshape(db, t_new, KV_ROW)
    xnew = jnp.pad(kvc_s3, ((0, 0), (0, CMP_STRIDE - t_new), (0, 0)))
    xnew = xnew.reshape(db, CMP_STRIDE, N_KV_SLABS, HEAD_DIM).transpose(0, 2, 1, 3).reshape(db, N_KV_SLABS, -1)
    comp_s = _compress(_pages_t(cache_cmp_kv[lyr]), page_table, w1cat, pe2, w2, xnew, True, False)
    q5 = q_s.reshape(db, t_new, KV_HEADS, GQA_GROUP, HEAD_DIM)
    eye = jnp.eye(KV_HEADS, dtype=BF16)
    qbd = q5.transpose(0, 2, 4, 3, 1)[:, :, :, :, None, :] * eye[None, :, None, None, :, None]
    qbd = jnp.pad(qbd.reshape(db, KV_HEADS * HEAD_DIM, GQA_GROUP, KV_HEADS * t_new),
                  ((0, 0), (0, 2 * LANES - KV_HEADS * HEAD_DIM), (0, 0), (0, COLS_PER_HEAD - KV_HEADS * t_new)))
    qbd = qbd.reshape(db, 2 * LANES, Q_COLS)
    g_s3 = g_s.reshape(db, t_new, LANES)
    buf = state_win_kv.shape[2]
    win_t = state_win_kv[lyr].transpose(0, 2, 3, 4, 1).reshape(db, KV_ROW // LANES, LANES, buf)
    nsa_s = _nsa_sample(_pages_t(cache_slc_kv[lyr]), page_table, qbd, comp_s, kvs_s3,
                        win_t, kvw_s3, g_s3, past)
    y_s = _out(x_sample.reshape(1, n_tok_s, d), pool_s.reshape(1, n_tok_s, POOL_WIDTH), zp_s,
               nsa_s.reshape(1, n_tok_s, NSA_WIDTH), zn_s, per_token(gate), gn_pool[lyr], gn_nsa[lyr], w_out_bf16,
               final_norm, n_tok_s).reshape(db, t_new, d)

    kv_shape = (2, KV_HEADS, HEAD_DIM)
    win_len = min(WINDOW, s_len)
    new_win_s = jnp.concatenate([state_win_kv[lyr], kvw_s3.reshape((db, t_new) + kv_shape)], axis=1)[:, -buf:]
    hist_len = state_pool.shape[2]
    new_pool_s = jnp.concatenate([state_pool[lyr], u_s3], axis=1)[:, -hist_len:]

    def rows_from_t(v_t):
        return v_t.reshape((v_t.shape[0],) + kv_shape + (v_t.shape[-1],)).transpose(0, 4, 1, 2, 3)[None]

    return (y_p, y_s,
            rows_from_t(kvc_t), kvc_s3.reshape((1, db, t_new) + kv_shape),
            rows_from_t(kvs_t), kvs_s3.reshape((1, db, t_new) + kv_shape),
            rows_from_t(kvw_t[..., s_len - win_len:]), new_win_s[None],
            u_p[:, s_len - hist_len:][None], new_pool_s[None])
```

```python
import functools

import numpy as np
import jax
import jax.numpy as jnp
from jax import lax
from jax.experimental import pallas as pl
from jax.experimental.pallas import tpu as pltpu

F32 = jnp.float32
BF16 = jnp.bfloat16

HEAD_DIM = 64
GQA_GROUP = 4
KV_HEADS = 3
N_HEADS = KV_HEADS * GQA_GROUP
N_KV_SLABS = 2 * KV_HEADS
KV_ROW = N_KV_SLABS * HEAD_DIM
N_BRANCH = 3
POOL_WINDOWS = (2, 4, 8, 16)
POOL_HALO = 16
CMP_BLOCK = 32
CMP_STRIDE = 16
CMP_SPAN = CMP_BLOCK // CMP_STRIDE
CMP_HID = 2 * HEAD_DIM
SEL_BLOCK = 64
N_SEL = 16
WINDOW = 512
PAGE_SIZE = 128
CHUNKS_PER_PAGE = PAGE_SIZE // CMP_STRIDE
CHUNK_PITCH = 24
EPS = 1e-6
NEG = -1e30
FORCE = 1e4
LOG2_E = 1.4426950408889634
VT_ROWS = 80
PAD_SCORE = -1e38
TAKEN_SCORE = -3e38

LANES = 128
SUBLANES = 8
VMEM_LIMIT_BYTES = 56 * 1024 * 1024


def _dot(a, b):
    return jnp.dot(a, b, preferred_element_type=F32)


def _dot_nt(a, b):
    return lax.dot_general(a, b, (((1,), (1,)), ((), ())), preferred_element_type=F32)


def _dot_tn(a, b):
    return lax.dot_general(a, b, (((0,), (0,)), ((), ())), preferred_element_type=F32)


def _sigmoid(v):
    return 1.0 / (1.0 + jnp.exp(-v))


def _rms(v, g):
    return v * lax.rsqrt(jnp.mean(v * v, axis=-1, keepdims=True) + EPS) * g


def _split3(v):
    hi = v.astype(BF16)
    r1 = v - hi.astype(F32)
    mid = r1.astype(BF16)
    lo = (r1 - mid.astype(F32)).astype(BF16)
    return hi, mid, lo


def _params(*sem):
    return pltpu.CompilerParams(dimension_semantics=sem, vmem_limit_bytes=VMEM_LIMIT_BYTES)


def _ada_kernel(c_ref, w_ref, b_ref, o_ref):
    c = c_ref[...]
    a = (c * _sigmoid(c)).astype(BF16)
    o_ref[...] = _dot(a, w_ref[...].astype(BF16)) + b_ref[...]


def _ada(c, w_ada, b_ada):
    m, d = c.shape
    n = w_ada.shape[1]
    tn = 512
    return pl.pallas_call(
        _ada_kernel,
        grid=(n // tn,),
        in_specs=[pl.BlockSpec((m, d), lambda j: (0, 0)),
                  pl.BlockSpec((d, tn), lambda j: (0, j)),
                  pl.BlockSpec((1, tn), lambda j: (0, j))],
        out_specs=pl.BlockSpec((m, tn), lambda j: (0, j)),
        out_shape=jax.ShapeDtypeStruct((m, n), F32),
        compiler_params=_params("arbitrary"),
        name="ada",
    )(c, w_ada, b_ada.reshape(1, n))


POOL_WIDTH = 256
NSA_WIDTH = N_HEADS * HEAD_DIM
GATE_COLS = N_BRANCH * N_HEADS
MAX_SEL_BLOCKS = LANES - HEAD_DIM
_SEG = {}
_off = 0
for _name, _w in (("u", POOL_WIDTH), ("zp", POOL_WIDTH), ("q", NSA_WIDTH), ("zn", NSA_WIDTH),
                  ("kvc", KV_ROW), ("kvs", KV_ROW), ("kvw", KV_ROW)):
    _SEG[_name] = (_off, _w)
    _off += _w
IN_MAIN = _off
IN_WIDTH = IN_MAIN + GATE_COLS


GATE_ROWS = 16


def _in_proj_kernel(x_ref, sc_ref, sh_ref, nw_ref, w_ref, wg_ref, u_ref, zp_ref, zn_ref, *rest, attn_layouts):
    tm = x_ref.shape[1]
    h = (_rms(x_ref[0], nw_ref[...]) * (1.0 + sc_ref[0]) + sh_ref[0]).astype(BF16)

    def seg(name):
        s, w = _SEG[name]
        return _dot(h, w_ref[:, s:s + w])

    u_ref[0] = seg("u")
    zp_ref[0] = seg("zp")
    zn_ref[0] = seg("zn")
    q = seg("q") * (HEAD_DIM ** -0.5 * LOG2_E)
    kvc = seg("kvc")
    kvs = seg("kvs")
    kvw = seg("kvw")
    sig = _sigmoid(_dot(h, wg_ref[...]))
    if not attn_layouts:
        kvc_ref, kvs_ref, kvw_ref, q_ref, g_ref = rest
        kvc_ref[0] = kvc
        kvs_ref[0] = kvs
        kvw_ref[0] = kvw
        q_ref[0] = q.astype(BF16)
        g_ref[0] = sig
        return

    qt_ref, ks_aug_ref, kw_ref, vst_ref, vwt_ref, gt_ref, kvct_ref, kvst_ref, kvwt_ref = rest
    qt_ref[0] = q.T.astype(BF16).reshape(N_HEADS, HEAD_DIM, tm)
    kvs_t32 = kvs.T
    kvw_t32 = kvw.T
    kvct_ref[0] = kvc.T.reshape(KV_ROW // LANES, LANES, tm)
    kvst_ref[0] = kvs_t32.reshape(KV_ROW // LANES, LANES, tm)
    kvwt_ref[0] = kvw_t32.reshape(KV_ROW // LANES, LANES, tm)
    pos = pl.program_id(1) * tm + lax.broadcasted_iota(jnp.int32, (tm, 1), 0)
    onehot = (lax.broadcasted_iota(jnp.int32, (1, MAX_SEL_BLOCKS), 1) == pos // SEL_BLOCK).astype(BF16)
    kvs_t = kvs_t32.astype(BF16)
    kvw_t = kvw_t32.astype(BF16)
    v0 = KV_HEADS * HEAD_DIM
    tail = (lax.broadcasted_iota(jnp.int32, (VT_ROWS - HEAD_DIM, LANES), 0) == 0).astype(BF16)
    for g in range(KV_HEADS):
        ks_aug_ref[0, g] = jnp.concatenate([kvs[:, g * HEAD_DIM:(g + 1) * HEAD_DIM].astype(BF16), onehot], axis=1)
        kw_ref[0, g] = kvw[:, g * HEAD_DIM:(g + 1) * HEAD_DIM].astype(BF16)
        for c in range(tm // LANES):
            sl = (slice(v0 + g * HEAD_DIM, v0 + (g + 1) * HEAD_DIM), slice(c * LANES, (c + 1) * LANES))
            vst_ref[0, g, c] = jnp.concatenate([kvs_t[sl], tail], axis=0)
            vwt_ref[0, g, c] = jnp.concatenate([kvw_t[sl], tail], axis=0)
        per_group = GQA_GROUP * N_BRANCH
        rolled = sig if g == 0 else pltpu.roll(sig, LANES - per_group * g, 1)
        gt_ref[0, g] = rolled.T[:GATE_ROWS, :]


def _cast_kernel(w_ref, o_ref):
    o_ref[...] = w_ref[...].astype(o_ref.dtype)


def _cast_columns(w, width, tile, dtype):
    rows = w.shape[0]
    assert width % tile == 0 and tile % LANES == 0 and width <= w.shape[1]
    return pl.pallas_call(
        _cast_kernel,
        grid=(width // tile,),
        in_specs=[pl.BlockSpec((rows, tile), lambda j: (0, j))],
        out_specs=pl.BlockSpec((rows, tile), lambda j: (0, j)),
        out_shape=jax.ShapeDtypeStruct((rows, width), dtype),
        compiler_params=_params("arbitrary"),
        name="cast_columns",
    )(w)


def _in_proj(x, scale, shift, norm_w, w_main, w_gate, tm, attn_layouts):
    nb, t, d = x.shape
    r = scale.shape[1]
    assert t % tm == 0 and r in (1, t) and (tm % LANES == 0 or not attn_layouts)
    if r == 1:
        mod_spec = pl.BlockSpec((1, 1, d), lambda b, i: (b, 0, 0))
    else:
        mod_spec = pl.BlockSpec((1, tm, d), lambda b, i: (b, i, 0))

    def tok(width):
        return pl.BlockSpec((1, tm, width), lambda b, i: (b, i, 0))

    def hm(n, width=HEAD_DIM):
        return pl.BlockSpec((1, n, tm, width), lambda b, i: (b, 0, i, 0))

    def sd(*shape, dtype=F32):
        return jax.ShapeDtypeStruct(shape, dtype)

    out_specs = [tok(POOL_WIDTH), tok(POOL_WIDTH), tok(NSA_WIDTH)]
    out_shape = [sd(nb, t, POOL_WIDTH), sd(nb, t, POOL_WIDTH), sd(nb, t, NSA_WIDTH)]
    if attn_layouts:
        lane_tiles = tm // LANES
        vt_spec = pl.BlockSpec((1, KV_HEADS, lane_tiles, VT_ROWS, LANES), lambda b, i: (b, 0, i, 0, 0))
        vt_shape = sd(nb, KV_HEADS, t // LANES, VT_ROWS, LANES, dtype=BF16)
        rows_t_spec = pl.BlockSpec((1, KV_ROW // LANES, LANES, tm), lambda b, i: (b, 0, 0, i))
        rows_t_shape = sd(nb, KV_ROW // LANES, LANES, t)
        out_specs += [pl.BlockSpec((1, N_HEADS, HEAD_DIM, tm), lambda b, i: (b, 0, 0, i)),
                      hm(KV_HEADS, HEAD_DIM + MAX_SEL_BLOCKS), hm(KV_HEADS), vt_spec, vt_spec,
                      pl.BlockSpec((1, KV_HEADS, GATE_ROWS, tm), lambda b, i: (b, 0, 0, i)),
                      rows_t_spec, rows_t_spec, rows_t_spec]
        out_shape += [sd(nb, N_HEADS, HEAD_DIM, t, dtype=BF16),
                      sd(nb, KV_HEADS, t, HEAD_DIM + MAX_SEL_BLOCKS, dtype=BF16),
                      sd(nb, KV_HEADS, t, HEAD_DIM, dtype=BF16), vt_shape, vt_shape,
                      sd(nb, KV_HEADS, GATE_ROWS, t), rows_t_shape, rows_t_shape, rows_t_shape]
    else:
        out_specs += [tok(KV_ROW), tok(KV_ROW), tok(KV_ROW), tok(NSA_WIDTH), tok(LANES)]
        out_shape += [sd(nb, t, KV_ROW), sd(nb, t, KV_ROW), sd(nb, t, KV_ROW),
                      sd(nb, t, NSA_WIDTH, dtype=BF16), sd(nb, t, LANES)]
    return pl.pallas_call(
        functools.partial(_in_proj_kernel, attn_layouts=attn_layouts),
        grid=(nb, t // tm),
        in_specs=[tok(d), mod_spec, mod_spec,
                  pl.BlockSpec((1, d), lambda b, i: (0, 0)),
                  pl.BlockSpec((d, IN_MAIN), lambda b, i: (0, 0)),
                  pl.BlockSpec((d, LANES), lambda b, i: (0, 0))],
        out_specs=out_specs,
        out_shape=out_shape,
        compiler_params=_params("arbitrary", "arbitrary"),
        name="in_proj",
    )(x, scale, shift, norm_w.reshape(1, d), w_main, w_gate)


def _pool_kernel(hist_ref, uprev_ref, u_ref, wbd_ref, ps_ref, o_ref, *, pos0, tp):
    i = pl.program_id(1)
    c = u_ref.shape[-1]
    grp = lax.broadcasted_iota(jnp.int32, (1, c), 1) // (c // len(POOL_WINDOWS))
    pos = pos0 + i * tp + lax.broadcasted_iota(jnp.int32, (tp, 1), 0)
    win = jnp.full((1, c), float(POOL_WINDOWS[-1]), F32)
    for gi in range(len(POOL_WINDOWS) - 2, -1, -1):
        win = jnp.where(grp == gi, float(POOL_WINDOWS[gi]), win)
    cnt = jnp.minimum(win, (pos + 1).astype(F32))
    for bi in range(u_ref.shape[0]):
        u = u_ref[bi]
        halo = jnp.where(i == 0, hist_ref[bi], uprev_ref[bi])
        sums = []
        s = jnp.concatenate([halo, u], axis=0)
        for w in POOL_WINDOWS:
            s = s + pltpu.roll(s, w // 2, 0)
            sums.append(s[POOL_HALO:])
        tot = sums[-1]
        for gi in range(len(POOL_WINDOWS) - 2, -1, -1):
            tot = jnp.where(grp == gi, sums[gi], tot)
        dlt = tot / cnt - u
        o_ref[bi] = _dot(dlt.astype(BF16), wbd_ref[...]) * ps_ref[...]


def _pool(hist, u, wbd, pool_scale, pos0, tp):
    nb, t, c = u.shape
    assert t % tp == 0
    if t >= POOL_HALO:
        assert tp % POOL_HALO == 0
        uprev, ratio, bb = u, tp // POOL_HALO, 1
        prev_spec = pl.BlockSpec((bb, POOL_HALO, c), lambda b, i: (b, jnp.maximum(i * ratio - 1, 0), 0))
    else:
        assert t == tp
        uprev, bb = hist, nb
        prev_spec = pl.BlockSpec((bb, POOL_HALO, c), lambda b, i: (b, 0, 0))
    return pl.pallas_call(
        functools.partial(_pool_kernel, pos0=pos0, tp=tp),
        grid=(nb // bb, t // tp),
        in_specs=[pl.BlockSpec((bb, POOL_HALO, c), lambda b, i: (b, 0, 0)), prev_spec,
                  pl.BlockSpec((bb, tp, c), lambda b, i: (b, i, 0)),
                  pl.BlockSpec((c, c), lambda b, i: (0, 0)),
                  pl.BlockSpec((1, c), lambda b, i: (0, 0))],
        out_specs=pl.BlockSpec((bb, tp, c), lambda b, i: (b, i, 0)),
        out_shape=jax.ShapeDtypeStruct((nb, t, c), F32),
        compiler_params=_params("arbitrary", "arbitrary"),
        name="pool",
    )(hist, uprev, u, wbd, pool_scale.reshape(1, c))


def _gelu_tanh(v):
    return 0.5 * v * (1.0 + jnp.tanh(np.sqrt(2.0 / np.pi).astype(np.float32) * (v + 0.044715 * (v * v * v))))


def _compress_kernel(pt_ref, *refs, n_pages_step, n_steps, has_new, token_major):
    del pt_ref
    p_cnt = n_pages_step
    page_refs = refs[:p_cnt]
    w1p_ref, w1_ref, pe_ref, w2_ref = refs[p_cnt:p_cnt + 4]
    k = p_cnt + 4
    xnew_ref = None
    if has_new:
        xnew_ref = refs[k]
        k += 1
    if token_major:
        out_ref = refs[k]
        k += 1
    else:
        outk_ref, outvt_ref = refs[k:k + 2]
        k += 2
    slab_scr, carry_scr, bias_scr = refs[k:k + 3]
    i = pl.program_id(1)
    rows = CHUNKS_PER_PAGE * p_cnt
    row_id = lax.broadcasted_iota(jnp.int32, (rows, 1), 0)

    @pl.when(i == 0)
    def _():
        carry_scr[...] = jnp.zeros_like(carry_scr)
        for kk in range(2):
            pb = _dot(pe_ref[kk].astype(BF16), w1_ref[kk])
            bias_scr[kk] = pb[0:1, :CMP_HID] + pb[1:2, CMP_HID:]

    def store(kg, comp):
        if token_major:
            out_ref[0, :, kg * HEAD_DIM:(kg + 1) * HEAD_DIM] = comp.astype(BF16)
        elif kg < KV_HEADS:
            outk_ref[0, kg] = comp.astype(BF16)
        else:
            wide = jnp.concatenate([comp, jnp.zeros((rows, LANES - HEAD_DIM), F32)], axis=1)
            outvt_ref[0, kg - KV_HEADS] = wide.T[:HEAD_DIM, :].astype(BF16)

    def finish(kg, a0_prev, a1):
        kk = kg // KV_HEADS
        pre = a0_prev + a1 + bias_scr[kk]
        return _dot(_gelu_tanh(pre).astype(BF16), w2_ref[kk])

    def main():
        for cb in range(KV_ROW // LANES):
            for p in range(p_cnt):
                rows_t = page_refs[p][0, cb].astype(BF16).T.astype(F32)
                for ch in range(CHUNKS_PER_PAGE):
                    r0 = (p * CHUNKS_PER_PAGE + ch) * CHUNK_PITCH
                    slab_scr[cb, r0:r0 + CMP_STRIDE, :] = rows_t[ch * CMP_STRIDE:(ch + 1) * CMP_STRIDE, :]
            x = jnp.concatenate([slab_scr[cb, pl.ds(j, rows, stride=CHUNK_PITCH), :] for j in range(CMP_STRIDE)],
                                axis=1).astype(BF16)
            a_both = _dot(x, w1p_ref[cb])
            for half in range(2):
                kg = 2 * cb + half
                a = a_both[:, half * 2 * CMP_HID:(half + 1) * 2 * CMP_HID]
                a0 = a[:, :CMP_HID]
                a0_prev = jnp.where(row_id == 0, carry_scr[kg], pltpu.roll(a0, 1, 0))
                carry_scr[kg] = a0[rows - 1:rows, :]
                store(kg, finish(kg, a0_prev, a[:, CMP_HID:]))

    if not has_new:
        main()
    else:
        pl.when(i < n_steps)(main)

        @pl.when(i == n_steps)
        def _():
            for kg in range(N_KV_SLABS):
                xk = jnp.broadcast_to(xnew_ref[0, kg:kg + 1, :], (SUBLANES, CMP_STRIDE * HEAD_DIM)).astype(BF16)
                a1 = _dot(xk, w1_ref[kg // KV_HEADS])[0:1, CMP_HID:]
                comp = finish(kg, carry_scr[kg], a1)
                store(kg, jnp.where(row_id == 0, jnp.broadcast_to(comp, (rows, HEAD_DIM)), 0.0))


def _compress(pages, page_table, w1cat, pe2, w2, xnew, token_major, per_batch):
    nb, n_pages = page_table.shape
    p_cnt = min(32, n_pages)
    assert n_pages % p_cnt == 0
    n_steps = n_pages // p_cnt
    has_new = xnew is not None
    rows = CHUNKS_PER_PAGE * p_cnt
    grid_steps = n_steps + (1 if has_new else 0)
    tot_rows = rows * grid_steps
    xw = CMP_STRIDE * HEAD_DIM

    def page_spec(p):
        def imap(b, i, pt):
            step = jnp.minimum(i, n_steps - 1)
            page = pt[b * n_pages + step * p_cnt + p]
            return (b, 0, 0, page) if per_batch else (page, 0, 0, 0)
        return pl.BlockSpec((1, KV_ROW // LANES, LANES, PAGE_SIZE), imap)

    n_cb = KV_ROW // LANES
    w4 = w1cat.reshape(2, CMP_STRIDE, HEAD_DIM, 2 * CMP_HID)
    zero = jnp.zeros_like(w4[0])
    w1p = jnp.stack([jnp.concatenate([jnp.concatenate([w4[(2 * cb) // KV_HEADS], zero], axis=2),
                                      jnp.concatenate([zero, w4[(2 * cb + 1) // KV_HEADS]], axis=2)], axis=1)
                     for cb in range(n_cb)]).reshape(n_cb, CMP_STRIDE * LANES, 4 * CMP_HID)
    in_specs = [page_spec(p) for p in range(p_cnt)]
    in_specs += [pl.BlockSpec((n_cb, CMP_STRIDE * LANES, 4 * CMP_HID), lambda b, i, pt: (0, 0, 0)),
                 pl.BlockSpec((2, xw, 2 * CMP_HID), lambda b, i, pt: (0, 0, 0)),
                 pl.BlockSpec((2, SUBLANES, xw), lambda b, i, pt: (0, 0, 0)),
                 pl.BlockSpec((2, CMP_HID, HEAD_DIM), lambda b, i, pt: (0, 0, 0))]
    args = [pages] * p_cnt + [w1p, w1cat, pe2, w2]
    if has_new:
        in_specs.append(pl.BlockSpec((1, N_KV_SLABS, xw), lambda b, i, pt: (b, 0, 0)))
        args.append(xnew)
    if token_major:
        out_spec = pl.BlockSpec((1, rows, KV_ROW), lambda b, i, pt: (b, i, 0))
        out_shape = jax.ShapeDtypeStruct((nb, tot_rows, KV_ROW), BF16)
    else:
        assert rows % LANES == 0
        out_spec = [pl.BlockSpec((1, KV_HEADS, rows, HEAD_DIM), lambda b, i, pt: (b, 0, i, 0)),
                    pl.BlockSpec((1, KV_HEADS, HEAD_DIM, rows), lambda b, i, pt: (b, 0, 0, i))]
        out_shape = [jax.ShapeDtypeStruct((nb, KV_HEADS, tot_rows, HEAD_DIM), BF16),
                     jax.ShapeDtypeStruct((nb, KV_HEADS, HEAD_DIM, tot_rows), BF16)]
    return pl.pallas_call(
        functools.partial(_compress_kernel, n_pages_step=p_cnt, n_steps=n_steps, has_new=has_new,
                          token_major=token_major),
        grid_spec=pltpu.PrefetchScalarGridSpec(
            num_scalar_prefetch=1, grid=(nb, grid_steps), in_specs=in_specs, out_specs=out_spec,
            scratch_shapes=[pltpu.VMEM((n_cb, rows * CHUNK_PITCH, LANES), F32),
                            pltpu.VMEM((N_KV_SLABS, 1, CMP_HID), F32),
                            pltpu.VMEM((2, 1, CMP_HID), F32)]),
        out_shape=out_shape,
        compiler_params=_params("arbitrary", "arbitrary"),
        name="compress_new" if has_new else "compress",
    )(page_table.reshape(-1), *args)


def _softmax_parts(s, valid, axis):
    s = jnp.where(valid, s, NEG)
    m = jnp.max(s, axis=axis, keepdims=True)
    e = jnp.where(valid, jnp.exp2(s - m), 0.0)
    return e, jnp.sum(e, axis=axis, keepdims=True)


def _nsa_prompt_kernel(qt_ref, kc_ref, vct_ref, ks_ref, vst_ref, kw_ref, vwt_ref, gt_ref, mt_ref, o_ref, *,
                       tq, kb, n_cmp, k_top):
    qi = pl.program_id(2)
    qs = qi * tq
    r4 = GQA_GROUP
    cols = r4 * tq
    qt = jnp.concatenate([qt_ref[0, r] for r in range(r4)], axis=1)
    tpos = qs + lax.broadcasted_iota(jnp.int32, (1, tq), 1)

    def per_head(v):
        return jnp.concatenate([v] * r4, axis=1)

    def value_tiles(ref, k0, width):
        j0 = k0 // LANES
        return jnp.concatenate([ref[0, 0, j0 + j] for j in range(width // LANES)], axis=1)

    def normalise(acc):
        return acc[:HEAD_DIM] * (1.0 / acc[HEAD_DIM:HEAD_DIM + 1])

    nc = kc_ref.shape[2]
    s = _dot(kc_ref[0, 0], qt)
    cid = lax.broadcasted_iota(jnp.int32, (nc, 1), 0)
    valid = (cid >= 1) & (cid <= n_cmp) & (cid * CMP_STRIDE + (CMP_BLOCK - CMP_STRIDE - 1) <= tpos)
    s = s + per_head(jnp.where(valid, 0.0, NEG))
    e = jnp.exp2(s - jnp.max(s, axis=0, keepdims=True))
    den = jnp.sum(e, axis=0, keepdims=True)
    has_key = per_head(jnp.where(tpos >= CMP_BLOCK - 1, 1.0, 0.0)) if n_cmp >= 1 else 0.0
    p = e * (has_key / jnp.maximum(den, 1e-30))
    o_c = _dot(vct_ref[0, 0], p.astype(BF16))
    pg = p[:, 0:tq]
    for r in range(1, r4):
        pg = pg + p[:, r * tq:(r + 1) * tq]

    mt = mt_ref[...]
    imp = sum(_dot(mt, piece) for piece in _split3(pg))
    n_blk = imp.shape[0]
    blk = lax.broadcasted_iota(jnp.int32, (n_blk, 1), 0)
    cur = (qs + lax.broadcasted_iota(jnp.int32, (1, tq), 1)) // SEL_BLOCK
    score = jnp.where(blk <= cur, imp, -1.0)
    score = jnp.where((blk == 0) | (blk == cur) | (blk == cur - 1), FORCE, score)
    sub = lax.broadcasted_iota(jnp.int32, (SUBLANES, 1), 0)
    tiles = [score[v * SUBLANES:(v + 1) * SUBLANES, :] for v in range(n_blk // SUBLANES)]
    ranks = [jnp.zeros((SUBLANES, tq), F32) for _ in tiles]
    for j in range(n_blk):
        sj = score[j:j + 1, :]
        for v, sc in enumerate(tiles):
            if v > j // SUBLANES:
                ahead = sj >= sc
            elif v < j // SUBLANES:
                ahead = sj > sc
            else:
                ahead = (sj > sc) | ((sj == sc) & (sub > j % SUBLANES))
            ranks[v] = ranks[v] + jnp.where(ahead, 1.0, 0.0)
    rank = jnp.concatenate(ranks, axis=0)
    sel_bias = jnp.where((rank < float(k_top)) & (blk <= cur), 0.0, NEG).astype(BF16)
    q_aug = jnp.concatenate([qt, jnp.concatenate([sel_bias] * r4, axis=1)], axis=0)

    def scores(k0):
        return _dot(ks_ref[0, 0, pl.ds(k0, kb), :], q_aug)

    def absorb(sk, values, state):
        m_run, acc = state
        m_new = jnp.maximum(m_run, jnp.max(sk, axis=0, keepdims=True))
        pv = _dot(values, jnp.exp2(sk - m_new).astype(BF16))
        return m_new, jnp.exp2(m_run - m_new) * acc + pv

    def tile(kt, state):
        k0 = pl.multiple_of(kt * kb, kb)
        return absorb(scores(k0), value_tiles(vst_ref, k0, kb), state)

    init = (jnp.full((1, cols), NEG, F32), jnp.zeros((VT_ROWS, cols), F32))
    m_run, acc = lax.fori_loop(0, qi, tile, init)

    n_sub = tq // LANES

    def sub_cols(v, j):
        return jnp.concatenate([v[:, r * tq + j * LANES:r * tq + (j + 1) * LANES] for r in range(r4)], axis=1)

    def join_subs(parts):
        return jnp.concatenate([parts[j][:, r * LANES:(r + 1) * LANES] for r in range(r4) for j in range(n_sub)],
                               axis=1)

    wl = min(kw_ref.shape[2], WINDOW + LANES)
    sel_parts, win_parts = [], []
    for j in range(n_sub):
        tpos_j = tpos[:, j * LANES:(j + 1) * LANES]
        heads_j = lambda v: jnp.concatenate([v] * r4, axis=1)
        nk = (j + 1) * LANES
        kpos = qs + lax.broadcasted_iota(jnp.int32, (nk, 1), 0)
        sk = _dot(ks_ref[0, 0, pl.ds(pl.multiple_of(qs, tq), nk), :], sub_cols(q_aug, j))
        sk = sk + heads_j(jnp.where(kpos <= tpos_j, 0.0, NEG))
        state_j = absorb(sk, value_tiles(vst_ref, qs, nk), (sub_cols(m_run, j), sub_cols(acc, j)))
        sel_parts.append(normalise(state_j[1]))
        w0 = pl.multiple_of(jnp.maximum(qs + (j + 1) * LANES - wl, 0), LANES)
        dq = tpos_j - (w0 + lax.broadcasted_iota(jnp.int32, (wl, 1), 0))
        sw = _dot(kw_ref[0, 0, pl.ds(w0, wl), :], sub_cols(qt, j))
        sw = sw + heads_j(jnp.where((dq >= 0) & (dq < WINDOW), 0.0, NEG))
        ew = jnp.exp2(sw - jnp.max(sw, axis=0, keepdims=True))
        win_parts.append(normalise(_dot(value_tiles(vwt_ref, w0, wl), ew.astype(BF16))))
    o_s = join_subs(sel_parts)
    o_w = join_subs(win_parts)

    gt = gt_ref[0, 0]
    heads = []
    for r in range(r4):
        c0 = r * N_BRANCH
        sl = slice(r * tq, (r + 1) * tq)
        heads.append(gt[c0:c0 + 1] * o_c[:, sl] + gt[c0 + 1:c0 + 2] * o_s[:, sl] + gt[c0 + 2:c0 + 3] * o_w[:, sl])
    o_ref[0] = jnp.concatenate(heads, axis=0).T


def _importance_matrix(n_blk_pad, n_col):
    mt = np.zeros((n_blk_pad, n_col), np.float32)
    ratio = SEL_BLOCK // CMP_STRIDE
    for b in range(n_blk_pad):
        for c, w in zip(range(ratio * b, ratio * b + ratio + 1), (1.0,) + (2.0,) * (ratio - 1) + (1.0,)):
            if c < n_col:
                mt[b, c] = w
    return mt


def _nsa_prompt(q_t, comp_k, comp_vt, ks_aug, vs_t, kw, vw_t, gates_t, tq, kb):
    b_sz, _, _, s_len = q_t.shape
    nc = comp_k.shape[2]
    assert s_len % SEL_BLOCK == 0 and s_len % kb == 0 and kb % tq == 0 and s_len % tq == 0 and CMP_SPAN == 2
    assert tq % LANES == 0 and kb == tq
    n_blk = s_len // SEL_BLOCK
    assert n_blk <= MAX_SEL_BLOCKS
    n_cmp = s_len // CMP_STRIDE - CMP_SPAN + 1
    mt = jnp.asarray(_importance_matrix(MAX_SEL_BLOCKS, nc), BF16)

    def per_group(*shape):
        return pl.BlockSpec((1, 1) + shape, lambda b, g, i: (b, g) + (0,) * len(shape))

    return pl.pallas_call(
        functools.partial(_nsa_prompt_kernel, tq=tq, kb=kb, n_cmp=n_cmp, k_top=min(N_SEL, n_blk)),
        grid=(b_sz, KV_HEADS, s_len // tq),
        in_specs=[pl.BlockSpec((1, GQA_GROUP, HEAD_DIM, tq), lambda b, g, i: (b, g, 0, i)),
                  per_group(nc, HEAD_DIM), per_group(HEAD_DIM, nc),
                  per_group(s_len, HEAD_DIM + MAX_SEL_BLOCKS), per_group(s_len // LANES, VT_ROWS, LANES),
                  per_group(s_len, HEAD_DIM), per_group(s_len // LANES, VT_ROWS, LANES),
                  pl.BlockSpec((1, 1, GATE_ROWS, tq), lambda b, g, i: (b, g, 0, i)),
                  pl.BlockSpec((MAX_SEL_BLOCKS, nc), lambda b, g, i: (0, 0))],
        out_specs=pl.BlockSpec((1, tq, GQA_GROUP * HEAD_DIM), lambda b, g, i: (b, i, g)),
        out_shape=jax.ShapeDtypeStruct((b_sz, s_len, NSA_WIDTH), F32),
        compiler_params=_params("arbitrary", "arbitrary", "arbitrary"),
        name="nsa_prompt",
    )(q_t, comp_k, comp_vt, ks_aug, vs_t, kw, vw_t, gates_t, mt)


Q_COLS = LANES
COLS_PER_HEAD = Q_COLS // GQA_GROUP
V_OFF = LANES
V_COL0 = KV_HEADS * HEAD_DIM - V_OFF

def _nsa_sample_kernel(pt_ref, *refs, n_pages_step, n_steps, t_new, past, n_cmp, n_sel, k_top):
    del pt_ref
    p_cnt = n_pages_step
    page_refs = refs[:p_cnt]
    (qbd_ref, comp_ref, knew_ref, win_ref, wnew_ref, g_ref,
     o_ref, kv_scr, bias_scr, p_scr, m_scr, l_scr, acc_scr, oc_scr) = refs[p_cnt:]
    i = pl.program_id(1)
    qbd = qbd_ref[0]
    col = lax.broadcasted_iota(jnp.int32, (1, Q_COLS), 1)
    tok = col & (SUBLANES - 1)
    qpos = past + tok
    kwin = slice(0, 2 * LANES)
    vwin = slice(V_OFF, V_OFF + 2 * LANES)
    k_lanes = KV_HEADS * HEAD_DIM
    blocks_per_page = PAGE_SIZE // SEL_BLOCK
    nb_step = blocks_per_page * p_cnt
    keys_step = PAGE_SIZE * p_cnt

    def attend(rows_bf16, valid):
        s = _dot(rows_bf16[:, kwin], qbd)
        return jnp.where(valid, s, NEG)

    @pl.when((pl.program_id(0) == 0) & (i == 0))
    def _():
        key_blk = lax.broadcasted_iota(jnp.int32, (keys_step, 1), 0) // SEL_BLOCK
        lane_blk = lax.broadcasted_iota(jnp.int32, (1, 2 * LANES - k_lanes), 1)
        kv_scr[:, k_lanes:2 * LANES] = (key_blk == lane_blk).astype(BF16)

    @pl.when(i == 0)
    def _():
        comp = comp_ref[0]
        ncp = comp.shape[0]
        cid = lax.broadcasted_iota(jnp.int32, (ncp, 1), 0)
        valid = (cid >= 1) & (cid <= n_cmp) & (cid * CMP_STRIDE + (CMP_BLOCK - CMP_STRIDE - 1) <= qpos)
        s = attend(comp, valid)
        e, den = _softmax_parts(s, valid, 0)
        p = e * (1.0 / jnp.maximum(den, 1e-30))
        oc_scr[...] = _dot_tn(p.astype(BF16), comp[:, vwin])
        nbp = bias_scr.shape[0]
        ratio = SEL_BLOCK // CMP_STRIDE
        p_scr[0:ncp, :] = p
        p_scr[ncp:, :] = jnp.zeros((p_scr.shape[0] - ncp, Q_COLS), F32)
        taps = [p_scr[pl.ds(k, nbp, stride=ratio), :] for k in range(ratio + 1)]
        imp = taps[0] + taps[ratio]
        for k in range(1, ratio):
            imp = imp + 2.0 * taps[k]
        imp = imp + pltpu.roll(imp, COLS_PER_HEAD, 1)
        imp = imp + pltpu.roll(imp, 2 * COLS_PER_HEAD, 1)
        blk = lax.broadcasted_iota(jnp.int32, (nbp, 1), 0)
        cur = qpos // SEL_BLOCK
        score = jnp.where(blk <= cur, imp, -1.0)
        score = jnp.where((blk == 0) | (blk == cur) | (blk == cur - 1), FORCE, score)
        score = jnp.where(blk < n_sel, score, PAD_SCORE)
        blk_f = blk.astype(F32)
        bias = jnp.full((nbp, Q_COLS), NEG, F32)
        for _ in range(k_top):
            best = jnp.max(score, axis=0, keepdims=True)
            first = jnp.min(jnp.where(score == best, blk_f, float(nbp)), axis=0, keepdims=True)
            hit = blk_f == first
            bias = jnp.where(hit, 0.0, bias)
            score = jnp.where(hit, TAKEN_SCORE, score)
        bias_scr[...] = bias
        m_scr[...] = jnp.full_like(m_scr, NEG)
        l_scr[...] = jnp.zeros_like(l_scr)
        acc_scr[...] = jnp.zeros_like(acc_scr)

    def as_col(v):
        return jnp.broadcast_to(v, (SUBLANES, Q_COLS)).T[:, 0:1]

    def partial_softmax(s, v_rows):
        m_g = jnp.max(s, axis=0, keepdims=True)
        e = jnp.exp2(s - m_g)
        return m_g, jnp.sum(e, axis=0, keepdims=True), _dot_tn(e.astype(BF16), v_rows)

    def accumulate(*parts):
        m_run = m_scr[0:1, :]
        m_new = m_run
        for m_g, _, _ in parts:
            m_new = jnp.maximum(m_new, m_g)
        w_run = jnp.exp2(m_run - m_new)
        l_new = w_run * l_scr[0:1, :]
        acc_new = as_col(w_run) * acc_scr[...]
        for m_g, l_g, acc_g in parts:
            w_g = jnp.exp2(m_g - m_new)
            l_new = l_new + w_g * l_g
            acc_new = acc_new + as_col(w_g) * acc_g
        m_scr[0:1, :] = m_new
        l_scr[0:1, :] = l_new
        acc_scr[...] = acc_new

    bias_step = bias_scr[pl.ds(pl.multiple_of(i * nb_step, nb_step), nb_step), :].astype(BF16)
    pad = 2 * LANES - k_lanes - nb_step
    q_parts = [qbd[:k_lanes], bias_step] + ([jnp.zeros((pad, Q_COLS), BF16)] if pad else [])
    q_step = jnp.concatenate(q_parts, axis=0)
    for p in range(p_cnt):
        rows_p = slice(p * PAGE_SIZE, (p + 1) * PAGE_SIZE)
        tiles = [page_refs[p][0, cb].astype(BF16).T for cb in range(KV_ROW // LANES)]
        kv_scr[rows_p, 0:LANES] = tiles[0]
        kv_scr[rows_p, LANES:k_lanes] = tiles[1][:, :k_lanes - LANES]
        kv_scr[rows_p, 2 * LANES:3 * LANES] = tiles[1]
        kv_scr[rows_p, 3 * LANES:4 * LANES] = tiles[2]
    accumulate(partial_softmax(_dot(kv_scr[:, 0:2 * LANES], q_step), kv_scr[:, 2 * LANES:4 * LANES]))

    @pl.when(i == n_steps - 1)
    def _():
        knew = knew_ref[0].astype(BF16)
        krow = lax.broadcasted_iota(jnp.int32, (t_new, 1), 0)
        new_blk = past // SEL_BLOCK
        s_new = _dot(knew[:, kwin], qbd) + bias_scr[new_blk:new_blk + 1, :]
        accumulate(partial_softmax(jnp.where(krow <= tok, s_new, NEG), knew[:, vwin]))
        o_s = acc_scr[...] * as_col(1.0 / l_scr[0:1, :])

        wbuf = jnp.concatenate([win_ref[0, cb].astype(BF16).T for cb in range(KV_ROW // LANES)], axis=1)
        wnew = wnew_ref[0].astype(BF16)
        buf = wbuf.shape[0]
        pos_b = past - buf + lax.broadcasted_iota(jnp.int32, (buf, 1), 0)
        pos_n = past + krow
        ok_b = (qpos - pos_b >= 0) & (qpos - pos_b < WINDOW) & (pos_b >= 0)
        ok_n = (qpos - pos_n >= 0) & (qpos - pos_n < WINDOW)
        s_b = attend(wbuf, ok_b)
        s_n = attend(wnew, ok_n)
        m_w = jnp.maximum(jnp.max(s_b, axis=0, keepdims=True), jnp.max(s_n, axis=0, keepdims=True))
        e_b = jnp.where(ok_b, jnp.exp2(s_b - m_w), 0.0)
        e_n = jnp.where(ok_n, jnp.exp2(s_n - m_w), 0.0)
        den = jnp.sum(e_b, axis=0, keepdims=True) + jnp.sum(e_n, axis=0, keepdims=True)
        o_w = _dot_tn(e_b.astype(BF16), wbuf[:, vwin]) + _dot_tn(e_n.astype(BF16), wnew[:, vwin])
        o_w = o_w * jnp.broadcast_to(1.0 / jnp.maximum(den, 1e-30), (SUBLANES, Q_COLS)).T[:, 0:1]
        o_c = oc_scr[...]

        gt = g_ref[0]
        for g in range(KV_HEADS):
            for r in range(GQA_GROUP):
                r0 = r * COLS_PER_HEAD + g * SUBLANES
                c0 = V_COL0 + g * HEAD_DIM
                gc = (g * GQA_GROUP + r) * N_BRANCH
                blk_o = [o[r0:r0 + t_new, c0:c0 + HEAD_DIM] for o in (o_c, o_s, o_w)]
                h0 = (g * GQA_GROUP + r) * HEAD_DIM
                o_ref[0, :, h0:h0 + HEAD_DIM] = (gt[:, gc:gc + 1] * blk_o[0] + gt[:, gc + 1:gc + 2] * blk_o[1]
                                                 + gt[:, gc + 2:gc + 3] * blk_o[2])


def _nsa_sample(cache_slc, page_table, qbd, comp_tm, knew, win_t, wnew, gates, past):
    db, n_pages = page_table.shape
    t_new = knew.shape[1]
    assert t_new == SUBLANES and KV_HEADS * t_new <= COLS_PER_HEAD and past % SEL_BLOCK == 0 and t_new <= SEL_BLOCK
    p_cnt = min(32, n_pages)
    assert n_pages % p_cnt == 0
    n_steps = n_pages // p_cnt
    ncp = comp_tm.shape[1]
    n_str = -(-(past + t_new) // CMP_STRIDE)
    n_cmp = n_str - CMP_SPAN + 1
    n_sel = -(-(past + t_new) // SEL_BLOCK)
    blocks_step = (PAGE_SIZE // SEL_BLOCK) * p_cnt
    nbp = -(-(n_sel + 1) // blocks_step) * blocks_step
    keys_step = PAGE_SIZE * p_cnt
    assert blocks_step <= 2 * LANES - KV_HEADS * HEAD_DIM
    assert CMP_SPAN == 2
    p_rows = max(ncp, (SEL_BLOCK // CMP_STRIDE) * nbp) + SUBLANES
    buf = win_t.shape[-1]
    assert buf % LANES == 0

    def page_spec(p):
        return pl.BlockSpec((1, KV_ROW // LANES, LANES, PAGE_SIZE),
                            lambda b, i, pt: (pt[b * n_pages + i * p_cnt + p], 0, 0, 0))

    def per_b(*shape):
        return pl.BlockSpec((1,) + shape, lambda b, i, pt: (b,) + (0,) * len(shape))

    in_specs = [page_spec(p) for p in range(p_cnt)]
    in_specs += [per_b(2 * LANES, Q_COLS), per_b(ncp, KV_ROW),
                 per_b(t_new, KV_ROW), per_b(KV_ROW // LANES, LANES, buf), per_b(t_new, KV_ROW), per_b(t_new, LANES)]
    return pl.pallas_call(
        functools.partial(_nsa_sample_kernel, n_pages_step=p_cnt, n_steps=n_steps, t_new=t_new, past=past,
                          n_cmp=n_cmp, n_sel=n_sel, k_top=min(N_SEL, n_sel)),
        grid_spec=pltpu.PrefetchScalarGridSpec(
            num_scalar_prefetch=1, grid=(db, n_steps), in_specs=in_specs,
            out_specs=per_b(t_new, NSA_WIDTH),
            scratch_shapes=[pltpu.VMEM((keys_step, 4 * LANES), BF16),
                            pltpu.VMEM((nbp, Q_COLS), F32), pltpu.VMEM((p_rows, Q_COLS), F32),
                            pltpu.VMEM((SUBLANES, Q_COLS), F32), pltpu.VMEM((SUBLANES, Q_COLS), F32),
                            pltpu.VMEM((Q_COLS, 2 * LANES), F32), pltpu.VMEM((Q_COLS, 2 * LANES), F32)]),
        out_shape=jax.ShapeDtypeStruct((db, t_new, NSA_WIDTH), F32),
        compiler_params=_params("arbitrary", "arbitrary"),
        name="nsa_sample",
    )(page_table.reshape(-1), *([cache_slc] * p_cnt), qbd, comp_tm, knew, win_t, wnew, gates)


def _out_kernel(x_ref, po_ref, zp_ref, no_ref, zn_ref, gate_ref, gnp_ref, gnn_ref, w_ref, fn_ref, y_ref):
    zp = zp_ref[0]
    zn = zn_ref[0]
    mp = _rms(po_ref[0], gnp_ref[...]) * (zp * _sigmoid(zp))
    mn = _rms(no_ref[0], gnn_ref[...]) * (zn * _sigmoid(zn))
    m = jnp.concatenate([mp, mn], axis=-1).astype(BF16)
    xo = x_ref[0] + gate_ref[0] * _dot(m, w_ref[...])
    y_ref[0] = _rms(xo, fn_ref[...])


def _out(x, pool_o, zp, nsa_o, zn, gate, gn_pool, gn_nsa, w_out_bf16, final_norm, tm):
    nb, t, d = x.shape
    r = gate.shape[1]
    assert t % tm == 0 and r in (1, t)
    if r == 1:
        gate_spec = pl.BlockSpec((1, 1, d), lambda b, i: (b, 0, 0))
    else:
        gate_spec = pl.BlockSpec((1, tm, d), lambda b, i: (b, i, 0))

    def tok(width):
        return pl.BlockSpec((1, tm, width), lambda b, i: (b, i, 0))

    def const(*shape):
        return pl.BlockSpec(shape, lambda b, i: (0,) * len(shape))

    return pl.pallas_call(
        _out_kernel,
        grid=(nb, t // tm),
        in_specs=[tok(d), tok(POOL_WIDTH), tok(POOL_WIDTH), tok(NSA_WIDTH), tok(NSA_WIDTH), gate_spec,
                  const(1, POOL_WIDTH), const(1, NSA_WIDTH), const(d, d), const(1, d)],
        out_specs=tok(d),
        out_shape=jax.ShapeDtypeStruct((nb, t, d), F32),
        compiler_params=_params("arbitrary", "arbitrary"),
        name="out",
    )(x, pool_o, zp, nsa_o, zn, gate, gn_pool.reshape(1, -1), gn_nsa.reshape(1, -1), w_out_bf16,
      final_norm.reshape(1, d))


def _pages_t(pages):
    n = pages.shape[0]
    return pages.transpose(0, 2, 3, 4, 1).reshape(n, KV_ROW // LANES, LANES, PAGE_SIZE)


def _block_diag(pool_w):
    n, c, _ = pool_w.shape
    eye = jnp.eye(n, dtype=pool_w.dtype)
    return (eye[:, None, :, None] * pool_w[:, :, None, :]).reshape(n * c, n * c)


def _first_layer(phi_w1, phi_pe):
    w = phi_w1.reshape(CMP_SPAN, CMP_STRIDE, 2, HEAD_DIM, CMP_HID)
    w1cat = w.transpose(2, 1, 3, 0, 4).reshape(2, CMP_STRIDE * HEAD_DIM, CMP_SPAN * CMP_HID)
    pe = phi_pe.reshape(CMP_SPAN, CMP_STRIDE, 2, HEAD_DIM).transpose(2, 0, 1, 3).reshape(2, CMP_SPAN, -1)
    pe2 = jnp.concatenate([pe, jnp.zeros((2, SUBLANES - CMP_SPAN, pe.shape[-1]), pe.dtype)], axis=1)
    return w1cat.astype(BF16), pe2


def kernel(x_prompt, x_sample, c_prompt, c_sample, cache_cmp_kv, cache_slc_kv, state_win_kv, state_pool, page_table, norm_w, w_ada, b_ada, w_in, pool_w, pool_scale, phi_w1, phi_pe, phi_w2, gn_pool, gn_nsa, w_out, final_norm):
    b_sz, s_len, d = x_prompt.shape
    db, t_new, _ = x_sample.shape
    depth = norm_w.shape[0]
    assert depth == 1 and w_in.shape[-1] == IN_WIDTH and d // 4 == POOL_WIDTH
    n_phys = cache_cmp_kv.shape[1]
    n_pages = page_table.shape[1]
    past = n_pages * PAGE_SIZE
    lyr = 0

    n_c = b_sz + db
    c_all = jnp.concatenate([c_prompt, c_sample, jnp.zeros((-n_c % SUBLANES, d), F32)], axis=0)
    ada = _ada(c_all, w_ada[lyr], b_ada[lyr])
    shift, scale, gate = ada[:, :d], ada[:, d:2 * d], ada[:, 2 * d:]

    w_main = _cast_columns(w_in[lyr], IN_MAIN, 5 * LANES, BF16)
    w_gate = jnp.pad(w_in[lyr][:, IN_MAIN:].astype(BF16), ((0, 0), (0, LANES - GATE_COLS)))
    w_out_bf16 = w_out[lyr].astype(BF16)
    wbd = _block_diag(pool_w[lyr]).astype(BF16)
    w1cat, pe2 = _first_layer(phi_w1[lyr], phi_pe[lyr])
    w2 = phi_w2[lyr].astype(BF16)
    n_tok_s = db * t_new

    def per_token(v):
        return jnp.repeat(v[b_sz:n_c], t_new, axis=0)[None]

    tm = min(512, s_len)
    (u_p, zp_p, zn_p, q_t, ks_aug, kw_p, vs_t, vw_t, g_t, kvc_t, kvs_t, kvw_t) = _in_proj(
        x_prompt, scale[:b_sz, None], shift[:b_sz, None], norm_w[lyr], w_main, w_gate, tm, True)
    pool_p = _pool(jnp.zeros((b_sz, POOL_HALO, POOL_WIDTH), F32), u_p, wbd, pool_scale[lyr], 0, tm)
    pages_p = s_len // PAGE_SIZE
    in_order = jnp.tile(jnp.arange(pages_p, dtype=jnp.int32), (b_sz, 1))
    comp_k, comp_vt = _compress(kvc_t, in_order, w1cat, pe2, w2, None, False, True)
    nsa_p = _nsa_prompt(q_t, comp_k, comp_vt, ks_aug, vs_t, kw_p, vw_t, g_t, tq=512, kb=min(512, s_len))
    y_p = _out(x_prompt, pool_p, zp_p, nsa_p, zn_p, gate[:b_sz, None], gn_pool[lyr], gn_nsa[lyr], w_out_bf16,
               final_norm, tm)

    (u_s, zp_s, zn_s, kvc_s, kvs_s, kvw_s, q_s, g_s) = _in_proj(
        x_sample.reshape(1, n_tok_s, d), per_token(scale), per_token(shift), norm_w[lyr], w_main, w_gate, n_tok_s,
        False)
    u_s3 = u_s.reshape(db, t_new, POOL_WIDTH)
    hist = jnp.concatenate([jnp.zeros((db, POOL_HALO - state_pool.shape[2], POOL_WIDTH), F32), state_pool[lyr]], axis=1)
    pool_s = _pool(hist, u_s3, wbd, pool_scale[lyr], past, t_new)
    kvc_s3 = kvc_s.reshape(db, t_new, KV_ROW)
    kvs_s3 = kvs_s.reshape(db, t_new, KV_ROW)
    kvw_s3 = kvw_s.reshape(db, t_new, KV_ROW)
    xnew = jnp.pad(kvc_s3, ((0, 0), (0, CMP_STRIDE - t_new), (0, 0)))
    xnew = xnew.reshape(db, CMP_STRIDE, N_KV_SLABS, HEAD_DIM).transpose(0, 2, 1, 3).reshape(db, N_KV_SLABS, -1)
    comp_s = _compress(_pages_t(cache_cmp_kv[lyr]), page_table, w1cat, pe2, w2, xnew, True, False)
    q5 = q_s.reshape(db, t_new, KV_HEADS, GQA_GROUP, HEAD_DIM)
    eye = jnp.eye(KV_HEADS, dtype=BF16)
    qbd = q5.transpose(0, 2, 4, 3, 1)[:, :, :, :, None, :] * eye[None, :, None, None, :, None]
    qbd = jnp.pad(qbd.reshape(db, KV_HEADS * HEAD_DIM, GQA_GROUP, KV_HEADS * t_new),
                  ((0, 0), (0, 2 * LANES - KV_HEADS * HEAD_DIM), (0, 0), (0, COLS_PER_HEAD - KV_HEADS * t_new)))
    qbd = qbd.reshape(db, 2 * LANES, Q_COLS)
    g_s3 = g_s.reshape(db, t_new, LANES)
    buf = state_win_kv.shape[2]
    win_t = state_win_kv[lyr].transpose(0, 2, 3, 4, 1).reshape(db, KV_ROW // LANES, LANES, buf)
    nsa_s = _nsa_sample(_pages_t(cache_slc_kv[lyr]), page_table, qbd, comp_s, kvs_s3,
                        win_t, kvw_s3, g_s3, past)
    y_s = _out(x_sample.reshape(1, n_tok_s, d), pool_s.reshape(1, n_tok_s, POOL_WIDTH), zp_s,
               nsa_s.reshape(1, n_tok_s, NSA_WIDTH), zn_s, per_token(gate), gn_pool[lyr], gn_nsa[lyr], w_out_bf16,
               final_norm, n_tok_s).reshape(db, t_new, d)

    kv_shape = (2, KV_HEADS, HEAD_DIM)
    win_len = min(WINDOW, s_len)
    new_win_s = jnp.concatenate([state_win_kv[lyr], kvw_s3.reshape((db, t_new) + kv_shape)], axis=1)[:, -buf:]
    hist_len = state_pool.shape[2]
    new_pool_s = jnp.concatenate([state_pool[lyr], u_s3], axis=1)[:, -hist_len:]

    def rows_from_t(v_t):
        return v_t.reshape((v_t.shape[0],) + kv_shape + (v_t.shape[-1],)).transpose(0, 4, 1, 2, 3)[None]

    return (y_p, y_s,
            rows_from_t(kvc_t), kvc_s3.reshape((1, db, t_new) + kv_shape),
            rows_from_t(kvs_t), kvs_s3.reshape((1, db, t_new) + kv_shape),
            rows_from_t(kvw_t[..., s_len - win_len:]), new_win_s[None],
            u_p[:, s_len - hist_len:][None], new_pool_s[None])
```

```python
import functools

import numpy as np
import jax
import jax.numpy as jnp
from jax import lax
from jax.experimental import pallas as pl
from jax.experimental.pallas import tpu as pltpu

F32 = jnp.float32
BF16 = jnp.bfloat16

HEAD_DIM = 64
GQA_GROUP = 4
KV_HEADS = 3
N_HEADS = KV_HEADS * GQA_GROUP
N_KV_SLABS = 2 * KV_HEADS
KV_ROW = N_KV_SLABS * HEAD_DIM
N_BRANCH = 3
POOL_WINDOWS = (2, 4, 8, 16)
POOL_HALO = 16
CMP_BLOCK = 32
CMP_STRIDE = 16
CMP_SPAN = CMP_BLOCK // CMP_STRIDE
CMP_HID = 2 * HEAD_DIM
SEL_BLOCK = 64
N_SEL = 16
WINDOW = 512
PAGE_SIZE = 128
CHUNKS_PER_PAGE = PAGE_SIZE // CMP_STRIDE
CHUNK_PITCH = 24
EPS = 1e-6
NEG = -1e30
FORCE = 1e4
LOG2_E = 1.4426950408889634
VT_ROWS = 80
PAD_SCORE = -1e38
TAKEN_SCORE = -3e38

LANES = 128
SUBLANES = 8
VMEM_LIMIT_BYTES = 56 * 1024 * 1024


def _dot(a, b):
    return jnp.dot(a, b, preferred_element_type=F32)


def _dot_nt(a, b):
    return lax.dot_general(a, b, (((1,), (1,)), ((), ())), preferred_element_type=F32)


def _dot_tn(a, b):
    return lax.dot_general(a, b, (((0,), (0,)), ((), ())), preferred_element_type=F32)


def _sigmoid(v):
    return 1.0 / (1.0 + jnp.exp(-v))


def _rms(v, g):
    return v * lax.rsqrt(jnp.mean(v * v, axis=-1, keepdims=True) + EPS) * g


def _split3(v):
    hi = v.astype(BF16)
    r1 = v - hi.astype(F32)
    mid = r1.astype(BF16)
    lo = (r1 - mid.astype(F32)).astype(BF16)
    return hi, mid, lo


def _params(*sem):
    return pltpu.CompilerParams(dimension_semantics=sem, vmem_limit_bytes=VMEM_LIMIT_BYTES)


def _ada_kernel(c_ref, w_ref, b_ref, o_ref):
    c = c_ref[...]
    a = (c * _sigmoid(c)).astype(BF16)
    o_ref[...] = _dot(a, w_ref[...].astype(BF16)) + b_ref[...]


def _ada(c, w_ada, b_ada):
    m, d = c.shape
    n = w_ada.shape[1]
    tn = 512
    return pl.pallas_call(
        _ada_kernel,
        grid=(n // tn,),
        in_specs=[pl.BlockSpec((m, d), lambda j: (0, 0)),
                  pl.BlockSpec((d, tn), lambda j: (0, j)),
                  pl.BlockSpec((1, tn), lambda j: (0, j))],
        out_specs=pl.BlockSpec((m, tn), lambda j: (0, j)),
        out_shape=jax.ShapeDtypeStruct((m, n), F32),
        compiler_params=_params("arbitrary"),
        name="ada",
    )(c, w_ada, b_ada.reshape(1, n))


POOL_WIDTH = 256
NSA_WIDTH = N_HEADS * HEAD_DIM
GATE_COLS = N_BRANCH * N_HEADS
MAX_SEL_BLOCKS = LANES - HEAD_DIM
_SEG = {}
_off = 0
for _name, _w in (("u", POOL_WIDTH), ("zp", POOL_WIDTH), ("q", NSA_WIDTH), ("zn", NSA_WIDTH),
                  ("kvc", KV_ROW), ("kvs", KV_ROW), ("kvw", KV_ROW)):
    _SEG[_name] = (_off, _w)
    _off += _w
IN_MAIN = _off
IN_WIDTH = IN_MAIN + GATE_COLS


GATE_ROWS = 16


def _in_proj_kernel(x_ref, sc_ref, sh_ref, nw_ref, w_ref, wg_ref, u_ref, zp_ref, zn_ref, *rest, attn_layouts):
    tm = x_ref.shape[1]
    h = (_rms(x_ref[0], nw_ref[...]) * (1.0 + sc_ref[0]) + sh_ref[0]).astype(BF16)

    def seg(name):
        s, w = _SEG[name]
        return _dot_nt(h, w_ref[s:s + w, :])

    u_ref[0] = seg("u")
    zp_ref[0] = seg("zp")
    zn_ref[0] = seg("zn")
    q = seg("q") * (HEAD_DIM ** -0.5 * LOG2_E)
    kvc = seg("kvc")
    kvs = seg("kvs")
    kvw = seg("kvw")
    sig = _sigmoid(_dot_nt(h, wg_ref[...]))
    if not attn_layouts:
        kvc_ref, kvs_ref, kvw_ref, q_ref, g_ref = rest
        kvc_ref[0] = kvc
        kvs_ref[0] = kvs
        kvw_ref[0] = kvw
        q_ref[0] = q.astype(BF16)
        g_ref[0] = sig
        return

    qt_ref, ks_aug_ref, kw_ref, vst_ref, vwt_ref, gt_ref, kvct_ref, kvst_ref, kvwt_ref = rest
    qt_ref[0] = q.T.astype(BF16).reshape(N_HEADS, HEAD_DIM, tm)
    kvs_t32 = kvs.T
    kvw_t32 = kvw.T
    kvct_ref[0] = kvc.T.reshape(KV_ROW // LANES, LANES, tm)
    kvst_ref[0] = kvs_t32.reshape(KV_ROW // LANES, LANES, tm)
    kvwt_ref[0] = kvw_t32.reshape(KV_ROW // LANES, LANES, tm)
    pos = pl.program_id(1) * tm + lax.broadcasted_iota(jnp.int32, (tm, 1), 0)
    onehot = (lax.broadcasted_iota(jnp.int32, (1, MAX_SEL_BLOCKS), 1) == pos // SEL_BLOCK).astype(BF16)
    kvs_t = kvs_t32.astype(BF16)
    kvw_t = kvw_t32.astype(BF16)
    v0 = KV_HEADS * HEAD_DIM
    tail = (lax.broadcasted_iota(jnp.int32, (VT_ROWS - HEAD_DIM, LANES), 0) == 0).astype(BF16)
    for g in range(KV_HEADS):
        ks_aug_ref[0, g] = jnp.concatenate([kvs[:, g * HEAD_DIM:(g + 1) * HEAD_DIM].astype(BF16), onehot], axis=1)
        kw_ref[0, g] = kvw[:, g * HEAD_DIM:(g + 1) * HEAD_DIM].astype(BF16)
        for c in range(tm // LANES):
            sl = (slice(v0 + g * HEAD_DIM, v0 + (g + 1) * HEAD_DIM), slice(c * LANES, (c + 1) * LANES))
            vst_ref[0, g, c] = jnp.concatenate([kvs_t[sl], tail], axis=0)
            vwt_ref[0, g, c] = jnp.concatenate([kvw_t[sl], tail], axis=0)
        per_group = GQA_GROUP * N_BRANCH
        rolled = sig if g == 0 else pltpu.roll(sig, LANES - per_group * g, 1)
        gt_ref[0, g] = rolled.T[:GATE_ROWS, :]


def _cast_kernel(w_ref, o_ref):
    o_ref[...] = w_ref[...].astype(o_ref.dtype)


def _cast_rows(w, n_rows, tile, dtype):
    cols = w.shape[1]
    assert n_rows % tile == 0 and tile % SUBLANES == 0 and n_rows <= w.shape[0] and cols % LANES == 0
    return pl.pallas_call(
        _cast_kernel,
        grid=(n_rows // tile,),
        in_specs=[pl.BlockSpec((tile, cols), lambda j: (j, 0))],
        out_specs=pl.BlockSpec((tile, cols), lambda j: (j, 0)),
        out_shape=jax.ShapeDtypeStruct((n_rows, cols), dtype),
        compiler_params=_params("arbitrary"),
        name="cast_rows",
    )(w)


def _in_proj(x, scale, shift, norm_w, w_main, w_gate, tm, attn_layouts):
    nb, t, d = x.shape
    r = scale.shape[1]
    assert t % tm == 0 and r in (1, t) and (tm % LANES == 0 or not attn_layouts)
    if r == 1:
        mod_spec = pl.BlockSpec((1, 1, d), lambda b, i: (b, 0, 0))
    else:
        mod_spec = pl.BlockSpec((1, tm, d), lambda b, i: (b, i, 0))

    def tok(width):
        return pl.BlockSpec((1, tm, width), lambda b, i: (b, i, 0))

    def hm(n, width=HEAD_DIM):
        return pl.BlockSpec((1, n, tm, width), lambda b, i: (b, 0, i, 0))

    def sd(*shape, dtype=F32):
        return jax.ShapeDtypeStruct(shape, dtype)

    out_specs = [tok(POOL_WIDTH), tok(POOL_WIDTH), tok(NSA_WIDTH)]
    out_shape = [sd(nb, t, POOL_WIDTH), sd(nb, t, POOL_WIDTH), sd(nb, t, NSA_WIDTH)]
    if attn_layouts:
        lane_tiles = tm // LANES
        vt_spec = pl.BlockSpec((1, KV_HEADS, lane_tiles, VT_ROWS, LANES), lambda b, i: (b, 0, i, 0, 0))
        vt_shape = sd(nb, KV_HEADS, t // LANES, VT_ROWS, LANES, dtype=BF16)
        rows_t_spec = pl.BlockSpec((1, KV_ROW // LANES, LANES, tm), lambda b, i: (b, 0, 0, i))
        rows_t_shape = sd(nb, KV_ROW // LANES, LANES, t)
        out_specs += [pl.BlockSpec((1, N_HEADS, HEAD_DIM, tm), lambda b, i: (b, 0, 0, i)),
                      hm(KV_HEADS, HEAD_DIM + MAX_SEL_BLOCKS), hm(KV_HEADS), vt_spec, vt_spec,
                      pl.BlockSpec((1, KV_HEADS, GATE_ROWS, tm), lambda b, i: (b, 0, 0, i)),
                      rows_t_spec, rows_t_spec, rows_t_spec]
        out_shape += [sd(nb, N_HEADS, HEAD_DIM, t, dtype=BF16),
                      sd(nb, KV_HEADS, t, HEAD_DIM + MAX_SEL_BLOCKS, dtype=BF16),
                      sd(nb, KV_HEADS, t, HEAD_DIM, dtype=BF16), vt_shape, vt_shape,
                      sd(nb, KV_HEADS, GATE_ROWS, t), rows_t_shape, rows_t_shape, rows_t_shape]
    else:
        out_specs += [tok(KV_ROW), tok(KV_ROW), tok(KV_ROW), tok(NSA_WIDTH), tok(LANES)]
        out_shape += [sd(nb, t, KV_ROW), sd(nb, t, KV_ROW), sd(nb, t, KV_ROW),
                      sd(nb, t, NSA_WIDTH, dtype=BF16), sd(nb, t, LANES)]
    return pl.pallas_call(
        functools.partial(_in_proj_kernel, attn_layouts=attn_layouts),
        grid=(nb, t // tm),
        in_specs=[tok(d), mod_spec, mod_spec,
                  pl.BlockSpec((1, d), lambda b, i: (0, 0)),
                  pl.BlockSpec((IN_MAIN, d), lambda b, i: (0, 0)),
                  pl.BlockSpec((LANES, d), lambda b, i: (0, 0))],
        out_specs=out_specs,
        out_shape=out_shape,
        compiler_params=_params("arbitrary", "arbitrary"),
        name="in_proj",
    )(x, scale, shift, norm_w.reshape(1, d), w_main, w_gate)


def _pool_kernel(hist_ref, uprev_ref, u_ref, wbd_ref, ps_ref, o_ref, *, pos0, tp):
    i = pl.program_id(1)
    c = u_ref.shape[-1]
    grp = lax.broadcasted_iota(jnp.int32, (1, c), 1) // (c // len(POOL_WINDOWS))
    pos = pos0 + i * tp + lax.broadcasted_iota(jnp.int32, (tp, 1), 0)
    win = jnp.full((1, c), float(POOL_WINDOWS[-1]), F32)
    for gi in range(len(POOL_WINDOWS) - 2, -1, -1):
        win = jnp.where(grp == gi, float(POOL_WINDOWS[gi]), win)
    cnt = jnp.minimum(win, (pos + 1).astype(F32))
    for bi in range(u_ref.shape[0]):
        u = u_ref[bi]
        halo = jnp.where(i == 0, hist_ref[bi], uprev_ref[bi])
        sums = []
        s = jnp.concatenate([halo, u], axis=0)
        for w in POOL_WINDOWS:
            s = s + pltpu.roll(s, w // 2, 0)
            sums.append(s[POOL_HALO:])
        tot = sums[-1]
        for gi in range(len(POOL_WINDOWS) - 2, -1, -1):
            tot = jnp.where(grp == gi, sums[gi], tot)
        dlt = tot / cnt - u
        o_ref[bi] = _dot(dlt.astype(BF16), wbd_ref[...]) * ps_ref[...]


def _pool(hist, u, wbd, pool_scale, pos0, tp):
    nb, t, c = u.shape
    assert t % tp == 0
    if t >= POOL_HALO:
        assert tp % POOL_HALO == 0
        uprev, ratio, bb = u, tp // POOL_HALO, 1
        prev_spec = pl.BlockSpec((bb, POOL_HALO, c), lambda b, i: (b, jnp.maximum(i * ratio - 1, 0), 0))
    else:
        assert t == tp
        uprev, bb = hist, nb
        prev_spec = pl.BlockSpec((bb, POOL_HALO, c), lambda b, i: (b, 0, 0))
    return pl.pallas_call(
        functools.partial(_pool_kernel, pos0=pos0, tp=tp),
        grid=(nb // bb, t // tp),
        in_specs=[pl.BlockSpec((bb, POOL_HALO, c), lambda b, i: (b, 0, 0)), prev_spec,
                  pl.BlockSpec((bb, tp, c), lambda b, i: (b, i, 0)),
                  pl.BlockSpec((c, c), lambda b, i: (0, 0)),
                  pl.BlockSpec((1, c), lambda b, i: (0, 0))],
        out_specs=pl.BlockSpec((bb, tp, c), lambda b, i: (b, i, 0)),
        out_shape=jax.ShapeDtypeStruct((nb, t, c), F32),
        compiler_params=_params("arbitrary", "arbitrary"),
        name="pool",
    )(hist, uprev, u, wbd, pool_scale.reshape(1, c))


def _gelu_tanh(v):
    return 0.5 * v * (1.0 + jnp.tanh(np.sqrt(2.0 / np.pi).astype(np.float32) * (v + 0.044715 * (v * v * v))))


def _compress_kernel(pt_ref, *refs, n_pages_step, n_steps, has_new, token_major):
    del pt_ref
    p_cnt = n_pages_step
    page_refs = refs[:p_cnt]
    w1p_ref, w1_ref, pe_ref, w2_ref = refs[p_cnt:p_cnt + 4]
    k = p_cnt + 4
    a1new_ref = None
    if has_new:
        a1new_ref = refs[k]
        k += 1
    if token_major:
        out_ref = refs[k]
        k += 1
    else:
        outk_ref, outvt_ref = refs[k:k + 2]
        k += 2
    slab_scr, carry_scr, bias_scr = refs[k:k + 3]
    i = pl.program_id(1)
    rows = CHUNKS_PER_PAGE * p_cnt
    row_id = lax.broadcasted_iota(jnp.int32, (rows, 1), 0)

    @pl.when(i == 0)
    def _():
        carry_scr[...] = jnp.zeros_like(carry_scr)
        for kk in range(2):
            pb = _dot(pe_ref[kk].astype(BF16), w1_ref[kk])
            bias_scr[kk] = pb[0:1, :CMP_HID] + pb[1:2, CMP_HID:]

    def store(kg, comp):
        if token_major:
            out_ref[0, :, kg * HEAD_DIM:(kg + 1) * HEAD_DIM] = comp.astype(BF16)
        elif kg < KV_HEADS:
            outk_ref[0, kg] = comp.astype(BF16)
        else:
            wide = jnp.concatenate([comp, jnp.zeros((rows, LANES - HEAD_DIM), F32)], axis=1)
            outvt_ref[0, kg - KV_HEADS] = wide.T[:HEAD_DIM, :].astype(BF16)

    def finish(kg, a0_prev, a1):
        kk = kg // KV_HEADS
        pre = a0_prev + a1 + bias_scr[kk]
        return _dot(_gelu_tanh(pre).astype(BF16), w2_ref[kk])

    def main():
        for cb in range(KV_ROW // LANES):
            for p in range(p_cnt):
                rows_t = page_refs[p][0, cb].astype(BF16).T.astype(F32)
                for ch in range(CHUNKS_PER_PAGE):
                    r0 = (p * CHUNKS_PER_PAGE + ch) * CHUNK_PITCH
                    slab_scr[cb, r0:r0 + CMP_STRIDE, :] = rows_t[ch * CMP_STRIDE:(ch + 1) * CMP_STRIDE, :]
            x = jnp.concatenate([slab_scr[cb, pl.ds(j, rows, stride=CHUNK_PITCH), :] for j in range(CMP_STRIDE)],
                                axis=1).astype(BF16)
            a_both = _dot(x, w1p_ref[cb])
            for half in range(2):
                kg = 2 * cb + half
                a = a_both[:, half * 2 * CMP_HID:(half + 1) * 2 * CMP_HID]
                a0 = a[:, :CMP_HID]
                a0_prev = jnp.where(row_id == 0, carry_scr[kg], pltpu.roll(a0, 1, 0))
                carry_scr[kg] = a0[rows - 1:rows, :]
                store(kg, finish(kg, a0_prev, a[:, CMP_HID:]))

    if not has_new:
        main()
    else:
        pl.when(i < n_steps)(main)

        @pl.when(i == n_steps)
        def _():
            for kg in range(N_KV_SLABS):
                comp = finish(kg, carry_scr[kg], a1new_ref[0, kg:kg + 1, :])
                store(kg, jnp.where(row_id == 0, jnp.broadcast_to(comp, (rows, HEAD_DIM)), 0.0))


def _new_chunk_kernel(x_ref, w1_ref, o_ref):
    for kg in range(N_KV_SLABS):
        o_ref[kg] = _dot(x_ref[kg].astype(BF16), w1_ref[kg // KV_HEADS])[:, CMP_HID:]


def _new_chunk_first_layer(xnew, w1cat):
    n_slab, nb, xw = xnew.shape
    return pl.pallas_call(
        _new_chunk_kernel,
        out_shape=jax.ShapeDtypeStruct((n_slab, nb, CMP_HID), F32),
        compiler_params=pltpu.CompilerParams(vmem_limit_bytes=VMEM_LIMIT_BYTES),
        name="new_chunk",
    )(xnew, w1cat)


def _compress(pages, page_table, w1cat, pe2, w2, a1new, token_major, per_batch):
    nb, n_pages = page_table.shape
    p_cnt = min(32, n_pages)
    assert n_pages % p_cnt == 0
    n_steps = n_pages // p_cnt
    has_new = a1new is not None
    rows = CHUNKS_PER_PAGE * p_cnt
    grid_steps = n_steps + (1 if has_new else 0)
    tot_rows = rows * grid_steps
    xw = CMP_STRIDE * HEAD_DIM

    def page_spec(p):
        def imap(b, i, pt):
            step = jnp.minimum(i, n_steps - 1)
            page = pt[b * n_pages + step * p_cnt + p]
            return (b, 0, 0, page) if per_batch else (page, 0, 0, 0)
        return pl.BlockSpec((1, KV_ROW // LANES, LANES, PAGE_SIZE), imap)

    n_cb = KV_ROW // LANES
    w4 = w1cat.reshape(2, CMP_STRIDE, HEAD_DIM, 2 * CMP_HID)
    zero = jnp.zeros_like(w4[0])
    w1p = jnp.stack([jnp.concatenate([jnp.concatenate([w4[(2 * cb) // KV_HEADS], zero], axis=2),
                                      jnp.concatenate([zero, w4[(2 * cb + 1) // KV_HEADS]], axis=2)], axis=1)
                     for cb in range(n_cb)]).reshape(n_cb, CMP_STRIDE * LANES, 4 * CMP_HID)
    in_specs = [page_spec(p) for p in range(p_cnt)]
    in_specs += [pl.BlockSpec((n_cb, CMP_STRIDE * LANES, 4 * CMP_HID), lambda b, i, pt: (0, 0, 0)),
                 pl.BlockSpec((2, xw, 2 * CMP_HID), lambda b, i, pt: (0, 0, 0)),
                 pl.BlockSpec((2, SUBLANES, xw), lambda b, i, pt: (0, 0, 0)),
                 pl.BlockSpec((2, CMP_HID, HEAD_DIM), lambda b, i, pt: (0, 0, 0))]
    args = [pages] * p_cnt + [w1p, w1cat, pe2, w2]
    if has_new:
        in_specs.append(pl.BlockSpec((1, N_KV_SLABS, CMP_HID), lambda b, i, pt: (b, 0, 0)))
        args.append(a1new)
    if token_major:
        out_spec = pl.BlockSpec((1, rows, KV_ROW), lambda b, i, pt: (b, i, 0))
        out_shape = jax.ShapeDtypeStruct((nb, tot_rows, KV_ROW), BF16)
    else:
        assert rows % LANES == 0
        out_spec = [pl.BlockSpec((1, KV_HEADS, rows, HEAD_DIM), lambda b, i, pt: (b, 0, i, 0)),
                    pl.BlockSpec((1, KV_HEADS, HEAD_DIM, rows), lambda b, i, pt: (b, 0, 0, i))]
        out_shape = [jax.ShapeDtypeStruct((nb, KV_HEADS, tot_rows, HEAD_DIM), BF16),
                     jax.ShapeDtypeStruct((nb, KV_HEADS, HEAD_DIM, tot_rows), BF16)]
    return pl.pallas_call(
        functools.partial(_compress_kernel, n_pages_step=p_cnt, n_steps=n_steps, has_new=has_new,
                          token_major=token_major),
        grid_spec=pltpu.PrefetchScalarGridSpec(
            num_scalar_prefetch=1, grid=(nb, grid_steps), in_specs=in_specs, out_specs=out_spec,
            scratch_shapes=[pltpu.VMEM((n_cb, rows * CHUNK_PITCH, LANES), F32),
                            pltpu.VMEM((N_KV_SLABS, 1, CMP_HID), F32),
                            pltpu.VMEM((2, 1, CMP_HID), F32)]),
        out_shape=out_shape,
        compiler_params=_params("arbitrary", "arbitrary"),
        name="compress_new" if has_new else "compress",
    )(page_table.reshape(-1), *args)


def _softmax_parts(s, valid, axis):
    s = jnp.where(valid, s, NEG)
    m = jnp.max(s, axis=axis, keepdims=True)
    e = jnp.where(valid, jnp.exp2(s - m), 0.0)
    return e, jnp.sum(e, axis=axis, keepdims=True)


def _nsa_prompt_kernel(qt_ref, kc_ref, vct_ref, ks_ref, vst_ref, kw_ref, vwt_ref, gt_ref, mt_ref, o_ref, *,
                       tq, kb, n_cmp, k_top):
    qi = pl.program_id(2)
    qs = qi * tq
    r4 = GQA_GROUP
    cols = r4 * tq
    qt = jnp.concatenate([qt_ref[0, r] for r in range(r4)], axis=1)
    tpos = qs + lax.broadcasted_iota(jnp.int32, (1, tq), 1)

    def per_head(v):
        return jnp.concatenate([v] * r4, axis=1)

    def value_tiles(ref, k0, width):
        j0 = k0 // LANES
        return jnp.concatenate([ref[0, 0, j0 + j] for j in range(width // LANES)], axis=1)

    def normalise(acc):
        return acc[:HEAD_DIM] * (1.0 / acc[HEAD_DIM:HEAD_DIM + 1])

    nc = kc_ref.shape[2]
    s = _dot(kc_ref[0, 0], qt)
    cid = lax.broadcasted_iota(jnp.int32, (nc, 1), 0)
    valid = (cid >= 1) & (cid <= n_cmp) & (cid * CMP_STRIDE + (CMP_BLOCK - CMP_STRIDE - 1) <= tpos)
    s = s + per_head(jnp.where(valid, 0.0, NEG))
    e = jnp.exp2(s - jnp.max(s, axis=0, keepdims=True))
    den = jnp.sum(e, axis=0, keepdims=True)
    has_key = per_head(jnp.where(tpos >= CMP_BLOCK - 1, 1.0, 0.0)) if n_cmp >= 1 else 0.0
    p = e * (has_key / jnp.maximum(den, 1e-30))
    o_c = _dot(vct_ref[0, 0], p.astype(BF16))
    pg = p[:, 0:tq]
    for r in range(1, r4):
        pg = pg + p[:, r * tq:(r + 1) * tq]

    mt = mt_ref[...]
    imp = sum(_dot(mt, piece) for piece in _split3(pg))
    n_blk = imp.shape[0]
    blk = lax.broadcasted_iota(jnp.int32, (n_blk, 1), 0)
    cur = (qs + lax.broadcasted_iota(jnp.int32, (1, tq), 1)) // SEL_BLOCK
    score = jnp.where(blk <= cur, imp, -1.0)
    score = jnp.where((blk == 0) | (blk == cur) | (blk == cur - 1), FORCE, score)
    sub = lax.broadcasted_iota(jnp.int32, (SUBLANES, 1), 0)
    tiles = [score[v * SUBLANES:(v + 1) * SUBLANES, :] for v in range(n_blk // SUBLANES)]
    ranks = [jnp.zeros((SUBLANES, tq), F32) for _ in tiles]
    for j in range(n_blk):
        sj = score[j:j + 1, :]
        for v, sc in enumerate(tiles):
            if v > j // SUBLANES:
                ahead = sj >= sc
            elif v < j // SUBLANES:
                ahead = sj > sc
            else:
                ahead = (sj > sc) | ((sj == sc) & (sub > j % SUBLANES))
            ranks[v] = ranks[v] + jnp.where(ahead, 1.0, 0.0)
    rank = jnp.concatenate(ranks, axis=0)
    sel_bias = jnp.where((rank < float(k_top)) & (blk <= cur), 0.0, NEG).astype(BF16)
    q_aug = jnp.concatenate([qt, jnp.concatenate([sel_bias] * r4, axis=1)], axis=0)

    def scores(k0):
        return _dot(ks_ref[0, 0, pl.ds(k0, kb), :], q_aug)

    def absorb(sk, values, state):
        m_run, acc = state
        m_new = jnp.maximum(m_run, jnp.max(sk, axis=0, keepdims=True))
        pv = _dot(values, jnp.exp2(sk - m_new).astype(BF16))
        return m_new, jnp.exp2(m_run - m_new) * acc + pv

    def tile(kt, state):
        k0 = pl.multiple_of(kt * kb, kb)
        return absorb(scores(k0), value_tiles(vst_ref, k0, kb), state)

    init = (jnp.full((1, cols), NEG, F32), jnp.zeros((VT_ROWS, cols), F32))
    m_run, acc = lax.fori_loop(0, qi, tile, init)

    n_sub = tq // LANES

    def sub_cols(v, j):
        return jnp.concatenate([v[:, r * tq + j * LANES:r * tq + (j + 1) * LANES] for r in range(r4)], axis=1)

    def join_subs(parts):
        return jnp.concatenate([parts[j][:, r * LANES:(r + 1) * LANES] for r in range(r4) for j in range(n_sub)],
                               axis=1)

    wl = min(kw_ref.shape[2], WINDOW + LANES)
    sel_parts, win_parts = [], []
    for j in range(n_sub):
        tpos_j = tpos[:, j * LANES:(j + 1) * LANES]
        heads_j = lambda v: jnp.concatenate([v] * r4, axis=1)
        nk = (j + 1) * LANES
        kpos = qs + lax.broadcasted_iota(jnp.int32, (nk, 1), 0)
        sk = _dot(ks_ref[0, 0, pl.ds(pl.multiple_of(qs, tq), nk), :], sub_cols(q_aug, j))
        sk = sk + heads_j(jnp.where(kpos <= tpos_j, 0.0, NEG))
        state_j = absorb(sk, value_tiles(vst_ref, qs, nk), (sub_cols(m_run, j), sub_cols(acc, j)))
        sel_parts.append(normalise(state_j[1]))
        w0 = pl.multiple_of(jnp.maximum(qs + (j + 1) * LANES - wl, 0), LANES)
        dq = tpos_j - (w0 + lax.broadcasted_iota(jnp.int32, (wl, 1), 0))
        sw = _dot(kw_ref[0, 0, pl.ds(w0, wl), :], sub_cols(qt, j))
        sw = sw + heads_j(jnp.where((dq >= 0) & (dq < WINDOW), 0.0, NEG))
        ew = jnp.exp2(sw - jnp.max(sw, axis=0, keepdims=True))
        win_parts.append(normalise(_dot(value_tiles(vwt_ref, w0, wl), ew.astype(BF16))))
    o_s = join_subs(sel_parts)
    o_w = join_subs(win_parts)

    gt = gt_ref[0, 0]
    heads = []
    for r in range(r4):
        c0 = r * N_BRANCH
        sl = slice(r * tq, (r + 1) * tq)
        heads.append(gt[c0:c0 + 1] * o_c[:, sl] + gt[c0 + 1:c0 + 2] * o_s[:, sl] + gt[c0 + 2:c0 + 3] * o_w[:, sl])
    o_ref[0] = jnp.concatenate(heads, axis=0).T


def _importance_matrix(n_blk_pad, n_col):
    mt = np.zeros((n_blk_pad, n_col), np.float32)
    ratio = SEL_BLOCK // CMP_STRIDE
    for b in range(n_blk_pad):
        for c, w in zip(range(ratio * b, ratio * b + ratio + 1), (1.0,) + (2.0,) * (ratio - 1) + (1.0,)):
            if c < n_col:
                mt[b, c] = w
    return mt


def _nsa_prompt(q_t, comp_k, comp_vt, ks_aug, vs_t, kw, vw_t, gates_t, tq, kb):
    b_sz, _, _, s_len = q_t.shape
    nc = comp_k.shape[2]
    assert s_len % SEL_BLOCK == 0 and s_len % kb == 0 and kb % tq == 0 and s_len % tq == 0 and CMP_SPAN == 2
    assert tq % LANES == 0 and kb == tq
    n_blk = s_len // SEL_BLOCK
    assert n_blk <= MAX_SEL_BLOCKS
    n_cmp = s_len // CMP_STRIDE - CMP_SPAN + 1
    mt = jnp.asarray(_importance_matrix(MAX_SEL_BLOCKS, nc), BF16)

    def per_group(*shape):
        return pl.BlockSpec((1, 1) + shape, lambda b, g, i: (b, g) + (0,) * len(shape))

    return pl.pallas_call(
        functools.partial(_nsa_prompt_kernel, tq=tq, kb=kb, n_cmp=n_cmp, k_top=min(N_SEL, n_blk)),
        grid=(b_sz, KV_HEADS, s_len // tq),
        in_specs=[pl.BlockSpec((1, GQA_GROUP, HEAD_DIM, tq), lambda b, g, i: (b, g, 0, i)),
                  per_group(nc, HEAD_DIM), per_group(HEAD_DIM, nc),
                  per_group(s_len, HEAD_DIM + MAX_SEL_BLOCKS), per_group(s_len // LANES, VT_ROWS, LANES),
                  per_group(s_len, HEAD_DIM), per_group(s_len // LANES, VT_ROWS, LANES),
                  pl.BlockSpec((1, 1, GATE_ROWS, tq), lambda b, g, i: (b, g, 0, i)),
                  pl.BlockSpec((MAX_SEL_BLOCKS, nc), lambda b, g, i: (0, 0))],
        out_specs=pl.BlockSpec((1, tq, GQA_GROUP * HEAD_DIM), lambda b, g, i: (b, i, g)),
        out_shape=jax.ShapeDtypeStruct((b_sz, s_len, NSA_WIDTH), F32),
        compiler_params=_params("arbitrary", "arbitrary", "arbitrary"),
        name="nsa_prompt",
    )(q_t, comp_k, comp_vt, ks_aug, vs_t, kw, vw_t, gates_t, mt)


Q_COLS = LANES
COLS_PER_HEAD = Q_COLS // GQA_GROUP
V_OFF = LANES
V_COL0 = KV_HEADS * HEAD_DIM - V_OFF

def _nsa_sample_kernel(pt_ref, *refs, n_pages_step, n_steps, t_new, past, n_cmp, n_sel, k_top):
    del pt_ref
    p_cnt = n_pages_step
    page_refs = refs[:p_cnt]
    (qbd_ref, comp_ref, knew_ref, win_ref, wnew_ref, g_ref,
     o_ref, kv_scr, bias_scr, p_scr, m_scr, l_scr, acc_scr, oc_scr) = refs[p_cnt:]
    i = pl.program_id(1)
    qbd = qbd_ref[0]
    col = lax.broadcasted_iota(jnp.int32, (1, Q_COLS), 1)
    tok = col & (SUBLANES - 1)
    qpos = past + tok
    kwin = slice(0, 2 * LANES)
    vwin = slice(V_OFF, V_OFF + 2 * LANES)
    k_lanes = KV_HEADS * HEAD_DIM
    blocks_per_page = PAGE_SIZE // SEL_BLOCK
    nb_step = blocks_per_page * p_cnt
    keys_step = PAGE_SIZE * p_cnt

    def attend(rows_bf16, valid):
        s = _dot(rows_bf16[:, kwin], qbd)
        return jnp.where(valid, s, NEG)

    @pl.when((pl.program_id(0) == 0) & (i == 0))
    def _():
        key_blk = lax.broadcasted_iota(jnp.int32, (keys_step, 1), 0) // SEL_BLOCK
        lane_blk = lax.broadcasted_iota(jnp.int32, (1, 2 * LANES - k_lanes), 1)
        kv_scr[:, k_lanes:2 * LANES] = (key_blk == lane_blk).astype(BF16)

    @pl.when(i == 0)
    def _():
        comp = comp_ref[0]
        ncp = comp.shape[0]
        cid = lax.broadcasted_iota(jnp.int32, (ncp, 1), 0)
        valid = (cid >= 1) & (cid <= n_cmp) & (cid * CMP_STRIDE + (CMP_BLOCK - CMP_STRIDE - 1) <= qpos)
        s = attend(comp, valid)
        e, den = _softmax_parts(s, valid, 0)
        p = e * (1.0 / jnp.maximum(den, 1e-30))
        oc_scr[...] = _dot_tn(p.astype(BF16), comp[:, vwin])
        nbp = bias_scr.shape[0]
        ratio = SEL_BLOCK // CMP_STRIDE
        p_scr[0:ncp, :] = p
        p_scr[ncp:, :] = jnp.zeros((p_scr.shape[0] - ncp, Q_COLS), F32)
        taps = [p_scr[pl.ds(k, nbp, stride=ratio), :] for k in range(ratio + 1)]
        imp = taps[0] + taps[ratio]
        for k in range(1, ratio):
            imp = imp + 2.0 * taps[k]
        imp = imp + pltpu.roll(imp, COLS_PER_HEAD, 1)
        imp = imp + pltpu.roll(imp, 2 * COLS_PER_HEAD, 1)
        blk = lax.broadcasted_iota(jnp.int32, (nbp, 1), 0)
        cur = qpos // SEL_BLOCK
        score = jnp.where(blk <= cur, imp, -1.0)
        score = jnp.where((blk == 0) | (blk == cur) | (blk == cur - 1), FORCE, score)
        score = jnp.where(blk < n_sel, score, PAD_SCORE)
        blk_f = blk.astype(F32)
        bias = jnp.full((nbp, Q_COLS), NEG, F32)
        for _ in range(k_top):
            best = jnp.max(score, axis=0, keepdims=True)
            first = jnp.min(jnp.where(score == best, blk_f, float(nbp)), axis=0, keepdims=True)
            hit = blk_f == first
            bias = jnp.where(hit, 0.0, bias)
            score = jnp.where(hit, TAKEN_SCORE, score)
        bias_scr[...] = bias
        m_scr[...] = jnp.full_like(m_scr, NEG)
        l_scr[...] = jnp.zeros_like(l_scr)
        acc_scr[...] = jnp.zeros_like(acc_scr)

    def as_col(v):
        return jnp.broadcast_to(v, (SUBLANES, Q_COLS)).T[:, 0:1]

    def partial_softmax(s, v_rows):
        m_g = jnp.max(s, axis=0, keepdims=True)
        e = jnp.exp2(s - m_g)
        return m_g, jnp.sum(e, axis=0, keepdims=True), _dot_tn(e.astype(BF16), v_rows)

    def accumulate(*parts):
        m_run = m_scr[0:1, :]
        m_new = m_run
        for m_g, _, _ in parts:
            m_new = jnp.maximum(m_new, m_g)
        w_run = jnp.exp2(m_run - m_new)
        l_new = w_run * l_scr[0:1, :]
        acc_new = as_col(w_run) * acc_scr[...]
        for m_g, l_g, acc_g in parts:
            w_g = jnp.exp2(m_g - m_new)
            l_new = l_new + w_g * l_g
            acc_new = acc_new + as_col(w_g) * acc_g
        m_scr[0:1, :] = m_new
        l_scr[0:1, :] = l_new
        acc_scr[...] = acc_new

    bias_step = bias_scr[pl.ds(pl.multiple_of(i * nb_step, nb_step), nb_step), :].astype(BF16)
    pad = 2 * LANES - k_lanes - nb_step
    q_parts = [qbd[:k_lanes], bias_step] + ([jnp.zeros((pad, Q_COLS), BF16)] if pad else [])
    q_step = jnp.concatenate(q_parts, axis=0)
    for p in range(p_cnt):
        rows_p = slice(p * PAGE_SIZE, (p + 1) * PAGE_SIZE)
        tiles = [page_refs[p][0, cb].astype(BF16).T for cb in range(KV_ROW // LANES)]
        kv_scr[rows_p, 0:LANES] = tiles[0]
        kv_scr[rows_p, LANES:k_lanes] = tiles[1][:, :k_lanes - LANES]
        kv_scr[rows_p, 2 * LANES:3 * LANES] = tiles[1]
        kv_scr[rows_p, 3 * LANES:4 * LANES] = tiles[2]
    accumulate(partial_softmax(_dot(kv_scr[:, 0:2 * LANES], q_step), kv_scr[:, 2 * LANES:4 * LANES]))

    @pl.when(i == n_steps - 1)
    def _():
        knew = knew_ref[0].astype(BF16)
        krow = lax.broadcasted_iota(jnp.int32, (t_new, 1), 0)
        new_blk = past // SEL_BLOCK
        s_new = _dot(knew[:, kwin], qbd) + bias_scr[new_blk:new_blk + 1, :]
        accumulate(partial_softmax(jnp.where(krow <= tok, s_new, NEG), knew[:, vwin]))
        o_s = acc_scr[...] * as_col(1.0 / l_scr[0:1, :])

        wbuf = jnp.concatenate([win_ref[0, cb].astype(BF16).T for cb in range(KV_ROW // LANES)], axis=1)
        wnew = wnew_ref[0].astype(BF16)
        buf = wbuf.shape[0]
        pos_b = past - buf + lax.broadcasted_iota(jnp.int32, (buf, 1), 0)
        pos_n = past + krow
        ok_b = (qpos - pos_b >= 0) & (qpos - pos_b < WINDOW) & (pos_b >= 0)
        ok_n = (qpos - pos_n >= 0) & (qpos - pos_n < WINDOW)
        s_b = attend(wbuf, ok_b)
        s_n = attend(wnew, ok_n)
        m_w = jnp.maximum(jnp.max(s_b, axis=0, keepdims=True), jnp.max(s_n, axis=0, keepdims=True))
        e_b = jnp.where(ok_b, jnp.exp2(s_b - m_w), 0.0)
        e_n = jnp.where(ok_n, jnp.exp2(s_n - m_w), 0.0)
        den = jnp.sum(e_b, axis=0, keepdims=True) + jnp.sum(e_n, axis=0, keepdims=True)
        o_w = _dot_tn(e_b.astype(BF16), wbuf[:, vwin]) + _dot_tn(e_n.astype(BF16), wnew[:, vwin])
        o_w = o_w * jnp.broadcast_to(1.0 / jnp.maximum(den, 1e-30), (SUBLANES, Q_COLS)).T[:, 0:1]
        o_c = oc_scr[...]

        gt = g_ref[0]
        for g in range(KV_HEADS):
            for r in range(GQA_GROUP):
                r0 = r * COLS_PER_HEAD + g * SUBLANES
                c0 = V_COL0 + g * HEAD_DIM
                gc = (g * GQA_GROUP + r) * N_BRANCH
                blk_o = [o[r0:r0 + t_new, c0:c0 + HEAD_DIM] for o in (o_c, o_s, o_w)]
                h0 = (g * GQA_GROUP + r) * HEAD_DIM
                o_ref[0, :, h0:h0 + HEAD_DIM] = (gt[:, gc:gc + 1] * blk_o[0] + gt[:, gc + 1:gc + 2] * blk_o[1]
                                                 + gt[:, gc + 2:gc + 3] * blk_o[2])


def _nsa_sample(cache_slc, page_table, qbd, comp_tm, knew, win_t, wnew, gates, past):
    db, n_pages = page_table.shape
    t_new = knew.shape[1]
    assert t_new == SUBLANES and KV_HEADS * t_new <= COLS_PER_HEAD and past % SEL_BLOCK == 0 and t_new <= SEL_BLOCK
    p_cnt = min(32, n_pages)
    assert n_pages % p_cnt == 0
    n_steps = n_pages // p_cnt
    ncp = comp_tm.shape[1]
    n_str = -(-(past + t_new) // CMP_STRIDE)
    n_cmp = n_str - CMP_SPAN + 1
    n_sel = -(-(past + t_new) // SEL_BLOCK)
    blocks_step = (PAGE_SIZE // SEL_BLOCK) * p_cnt
    nbp = -(-(n_sel + 1) // blocks_step) * blocks_step
    keys_step = PAGE_SIZE * p_cnt
    assert blocks_step <= 2 * LANES - KV_HEADS * HEAD_DIM
    assert CMP_SPAN == 2
    p_rows = max(ncp, (SEL_BLOCK // CMP_STRIDE) * nbp) + SUBLANES
    buf = win_t.shape[-1]
    assert buf % LANES == 0

    def page_spec(p):
        return pl.BlockSpec((1, KV_ROW // LANES, LANES, PAGE_SIZE),
                            lambda b, i, pt: (pt[b * n_pages + i * p_cnt + p], 0, 0, 0))

    def per_b(*shape):
        return pl.BlockSpec((1,) + shape, lambda b, i, pt: (b,) + (0,) * len(shape))

    in_specs = [page_spec(p) for p in range(p_cnt)]
    in_specs += [per_b(2 * LANES, Q_COLS), per_b(ncp, KV_ROW),
                 per_b(t_new, KV_ROW), per_b(KV_ROW // LANES, LANES, buf), per_b(t_new, KV_ROW), per_b(t_new, LANES)]
    return pl.pallas_call(
        functools.partial(_nsa_sample_kernel, n_pages_step=p_cnt, n_steps=n_steps, t_new=t_new, past=past,
                          n_cmp=n_cmp, n_sel=n_sel, k_top=min(N_SEL, n_sel)),
        grid_spec=pltpu.PrefetchScalarGridSpec(
            num_scalar_prefetch=1, grid=(db, n_steps), in_specs=in_specs,
            out_specs=per_b(t_new, NSA_WIDTH),
            scratch_shapes=[pltpu.VMEM((keys_step, 4 * LANES), BF16),
                            pltpu.VMEM((nbp, Q_COLS), F32), pltpu.VMEM((p_rows, Q_COLS), F32),
                            pltpu.VMEM((SUBLANES, Q_COLS), F32), pltpu.VMEM((SUBLANES, Q_COLS), F32),
                            pltpu.VMEM((Q_COLS, 2 * LANES), F32), pltpu.VMEM((Q_COLS, 2 * LANES), F32)]),
        out_shape=jax.ShapeDtypeStruct((db, t_new, NSA_WIDTH), F32),
        compiler_params=_params("arbitrary", "arbitrary"),
        name="nsa_sample",
    )(page_table.reshape(-1), *([cache_slc] * p_cnt), qbd, comp_tm, knew, win_t, wnew, gates)


def _out_kernel(x_ref, po_ref, zp_ref, no_ref, zn_ref, gate_ref, gnp_ref, gnn_ref, w_ref, fn_ref, y_ref):
    zp = zp_ref[0]
    zn = zn_ref[0]
    mp = _rms(po_ref[0], gnp_ref[...]) * (zp * _sigmoid(zp))
    mn = _rms(no_ref[0], gnn_ref[...]) * (zn * _sigmoid(zn))
    m = jnp.concatenate([mp, mn], axis=-1).astype(BF16)
    xo = x_ref[0] + gate_ref[0] * _dot(m, w_ref[...])
    y_ref[0] = _rms(xo, fn_ref[...])


def _out(x, pool_o, zp, nsa_o, zn, gate, gn_pool, gn_nsa, w_out_bf16, final_norm, tm):
    nb, t, d = x.shape
    r = gate.shape[1]
    assert t % tm == 0 and r in (1, t)
    if r == 1:
        gate_spec = pl.BlockSpec((1, 1, d), lambda b, i: (b, 0, 0))
    else:
        gate_spec = pl.BlockSpec((1, tm, d), lambda b, i: (b, i, 0))

    def tok(width):
        return pl.BlockSpec((1, tm, width), lambda b, i: (b, i, 0))

    def const(*shape):
        return pl.BlockSpec(shape, lambda b, i: (0,) * len(shape))

    return pl.pallas_call(
        _out_kernel,
        grid=(nb, t // tm),
        in_specs=[tok(d), tok(POOL_WIDTH), tok(POOL_WIDTH), tok(NSA_WIDTH), tok(NSA_WIDTH), gate_spec,
                  const(1, POOL_WIDTH), const(1, NSA_WIDTH), const(d, d), const(1, d)],
        out_specs=tok(d),
        out_shape=jax.ShapeDtypeStruct((nb, t, d), F32),
        compiler_params=_params("arbitrary", "arbitrary"),
        name="out",
    )(x, pool_o, zp, nsa_o, zn, gate, gn_pool.reshape(1, -1), gn_nsa.reshape(1, -1), w_out_bf16,
      final_norm.reshape(1, d))


def _pages_t(pages):
    n = pages.shape[0]
    return pages.transpose(0, 2, 3, 4, 1).reshape(n, KV_ROW // LANES, LANES, PAGE_SIZE)


def _block_diag(pool_w):
    n, c, _ = pool_w.shape
    eye = jnp.eye(n, dtype=pool_w.dtype)
    return (eye[:, None, :, None] * pool_w[:, :, None, :]).reshape(n * c, n * c)


def _first_layer(phi_w1, phi_pe):
    w = phi_w1.reshape(CMP_SPAN, CMP_STRIDE, 2, HEAD_DIM, CMP_HID)
    w1cat = w.transpose(2, 1, 3, 0, 4).reshape(2, CMP_STRIDE * HEAD_DIM, CMP_SPAN * CMP_HID)
    pe = phi_pe.reshape(CMP_SPAN, CMP_STRIDE, 2, HEAD_DIM).transpose(2, 0, 1, 3).reshape(2, CMP_SPAN, -1)
    pe2 = jnp.concatenate([pe, jnp.zeros((2, SUBLANES - CMP_SPAN, pe.shape[-1]), pe.dtype)], axis=1)
    return w1cat.astype(BF16), pe2


def kernel(x_prompt, x_sample, c_prompt, c_sample, cache_cmp_kv, cache_slc_kv, state_win_kv, state_pool, page_table, norm_w, w_ada, b_ada, w_in, pool_w, pool_scale, phi_w1, phi_pe, phi_w2, gn_pool, gn_nsa, w_out, final_norm):
    b_sz, s_len, d = x_prompt.shape
    db, t_new, _ = x_sample.shape
    depth = norm_w.shape[0]
    assert depth == 1 and w_in.shape[-1] == IN_WIDTH and d // 4 == POOL_WIDTH
    n_phys = cache_cmp_kv.shape[1]
    n_pages = page_table.shape[1]
    past = n_pages * PAGE_SIZE
    lyr = 0

    n_c = b_sz + db
    c_all = jnp.concatenate([c_prompt, c_sample, jnp.zeros((-n_c % SUBLANES, d), F32)], axis=0)
    ada = _ada(c_all, w_ada[lyr], b_ada[lyr])
    shift, scale, gate = ada[:, :d], ada[:, d:2 * d], ada[:, 2 * d:]

    w_in_t = w_in[lyr].T
    w_main = _cast_rows(w_in_t, IN_MAIN, 5 * LANES, BF16)
    w_gate = jnp.pad(w_in_t[IN_MAIN:].astype(BF16), ((0, LANES - GATE_COLS), (0, 0)))
    w_out_bf16 = w_out[lyr].astype(BF16)
    wbd = _block_diag(pool_w[lyr]).astype(BF16)
    w1cat, pe2 = _first_layer(phi_w1[lyr], phi_pe[lyr])
    w2 = phi_w2[lyr].astype(BF16)
    n_tok_s = db * t_new

    def per_token(v):
        return jnp.repeat(v[b_sz:n_c], t_new, axis=0)[None]

    tm = min(512, s_len)
    (u_p, zp_p, zn_p, q_t, ks_aug, kw_p, vs_t, vw_t, g_t, kvc_t, kvs_t, kvw_t) = _in_proj(
        x_prompt, scale[:b_sz, None], shift[:b_sz, None], norm_w[lyr], w_main, w_gate, tm, True)
    pool_p = _pool(jnp.zeros((b_sz, POOL_HALO, POOL_WIDTH), F32), u_p, wbd, pool_scale[lyr], 0, tm)
    pages_p = s_len // PAGE_SIZE
    in_order = jnp.tile(jnp.arange(pages_p, dtype=jnp.int32), (b_sz, 1))
    comp_k, comp_vt = _compress(kvc_t, in_order, w1cat, pe2, w2, None, False, True)
    nsa_p = _nsa_prompt(q_t, comp_k, comp_vt, ks_aug, vs_t, kw_p, vw_t, g_t, tq=512, kb=min(512, s_len))
    y_p = _out(x_prompt, pool_p, zp_p, nsa_p, zn_p, gate[:b_sz, None], gn_pool[lyr], gn_nsa[lyr], w_out_bf16,
               final_norm, tm)

    (u_s, zp_s, zn_s, kvc_s, kvs_s, kvw_s, q_s, g_s) = _in_proj(
        x_sample.reshape(1, n_tok_s, d), per_token(scale), per_token(shift), norm_w[lyr], w_main, w_gate, n_tok_s,
        False)
    u_s3 = u_s.reshape(db, t_new, POOL_WIDTH)
    hist = jnp.concatenate([jnp.zeros((db, POOL_HALO - state_pool.shape[2], POOL_WIDTH), F32), state_pool[lyr]], axis=1)
    pool_s = _pool(hist, u_s3, wbd, pool_scale[lyr], past, t_new)
    kvc_s3 = kvc_s.reshape(db, t_new, KV_ROW)
    kvs_s3 = kvs_s.reshape(db, t_new, KV_ROW)
    kvw_s3 = kvw_s.reshape(db, t_new, KV_ROW)
    xnew = jnp.pad(kvc_s3, ((0, 0), (0, CMP_STRIDE - t_new), (0, 0)))
    xnew = xnew.reshape(db, CMP_STRIDE, N_KV_SLABS, HEAD_DIM).transpose(2, 0, 1, 3).reshape(N_KV_SLABS, db, -1)
    a1new = _new_chunk_first_layer(xnew, w1cat).transpose(1, 0, 2)
    comp_s = _compress(_pages_t(cache_cmp_kv[lyr]), page_table, w1cat, pe2, w2, a1new, True, False)
    q5 = q_s.reshape(db, t_new, KV_HEADS, GQA_GROUP, HEAD_DIM)
    eye = jnp.eye(KV_HEADS, dtype=BF16)
    qbd = q5.transpose(0, 2, 4, 3, 1)[:, :, :, :, None, :] * eye[None, :, None, None, :, None]
    qbd = jnp.pad(qbd.reshape(db, KV_HEADS * HEAD_DIM, GQA_GROUP, KV_HEADS * t_new),
                  ((0, 0), (0, 2 * LANES - KV_HEADS * HEAD_DIM), (0, 0), (0, COLS_PER_HEAD - KV_HEADS * t_new)))
    qbd = qbd.reshape(db, 2 * LANES, Q_COLS)
    g_s3 = g_s.reshape(db, t_new, LANES)
    buf = state_win_kv.shape[2]
    win_t = state_win_kv[lyr].transpose(0, 2, 3, 4, 1).reshape(db, KV_ROW // LANES, LANES, buf)
    nsa_s = _nsa_sample(_pages_t(cache_slc_kv[lyr]), page_table, qbd, comp_s, kvs_s3,
                        win_t, kvw_s3, g_s3, past)
    y_s = _out(x_sample.reshape(1, n_tok_s, d), pool_s.reshape(1, n_tok_s, POOL_WIDTH), zp_s,
               nsa_s.reshape(1, n_tok_s, NSA_WIDTH), zn_s, per_token(gate), gn_pool[lyr], gn_nsa[lyr], w_out_bf16,
               final_norm, n_tok_s).reshape(db, t_new, d)

    kv_shape = (2, KV_HEADS, HEAD_DIM)
    win_len = min(WINDOW, s_len)
    new_win_s = jnp.concatenate([state_win_kv[lyr], kvw_s3.reshape((db, t_new) + kv_shape)], axis=1)[:, -buf:]
    hist_len = state_pool.shape[2]
    new_pool_s = jnp.concatenate([state_pool[lyr], u_s3], axis=1)[:, -hist_len:]

    def rows_from_t(v_t):
        return v_t.reshape((v_t.shape[0],) + kv_shape + (v_t.shape[-1],)).transpose(0, 4, 1, 2, 3)[None]

    return (y_p, y_s,
            rows_from_t(kvc_t), kvc_s3.reshape((1, db, t_new) + kv_shape),
            rows_from_t(kvs_t), kvs_s3.reshape((1, db, t_new) + kv_shape),
            rows_from_t(kvw_t[..., s_len - win_len:]), new_win_s[None],
            u_p[:, s_len - hist_len:][None], new_pool_s[None])
```

```python
import functools

import numpy as np
import jax
import jax.numpy as jnp
from jax import lax
from jax.experimental import pallas as pl
from jax.experimental.pallas import tpu as pltpu

F32 = jnp.float32
BF16 = jnp.bfloat16

HEAD_DIM = 64
GQA_GROUP = 4
KV_HEADS = 3
N_HEADS = KV_HEADS * GQA_GROUP
N_KV_SLABS = 2 * KV_HEADS
KV_ROW = N_KV_SLABS * HEAD_DIM
N_BRANCH = 3
POOL_WINDOWS = (2, 4, 8, 16)
POOL_HALO = 16
CMP_BLOCK = 32
CMP_STRIDE = 16
CMP_SPAN = CMP_BLOCK // CMP_STRIDE
CMP_HID = 2 * HEAD_DIM
SEL_BLOCK = 64
N_SEL = 16
WINDOW = 512
PAGE_SIZE = 128
CHUNKS_PER_PAGE = PAGE_SIZE // CMP_STRIDE
CHUNK_PITCH = 24
EPS = 1e-6
NEG = -1e30
FORCE = 1e4
LOG2_E = 1.4426950408889634
VT_ROWS = 80
PAD_SCORE = -1e38
TAKEN_SCORE = -3e38

LANES = 128
SUBLANES = 8
VMEM_LIMIT_BYTES = 56 * 1024 * 1024

ROW_TILE = 512
ATTN_TILE = 1024
PAGES_PER_STEP = 32


def _dot(a, b):
    return jnp.dot(a, b, preferred_element_type=F32)


def _dot_nt(a, b):
    return lax.dot_general(a, b, (((1,), (1,)), ((), ())), preferred_element_type=F32)


def _dot_tn(a, b):
    return lax.dot_general(a, b, (((0,), (0,)), ((), ())), preferred_element_type=F32)


def _sigmoid(v):
    return 1.0 / (1.0 + jnp.exp(-v))


def _rms(v, g):
    return v * lax.rsqrt(jnp.mean(v * v, axis=-1, keepdims=True) + EPS) * g


def _split3(v):
    hi = v.astype(BF16)
    r1 = v - hi.astype(F32)
    mid = r1.astype(BF16)
    lo = (r1 - mid.astype(F32)).astype(BF16)
    return hi, mid, lo


def _params(*sem):
    return pltpu.CompilerParams(dimension_semantics=sem, vmem_limit_bytes=VMEM_LIMIT_BYTES)


def _ada_kernel(c_ref, w_ref, b_ref, o_ref):
    c = c_ref[...]
    a = (c * _sigmoid(c)).astype(BF16)
    o_ref[...] = _dot(a, w_ref[...].astype(BF16)) + b_ref[...]


def _ada(c, w_ada, b_ada):
    m, d = c.shape
    n = w_ada.shape[1]
    tn = 512
    return pl.pallas_call(
        _ada_kernel,
        grid=(n // tn,),
        in_specs=[pl.BlockSpec((m, d), lambda j: (0, 0)),
                  pl.BlockSpec((d, tn), lambda j: (0, j)),
                  pl.BlockSpec((1, tn), lambda j: (0, j))],
        out_specs=pl.BlockSpec((m, tn), lambda j: (0, j)),
        out_shape=jax.ShapeDtypeStruct((m, n), F32),
        compiler_params=_params("arbitrary"),
        name="ada",
    )(c, w_ada, b_ada.reshape(1, n))


POOL_WIDTH = 256
NSA_WIDTH = N_HEADS * HEAD_DIM
GATE_COLS = N_BRANCH * N_HEADS
MAX_SEL_BLOCKS = LANES - HEAD_DIM
_SEG = {}
_off = 0
for _name, _w in (("u", POOL_WIDTH), ("zp", POOL_WIDTH), ("q", NSA_WIDTH), ("zn", NSA_WIDTH),
                  ("kvc", KV_ROW), ("kvs", KV_ROW), ("kvw", KV_ROW)):
    _SEG[_name] = (_off, _w)
    _off += _w
IN_MAIN = _off
IN_WIDTH = IN_MAIN + GATE_COLS


GATE_ROWS = 16


def _in_proj_kernel(x_ref, sc_ref, sh_ref, nw_ref, w_ref, wg_ref, u_ref, zp_ref, zn_ref, *rest, attn_layouts):
    tm = x_ref.shape[1]
    h = (_rms(x_ref[0], nw_ref[...]) * (1.0 + sc_ref[0]) + sh_ref[0]).astype(BF16)

    def seg(name):
        s, w = _SEG[name]
        return _dot_nt(h, w_ref[s:s + w, :])

    u_ref[0] = seg("u")
    zp_ref[0] = seg("zp")
    zn_ref[0] = seg("zn")
    q = seg("q") * (HEAD_DIM ** -0.5 * LOG2_E)
    kvc = seg("kvc")
    kvs = seg("kvs")
    kvw = seg("kvw")
    sig = _sigmoid(_dot_nt(h, wg_ref[...]))
    if not attn_layouts:
        kvc_ref, kvs_ref, kvw_ref, q_ref, g_ref = rest
        kvc_ref[0] = kvc
        kvs_ref[0] = kvs
        kvw_ref[0] = kvw
        q_ref[0] = q.astype(BF16)
        g_ref[0] = sig
        return

    qt_ref, ks_aug_ref, kw_ref, vst_ref, vwt_ref, gt_ref, kvct_ref, kvst_ref, kvwt_ref = rest
    qt_ref[0] = q.T.astype(BF16).reshape(N_HEADS, HEAD_DIM, tm)
    kvs_t32 = kvs.T
    kvw_t32 = kvw.T
    kvct_ref[0] = kvc.T.reshape(KV_ROW // LANES, LANES, tm)
    kvst_ref[0] = kvs_t32.reshape(KV_ROW // LANES, LANES, tm)
    kvwt_ref[0] = kvw_t32.reshape(KV_ROW // LANES, LANES, tm)
    pos = pl.program_id(1) * tm + lax.broadcasted_iota(jnp.int32, (tm, 1), 0)
    onehot = (lax.broadcasted_iota(jnp.int32, (1, MAX_SEL_BLOCKS), 1) == pos // SEL_BLOCK).astype(BF16)
    kvs_t = kvs_t32.astype(BF16)
    kvw_t = kvw_t32.astype(BF16)
    v0 = KV_HEADS * HEAD_DIM
    tail = (lax.broadcasted_iota(jnp.int32, (VT_ROWS - HEAD_DIM, LANES), 0) == 0).astype(BF16)
    for g in range(KV_HEADS):
        ks_aug_ref[0, g] = jnp.concatenate([kvs[:, g * HEAD_DIM:(g + 1) * HEAD_DIM].astype(BF16), onehot], axis=1)
        kw_ref[0, g] = kvw[:, g * HEAD_DIM:(g + 1) * HEAD_DIM].astype(BF16)
        for c in range(tm // LANES):
            sl = (slice(v0 + g * HEAD_DIM, v0 + (g + 1) * HEAD_DIM), slice(c * LANES, (c + 1) * LANES))
            vst_ref[0, g, c] = jnp.concatenate([kvs_t[sl], tail], axis=0)
            vwt_ref[0, g, c] = jnp.concatenate([kvw_t[sl], tail], axis=0)
        per_group = GQA_GROUP * N_BRANCH
        rolled = sig if g == 0 else pltpu.roll(sig, LANES - per_group * g, 1)
        gt_ref[0, g] = rolled.T[:GATE_ROWS, :]


def _cast_kernel(w_ref, o_ref):
    o_ref[...] = w_ref[...].astype(o_ref.dtype)


def _cast_rows(w, n_rows, tile, dtype):
    cols = w.shape[1]
    assert n_rows % tile == 0 and tile % SUBLANES == 0 and n_rows <= w.shape[0] and cols % LANES == 0
    return pl.pallas_call(
        _cast_kernel,
        grid=(n_rows // tile,),
        in_specs=[pl.BlockSpec((tile, cols), lambda j: (j, 0))],
        out_specs=pl.BlockSpec((tile, cols), lambda j: (j, 0)),
        out_shape=jax.ShapeDtypeStruct((n_rows, cols), dtype),
        compiler_params=_params("arbitrary"),
        name="cast_rows",
    )(w)


def _in_proj(x, scale, shift, norm_w, w_main, w_gate, tm, attn_layouts):
    nb, t, d = x.shape
    r = scale.shape[1]
    assert t % tm == 0 and r in (1, t) and (tm % LANES == 0 or not attn_layouts)
    if r == 1:
        mod_spec = pl.BlockSpec((1, 1, d), lambda b, i: (b, 0, 0))
    else:
        mod_spec = pl.BlockSpec((1, tm, d), lambda b, i: (b, i, 0))

    def tok(width):
        return pl.BlockSpec((1, tm, width), lambda b, i: (b, i, 0))

    def hm(n, width=HEAD_DIM):
        return pl.BlockSpec((1, n, tm, width), lambda b, i: (b, 0, i, 0))

    def sd(*shape, dtype=F32):
        return jax.ShapeDtypeStruct(shape, dtype)

    out_specs = [tok(POOL_WIDTH), tok(POOL_WIDTH), tok(NSA_WIDTH)]
    out_shape = [sd(nb, t, POOL_WIDTH), sd(nb, t, POOL_WIDTH), sd(nb, t, NSA_WIDTH)]
    if attn_layouts:
        lane_tiles = tm // LANES
        vt_spec = pl.BlockSpec((1, KV_HEADS, lane_tiles, VT_ROWS, LANES), lambda b, i: (b, 0, i, 0, 0))
        vt_shape = sd(nb, KV_HEADS, t // LANES, VT_ROWS, LANES, dtype=BF16)
        rows_t_spec = pl.BlockSpec((1, KV_ROW // LANES, LANES, tm), lambda b, i: (b, 0, 0, i))
        rows_t_shape = sd(nb, KV_ROW // LANES, LANES, t)
        out_specs += [pl.BlockSpec((1, N_HEADS, HEAD_DIM, tm), lambda b, i: (b, 0, 0, i)),
                      hm(KV_HEADS, HEAD_DIM + MAX_SEL_BLOCKS), hm(KV_HEADS), vt_spec, vt_spec,
                      pl.BlockSpec((1, KV_HEADS, GATE_ROWS, tm), lambda b, i: (b, 0, 0, i)),
                      rows_t_spec, rows_t_spec, rows_t_spec]
        out_shape += [sd(nb, N_HEADS, HEAD_DIM, t, dtype=BF16),
                      sd(nb, KV_HEADS, t, HEAD_DIM + MAX_SEL_BLOCKS, dtype=BF16),
                      sd(nb, KV_HEADS, t, HEAD_DIM, dtype=BF16), vt_shape, vt_shape,
                      sd(nb, KV_HEADS, GATE_ROWS, t), rows_t_shape, rows_t_shape, rows_t_shape]
    else:
        out_specs += [tok(KV_ROW), tok(KV_ROW), tok(KV_ROW), tok(NSA_WIDTH), tok(LANES)]
        out_shape += [sd(nb, t, KV_ROW), sd(nb, t, KV_ROW), sd(nb, t, KV_ROW),
                      sd(nb, t, NSA_WIDTH, dtype=BF16), sd(nb, t, LANES)]
    return pl.pallas_call(
        functools.partial(_in_proj_kernel, attn_layouts=attn_layouts),
        grid=(nb, t // tm),
        in_specs=[tok(d), mod_spec, mod_spec,
                  pl.BlockSpec((1, d), lambda b, i: (0, 0)),
                  pl.BlockSpec((IN_MAIN, d), lambda b, i: (0, 0)),
                  pl.BlockSpec((LANES, d), lambda b, i: (0, 0))],
        out_specs=out_specs,
        out_shape=out_shape,
        compiler_params=_params("arbitrary", "arbitrary"),
        name="in_proj",
    )(x, scale, shift, norm_w.reshape(1, d), w_main, w_gate)


def _pool_kernel(hist_ref, uprev_ref, u_ref, wbd_ref, ps_ref, o_ref, *, pos0, tp):
    i = pl.program_id(1)
    c = u_ref.shape[-1]
    grp = lax.broadcasted_iota(jnp.int32, (1, c), 1) // (c // len(POOL_WINDOWS))
    pos = pos0 + i * tp + lax.broadcasted_iota(jnp.int32, (tp, 1), 0)
    win = jnp.full((1, c), float(POOL_WINDOWS[-1]), F32)
    for gi in range(len(POOL_WINDOWS) - 2, -1, -1):
        win = jnp.where(grp == gi, float(POOL_WINDOWS[gi]), win)
    cnt = jnp.minimum(win, (pos + 1).astype(F32))
    for bi in range(u_ref.shape[0]):
        u = u_ref[bi]
        halo = jnp.where(i == 0, hist_ref[bi], uprev_ref[bi])
        sums = []
        s = jnp.concatenate([halo, u], axis=0)
        for w in POOL_WINDOWS:
            s = s + pltpu.roll(s, w // 2, 0)
            sums.append(s[POOL_HALO:])
        tot = sums[-1]
        for gi in range(len(POOL_WINDOWS) - 2, -1, -1):
            tot = jnp.where(grp == gi, sums[gi], tot)
        dlt = tot / cnt - u
        o_ref[bi] = _dot(dlt.astype(BF16), wbd_ref[...]) * ps_ref[...]


def _pool(hist, u, wbd, pool_scale, pos0, tp):
    nb, t, c = u.shape
    assert t % tp == 0
    if t >= POOL_HALO:
        assert tp % POOL_HALO == 0
        uprev, ratio, bb = u, tp // POOL_HALO, 1
        prev_spec = pl.BlockSpec((bb, POOL_HALO, c), lambda b, i: (b, jnp.maximum(i * ratio - 1, 0), 0))
    else:
        assert t == tp
        uprev, bb = hist, nb
        prev_spec = pl.BlockSpec((bb, POOL_HALO, c), lambda b, i: (b, 0, 0))
    return pl.pallas_call(
        functools.partial(_pool_kernel, pos0=pos0, tp=tp),
        grid=(nb // bb, t // tp),
        in_specs=[pl.BlockSpec((bb, POOL_HALO, c), lambda b, i: (b, 0, 0)), prev_spec,
                  pl.BlockSpec((bb, tp, c), lambda b, i: (b, i, 0)),
                  pl.BlockSpec((c, c), lambda b, i: (0, 0)),
                  pl.BlockSpec((1, c), lambda b, i: (0, 0))],
        out_specs=pl.BlockSpec((bb, tp, c), lambda b, i: (b, i, 0)),
        out_shape=jax.ShapeDtypeStruct((nb, t, c), F32),
        compiler_params=_params("arbitrary", "arbitrary"),
        name="pool",
    )(hist, uprev, u, wbd, pool_scale.reshape(1, c))


def _gelu_tanh(v):
    return 0.5 * v * (1.0 + jnp.tanh(np.sqrt(2.0 / np.pi).astype(np.float32) * (v + 0.044715 * (v * v * v))))


def _compress_kernel(pt_ref, *refs, n_pages_step, n_steps, has_new, token_major):
    del pt_ref
    p_cnt = n_pages_step
    page_refs = refs[:p_cnt]
    w1p_ref, w1_ref, pe_ref, w2_ref = refs[p_cnt:p_cnt + 4]
    k = p_cnt + 4
    a1new_ref = None
    if has_new:
        a1new_ref = refs[k]
        k += 1
    if token_major:
        out_ref = refs[k]
        k += 1
    else:
        outk_ref, outvt_ref = refs[k:k + 2]
        k += 2
    slab_scr, carry_scr, bias_scr = refs[k:k + 3]
    i = pl.program_id(1)
    rows = CHUNKS_PER_PAGE * p_cnt
    row_id = lax.broadcasted_iota(jnp.int32, (rows, 1), 0)

    @pl.when(i == 0)
    def _():
        carry_scr[...] = jnp.zeros_like(carry_scr)
        for kk in range(2):
            pb = _dot(pe_ref[kk].astype(BF16), w1_ref[kk])
            bias_scr[kk] = pb[0:1, :CMP_HID] + pb[1:2, CMP_HID:]

    def store(kg, comp):
        if token_major:
            out_ref[0, :, kg * HEAD_DIM:(kg + 1) * HEAD_DIM] = comp.astype(BF16)
        elif kg < KV_HEADS:
            outk_ref[0, kg] = comp.astype(BF16)
        else:
            wide = jnp.concatenate([comp, jnp.zeros((rows, LANES - HEAD_DIM), F32)], axis=1)
            outvt_ref[0, kg - KV_HEADS] = wide.T[:HEAD_DIM, :].astype(BF16)

    def finish(kg, a0_prev, a1):
        kk = kg // KV_HEADS
        pre = a0_prev + a1 + bias_scr[kk]
        return _dot(_gelu_tanh(pre).astype(BF16), w2_ref[kk])

    def main():
        for cb in range(KV_ROW // LANES):
            for p in range(p_cnt):
                rows_t = page_refs[p][0, cb].astype(BF16).T.astype(F32)
                for ch in range(CHUNKS_PER_PAGE):
                    r0 = (p * CHUNKS_PER_PAGE + ch) * CHUNK_PITCH
                    slab_scr[cb, r0:r0 + CMP_STRIDE, :] = rows_t[ch * CMP_STRIDE:(ch + 1) * CMP_STRIDE, :]
            x = jnp.concatenate([slab_scr[cb, pl.ds(j, rows, stride=CHUNK_PITCH), :] for j in range(CMP_STRIDE)],
                                axis=1).astype(BF16)
            a_both = _dot(x, w1p_ref[cb])
            for half in range(2):
                kg = 2 * cb + half
                a = a_both[:, half * 2 * CMP_HID:(half + 1) * 2 * CMP_HID]
                a0 = a[:, :CMP_HID]
                a0_prev = jnp.where(row_id == 0, carry_scr[kg], pltpu.roll(a0, 1, 0))
                carry_scr[kg] = a0[rows - 1:rows, :]
                store(kg, finish(kg, a0_prev, a[:, CMP_HID:]))

    if not has_new:
        main()
    else:
        pl.when(i < n_steps)(main)

        @pl.when(i == n_steps)
        def _():
            for kg in range(N_KV_SLABS):
                comp = finish(kg, carry_scr[kg], a1new_ref[0, kg:kg + 1, :])
                store(kg, jnp.where(row_id == 0, jnp.broadcast_to(comp, (rows, HEAD_DIM)), 0.0))


def _new_chunk_kernel(x_ref, w1_ref, o_ref):
    for kg in range(N_KV_SLABS):
        o_ref[kg] = _dot(x_ref[kg].astype(BF16), w1_ref[kg // KV_HEADS])[:, CMP_HID:]


def _new_chunk_first_layer(xnew, w1cat):
    n_slab, nb, xw = xnew.shape
    return pl.pallas_call(
        _new_chunk_kernel,
        out_shape=jax.ShapeDtypeStruct((n_slab, nb, CMP_HID), F32),
        compiler_params=pltpu.CompilerParams(vmem_limit_bytes=VMEM_LIMIT_BYTES),
        name="new_chunk",
    )(xnew, w1cat)


def _compress(pages, page_table, w1cat, pe2, w2, a1new, token_major, per_batch):
    nb, n_pages = page_table.shape
    p_cnt = min(PAGES_PER_STEP, n_pages)
    assert n_pages % p_cnt == 0
    n_steps = n_pages // p_cnt
    has_new = a1new is not None
    rows = CHUNKS_PER_PAGE * p_cnt
    grid_steps = n_steps + (1 if has_new else 0)
    tot_rows = rows * grid_steps
    xw = CMP_STRIDE * HEAD_DIM

    def page_spec(p):
        def imap(b, i, pt):
            step = jnp.minimum(i, n_steps - 1)
            page = pt[b * n_pages + step * p_cnt + p]
            return (b, 0, 0, page) if per_batch else (page, 0, 0, 0)
        return pl.BlockSpec((1, KV_ROW // LANES, LANES, PAGE_SIZE), imap)

    n_cb = KV_ROW // LANES
    w4 = w1cat.reshape(2, CMP_STRIDE, HEAD_DIM, 2 * CMP_HID)
    zero = jnp.zeros_like(w4[0])
    w1p = jnp.stack([jnp.concatenate([jnp.concatenate([w4[(2 * cb) // KV_HEADS], zero], axis=2),
                                      jnp.concatenate([zero, w4[(2 * cb + 1) // KV_HEADS]], axis=2)], axis=1)
                     for cb in range(n_cb)]).reshape(n_cb, CMP_STRIDE * LANES, 4 * CMP_HID)
    in_specs = [page_spec(p) for p in range(p_cnt)]
    in_specs += [pl.BlockSpec((n_cb, CMP_STRIDE * LANES, 4 * CMP_HID), lambda b, i, pt: (0, 0, 0)),
                 pl.BlockSpec((2, xw, 2 * CMP_HID), lambda b, i, pt: (0, 0, 0)),
                 pl.BlockSpec((2, SUBLANES, xw), lambda b, i, pt: (0, 0, 0)),
                 pl.BlockSpec((2, CMP_HID, HEAD_DIM), lambda b, i, pt: (0, 0, 0))]
    args = [pages] * p_cnt + [w1p, w1cat, pe2, w2]
    if has_new:
        in_specs.append(pl.BlockSpec((1, N_KV_SLABS, CMP_HID), lambda b, i, pt: (b, 0, 0)))
        args.append(a1new)
    if token_major:
        out_spec = pl.BlockSpec((1, rows, KV_ROW), lambda b, i, pt: (b, i, 0))
        out_shape = jax.ShapeDtypeStruct((nb, tot_rows, KV_ROW), BF16)
    else:
        assert rows % LANES == 0
        out_spec = [pl.BlockSpec((1, KV_HEADS, rows, HEAD_DIM), lambda b, i, pt: (b, 0, i, 0)),
                    pl.BlockSpec((1, KV_HEADS, HEAD_DIM, rows), lambda b, i, pt: (b, 0, 0, i))]
        out_shape = [jax.ShapeDtypeStruct((nb, KV_HEADS, tot_rows, HEAD_DIM), BF16),
                     jax.ShapeDtypeStruct((nb, KV_HEADS, HEAD_DIM, tot_rows), BF16)]
    return pl.pallas_call(
        functools.partial(_compress_kernel, n_pages_step=p_cnt, n_steps=n_steps, has_new=has_new,
                          token_major=token_major),
        grid_spec=pltpu.PrefetchScalarGridSpec(
            num_scalar_prefetch=1, grid=(nb, grid_steps), in_specs=in_specs, out_specs=out_spec,
            scratch_shapes=[pltpu.VMEM((n_cb, rows * CHUNK_PITCH, LANES), F32),
                            pltpu.VMEM((N_KV_SLABS, 1, CMP_HID), F32),
                            pltpu.VMEM((2, 1, CMP_HID), F32)]),
        out_shape=out_shape,
        compiler_params=_params("arbitrary", "arbitrary"),
        name="compress_new" if has_new else "compress",
    )(page_table.reshape(-1), *args)


def _softmax_parts(s, valid, axis):
    s = jnp.where(valid, s, NEG)
    m = jnp.max(s, axis=axis, keepdims=True)
    e = jnp.where(valid, jnp.exp2(s - m), 0.0)
    return e, jnp.sum(e, axis=axis, keepdims=True)


def _nsa_prompt_kernel(qt_ref, kc_ref, vct_ref, ks_ref, vst_ref, kw_ref, vwt_ref, gt_ref, mt_ref, o_ref, *,
                       tq, kb, n_cmp, k_top):
    qi = pl.program_id(2)
    qs = qi * tq
    r4 = GQA_GROUP
    cols = r4 * tq
    qt = jnp.concatenate([qt_ref[0, r] for r in range(r4)], axis=1)
    tpos = qs + lax.broadcasted_iota(jnp.int32, (1, tq), 1)

    def per_head(v):
        return jnp.concatenate([v] * r4, axis=1)

    def value_tiles(ref, k0, width):
        j0 = k0 // LANES
        return jnp.concatenate([ref[0, 0, j0 + j] for j in range(width // LANES)], axis=1)

    def normalise(acc):
        return acc[:HEAD_DIM] * (1.0 / acc[HEAD_DIM:HEAD_DIM + 1])

    nc = kc_ref.shape[2]
    s = _dot(kc_ref[0, 0], qt)
    cid = lax.broadcasted_iota(jnp.int32, (nc, 1), 0)
    valid = (cid >= 1) & (cid <= n_cmp) & (cid * CMP_STRIDE + (CMP_BLOCK - CMP_STRIDE - 1) <= tpos)
    s = s + per_head(jnp.where(valid, 0.0, NEG))
    e = jnp.exp2(s - jnp.max(s, axis=0, keepdims=True))
    den = jnp.sum(e, axis=0, keepdims=True)
    has_key = per_head(jnp.where(tpos >= CMP_BLOCK - 1, 1.0, 0.0)) if n_cmp >= 1 else 0.0
    p = e * (has_key / jnp.maximum(den, 1e-30))
    o_c = _dot(vct_ref[0, 0], p.astype(BF16))
    pg = p[:, 0:tq]
    for r in range(1, r4):
        pg = pg + p[:, r * tq:(r + 1) * tq]

    mt = mt_ref[...]
    imp = sum(_dot(mt, piece) for piece in _split3(pg))
    n_blk = imp.shape[0]
    blk = lax.broadcasted_iota(jnp.int32, (n_blk, 1), 0)
    cur = (qs + lax.broadcasted_iota(jnp.int32, (1, tq), 1)) // SEL_BLOCK
    score = jnp.where(blk <= cur, imp, -1.0)
    score = jnp.where((blk == 0) | (blk == cur) | (blk == cur - 1), FORCE, score)
    sub = lax.broadcasted_iota(jnp.int32, (SUBLANES, 1), 0)
    tiles = [score[v * SUBLANES:(v + 1) * SUBLANES, :] for v in range(n_blk // SUBLANES)]
    ranks = [jnp.zeros((SUBLANES, tq), F32) for _ in tiles]
    for j in range(n_blk):
        sj = score[j:j + 1, :]
        for v, sc in enumerate(tiles):
            if v > j // SUBLANES:
                ahead = sj >= sc
            elif v < j // SUBLANES:
                ahead = sj > sc
            else:
                ahead = (sj > sc) | ((sj == sc) & (sub > j % SUBLANES))
            ranks[v] = ranks[v] + jnp.where(ahead, 1.0, 0.0)
    rank = jnp.concatenate(ranks, axis=0)
    sel_bias = jnp.where((rank < float(k_top)) & (blk <= cur), 0.0, NEG).astype(BF16)
    q_aug = jnp.concatenate([qt, jnp.concatenate([sel_bias] * r4, axis=1)], axis=0)

    def scores(k0):
        return _dot(ks_ref[0, 0, pl.ds(k0, kb), :], q_aug)

    def absorb(sk, values, state):
        m_run, acc = state
        m_new = jnp.maximum(m_run, jnp.max(sk, axis=0, keepdims=True))
        pv = _dot(values, jnp.exp2(sk - m_new).astype(BF16))
        return m_new, jnp.exp2(m_run - m_new) * acc + pv

    def tile(kt, state):
        k0 = pl.multiple_of(kt * kb, kb)
        return absorb(scores(k0), value_tiles(vst_ref, k0, kb), state)

    init = (jnp.full((1, cols), NEG, F32), jnp.zeros((VT_ROWS, cols), F32))
    m_run, acc = lax.fori_loop(0, qi, tile, init)

    n_sub = tq // LANES

    def sub_cols(v, j):
        return jnp.concatenate([v[:, r * tq + j * LANES:r * tq + (j + 1) * LANES] for r in range(r4)], axis=1)

    def join_subs(parts):
        return jnp.concatenate([parts[j][:, r * LANES:(r + 1) * LANES] for r in range(r4) for j in range(n_sub)],
                               axis=1)

    wl = min(kw_ref.shape[2], WINDOW + LANES)
    sel_parts, win_parts = [], []
    for j in range(n_sub):
        tpos_j = tpos[:, j * LANES:(j + 1) * LANES]
        heads_j = lambda v: jnp.concatenate([v] * r4, axis=1)
        nk = (j + 1) * LANES
        kpos = qs + lax.broadcasted_iota(jnp.int32, (nk, 1), 0)
        sk = _dot(ks_ref[0, 0, pl.ds(pl.multiple_of(qs, tq), nk), :], sub_cols(q_aug, j))
        sk = sk + heads_j(jnp.where(kpos <= tpos_j, 0.0, NEG))
        state_j = absorb(sk, value_tiles(vst_ref, qs, nk), (sub_cols(m_run, j), sub_cols(acc, j)))
        sel_parts.append(normalise(state_j[1]))
        w0 = pl.multiple_of(jnp.maximum(qs + (j + 1) * LANES - wl, 0), LANES)
        dq = tpos_j - (w0 + lax.broadcasted_iota(jnp.int32, (wl, 1), 0))
        sw = _dot(kw_ref[0, 0, pl.ds(w0, wl), :], sub_cols(qt, j))
        sw = sw + heads_j(jnp.where((dq >= 0) & (dq < WINDOW), 0.0, NEG))
        ew = jnp.exp2(sw - jnp.max(sw, axis=0, keepdims=True))
        win_parts.append(normalise(_dot(value_tiles(vwt_ref, w0, wl), ew.astype(BF16))))
    o_s = join_subs(sel_parts)
    o_w = join_subs(win_parts)

    gt = gt_ref[0, 0]
    heads = []
    for r in range(r4):
        c0 = r * N_BRANCH
        sl = slice(r * tq, (r + 1) * tq)
        heads.append(gt[c0:c0 + 1] * o_c[:, sl] + gt[c0 + 1:c0 + 2] * o_s[:, sl] + gt[c0 + 2:c0 + 3] * o_w[:, sl])
    o_ref[0] = jnp.concatenate(heads, axis=0).T


def _importance_matrix(n_blk_pad, n_col):
    mt = np.zeros((n_blk_pad, n_col), np.float32)
    ratio = SEL_BLOCK // CMP_STRIDE
    for b in range(n_blk_pad):
        for c, w in zip(range(ratio * b, ratio * b + ratio + 1), (1.0,) + (2.0,) * (ratio - 1) + (1.0,)):
            if c < n_col:
                mt[b, c] = w
    return mt


def _nsa_prompt(q_t, comp_k, comp_vt, ks_aug, vs_t, kw, vw_t, gates_t, tq, kb):
    b_sz, _, _, s_len = q_t.shape
    nc = comp_k.shape[2]
    assert s_len % SEL_BLOCK == 0 and s_len % kb == 0 and kb % tq == 0 and s_len % tq == 0 and CMP_SPAN == 2
    assert tq % LANES == 0 and kb == tq
    n_blk = s_len // SEL_BLOCK
    assert n_blk <= MAX_SEL_BLOCKS
    n_cmp = s_len // CMP_STRIDE - CMP_SPAN + 1
    mt = jnp.asarray(_importance_matrix(MAX_SEL_BLOCKS, nc), BF16)

    def per_group(*shape):
        return pl.BlockSpec((1, 1) + shape, lambda b, g, i: (b, g) + (0,) * len(shape))

    return pl.pallas_call(
        functools.partial(_nsa_prompt_kernel, tq=tq, kb=kb, n_cmp=n_cmp, k_top=min(N_SEL, n_blk)),
        grid=(b_sz, KV_HEADS, s_len // tq),
        in_specs=[pl.BlockSpec((1, GQA_GROUP, HEAD_DIM, tq), lambda b, g, i: (b, g, 0, i)),
                  per_group(nc, HEAD_DIM), per_group(HEAD_DIM, nc),
                  per_group(s_len, HEAD_DIM + MAX_SEL_BLOCKS), per_group(s_len // LANES, VT_ROWS, LANES),
                  per_group(s_len, HEAD_DIM), per_group(s_len // LANES, VT_ROWS, LANES),
                  pl.BlockSpec((1, 1, GATE_ROWS, tq), lambda b, g, i: (b, g, 0, i)),
                  pl.BlockSpec((MAX_SEL_BLOCKS, nc), lambda b, g, i: (0, 0))],
        out_specs=pl.BlockSpec((1, tq, GQA_GROUP * HEAD_DIM), lambda b, g, i: (b, i, g)),
        out_shape=jax.ShapeDtypeStruct((b_sz, s_len, NSA_WIDTH), F32),
        compiler_params=_params("arbitrary", "arbitrary", "arbitrary"),
        name="nsa_prompt",
    )(q_t, comp_k, comp_vt, ks_aug, vs_t, kw, vw_t, gates_t, mt)


Q_COLS = LANES
COLS_PER_HEAD = Q_COLS // GQA_GROUP
V_OFF = LANES
V_COL0 = KV_HEADS * HEAD_DIM - V_OFF

def _nsa_sample_kernel(pt_ref, *refs, n_pages_step, n_steps, t_new, past, n_cmp, n_sel, k_top):
    del pt_ref
    p_cnt = n_pages_step
    page_refs = refs[:p_cnt]
    (qbd_ref, comp_ref, knew_ref, win_ref, wnew_ref, g_ref,
     o_ref, kv_scr, bias_scr, p_scr, m_scr, l_scr, acc_scr, oc_scr) = refs[p_cnt:]
    i = pl.program_id(1)
    qbd = qbd_ref[0]
    col = lax.broadcasted_iota(jnp.int32, (1, Q_COLS), 1)
    tok = col & (SUBLANES - 1)
    qpos = past + tok
    kwin = slice(0, 2 * LANES)
    vwin = slice(V_OFF, V_OFF + 2 * LANES)
    k_lanes = KV_HEADS * HEAD_DIM
    blocks_per_page = PAGE_SIZE // SEL_BLOCK
    nb_step = blocks_per_page * p_cnt
    keys_step = PAGE_SIZE * p_cnt

    def attend(rows_bf16, valid):
        s = _dot(rows_bf16[:, kwin], qbd)
        return jnp.where(valid, s, NEG)

    @pl.when((pl.program_id(0) == 0) & (i == 0))
    def _():
        key_blk = lax.broadcasted_iota(jnp.int32, (keys_step, 1), 0) // SEL_BLOCK
        lane_blk = lax.broadcasted_iota(jnp.int32, (1, 2 * LANES - k_lanes), 1)
        kv_scr[:, k_lanes:2 * LANES] = (key_blk == lane_blk).astype(BF16)

    @pl.when(i == 0)
    def _():
        comp = comp_ref[0]
        ncp = comp.shape[0]
        cid = lax.broadcasted_iota(jnp.int32, (ncp, 1), 0)
        valid = (cid >= 1) & (cid <= n_cmp) & (cid * CMP_STRIDE + (CMP_BLOCK - CMP_STRIDE - 1) <= qpos)
        s = attend(comp, valid)
        e, den = _softmax_parts(s, valid, 0)
        p = e * (1.0 / jnp.maximum(den, 1e-30))
        oc_scr[...] = _dot_tn(p.astype(BF16), comp[:, vwin])
        nbp = bias_scr.shape[0]
        ratio = SEL_BLOCK // CMP_STRIDE
        p_scr[0:ncp, :] = p
        p_scr[ncp:, :] = jnp.zeros((p_scr.shape[0] - ncp, Q_COLS), F32)
        taps = [p_scr[pl.ds(k, nbp, stride=ratio), :] for k in range(ratio + 1)]
        imp = taps[0] + taps[ratio]
        for k in range(1, ratio):
            imp = imp + 2.0 * taps[k]
        imp = imp + pltpu.roll(imp, COLS_PER_HEAD, 1)
        imp = imp + pltpu.roll(imp, 2 * COLS_PER_HEAD, 1)
        blk = lax.broadcasted_iota(jnp.int32, (nbp, 1), 0)
        cur = qpos // SEL_BLOCK
        score = jnp.where(blk <= cur, imp, -1.0)
        score = jnp.where((blk == 0) | (blk == cur) | (blk == cur - 1), FORCE, score)
        score = jnp.where(blk < n_sel, score, PAD_SCORE)
        blk_f = blk.astype(F32)
        bias = jnp.full((nbp, Q_COLS), NEG, F32)
        for _ in range(k_top):
            best = jnp.max(score, axis=0, keepdims=True)
            first = jnp.min(jnp.where(score == best, blk_f, float(nbp)), axis=0, keepdims=True)
            hit = blk_f == first
            bias = jnp.where(hit, 0.0, bias)
            score = jnp.where(hit, TAKEN_SCORE, score)
        bias_scr[...] = bias
        m_scr[...] = jnp.full_like(m_scr, NEG)
        l_scr[...] = jnp.zeros_like(l_scr)
        acc_scr[...] = jnp.zeros_like(acc_scr)

    def as_col(v):
        return jnp.broadcast_to(v, (SUBLANES, Q_COLS)).T[:, 0:1]

    def partial_softmax(s, v_rows):
        m_g = jnp.max(s, axis=0, keepdims=True)
        e = jnp.exp2(s - m_g)
        return m_g, jnp.sum(e, axis=0, keepdims=True), _dot_tn(e.astype(BF16), v_rows)

    def accumulate(*parts):
        m_run = m_scr[0:1, :]
        m_new = m_run
        for m_g, _, _ in parts:
            m_new = jnp.maximum(m_new, m_g)
        w_run = jnp.exp2(m_run - m_new)
        l_new = w_run * l_scr[0:1, :]
        acc_new = as_col(w_run) * acc_scr[...]
        for m_g, l_g, acc_g in parts:
            w_g = jnp.exp2(m_g - m_new)
            l_new = l_new + w_g * l_g
            acc_new = acc_new + as_col(w_g) * acc_g
        m_scr[0:1, :] = m_new
        l_scr[0:1, :] = l_new
        acc_scr[...] = acc_new

    bias_step = bias_scr[pl.ds(pl.multiple_of(i * nb_step, nb_step), nb_step), :].astype(BF16)
    pad = 2 * LANES - k_lanes - nb_step
    q_parts = [qbd[:k_lanes], bias_step] + ([jnp.zeros((pad, Q_COLS), BF16)] if pad else [])
    q_step = jnp.concatenate(q_parts, axis=0)
    for p in range(p_cnt):
        rows_p = slice(p * PAGE_SIZE, (p + 1) * PAGE_SIZE)
        tiles = [page_refs[p][0, cb].astype(BF16).T for cb in range(KV_ROW // LANES)]
        kv_scr[rows_p, 0:LANES] = tiles[0]
        kv_scr[rows_p, LANES:k_lanes] = tiles[1][:, :k_lanes - LANES]
        kv_scr[rows_p, 2 * LANES:3 * LANES] = tiles[1]
        kv_scr[rows_p, 3 * LANES:4 * LANES] = tiles[2]
    accumulate(partial_softmax(_dot(kv_scr[:, 0:2 * LANES], q_step), kv_scr[:, 2 * LANES:4 * LANES]))

    @pl.when(i == n_steps - 1)
    def _():
        knew = knew_ref[0].astype(BF16)
        krow = lax.broadcasted_iota(jnp.int32, (t_new, 1), 0)
        new_blk = past // SEL_BLOCK
        s_new = _dot(knew[:, kwin], qbd) + bias_scr[new_blk:new_blk + 1, :]
        accumulate(partial_softmax(jnp.where(krow <= tok, s_new, NEG), knew[:, vwin]))
        o_s = acc_scr[...] * as_col(1.0 / l_scr[0:1, :])

        wbuf = jnp.concatenate([win_ref[0, cb].astype(BF16).T for cb in range(KV_ROW // LANES)], axis=1)
        wnew = wnew_ref[0].astype(BF16)
        buf = wbuf.shape[0]
        pos_b = past - buf + lax.broadcasted_iota(jnp.int32, (buf, 1), 0)
        pos_n = past + krow
        ok_b = (qpos - pos_b >= 0) & (qpos - pos_b < WINDOW) & (pos_b >= 0)
        ok_n = (qpos - pos_n >= 0) & (qpos - pos_n < WINDOW)
        s_b = attend(wbuf, ok_b)
        s_n = attend(wnew, ok_n)
        m_w = jnp.maximum(jnp.max(s_b, axis=0, keepdims=True), jnp.max(s_n, axis=0, keepdims=True))
        e_b = jnp.where(ok_b, jnp.exp2(s_b - m_w), 0.0)
        e_n = jnp.where(ok_n, jnp.exp2(s_n - m_w), 0.0)
        den = jnp.sum(e_b, axis=0, keepdims=True) + jnp.sum(e_n, axis=0, keepdims=True)
        o_w = _dot_tn(e_b.astype(BF16), wbuf[:, vwin]) + _dot_tn(e_n.astype(BF16), wnew[:, vwin])
        o_w = o_w * jnp.broadcast_to(1.0 / jnp.maximum(den, 1e-30), (SUBLANES, Q_COLS)).T[:, 0:1]
        o_c = oc_scr[...]

        gt = g_ref[0]
        for g in range(KV_HEADS):
            for r in range(GQA_GROUP):
                r0 = r * COLS_PER_HEAD + g * SUBLANES
                c0 = V_COL0 + g * HEAD_DIM
                gc = (g * GQA_GROUP + r) * N_BRANCH
                blk_o = [o[r0:r0 + t_new, c0:c0 + HEAD_DIM] for o in (o_c, o_s, o_w)]
                h0 = (g * GQA_GROUP + r) * HEAD_DIM
                o_ref[0, :, h0:h0 + HEAD_DIM] = (gt[:, gc:gc + 1] * blk_o[0] + gt[:, gc + 1:gc + 2] * blk_o[1]
                                                 + gt[:, gc + 2:gc + 3] * blk_o[2])


def _nsa_sample(cache_slc, page_table, qbd, comp_tm, knew, win_t, wnew, gates, past):
    db, n_pages = page_table.shape
    t_new = knew.shape[1]
    assert t_new == SUBLANES and KV_HEADS * t_new <= COLS_PER_HEAD and past % SEL_BLOCK == 0 and t_new <= SEL_BLOCK
    p_cnt = min(PAGES_PER_STEP, n_pages)
    assert n_pages % p_cnt == 0
    n_steps = n_pages // p_cnt
    ncp = comp_tm.shape[1]
    n_str = -(-(past + t_new) // CMP_STRIDE)
    n_cmp = n_str - CMP_SPAN + 1
    n_sel = -(-(past + t_new) // SEL_BLOCK)
    blocks_step = (PAGE_SIZE // SEL_BLOCK) * p_cnt
    nbp = -(-max(n_sel, n_steps * blocks_step + 1) // SUBLANES) * SUBLANES
    keys_step = PAGE_SIZE * p_cnt
    assert blocks_step <= 2 * LANES - KV_HEADS * HEAD_DIM
    assert CMP_SPAN == 2
    p_rows = max(ncp, (SEL_BLOCK // CMP_STRIDE) * nbp) + SUBLANES
    buf = win_t.shape[-1]
    assert buf % LANES == 0

    def page_spec(p):
        return pl.BlockSpec((1, KV_ROW // LANES, LANES, PAGE_SIZE),
                            lambda b, i, pt: (pt[b * n_pages + i * p_cnt + p], 0, 0, 0))

    def per_b(*shape):
        return pl.BlockSpec((1,) + shape, lambda b, i, pt: (b,) + (0,) * len(shape))

    in_specs = [page_spec(p) for p in range(p_cnt)]
    in_specs += [per_b(2 * LANES, Q_COLS), per_b(ncp, KV_ROW),
                 per_b(t_new, KV_ROW), per_b(KV_ROW // LANES, LANES, buf), per_b(t_new, KV_ROW), per_b(t_new, LANES)]
    return pl.pallas_call(
        functools.partial(_nsa_sample_kernel, n_pages_step=p_cnt, n_steps=n_steps, t_new=t_new, past=past,
                          n_cmp=n_cmp, n_sel=n_sel, k_top=min(N_SEL, n_sel)),
        grid_spec=pltpu.PrefetchScalarGridSpec(
            num_scalar_prefetch=1, grid=(db, n_steps), in_specs=in_specs,
            out_specs=per_b(t_new, NSA_WIDTH),
            scratch_shapes=[pltpu.VMEM((keys_step, 4 * LANES), BF16),
                            pltpu.VMEM((nbp, Q_COLS), F32), pltpu.VMEM((p_rows, Q_COLS), F32),
                            pltpu.VMEM((SUBLANES, Q_COLS), F32), pltpu.VMEM((SUBLANES, Q_COLS), F32),
                            pltpu.VMEM((Q_COLS, 2 * LANES), F32), pltpu.VMEM((Q_COLS, 2 * LANES), F32)]),
        out_shape=jax.ShapeDtypeStruct((db, t_new, NSA_WIDTH), F32),
        compiler_params=_params("arbitrary", "arbitrary"),
        name="nsa_sample",
    )(page_table.reshape(-1), *([cache_slc] * p_cnt), qbd, comp_tm, knew, win_t, wnew, gates)


def _out_kernel(x_ref, po_ref, zp_ref, no_ref, zn_ref, gate_ref, gnp_ref, gnn_ref, w_ref, fn_ref, y_ref):
    zp = zp_ref[0]
    zn = zn_ref[0]
    mp = _rms(po_ref[0], gnp_ref[...]) * (zp * _sigmoid(zp))
    mn = _rms(no_ref[0], gnn_ref[...]) * (zn * _sigmoid(zn))
    m = jnp.concatenate([mp, mn], axis=-1).astype(BF16)
    xo = x_ref[0] + gate_ref[0] * _dot(m, w_ref[...])
    y_ref[0] = _rms(xo, fn_ref[...])


def _out(x, pool_o, zp, nsa_o, zn, gate, gn_pool, gn_nsa, w_out_bf16, final_norm, tm):
    nb, t, d = x.shape
    r = gate.shape[1]
    assert t % tm == 0 and r in (1, t)
    if r == 1:
        gate_spec = pl.BlockSpec((1, 1, d), lambda b, i: (b, 0, 0))
    else:
        gate_spec = pl.BlockSpec((1, tm, d), lambda b, i: (b, i, 0))

    def tok(width):
        return pl.BlockSpec((1, tm, width), lambda b, i: (b, i, 0))

    def const(*shape):
        return pl.BlockSpec(shape, lambda b, i: (0,) * len(shape))

    return pl.pallas_call(
        _out_kernel,
        grid=(nb, t // tm),
        in_specs=[tok(d), tok(POOL_WIDTH), tok(POOL_WIDTH), tok(NSA_WIDTH), tok(NSA_WIDTH), gate_spec,
                  const(1, POOL_WIDTH), const(1, NSA_WIDTH), const(d, d), const(1, d)],
        out_specs=tok(d),
        out_shape=jax.ShapeDtypeStruct((nb, t, d), F32),
        compiler_params=_params("arbitrary", "arbitrary"),
        name="out",
    )(x, pool_o, zp, nsa_o, zn, gate, gn_pool.reshape(1, -1), gn_nsa.reshape(1, -1), w_out_bf16,
      final_norm.reshape(1, d))


def _pages_t(pages):
    n = pages.shape[0]
    return pages.transpose(0, 2, 3, 4, 1).reshape(n, KV_ROW // LANES, LANES, PAGE_SIZE)


def _block_diag(pool_w):
    n, c, _ = pool_w.shape
    eye = jnp.eye(n, dtype=pool_w.dtype)
    return (eye[:, None, :, None] * pool_w[:, :, None, :]).reshape(n * c, n * c)


def _first_layer(phi_w1, phi_pe):
    w = phi_w1.reshape(CMP_SPAN, CMP_STRIDE, 2, HEAD_DIM, CMP_HID)
    w1cat = w.transpose(2, 1, 3, 0, 4).reshape(2, CMP_STRIDE * HEAD_DIM, CMP_SPAN * CMP_HID)
    pe = phi_pe.reshape(CMP_SPAN, CMP_STRIDE, 2, HEAD_DIM).transpose(2, 0, 1, 3).reshape(2, CMP_SPAN, -1)
    pe2 = jnp.concatenate([pe, jnp.zeros((2, SUBLANES - CMP_SPAN, pe.shape[-1]), pe.dtype)], axis=1)
    return w1cat.astype(BF16), pe2


def kernel(x_prompt, x_sample, c_prompt, c_sample, cache_cmp_kv, cache_slc_kv, state_win_kv, state_pool, page_table, norm_w, w_ada, b_ada, w_in, pool_w, pool_scale, phi_w1, phi_pe, phi_w2, gn_pool, gn_nsa, w_out, final_norm):
    b_sz, s_len, d = x_prompt.shape
    db, t_new, _ = x_sample.shape
    depth = norm_w.shape[0]
    assert depth == 1 and w_in.shape[-1] == IN_WIDTH and d // 4 == POOL_WIDTH
    n_phys = cache_cmp_kv.shape[1]
    n_pages = page_table.shape[1]
    past = n_pages * PAGE_SIZE
    lyr = 0

    n_c = b_sz + db
    c_all = jnp.concatenate([c_prompt, c_sample, jnp.zeros((-n_c % SUBLANES, d), F32)], axis=0)
    ada = _ada(c_all, w_ada[lyr], b_ada[lyr])
    shift, scale, gate = ada[:, :d], ada[:, d:2 * d], ada[:, 2 * d:]

    w_in_t = w_in[lyr].T
    w_main = _cast_rows(w_in_t, IN_MAIN, 5 * LANES, BF16)
    w_gate = jnp.pad(w_in_t[IN_MAIN:].astype(BF16), ((0, LANES - GATE_COLS), (0, 0)))
    w_out_bf16 = w_out[lyr].astype(BF16)
    wbd = _block_diag(pool_w[lyr]).astype(BF16)
    w1cat, pe2 = _first_layer(phi_w1[lyr], phi_pe[lyr])
    w2 = phi_w2[lyr].astype(BF16)
    n_tok_s = db * t_new

    def per_token(v):
        return jnp.repeat(v[b_sz:n_c], t_new, axis=0)[None]

    tm = min(ROW_TILE, s_len)
    tq = min(ATTN_TILE, s_len)
    (u_p, zp_p, zn_p, q_t, ks_aug, kw_p, vs_t, vw_t, g_t, kvc_t, kvs_t, kvw_t) = _in_proj(
        x_prompt, scale[:b_sz, None], shift[:b_sz, None], norm_w[lyr], w_main, w_gate, tm, True)
    pool_p = _pool(jnp.zeros((b_sz, POOL_HALO, POOL_WIDTH), F32), u_p, wbd, pool_scale[lyr], 0, tm)
    pages_p = s_len // PAGE_SIZE
    in_order = jnp.tile(jnp.arange(pages_p, dtype=jnp.int32), (b_sz, 1))
    comp_k, comp_vt = _compress(kvc_t, in_order, w1cat, pe2, w2, None, False, True)
    nsa_p = _nsa_prompt(q_t, comp_k, comp_vt, ks_aug, vs_t, kw_p, vw_t, g_t, tq=tq, kb=tq)
    y_p = _out(x_prompt, pool_p, zp_p, nsa_p, zn_p, gate[:b_sz, None], gn_pool[lyr], gn_nsa[lyr], w_out_bf16,
               final_norm, tm)

    (u_s, zp_s, zn_s, kvc_s, kvs_s, kvw_s, q_s, g_s) = _in_proj(
        x_sample.reshape(1, n_tok_s, d), per_token(scale), per_token(shift), norm_w[lyr], w_main, w_gate, n_tok_s,
        False)
    u_s3 = u_s.reshape(db, t_new, POOL_WIDTH)
    hist = jnp.concatenate([jnp.zeros((db, POOL_HALO - state_pool.shape[2], POOL_WIDTH), F32), state_pool[lyr]], axis=1)
    pool_s = _pool(hist, u_s3, wbd, pool_scale[lyr], past, t_new)
    kvc_s3 = kvc_s.reshape(db, t_new, KV_ROW)
    kvs_s3 = kvs_s.reshape(db, t_new, KV_ROW)
    kvw_s3 = kvw_s.reshape(db, t_new, KV_ROW)
    xnew = jnp.pad(kvc_s3, ((0, 0), (0, CMP_STRIDE - t_new), (0, 0)))
    xnew = xnew.reshape(db, CMP_STRIDE, N_KV_SLABS, HEAD_DIM).transpose(2, 0, 1, 3).reshape(N_KV_SLABS, db, -1)
    a1new = _new_chunk_first_layer(xnew, w1cat).transpose(1, 0, 2)
    comp_s = _compress(_pages_t(cache_cmp_kv[lyr]), page_table, w1cat, pe2, w2, a1new, True, False)
    q5 = q_s.reshape(db, t_new, KV_HEADS, GQA_GROUP, HEAD_DIM)
    eye = jnp.eye(KV_HEADS, dtype=BF16)
    qbd = q5.transpose(0, 2, 4, 3, 1)[:, :, :, :, None, :] * eye[None, :, None, None, :, None]
    qbd = jnp.pad(qbd.reshape(db, KV_HEADS * HEAD_DIM, GQA_GROUP, KV_HEADS * t_new),
                  ((0, 0), (0, 2 * LANES - KV_HEADS * HEAD_DIM), (0, 0), (0, COLS_PER_HEAD - KV_HEADS * t_new)))
    qbd = qbd.reshape(db, 2 * LANES, Q_COLS)
    g_s3 = g_s.reshape(db, t_new, LANES)
    buf = state_win_kv.shape[2]
    win_t = state_win_kv[lyr].transpose(0, 2, 3, 4, 1).reshape(db, KV_ROW // LANES, LANES, buf)
    nsa_s = _nsa_sample(_pages_t(cache_slc_kv[lyr]), page_table, qbd, comp_s, kvs_s3,
                        win_t, kvw_s3, g_s3, past)
    y_s = _out(x_sample.reshape(1, n_tok_s, d), pool_s.reshape(1, n_tok_s, POOL_WIDTH), zp_s,
               nsa_s.reshape(1, n_tok_s, NSA_WIDTH), zn_s, per_token(gate), gn_pool[lyr], gn_nsa[lyr], w_out_bf16,
               final_norm, n_tok_s).reshape(db, t_new, d)

    kv_shape = (2, KV_HEADS, HEAD_DIM)
    win_len = min(WINDOW, s_len)
    new_win_s = jnp.concatenate([state_win_kv[lyr], kvw_s3.reshape((db, t_new) + kv_shape)], axis=1)[:, -buf:]
    hist_len = state_pool.shape[2]
    new_pool_s = jnp.concatenate([state_pool[lyr], u_s3], axis=1)[:, -hist_len:]

    def rows_from_t(v_t):
        return v_t.reshape((v_t.shape[0],) + kv_shape + (v_t.shape[-1],)).transpose(0, 4, 1, 2, 3)[None]

    return (y_p, y_s,
            rows_from_t(kvc_t), kvc_s3.reshape((1, db, t_new) + kv_shape),
            rows_from_t(kvs_t), kvs_s3.reshape((1, db, t_new) + kv_shape),
            rows_from_t(kvw_t[..., s_len - win_len:]), new_win_s[None],
            u_p[:, s_len - hist_len:][None], new_pool_s[None])
```

```python
import functools

import numpy as np
import jax
import jax.numpy as jnp
from jax import lax
from jax.experimental import pallas as pl
from jax.experimental.pallas import tpu as pltpu

F32 = jnp.float32
BF16 = jnp.bfloat16

HEAD_DIM = 64
GQA_GROUP = 4
KV_HEADS = 3
N_HEADS = KV_HEADS * GQA_GROUP
N_KV_SLABS = 2 * KV_HEADS
KV_ROW = N_KV_SLABS * HEAD_DIM
N_BRANCH = 3
POOL_WINDOWS = (2, 4, 8, 16)
POOL_HALO = 16
CMP_BLOCK = 32
CMP_STRIDE = 16
CMP_SPAN = CMP_BLOCK // CMP_STRIDE
CMP_HID = 2 * HEAD_DIM
SEL_BLOCK = 64
N_SEL = 16
WINDOW = 512
PAGE_SIZE = 128
CHUNKS_PER_PAGE = PAGE_SIZE // CMP_STRIDE
CHUNK_PITCH = 24
EPS = 1e-6
NEG = -1e30
FORCE = 1e4
LOG2_E = 1.4426950408889634
VT_ROWS = 80
PAD_SCORE = -1e38
TAKEN_SCORE = -3e38

LANES = 128
SUBLANES = 8
VMEM_LIMIT_BYTES = 56 * 1024 * 1024

ROW_TILE = 512
OUT_TILE = 1024
POOL_TILE = 4096
ATTN_TILE = 1024
PAGES_PER_STEP = 32


def _dot(a, b):
    return jnp.dot(a, b, preferred_element_type=F32)


def _dot_nt(a, b):
    return lax.dot_general(a, b, (((1,), (1,)), ((), ())), preferred_element_type=F32)


def _dot_tn(a, b):
    return lax.dot_general(a, b, (((0,), (0,)), ((), ())), preferred_element_type=F32)


def _sigmoid(v):
    return 1.0 / (1.0 + jnp.exp(-v))


def _rms(v, g):
    return v * lax.rsqrt(jnp.mean(v * v, axis=-1, keepdims=True) + EPS) * g


def _split3(v):
    hi = v.astype(BF16)
    r1 = v - hi.astype(F32)
    mid = r1.astype(BF16)
    lo = (r1 - mid.astype(F32)).astype(BF16)
    return hi, mid, lo


def _params(*sem):
    return pltpu.CompilerParams(dimension_semantics=sem, vmem_limit_bytes=VMEM_LIMIT_BYTES)


def _ada_kernel(c_ref, w_ref, b_ref, o_ref):
    c = c_ref[...]
    a = (c * _sigmoid(c)).astype(BF16)
    o_ref[...] = _dot(a, w_ref[...].astype(BF16)) + b_ref[...]


def _ada(c, w_ada, b_ada):
    m, d = c.shape
    n = w_ada.shape[1]
    tn = 512
    return pl.pallas_call(
        _ada_kernel,
        grid=(n // tn,),
        in_specs=[pl.BlockSpec((m, d), lambda j: (0, 0)),
                  pl.BlockSpec((d, tn), lambda j: (0, j)),
                  pl.BlockSpec((1, tn), lambda j: (0, j))],
        out_specs=pl.BlockSpec((m, tn), lambda j: (0, j)),
        out_shape=jax.ShapeDtypeStruct((m, n), F32),
        compiler_params=_params("arbitrary"),
        name="ada",
    )(c, w_ada, b_ada.reshape(1, n))


POOL_WIDTH = 256
NSA_WIDTH = N_HEADS * HEAD_DIM
GATE_COLS = N_BRANCH * N_HEADS
MAX_SEL_BLOCKS = LANES - HEAD_DIM
_SEG = {}
_off = 0
for _name, _w in (("u", POOL_WIDTH), ("zp", POOL_WIDTH), ("q", NSA_WIDTH), ("zn", NSA_WIDTH),
                  ("kvc", KV_ROW), ("kvs", KV_ROW), ("kvw", KV_ROW)):
    _SEG[_name] = (_off, _w)
    _off += _w
IN_MAIN = _off
IN_WIDTH = IN_MAIN + GATE_COLS


GATE_ROWS = 16


def _in_proj_kernel(x_ref, sc_ref, sh_ref, nw_ref, w_ref, wg_ref, u_ref, zp_ref, zn_ref, *rest, attn_layouts):
    tm = x_ref.shape[1]
    h = (_rms(x_ref[0], nw_ref[...]) * (1.0 + sc_ref[0]) + sh_ref[0]).astype(BF16)

    def seg(name):
        s, w = _SEG[name]
        return _dot_nt(h, w_ref[s:s + w, :])

    u_ref[0] = seg("u")
    zp_ref[0] = seg("zp")
    zn_ref[0] = seg("zn")
    q = seg("q") * (HEAD_DIM ** -0.5 * LOG2_E)
    kvc = seg("kvc")
    kvs = seg("kvs")
    kvw = seg("kvw")
    sig = _sigmoid(_dot_nt(h, wg_ref[...]))
    if not attn_layouts:
        kvc_ref, kvs_ref, kvw_ref, q_ref, g_ref = rest
        kvc_ref[0] = kvc
        kvs_ref[0] = kvs
        kvw_ref[0] = kvw
        q_ref[0] = q.astype(BF16)
        g_ref[0] = sig
        return

    qt_ref, ks_aug_ref, kw_ref, vst_ref, vwt_ref, gt_ref, kvct_ref, kvst_ref, kvwt_ref = rest
    qt_ref[0] = q.T.astype(BF16).reshape(N_HEADS, HEAD_DIM, tm)
    kvs_t32 = kvs.T
    kvw_t32 = kvw.T
    kvct_ref[0] = kvc.T.reshape(KV_ROW // LANES, LANES, tm)
    kvst_ref[0] = kvs_t32.reshape(KV_ROW // LANES, LANES, tm)
    kvwt_ref[0] = kvw_t32.reshape(KV_ROW // LANES, LANES, tm)
    pos = pl.program_id(1) * tm + lax.broadcasted_iota(jnp.int32, (tm, 1), 0)
    onehot = (lax.broadcasted_iota(jnp.int32, (1, MAX_SEL_BLOCKS), 1) == pos // SEL_BLOCK).astype(BF16)
    kvs_t = kvs_t32.astype(BF16)
    kvw_t = kvw_t32.astype(BF16)
    v0 = KV_HEADS * HEAD_DIM
    tail = (lax.broadcasted_iota(jnp.int32, (VT_ROWS - HEAD_DIM, LANES), 0) == 0).astype(BF16)
    for g in range(KV_HEADS):
        ks_aug_ref[0, g] = jnp.concatenate([kvs[:, g * HEAD_DIM:(g + 1) * HEAD_DIM].astype(BF16), onehot], axis=1)
        kw_ref[0, g] = kvw[:, g * HEAD_DIM:(g + 1) * HEAD_DIM].astype(BF16)
        for c in range(tm // LANES):
            sl = (slice(v0 + g * HEAD_DIM, v0 + (g + 1) * HEAD_DIM), slice(c * LANES, (c + 1) * LANES))
            vst_ref[0, g, c] = jnp.concatenate([kvs_t[sl], tail], axis=0)
            vwt_ref[0, g, c] = jnp.concatenate([kvw_t[sl], tail], axis=0)
        per_group = GQA_GROUP * N_BRANCH
        rolled = sig if g == 0 else pltpu.roll(sig, LANES - per_group * g, 1)
        gt_ref[0, g] = rolled.T[:GATE_ROWS, :]


def _cast_kernel(w_ref, o_ref):
    o_ref[...] = w_ref[...].astype(o_ref.dtype)


def _cast_rows(w, n_rows, tile, dtype):
    cols = w.shape[1]
    assert n_rows % tile == 0 and tile % SUBLANES == 0 and n_rows <= w.shape[0] and cols % LANES == 0
    return pl.pallas_call(
        _cast_kernel,
        grid=(n_rows // tile,),
        in_specs=[pl.BlockSpec((tile, cols), lambda j: (j, 0))],
        out_specs=pl.BlockSpec((tile, cols), lambda j: (j, 0)),
        out_shape=jax.ShapeDtypeStruct((n_rows, cols), dtype),
        compiler_params=_params("arbitrary"),
        name="cast_rows",
    )(w)


def _in_proj(x, scale, shift, norm_w, w_main, w_gate, tm, attn_layouts):
    nb, t, d = x.shape
    r = scale.shape[1]
    assert t % tm == 0 and r in (1, t) and (tm % LANES == 0 or not attn_layouts)
    if r == 1:
        mod_spec = pl.BlockSpec((1, 1, d), lambda b, i: (b, 0, 0))
    else:
        mod_spec = pl.BlockSpec((1, tm, d), lambda b, i: (b, i, 0))

    def tok(width):
        return pl.BlockSpec((1, tm, width), lambda b, i: (b, i, 0))

    def hm(n, width=HEAD_DIM):
        return pl.BlockSpec((1, n, tm, width), lambda b, i: (b, 0, i, 0))

    def sd(*shape, dtype=F32):
        return jax.ShapeDtypeStruct(shape, dtype)

    out_specs = [tok(POOL_WIDTH), tok(POOL_WIDTH), tok(NSA_WIDTH)]
    out_shape = [sd(nb, t, POOL_WIDTH), sd(nb, t, POOL_WIDTH), sd(nb, t, NSA_WIDTH)]
    if attn_layouts:
        lane_tiles = tm // LANES
        vt_spec = pl.BlockSpec((1, KV_HEADS, lane_tiles, VT_ROWS, LANES), lambda b, i: (b, 0, i, 0, 0))
        vt_shape = sd(nb, KV_HEADS, t // LANES, VT_ROWS, LANES, dtype=BF16)
        rows_t_spec = pl.BlockSpec((1, KV_ROW // LANES, LANES, tm), lambda b, i: (b, 0, 0, i))
        rows_t_shape = sd(nb, KV_ROW // LANES, LANES, t)
        out_specs += [pl.BlockSpec((1, N_HEADS, HEAD_DIM, tm), lambda b, i: (b, 0, 0, i)),
                      hm(KV_HEADS, HEAD_DIM + MAX_SEL_BLOCKS), hm(KV_HEADS), vt_spec, vt_spec,
                      pl.BlockSpec((1, KV_HEADS, GATE_ROWS, tm), lambda b, i: (b, 0, 0, i)),
                      rows_t_spec, rows_t_spec, rows_t_spec]
        out_shape += [sd(nb, N_HEADS, HEAD_DIM, t, dtype=BF16),
                      sd(nb, KV_HEADS, t, HEAD_DIM + MAX_SEL_BLOCKS, dtype=BF16),
                      sd(nb, KV_HEADS, t, HEAD_DIM, dtype=BF16), vt_shape, vt_shape,
                      sd(nb, KV_HEADS, GATE_ROWS, t), rows_t_shape, rows_t_shape, rows_t_shape]
    else:
        out_specs += [tok(KV_ROW), tok(KV_ROW), tok(KV_ROW), tok(NSA_WIDTH), tok(LANES)]
        out_shape += [sd(nb, t, KV_ROW), sd(nb, t, KV_ROW), sd(nb, t, KV_ROW),
                      sd(nb, t, NSA_WIDTH, dtype=BF16), sd(nb, t, LANES)]
    return pl.pallas_call(
        functools.partial(_in_proj_kernel, attn_layouts=attn_layouts),
        grid=(nb, t // tm),
        in_specs=[tok(d), mod_spec, mod_spec,
                  pl.BlockSpec((1, d), lambda b, i: (0, 0)),
                  pl.BlockSpec((IN_MAIN, d), lambda b, i: (0, 0)),
                  pl.BlockSpec((LANES, d), lambda b, i: (0, 0))],
        out_specs=out_specs,
        out_shape=out_shape,
        compiler_params=_params("arbitrary", "arbitrary"),
        name="in_proj",
    )(x, scale, shift, norm_w.reshape(1, d), w_main, w_gate)


def _pool_kernel(hist_ref, uprev_ref, u_ref, wbd_ref, ps_ref, o_ref, *, pos0, tp):
    i = pl.program_id(1)
    c = u_ref.shape[-1]
    grp = lax.broadcasted_iota(jnp.int32, (1, c), 1) // (c // len(POOL_WINDOWS))
    pos = pos0 + i * tp + lax.broadcasted_iota(jnp.int32, (tp, 1), 0)
    win = jnp.full((1, c), float(POOL_WINDOWS[-1]), F32)
    for gi in range(len(POOL_WINDOWS) - 2, -1, -1):
        win = jnp.where(grp == gi, float(POOL_WINDOWS[gi]), win)
    cnt = jnp.minimum(win, (pos + 1).astype(F32))
    for bi in range(u_ref.shape[0]):
        u = u_ref[bi]
        halo = jnp.where(i == 0, hist_ref[bi], uprev_ref[bi])
        sums = []
        s = jnp.concatenate([halo, u], axis=0)
        for w in POOL_WINDOWS:
            s = s + pltpu.roll(s, w // 2, 0)
            sums.append(s[POOL_HALO:])
        tot = sums[-1]
        for gi in range(len(POOL_WINDOWS) - 2, -1, -1):
            tot = jnp.where(grp == gi, sums[gi], tot)
        dlt = tot / cnt - u
        o_ref[bi] = _dot(dlt.astype(BF16), wbd_ref[...]) * ps_ref[...]


def _pool(hist, u, wbd, pool_scale, pos0, tp):
    nb, t, c = u.shape
    assert t % tp == 0
    if t >= POOL_HALO:
        assert tp % POOL_HALO == 0
        uprev, ratio, bb = u, tp // POOL_HALO, 1
        prev_spec = pl.BlockSpec((bb, POOL_HALO, c), lambda b, i: (b, jnp.maximum(i * ratio - 1, 0), 0))
    else:
        assert t == tp
        uprev, bb = hist, nb
        prev_spec = pl.BlockSpec((bb, POOL_HALO, c), lambda b, i: (b, 0, 0))
    return pl.pallas_call(
        functools.partial(_pool_kernel, pos0=pos0, tp=tp),
        grid=(nb // bb, t // tp),
        in_specs=[pl.BlockSpec((bb, POOL_HALO, c), lambda b, i: (b, 0, 0)), prev_spec,
                  pl.BlockSpec((bb, tp, c), lambda b, i: (b, i, 0)),
                  pl.BlockSpec((c, c), lambda b, i: (0, 0)),
                  pl.BlockSpec((1, c), lambda b, i: (0, 0))],
        out_specs=pl.BlockSpec((bb, tp, c), lambda b, i: (b, i, 0)),
        out_shape=jax.ShapeDtypeStruct((nb, t, c), F32),
        compiler_params=_params("arbitrary", "arbitrary"),
        name="pool",
    )(hist, uprev, u, wbd, pool_scale.reshape(1, c))


def _gelu_tanh(v):
    return 0.5 * v * (1.0 + jnp.tanh(np.sqrt(2.0 / np.pi).astype(np.float32) * (v + 0.044715 * (v * v * v))))


def _compress_kernel(pt_ref, *refs, n_pages_step, n_steps, has_new, token_major):
    del pt_ref
    p_cnt = n_pages_step
    page_refs = refs[:p_cnt]
    w1p_ref, w1_ref, pe_ref, w2_ref = refs[p_cnt:p_cnt + 4]
    k = p_cnt + 4
    a1new_ref = None
    if has_new:
        a1new_ref = refs[k]
        k += 1
    if token_major:
        out_ref = refs[k]
        k += 1
    else:
        outk_ref, outvt_ref = refs[k:k + 2]
        k += 2
    slab_scr, carry_scr, bias_scr = refs[k:k + 3]
    i = pl.program_id(1)
    rows = CHUNKS_PER_PAGE * p_cnt
    row_id = lax.broadcasted_iota(jnp.int32, (rows, 1), 0)

    @pl.when(i == 0)
    def _():
        carry_scr[...] = jnp.zeros_like(carry_scr)
        for kk in range(2):
            pb = _dot(pe_ref[kk].astype(BF16), w1_ref[kk])
            bias_scr[kk] = pb[0:1, :CMP_HID] + pb[1:2, CMP_HID:]

    def store(kg, comp):
        if token_major:
            out_ref[0, :, kg * HEAD_DIM:(kg + 1) * HEAD_DIM] = comp.astype(BF16)
        elif kg < KV_HEADS:
            outk_ref[0, kg] = comp.astype(BF16)
        else:
            wide = jnp.concatenate([comp, jnp.zeros((rows, LANES - HEAD_DIM), F32)], axis=1)
            outvt_ref[0, kg - KV_HEADS] = wide.T[:HEAD_DIM, :].astype(BF16)

    def finish(kg, a0_prev, a1):
        kk = kg // KV_HEADS
        pre = a0_prev + a1 + bias_scr[kk]
        return _dot(_gelu_tanh(pre).astype(BF16), w2_ref[kk])

    def main():
        for cb in range(KV_ROW // LANES):
            for p in range(p_cnt):
                rows_t = page_refs[p][0, cb].astype(BF16).T.astype(F32)
                for ch in range(CHUNKS_PER_PAGE):
                    r0 = (p * CHUNKS_PER_PAGE + ch) * CHUNK_PITCH
                    slab_scr[cb, r0:r0 + CMP_STRIDE, :] = rows_t[ch * CMP_STRIDE:(ch + 1) * CMP_STRIDE, :]
            x = jnp.concatenate([slab_scr[cb, pl.ds(j, rows, stride=CHUNK_PITCH), :] for j in range(CMP_STRIDE)],
                                axis=1).astype(BF16)
            a_both = _dot(x, w1p_ref[cb])
            for half in range(2):
                kg = 2 * cb + half
                a = a_both[:, half * 2 * CMP_HID:(half + 1) * 2 * CMP_HID]
                a0 = a[:, :CMP_HID]
                a0_prev = jnp.where(row_id == 0, carry_scr[kg], pltpu.roll(a0, 1, 0))
                carry_scr[kg] = a0[rows - 1:rows, :]
                store(kg, finish(kg, a0_prev, a[:, CMP_HID:]))

    if not has_new:
        main()
    else:
        pl.when(i < n_steps)(main)

        @pl.when(i == n_steps)
        def _():
            for kg in range(N_KV_SLABS):
                comp = finish(kg, carry_scr[kg], a1new_ref[0, kg:kg + 1, :])
                store(kg, jnp.where(row_id == 0, jnp.broadcast_to(comp, (rows, HEAD_DIM)), 0.0))


def _new_chunk_kernel(x_ref, w1_ref, o_ref):
    for kg in range(N_KV_SLABS):
        o_ref[kg] = _dot(x_ref[kg].astype(BF16), w1_ref[kg // KV_HEADS])[:, CMP_HID:]


def _new_chunk_first_layer(xnew, w1cat):
    n_slab, nb, xw = xnew.shape
    return pl.pallas_call(
        _new_chunk_kernel,
        out_shape=jax.ShapeDtypeStruct((n_slab, nb, CMP_HID), F32),
        compiler_params=pltpu.CompilerParams(vmem_limit_bytes=VMEM_LIMIT_BYTES),
        name="new_chunk",
    )(xnew, w1cat)


def _compress(pages, page_table, w1cat, pe2, w2, a1new, token_major, per_batch):
    nb, n_pages = page_table.shape
    p_cnt = min(PAGES_PER_STEP, n_pages)
    assert n_pages % p_cnt == 0
    n_steps = n_pages // p_cnt
    has_new = a1new is not None
    rows = CHUNKS_PER_PAGE * p_cnt
    grid_steps = n_steps + (1 if has_new else 0)
    tot_rows = rows * grid_steps
    xw = CMP_STRIDE * HEAD_DIM

    def page_spec(p):
        def imap(b, i, pt):
            step = jnp.minimum(i, n_steps - 1)
            page = pt[b * n_pages + step * p_cnt + p]
            return (b, 0, 0, page) if per_batch else (page, 0, 0, 0)
        return pl.BlockSpec((1, KV_ROW // LANES, LANES, PAGE_SIZE), imap)

    n_cb = KV_ROW // LANES
    w4 = w1cat.reshape(2, CMP_STRIDE, HEAD_DIM, 2 * CMP_HID)
    zero = jnp.zeros_like(w4[0])
    w1p = jnp.stack([jnp.concatenate([jnp.concatenate([w4[(2 * cb) // KV_HEADS], zero], axis=2),
                                      jnp.concatenate([zero, w4[(2 * cb + 1) // KV_HEADS]], axis=2)], axis=1)
                     for cb in range(n_cb)]).reshape(n_cb, CMP_STRIDE * LANES, 4 * CMP_HID)
    in_specs = [page_spec(p) for p in range(p_cnt)]
    in_specs += [pl.BlockSpec((n_cb, CMP_STRIDE * LANES, 4 * CMP_HID), lambda b, i, pt: (0, 0, 0)),
                 pl.BlockSpec((2, xw, 2 * CMP_HID), lambda b, i, pt: (0, 0, 0)),
                 pl.BlockSpec((2, SUBLANES, xw), lambda b, i, pt: (0, 0, 0)),
                 pl.BlockSpec((2, CMP_HID, HEAD_DIM), lambda b, i, pt: (0, 0, 0))]
    args = [pages] * p_cnt + [w1p, w1cat, pe2, w2]
    if has_new:
        in_specs.append(pl.BlockSpec((1, N_KV_SLABS, CMP_HID), lambda b, i, pt: (b, 0, 0)))
        args.append(a1new)
    if token_major:
        out_spec = pl.BlockSpec((1, rows, KV_ROW), lambda b, i, pt: (b, i, 0))
        out_shape = jax.ShapeDtypeStruct((nb, tot_rows, KV_ROW), BF16)
    else:
        assert rows % LANES == 0
        out_spec = [pl.BlockSpec((1, KV_HEADS, rows, HEAD_DIM), lambda b, i, pt: (b, 0, i, 0)),
                    pl.BlockSpec((1, KV_HEADS, HEAD_DIM, rows), lambda b, i, pt: (b, 0, 0, i))]
        out_shape = [jax.ShapeDtypeStruct((nb, KV_HEADS, tot_rows, HEAD_DIM), BF16),
                     jax.ShapeDtypeStruct((nb, KV_HEADS, HEAD_DIM, tot_rows), BF16)]
    return pl.pallas_call(
        functools.partial(_compress_kernel, n_pages_step=p_cnt, n_steps=n_steps, has_new=has_new,
                          token_major=token_major),
        grid_spec=pltpu.PrefetchScalarGridSpec(
            num_scalar_prefetch=1, grid=(nb, grid_steps), in_specs=in_specs, out_specs=out_spec,
            scratch_shapes=[pltpu.VMEM((n_cb, rows * CHUNK_PITCH, LANES), F32),
                            pltpu.VMEM((N_KV_SLABS, 1, CMP_HID), F32),
                            pltpu.VMEM((2, 1, CMP_HID), F32)]),
        out_shape=out_shape,
        compiler_params=_params("arbitrary", "arbitrary"),
        name="compress_new" if has_new else "compress",
    )(page_table.reshape(-1), *args)


def _softmax_parts(s, valid, axis):
    s = jnp.where(valid, s, NEG)
    m = jnp.max(s, axis=axis, keepdims=True)
    e = jnp.where(valid, jnp.exp2(s - m), 0.0)
    return e, jnp.sum(e, axis=axis, keepdims=True)


def _nsa_prompt_kernel(qt_ref, kc_ref, vct_ref, ks_ref, vst_ref, kw_ref, vwt_ref, gt_ref, mt_ref, o_ref, *,
                       tq, kb, n_cmp, k_top):
    qi = pl.program_id(2)
    qs = qi * tq
    r4 = GQA_GROUP
    cols = r4 * tq
    qt = jnp.concatenate([qt_ref[0, r] for r in range(r4)], axis=1)
    tpos = qs + lax.broadcasted_iota(jnp.int32, (1, tq), 1)

    def per_head(v):
        return jnp.concatenate([v] * r4, axis=1)

    def value_tiles(ref, k0, width):
        j0 = k0 // LANES
        return jnp.concatenate([ref[0, 0, j0 + j] for j in range(width // LANES)], axis=1)

    def normalise(acc):
        return acc[:HEAD_DIM] * (1.0 / acc[HEAD_DIM:HEAD_DIM + 1])

    nc = kc_ref.shape[2]
    s = _dot(kc_ref[0, 0], qt)
    cid = lax.broadcasted_iota(jnp.int32, (nc, 1), 0)
    valid = (cid >= 1) & (cid <= n_cmp) & (cid * CMP_STRIDE + (CMP_BLOCK - CMP_STRIDE - 1) <= tpos)
    s = s + per_head(jnp.where(valid, 0.0, NEG))
    e = jnp.exp2(s - jnp.max(s, axis=0, keepdims=True))
    den = jnp.sum(e, axis=0, keepdims=True)
    has_key = per_head(jnp.where(tpos >= CMP_BLOCK - 1, 1.0, 0.0)) if n_cmp >= 1 else 0.0
    p = e * (has_key / jnp.maximum(den, 1e-30))
    o_c = _dot(vct_ref[0, 0], p.astype(BF16))
    pg = p[:, 0:tq]
    for r in range(1, r4):
        pg = pg + p[:, r * tq:(r + 1) * tq]

    mt = mt_ref[...]
    imp = sum(_dot(mt, piece) for piece in _split3(pg))
    n_blk = imp.shape[0]
    blk = lax.broadcasted_iota(jnp.int32, (n_blk, 1), 0)
    cur = (qs + lax.broadcasted_iota(jnp.int32, (1, tq), 1)) // SEL_BLOCK
    score = jnp.where(blk <= cur, imp, -1.0)
    score = jnp.where((blk == 0) | (blk == cur) | (blk == cur - 1), FORCE, score)
    sub = lax.broadcasted_iota(jnp.int32, (SUBLANES, 1), 0)
    tiles = [score[v * SUBLANES:(v + 1) * SUBLANES, :] for v in range(n_blk // SUBLANES)]
    ranks = [jnp.zeros((SUBLANES, tq), F32) for _ in tiles]
    for j in range(n_blk):
        sj = score[j:j + 1, :]
        for v, sc in enumerate(tiles):
            if v > j // SUBLANES:
                ahead = sj >= sc
            elif v < j // SUBLANES:
                ahead = sj > sc
            else:
                ahead = (sj > sc) | ((sj == sc) & (sub > j % SUBLANES))
            ranks[v] = ranks[v] + jnp.where(ahead, 1.0, 0.0)
    rank = jnp.concatenate(ranks, axis=0)
    sel_bias = jnp.where((rank < float(k_top)) & (blk <= cur), 0.0, NEG).astype(BF16)
    q_aug = jnp.concatenate([qt, jnp.concatenate([sel_bias] * r4, axis=1)], axis=0)

    def scores(k0):
        return _dot(ks_ref[0, 0, pl.ds(k0, kb), :], q_aug)

    def absorb(sk, values, state):
        m_run, acc = state
        m_new = jnp.maximum(m_run, jnp.max(sk, axis=0, keepdims=True))
        pv = _dot(values, jnp.exp2(sk - m_new).astype(BF16))
        return m_new, jnp.exp2(m_run - m_new) * acc + pv

    def tile(kt, state):
        k0 = pl.multiple_of(kt * kb, kb)
        return absorb(scores(k0), value_tiles(vst_ref, k0, kb), state)

    init = (jnp.full((1, cols), NEG, F32), jnp.zeros((VT_ROWS, cols), F32))
    m_run, acc = lax.fori_loop(0, qi, tile, init)

    n_sub = tq // LANES

    def sub_cols(v, j):
        return jnp.concatenate([v[:, r * tq + j * LANES:r * tq + (j + 1) * LANES] for r in range(r4)], axis=1)

    def join_subs(parts):
        return jnp.concatenate([parts[j][:, r * LANES:(r + 1) * LANES] for r in range(r4) for j in range(n_sub)],
                               axis=1)

    wl = min(kw_ref.shape[2], WINDOW + LANES)
    sel_parts, win_parts = [], []
    for j in range(n_sub):
        tpos_j = tpos[:, j * LANES:(j + 1) * LANES]
        heads_j = lambda v: jnp.concatenate([v] * r4, axis=1)
        nk = (j + 1) * LANES
        kpos = qs + lax.broadcasted_iota(jnp.int32, (nk, 1), 0)
        sk = _dot(ks_ref[0, 0, pl.ds(pl.multiple_of(qs, tq), nk), :], sub_cols(q_aug, j))
        sk = sk + heads_j(jnp.where(kpos <= tpos_j, 0.0, NEG))
        state_j = absorb(sk, value_tiles(vst_ref, qs, nk), (sub_cols(m_run, j), sub_cols(acc, j)))
        sel_parts.append(normalise(state_j[1]))
        w0 = pl.multiple_of(jnp.maximum(qs + (j + 1) * LANES - wl, 0), LANES)
        dq = tpos_j - (w0 + lax.broadcasted_iota(jnp.int32, (wl, 1), 0))
        sw = _dot(kw_ref[0, 0, pl.ds(w0, wl), :], sub_cols(qt, j))
        sw = sw + heads_j(jnp.where((dq >= 0) & (dq < WINDOW), 0.0, NEG))
        ew = jnp.exp2(sw - jnp.max(sw, axis=0, keepdims=True))
        win_parts.append(normalise(_dot(value_tiles(vwt_ref, w0, wl), ew.astype(BF16))))
    o_s = join_subs(sel_parts)
    o_w = join_subs(win_parts)

    gt = gt_ref[0, 0]
    heads = []
    for r in range(r4):
        c0 = r * N_BRANCH
        sl = slice(r * tq, (r + 1) * tq)
        heads.append(gt[c0:c0 + 1] * o_c[:, sl] + gt[c0 + 1:c0 + 2] * o_s[:, sl] + gt[c0 + 2:c0 + 3] * o_w[:, sl])
    o_ref[0] = jnp.concatenate(heads, axis=0).T


def _importance_matrix(n_blk_pad, n_col):
    mt = np.zeros((n_blk_pad, n_col), np.float32)
    ratio = SEL_BLOCK // CMP_STRIDE
    for b in range(n_blk_pad):
        for c, w in zip(range(ratio * b, ratio * b + ratio + 1), (1.0,) + (2.0,) * (ratio - 1) + (1.0,)):
            if c < n_col:
                mt[b, c] = w
    return mt


def _nsa_prompt(q_t, comp_k, comp_vt, ks_aug, vs_t, kw, vw_t, gates_t, tq, kb):
    b_sz, _, _, s_len = q_t.shape
    nc = comp_k.shape[2]
    assert s_len % SEL_BLOCK == 0 and s_len % kb == 0 and kb % tq == 0 and s_len % tq == 0 and CMP_SPAN == 2
    assert tq % LANES == 0 and kb == tq
    n_blk = s_len // SEL_BLOCK
    assert n_blk <= MAX_SEL_BLOCKS
    n_cmp = s_len // CMP_STRIDE - CMP_SPAN + 1
    mt = jnp.asarray(_importance_matrix(MAX_SEL_BLOCKS, nc), BF16)

    def per_group(*shape):
        return pl.BlockSpec((1, 1) + shape, lambda b, g, i: (b, g) + (0,) * len(shape))

    return pl.pallas_call(
        functools.partial(_nsa_prompt_kernel, tq=tq, kb=kb, n_cmp=n_cmp, k_top=min(N_SEL, n_blk)),
        grid=(b_sz, KV_HEADS, s_len // tq),
        in_specs=[pl.BlockSpec((1, GQA_GROUP, HEAD_DIM, tq), lambda b, g, i: (b, g, 0, i)),
                  per_group(nc, HEAD_DIM), per_group(HEAD_DIM, nc),
                  per_group(s_len, HEAD_DIM + MAX_SEL_BLOCKS), per_group(s_len // LANES, VT_ROWS, LANES),
                  per_group(s_len, HEAD_DIM), per_group(s_len // LANES, VT_ROWS, LANES),
                  pl.BlockSpec((1, 1, GATE_ROWS, tq), lambda b, g, i: (b, g, 0, i)),
                  pl.BlockSpec((MAX_SEL_BLOCKS, nc), lambda b, g, i: (0, 0))],
        out_specs=pl.BlockSpec((1, tq, GQA_GROUP * HEAD_DIM), lambda b, g, i: (b, i, g)),
        out_shape=jax.ShapeDtypeStruct((b_sz, s_len, NSA_WIDTH), F32),
        compiler_params=_params("arbitrary", "arbitrary", "arbitrary"),
        name="nsa_prompt",
    )(q_t, comp_k, comp_vt, ks_aug, vs_t, kw, vw_t, gates_t, mt)


Q_COLS = LANES
COLS_PER_HEAD = Q_COLS // GQA_GROUP
V_OFF = LANES
V_COL0 = KV_HEADS * HEAD_DIM - V_OFF

def _nsa_sample_kernel(pt_ref, *refs, n_pages_step, n_steps, t_new, past, n_cmp, n_sel, k_top):
    del pt_ref
    p_cnt = n_pages_step
    page_refs = refs[:p_cnt]
    (qbd_ref, comp_ref, knew_ref, win_ref, wnew_ref, g_ref,
     o_ref, kv_scr, bias_scr, p_scr, m_scr, l_scr, acc_scr, oc_scr) = refs[p_cnt:]
    i = pl.program_id(1)
    qbd = qbd_ref[0]
    col = lax.broadcasted_iota(jnp.int32, (1, Q_COLS), 1)
    tok = col & (SUBLANES - 1)
    qpos = past + tok
    kwin = slice(0, 2 * LANES)
    vwin = slice(V_OFF, V_OFF + 2 * LANES)
    k_lanes = KV_HEADS * HEAD_DIM
    blocks_per_page = PAGE_SIZE // SEL_BLOCK
    nb_step = blocks_per_page * p_cnt
    keys_step = PAGE_SIZE * p_cnt

    def attend(rows_bf16, valid):
        s = _dot(rows_bf16[:, kwin], qbd)
        return jnp.where(valid, s, NEG)

    @pl.when((pl.program_id(0) == 0) & (i == 0))
    def _():
        key_blk = lax.broadcasted_iota(jnp.int32, (keys_step, 1), 0) // SEL_BLOCK
        lane_blk = lax.broadcasted_iota(jnp.int32, (1, 2 * LANES - k_lanes), 1)
        kv_scr[:, k_lanes:2 * LANES] = (key_blk == lane_blk).astype(BF16)

    @pl.when(i == 0)
    def _():
        comp = comp_ref[0]
        ncp = comp.shape[0]
        cid = lax.broadcasted_iota(jnp.int32, (ncp, 1), 0)
        valid = (cid >= 1) & (cid <= n_cmp) & (cid * CMP_STRIDE + (CMP_BLOCK - CMP_STRIDE - 1) <= qpos)
        s = attend(comp, valid)
        e, den = _softmax_parts(s, valid, 0)
        p = e * (1.0 / jnp.maximum(den, 1e-30))
        oc_scr[...] = _dot_tn(p.astype(BF16), comp[:, vwin])
        nbp = bias_scr.shape[0]
        ratio = SEL_BLOCK // CMP_STRIDE
        p_scr[0:ncp, :] = p
        p_scr[ncp:, :] = jnp.zeros((p_scr.shape[0] - ncp, Q_COLS), F32)
        taps = [p_scr[pl.ds(k, nbp, stride=ratio), :] for k in range(ratio + 1)]
        imp = taps[0] + taps[ratio]
        for k in range(1, ratio):
            imp = imp + 2.0 * taps[k]
        imp = imp + pltpu.roll(imp, COLS_PER_HEAD, 1)
        imp = imp + pltpu.roll(imp, 2 * COLS_PER_HEAD, 1)
        blk = lax.broadcasted_iota(jnp.int32, (nbp, 1), 0)
        cur = qpos // SEL_BLOCK
        score = jnp.where(blk <= cur, imp, -1.0)
        score = jnp.where((blk == 0) | (blk == cur) | (blk == cur - 1), FORCE, score)
        score = jnp.where(blk < n_sel, score, PAD_SCORE)
        blk_f = blk.astype(F32)
        bias = jnp.full((nbp, Q_COLS), NEG, F32)
        for _ in range(k_top):
            best = jnp.max(score, axis=0, keepdims=True)
            first = jnp.min(jnp.where(score == best, blk_f, float(nbp)), axis=0, keepdims=True)
            hit = blk_f == first
            bias = jnp.where(hit, 0.0, bias)
            score = jnp.where(hit, TAKEN_SCORE, score)
        bias_scr[...] = bias
        m_scr[...] = jnp.full_like(m_scr, NEG)
        l_scr[...] = jnp.zeros_like(l_scr)
        acc_scr[...] = jnp.zeros_like(acc_scr)

    def as_col(v):
        return jnp.broadcast_to(v, (SUBLANES, Q_COLS)).T[:, 0:1]

    def partial_softmax(s, v_rows):
        m_g = jnp.max(s, axis=0, keepdims=True)
        e = jnp.exp2(s - m_g)
        return m_g, jnp.sum(e, axis=0, keepdims=True), _dot_tn(e.astype(BF16), v_rows)

    def accumulate(*parts):
        m_run = m_scr[0:1, :]
        m_new = m_run
        for m_g, _, _ in parts:
            m_new = jnp.maximum(m_new, m_g)
        w_run = jnp.exp2(m_run - m_new)
        l_new = w_run * l_scr[0:1, :]
        acc_new = as_col(w_run) * acc_scr[...]
        for m_g, l_g, acc_g in parts:
            w_g = jnp.exp2(m_g - m_new)
            l_new = l_new + w_g * l_g
            acc_new = acc_new + as_col(w_g) * acc_g
        m_scr[0:1, :] = m_new
        l_scr[0:1, :] = l_new
        acc_scr[...] = acc_new

    bias_step = bias_scr[pl.ds(pl.multiple_of(i * nb_step, nb_step), nb_step), :].astype(BF16)
    pad = 2 * LANES - k_lanes - nb_step
    q_parts = [qbd[:k_lanes], bias_step] + ([jnp.zeros((pad, Q_COLS), BF16)] if pad else [])
    q_step = jnp.concatenate(q_parts, axis=0)
    for p in range(p_cnt):
        rows_p = slice(p * PAGE_SIZE, (p + 1) * PAGE_SIZE)
        tiles = [page_refs[p][0, cb].astype(BF16).T for cb in range(KV_ROW // LANES)]
        kv_scr[rows_p, 0:LANES] = tiles[0]
        kv_scr[rows_p, LANES:k_lanes] = tiles[1][:, :k_lanes - LANES]
        kv_scr[rows_p, 2 * LANES:3 * LANES] = tiles[1]
        kv_scr[rows_p, 3 * LANES:4 * LANES] = tiles[2]
    accumulate(partial_softmax(_dot(kv_scr[:, 0:2 * LANES], q_step), kv_scr[:, 2 * LANES:4 * LANES]))

    @pl.when(i == n_steps - 1)
    def _():
        knew = knew_ref[0].astype(BF16)
        krow = lax.broadcasted_iota(jnp.int32, (t_new, 1), 0)
        new_blk = past // SEL_BLOCK
        s_new = _dot(knew[:, kwin], qbd) + bias_scr[new_blk:new_blk + 1, :]
        accumulate(partial_softmax(jnp.where(krow <= tok, s_new, NEG), knew[:, vwin]))
        o_s = acc_scr[...] * as_col(1.0 / l_scr[0:1, :])

        wbuf = jnp.concatenate([win_ref[0, cb].astype(BF16).T for cb in range(KV_ROW // LANES)], axis=1)
        wnew = wnew_ref[0].astype(BF16)
        buf = wbuf.shape[0]
        pos_b = past - buf + lax.broadcasted_iota(jnp.int32, (buf, 1), 0)
        pos_n = past + krow
        ok_b = (qpos - pos_b >= 0) & (qpos - pos_b < WINDOW) & (pos_b >= 0)
        ok_n = (qpos - pos_n >= 0) & (qpos - pos_n < WINDOW)
        s_b = attend(wbuf, ok_b)
        s_n = attend(wnew, ok_n)
        m_w = jnp.maximum(jnp.max(s_b, axis=0, keepdims=True), jnp.max(s_n, axis=0, keepdims=True))
        e_b = jnp.where(ok_b, jnp.exp2(s_b - m_w), 0.0)
        e_n = jnp.where(ok_n, jnp.exp2(s_n - m_w), 0.0)
        den = jnp.sum(e_b, axis=0, keepdims=True) + jnp.sum(e_n, axis=0, keepdims=True)
        o_w = _dot_tn(e_b.astype(BF16), wbuf[:, vwin]) + _dot_tn(e_n.astype(BF16), wnew[:, vwin])
        o_w = o_w * jnp.broadcast_to(1.0 / jnp.maximum(den, 1e-30), (SUBLANES, Q_COLS)).T[:, 0:1]
        o_c = oc_scr[...]

        gt = g_ref[0]
        for g in range(KV_HEADS):
            for r in range(GQA_GROUP):
                r0 = r * COLS_PER_HEAD + g * SUBLANES
                c0 = V_COL0 + g * HEAD_DIM
                gc = (g * GQA_GROUP + r) * N_BRANCH
                blk_o = [o[r0:r0 + t_new, c0:c0 + HEAD_DIM] for o in (o_c, o_s, o_w)]
                h0 = (g * GQA_GROUP + r) * HEAD_DIM
                o_ref[0, :, h0:h0 + HEAD_DIM] = (gt[:, gc:gc + 1] * blk_o[0] + gt[:, gc + 1:gc + 2] * blk_o[1]
                                                 + gt[:, gc + 2:gc + 3] * blk_o[2])


def _nsa_sample(cache_slc, page_table, qbd, comp_tm, knew, win_t, wnew, gates, past):
    db, n_pages = page_table.shape
    t_new = knew.shape[1]
    assert t_new == SUBLANES and KV_HEADS * t_new <= COLS_PER_HEAD and past % SEL_BLOCK == 0 and t_new <= SEL_BLOCK
    p_cnt = min(PAGES_PER_STEP, n_pages)
    assert n_pages % p_cnt == 0
    n_steps = n_pages // p_cnt
    ncp = comp_tm.shape[1]
    n_str = -(-(past + t_new) // CMP_STRIDE)
    n_cmp = n_str - CMP_SPAN + 1
    n_sel = -(-(past + t_new) // SEL_BLOCK)
    blocks_step = (PAGE_SIZE // SEL_BLOCK) * p_cnt
    nbp = -(-max(n_sel, n_steps * blocks_step + 1) // SUBLANES) * SUBLANES
    keys_step = PAGE_SIZE * p_cnt
    assert blocks_step <= 2 * LANES - KV_HEADS * HEAD_DIM
    assert CMP_SPAN == 2
    p_rows = max(ncp, (SEL_BLOCK // CMP_STRIDE) * nbp) + SUBLANES
    buf = win_t.shape[-1]
    assert buf % LANES == 0

    def page_spec(p):
        return pl.BlockSpec((1, KV_ROW // LANES, LANES, PAGE_SIZE),
                            lambda b, i, pt: (pt[b * n_pages + i * p_cnt + p], 0, 0, 0))

    def per_b(*shape):
        return pl.BlockSpec((1,) + shape, lambda b, i, pt: (b,) + (0,) * len(shape))

    in_specs = [page_spec(p) for p in range(p_cnt)]
    in_specs += [per_b(2 * LANES, Q_COLS), per_b(ncp, KV_ROW),
                 per_b(t_new, KV_ROW), per_b(KV_ROW // LANES, LANES, buf), per_b(t_new, KV_ROW), per_b(t_new, LANES)]
    return pl.pallas_call(
        functools.partial(_nsa_sample_kernel, n_pages_step=p_cnt, n_steps=n_steps, t_new=t_new, past=past,
                          n_cmp=n_cmp, n_sel=n_sel, k_top=min(N_SEL, n_sel)),
        grid_spec=pltpu.PrefetchScalarGridSpec(
            num_scalar_prefetch=1, grid=(db, n_steps), in_specs=in_specs,
            out_specs=per_b(t_new, NSA_WIDTH),
            scratch_shapes=[pltpu.VMEM((keys_step, 4 * LANES), BF16),
                            pltpu.VMEM((nbp, Q_COLS), F32), pltpu.VMEM((p_rows, Q_COLS), F32),
                            pltpu.VMEM((SUBLANES, Q_COLS), F32), pltpu.VMEM((SUBLANES, Q_COLS), F32),
                            pltpu.VMEM((Q_COLS, 2 * LANES), F32), pltpu.VMEM((Q_COLS, 2 * LANES), F32)]),
        out_shape=jax.ShapeDtypeStruct((db, t_new, NSA_WIDTH), F32),
        compiler_params=_params("arbitrary", "arbitrary"),
        name="nsa_sample",
    )(page_table.reshape(-1), *([cache_slc] * p_cnt), qbd, comp_tm, knew, win_t, wnew, gates)


def _out_kernel(x_ref, po_ref, zp_ref, no_ref, zn_ref, gate_ref, gnp_ref, gnn_ref, w_ref, fn_ref, y_ref):
    zp = zp_ref[0]
    zn = zn_ref[0]
    mp = _rms(po_ref[0], gnp_ref[...]) * (zp * _sigmoid(zp))
    mn = _rms(no_ref[0], gnn_ref[...]) * (zn * _sigmoid(zn))
    m = jnp.concatenate([mp, mn], axis=-1).astype(BF16)
    xo = x_ref[0] + gate_ref[0] * _dot(m, w_ref[...])
    y_ref[0] = _rms(xo, fn_ref[...])


def _out(x, pool_o, zp, nsa_o, zn, gate, gn_pool, gn_nsa, w_out_bf16, final_norm, tm):
    nb, t, d = x.shape
    r = gate.shape[1]
    assert t % tm == 0 and r in (1, t)
    if r == 1:
        gate_spec = pl.BlockSpec((1, 1, d), lambda b, i: (b, 0, 0))
    else:
        gate_spec = pl.BlockSpec((1, tm, d), lambda b, i: (b, i, 0))

    def tok(width):
        return pl.BlockSpec((1, tm, width), lambda b, i: (b, i, 0))

    def const(*shape):
        return pl.BlockSpec(shape, lambda b, i: (0,) * len(shape))

    return pl.pallas_call(
        _out_kernel,
        grid=(nb, t // tm),
        in_specs=[tok(d), tok(POOL_WIDTH), tok(POOL_WIDTH), tok(NSA_WIDTH), tok(NSA_WIDTH), gate_spec,
                  const(1, POOL_WIDTH), const(1, NSA_WIDTH), const(d, d), const(1, d)],
        out_specs=tok(d),
        out_shape=jax.ShapeDtypeStruct((nb, t, d), F32),
        compiler_params=_params("arbitrary", "arbitrary"),
        name="out",
    )(x, pool_o, zp, nsa_o, zn, gate, gn_pool.reshape(1, -1), gn_nsa.reshape(1, -1), w_out_bf16,
      final_norm.reshape(1, d))


def _pages_t(pages):
    n = pages.shape[0]
    return pages.transpose(0, 2, 3, 4, 1).reshape(n, KV_ROW // LANES, LANES, PAGE_SIZE)


def _block_diag(pool_w):
    n, c, _ = pool_w.shape
    eye = jnp.eye(n, dtype=pool_w.dtype)
    return (eye[:, None, :, None] * pool_w[:, :, None, :]).reshape(n * c, n * c)


def _first_layer(phi_w1, phi_pe):
    w = phi_w1.reshape(CMP_SPAN, CMP_STRIDE, 2, HEAD_DIM, CMP_HID)
    w1cat = w.transpose(2, 1, 3, 0, 4).reshape(2, CMP_STRIDE * HEAD_DIM, CMP_SPAN * CMP_HID)
    pe = phi_pe.reshape(CMP_SPAN, CMP_STRIDE, 2, HEAD_DIM).transpose(2, 0, 1, 3).reshape(2, CMP_SPAN, -1)
    pe2 = jnp.concatenate([pe, jnp.zeros((2, SUBLANES - CMP_SPAN, pe.shape[-1]), pe.dtype)], axis=1)
    return w1cat.astype(BF16), pe2


def kernel(x_prompt, x_sample, c_prompt, c_sample, cache_cmp_kv, cache_slc_kv, state_win_kv, state_pool, page_table, norm_w, w_ada, b_ada, w_in, pool_w, pool_scale, phi_w1, phi_pe, phi_w2, gn_pool, gn_nsa, w_out, final_norm):
    b_sz, s_len, d = x_prompt.shape
    db, t_new, _ = x_sample.shape
    depth = norm_w.shape[0]
    assert depth == 1 and w_in.shape[-1] == IN_WIDTH and d // 4 == POOL_WIDTH
    n_phys = cache_cmp_kv.shape[1]
    n_pages = page_table.shape[1]
    past = n_pages * PAGE_SIZE
    lyr = 0

    n_c = b_sz + db
    c_all = jnp.concatenate([c_prompt, c_sample, jnp.zeros((-n_c % SUBLANES, d), F32)], axis=0)
    ada = _ada(c_all, w_ada[lyr], b_ada[lyr])
    shift, scale, gate = ada[:, :d], ada[:, d:2 * d], ada[:, 2 * d:]

    w_in_t = w_in[lyr].T
    w_main = _cast_rows(w_in_t, IN_MAIN, 5 * LANES, BF16)
    w_gate = jnp.pad(w_in_t[IN_MAIN:].astype(BF16), ((0, LANES - GATE_COLS), (0, 0)))
    w_out_bf16 = w_out[lyr].astype(BF16)
    wbd = _block_diag(pool_w[lyr]).astype(BF16)
    w1cat, pe2 = _first_layer(phi_w1[lyr], phi_pe[lyr])
    w2 = phi_w2[lyr].astype(BF16)
    n_tok_s = db * t_new

    def per_token(v):
        return jnp.repeat(v[b_sz:n_c], t_new, axis=0)[None]

    tm = min(ROW_TILE, s_len)
    tq = min(ATTN_TILE, s_len)
    (u_p, zp_p, zn_p, q_t, ks_aug, kw_p, vs_t, vw_t, g_t, kvc_t, kvs_t, kvw_t) = _in_proj(
        x_prompt, scale[:b_sz, None], shift[:b_sz, None], norm_w[lyr], w_main, w_gate, tm, True)
    pool_p = _pool(jnp.zeros((b_sz, POOL_HALO, POOL_WIDTH), F32), u_p, wbd, pool_scale[lyr], 0,
                   min(POOL_TILE, s_len))
    pages_p = s_len // PAGE_SIZE
    in_order = jnp.tile(jnp.arange(pages_p, dtype=jnp.int32), (b_sz, 1))
    comp_k, comp_vt = _compress(kvc_t, in_order, w1cat, pe2, w2, None, False, True)
    nsa_p = _nsa_prompt(q_t, comp_k, comp_vt, ks_aug, vs_t, kw_p, vw_t, g_t, tq=tq, kb=tq)
    y_p = _out(x_prompt, pool_p, zp_p, nsa_p, zn_p, gate[:b_sz, None], gn_pool[lyr], gn_nsa[lyr], w_out_bf16,
               final_norm, min(OUT_TILE, s_len))

    (u_s, zp_s, zn_s, kvc_s, kvs_s, kvw_s, q_s, g_s) = _in_proj(
        x_sample.reshape(1, n_tok_s, d), per_token(scale), per_token(shift), norm_w[lyr], w_main, w_gate, n_tok_s,
        False)
    u_s3 = u_s.reshape(db, t_new, POOL_WIDTH)
    hist = jnp.concatenate([jnp.zeros((db, POOL_HALO - state_pool.shape[2], POOL_WIDTH), F32), state_pool[lyr]], axis=1)
    pool_s = _pool(hist, u_s3, wbd, pool_scale[lyr], past, t_new)
    kvc_s3 = kvc_s.reshape(db, t_new, KV_ROW)
    kvs_s3 = kvs_s.reshape(db, t_new, KV_ROW)
    kvw_s3 = kvw_s.reshape(db, t_new, KV_ROW)
    xnew = jnp.pad(kvc_s3, ((0, 0), (0, CMP_STRIDE - t_new), (0, 0)))
    xnew = xnew.reshape(db, CMP_STRIDE, N_KV_SLABS, HEAD_DIM).transpose(2, 0, 1, 3).reshape(N_KV_SLABS, db, -1)
    a1new = _new_chunk_first_layer(xnew, w1cat).transpose(1, 0, 2)
    comp_s = _compress(_pages_t(cache_cmp_kv[lyr]), page_table, w1cat, pe2, w2, a1new, True, False)
    q5 = q_s.reshape(db, t_new, KV_HEADS, GQA_GROUP, HEAD_DIM)
    eye = jnp.eye(KV_HEADS, dtype=BF16)
    qbd = q5.transpose(0, 2, 4, 3, 1)[:, :, :, :, None, :] * eye[None, :, None, None, :, None]
    qbd = jnp.pad(qbd.reshape(db, KV_HEADS * HEAD_DIM, GQA_GROUP, KV_HEADS * t_new),
                  ((0, 0), (0, 2 * LANES - KV_HEADS * HEAD_DIM), (0, 0), (0, COLS_PER_HEAD - KV_HEADS * t_new)))
    qbd = qbd.reshape(db, 2 * LANES, Q_COLS)
    g_s3 = g_s.reshape(db, t_new, LANES)
    buf = state_win_kv.shape[2]
    win_t = state_win_kv[lyr].transpose(0, 2, 3, 4, 1).reshape(db, KV_ROW // LANES, LANES, buf)
    nsa_s = _nsa_sample(_pages_t(cache_slc_kv[lyr]), page_table, qbd, comp_s, kvs_s3,
                        win_t, kvw_s3, g_s3, past)
    y_s = _out(x_sample.reshape(1, n_tok_s, d), pool_s.reshape(1, n_tok_s, POOL_WIDTH), zp_s,
               nsa_s.reshape(1, n_tok_s, NSA_WIDTH), zn_s, per_token(gate), gn_pool[lyr], gn_nsa[lyr], w_out_bf16,
               final_norm, n_tok_s).reshape(db, t_new, d)

    kv_shape = (2, KV_HEADS, HEAD_DIM)
    win_len = min(WINDOW, s_len)
    new_win_s = jnp.concatenate([state_win_kv[lyr], kvw_s3.reshape((db, t_new) + kv_shape)], axis=1)[:, -buf:]
    hist_len = state_pool.shape[2]
    new_pool_s = jnp.concatenate([state_pool[lyr], u_s3], axis=1)[:, -hist_len:]

    def rows_from_t(v_t):
        return v_t.reshape((v_t.shape[0],) + kv_shape + (v_t.shape[-1],)).transpose(0, 4, 1, 2, 3)[None]

    return (y_p, y_s,
            rows_from_t(kvc_t), kvc_s3.reshape((1, db, t_new) + kv_shape),
            rows_from_t(kvs_t), kvs_s3.reshape((1, db, t_new) + kv_shape),
            rows_from_t(kvw_t[..., s_len - win_len:]), new_win_s[None],
            u_p[:, s_len - hist_len:][None], new_pool_s[None])
```

```python
import functools

import numpy as np
import jax
import jax.numpy as jnp
from jax import lax
from jax.experimental import pallas as pl
from jax.experimental.pallas import tpu as pltpu

F32 = jnp.float32
BF16 = jnp.bfloat16

HEAD_DIM = 64
GQA_GROUP = 4
KV_HEADS = 3
N_HEADS = KV_HEADS * GQA_GROUP
N_KV_SLABS = 2 * KV_HEADS
KV_ROW = N_KV_SLABS * HEAD_DIM
N_BRANCH = 3
POOL_WINDOWS = (2, 4, 8, 16)
POOL_HALO = 16
CMP_BLOCK = 32
CMP_STRIDE = 16
CMP_SPAN = CMP_BLOCK // CMP_STRIDE
CMP_HID = 2 * HEAD_DIM
SEL_BLOCK = 64
N_SEL = 16
WINDOW = 512
PAGE_SIZE = 128
CHUNKS_PER_PAGE = PAGE_SIZE // CMP_STRIDE
CHUNK_PITCH = 24
EPS = 1e-6
NEG = -1e30
FORCE = 1e4
LOG2_E = 1.4426950408889634
VT_ROWS = 80
PAD_SCORE = -1e38
TAKEN_SCORE = -3e38

LANES = 128
SUBLANES = 8
VMEM_LIMIT_BYTES = 56 * 1024 * 1024

ROW_TILE = 512
OUT_TILE = 1024
POOL_TILE = 4096
ATTN_TILE = 1024
PAGES_PER_STEP = 32
BAND_SUB = 256


def _dot(a, b):
    return jnp.dot(a, b, preferred_element_type=F32)


def _dot_nt(a, b):
    return lax.dot_general(a, b, (((1,), (1,)), ((), ())), preferred_element_type=F32)


def _dot_tn(a, b):
    return lax.dot_general(a, b, (((0,), (0,)), ((), ())), preferred_element_type=F32)


def _sigmoid(v):
    return 1.0 / (1.0 + jnp.exp(-v))


def _rms(v, g):
    return v * lax.rsqrt(jnp.mean(v * v, axis=-1, keepdims=True) + EPS) * g


def _split3(v):
    hi = v.astype(BF16)
    r1 = v - hi.astype(F32)
    mid = r1.astype(BF16)
    lo = (r1 - mid.astype(F32)).astype(BF16)
    return hi, mid, lo


def _params(*sem):
    return pltpu.CompilerParams(dimension_semantics=sem, vmem_limit_bytes=VMEM_LIMIT_BYTES)


def _ada_kernel(c_ref, w_ref, b_ref, o_ref):
    c = c_ref[...]
    a = (c * _sigmoid(c)).astype(BF16)
    o_ref[...] = _dot(a, w_ref[...].astype(BF16)) + b_ref[...]


def _ada(c, w_ada, b_ada):
    m, d = c.shape
    n = w_ada.shape[1]
    tn = 512
    return pl.pallas_call(
        _ada_kernel,
        grid=(n // tn,),
        in_specs=[pl.BlockSpec((m, d), lambda j: (0, 0)),
                  pl.BlockSpec((d, tn), lambda j: (0, j)),
                  pl.BlockSpec((1, tn), lambda j: (0, j))],
        out_specs=pl.BlockSpec((m, tn), lambda j: (0, j)),
        out_shape=jax.ShapeDtypeStruct((m, n), F32),
        compiler_params=_params("arbitrary"),
        name="ada",
    )(c, w_ada, b_ada.reshape(1, n))


POOL_WIDTH = 256
NSA_WIDTH = N_HEADS * HEAD_DIM
GATE_COLS = N_BRANCH * N_HEADS
MAX_SEL_BLOCKS = LANES - HEAD_DIM
_SEG = {}
_off = 0
for _name, _w in (("u", POOL_WIDTH), ("zp", POOL_WIDTH), ("q", NSA_WIDTH), ("zn", NSA_WIDTH),
                  ("kvc", KV_ROW), ("kvs", KV_ROW), ("kvw", KV_ROW)):
    _SEG[_name] = (_off, _w)
    _off += _w
IN_MAIN = _off
IN_WIDTH = IN_MAIN + GATE_COLS


GATE_ROWS = 16


def _in_proj_kernel(x_ref, sc_ref, sh_ref, nw_ref, w_ref, wg_ref, u_ref, zp_ref, zn_ref, *rest, attn_layouts):
    tm = x_ref.shape[1]
    h = (_rms(x_ref[0], nw_ref[...]) * (1.0 + sc_ref[0]) + sh_ref[0]).astype(BF16)

    def seg(name):
        s, w = _SEG[name]
        return _dot_nt(h, w_ref[s:s + w, :])

    u_ref[0] = seg("u")
    zp_ref[0] = seg("zp")
    zn_ref[0] = seg("zn")
    q = seg("q") * (HEAD_DIM ** -0.5 * LOG2_E)
    kvc = seg("kvc")
    kvs = seg("kvs")
    kvw = seg("kvw")
    sig = _sigmoid(_dot_nt(h, wg_ref[...]))
    if not attn_layouts:
        kvc_ref, kvs_ref, kvw_ref, q_ref, g_ref = rest
        kvc_ref[0] = kvc
        kvs_ref[0] = kvs
        kvw_ref[0] = kvw
        q_ref[0] = q.astype(BF16)
        g_ref[0] = sig
        return

    qt_ref, ks_aug_ref, kw_ref, vst_ref, vwt_ref, gt_ref, kvct_ref, kvst_ref, kvwt_ref = rest
    qt_ref[0] = q.T.astype(BF16).reshape(N_HEADS, HEAD_DIM, tm)
    kvs_t32 = kvs.T
    kvw_t32 = kvw.T
    kvct_ref[0] = kvc.T.reshape(KV_ROW // LANES, LANES, tm)
    kvst_ref[0] = kvs_t32.reshape(KV_ROW // LANES, LANES, tm)
    kvwt_ref[0] = kvw_t32.reshape(KV_ROW // LANES, LANES, tm)
    pos = pl.program_id(1) * tm + lax.broadcasted_iota(jnp.int32, (tm, 1), 0)
    onehot = (lax.broadcasted_iota(jnp.int32, (1, MAX_SEL_BLOCKS), 1) == pos // SEL_BLOCK).astype(BF16)
    kvs_t = kvs_t32.astype(BF16)
    kvw_t = kvw_t32.astype(BF16)
    v0 = KV_HEADS * HEAD_DIM
    tail = (lax.broadcasted_iota(jnp.int32, (VT_ROWS - HEAD_DIM, LANES), 0) == 0).astype(BF16)
    for g in range(KV_HEADS):
        ks_aug_ref[0, g] = jnp.concatenate([kvs[:, g * HEAD_DIM:(g + 1) * HEAD_DIM].astype(BF16), onehot], axis=1)
        kw_ref[0, g] = kvw[:, g * HEAD_DIM:(g + 1) * HEAD_DIM].astype(BF16)
        for c in range(tm // LANES):
            sl = (slice(v0 + g * HEAD_DIM, v0 + (g + 1) * HEAD_DIM), slice(c * LANES, (c + 1) * LANES))
            vst_ref[0, g, c] = jnp.concatenate([kvs_t[sl], tail], axis=0)
            vwt_ref[0, g, c] = jnp.concatenate([kvw_t[sl], tail], axis=0)
        per_group = GQA_GROUP * N_BRANCH
        rolled = sig if g == 0 else pltpu.roll(sig, LANES - per_group * g, 1)
        gt_ref[0, g] = rolled.T[:GATE_ROWS, :]


def _cast_kernel(w_ref, o_ref):
    o_ref[...] = w_ref[...].astype(o_ref.dtype)


def _cast_rows(w, n_rows, tile, dtype):
    cols = w.shape[1]
    assert n_rows % tile == 0 and tile % SUBLANES == 0 and n_rows <= w.shape[0] and cols % LANES == 0
    return pl.pallas_call(
        _cast_kernel,
        grid=(n_rows // tile,),
        in_specs=[pl.BlockSpec((tile, cols), lambda j: (j, 0))],
        out_specs=pl.BlockSpec((tile, cols), lambda j: (j, 0)),
        out_shape=jax.ShapeDtypeStruct((n_rows, cols), dtype),
        compiler_params=_params("arbitrary"),
        name="cast_rows",
    )(w)


def _in_proj(x, scale, shift, norm_w, w_main, w_gate, tm, attn_layouts):
    nb, t, d = x.shape
    r = scale.shape[1]
    assert t % tm == 0 and r in (1, t) and (tm % LANES == 0 or not attn_layouts)
    if r == 1:
        mod_spec = pl.BlockSpec((1, 1, d), lambda b, i: (b, 0, 0))
    else:
        mod_spec = pl.BlockSpec((1, tm, d), lambda b, i: (b, i, 0))

    def tok(width):
        return pl.BlockSpec((1, tm, width), lambda b, i: (b, i, 0))

    def hm(n, width=HEAD_DIM):
        return pl.BlockSpec((1, n, tm, width), lambda b, i: (b, 0, i, 0))

    def sd(*shape, dtype=F32):
        return jax.ShapeDtypeStruct(shape, dtype)

    out_specs = [tok(POOL_WIDTH), tok(POOL_WIDTH), tok(NSA_WIDTH)]
    out_shape = [sd(nb, t, POOL_WIDTH), sd(nb, t, POOL_WIDTH), sd(nb, t, NSA_WIDTH)]
    if attn_layouts:
        lane_tiles = tm // LANES
        vt_spec = pl.BlockSpec((1, KV_HEADS, lane_tiles, VT_ROWS, LANES), lambda b, i: (b, 0, i, 0, 0))
        vt_shape = sd(nb, KV_HEADS, t // LANES, VT_ROWS, LANES, dtype=BF16)
        rows_t_spec = pl.BlockSpec((1, KV_ROW // LANES, LANES, tm), lambda b, i: (b, 0, 0, i))
        rows_t_shape = sd(nb, KV_ROW // LANES, LANES, t)
        out_specs += [pl.BlockSpec((1, N_HEADS, HEAD_DIM, tm), lambda b, i: (b, 0, 0, i)),
                      hm(KV_HEADS, HEAD_DIM + MAX_SEL_BLOCKS), hm(KV_HEADS), vt_spec, vt_spec,
                      pl.BlockSpec((1, KV_HEADS, GATE_ROWS, tm), lambda b, i: (b, 0, 0, i)),
                      rows_t_spec, rows_t_spec, rows_t_spec]
        out_shape += [sd(nb, N_HEADS, HEAD_DIM, t, dtype=BF16),
                      sd(nb, KV_HEADS, t, HEAD_DIM + MAX_SEL_BLOCKS, dtype=BF16),
                      sd(nb, KV_HEADS, t, HEAD_DIM, dtype=BF16), vt_shape, vt_shape,
                      sd(nb, KV_HEADS, GATE_ROWS, t), rows_t_shape, rows_t_shape, rows_t_shape]
    else:
        out_specs += [tok(KV_ROW), tok(KV_ROW), tok(KV_ROW), tok(NSA_WIDTH), tok(LANES)]
        out_shape += [sd(nb, t, KV_ROW), sd(nb, t, KV_ROW), sd(nb, t, KV_ROW),
                      sd(nb, t, NSA_WIDTH, dtype=BF16), sd(nb, t, LANES)]
    return pl.pallas_call(
        functools.partial(_in_proj_kernel, attn_layouts=attn_layouts),
        grid=(nb, t // tm),
        in_specs=[tok(d), mod_spec, mod_spec,
                  pl.BlockSpec((1, d), lambda b, i: (0, 0)),
                  pl.BlockSpec((IN_MAIN, d), lambda b, i: (0, 0)),
                  pl.BlockSpec((LANES, d), lambda b, i: (0, 0))],
        out_specs=out_specs,
        out_shape=out_shape,
        compiler_params=_params("arbitrary", "arbitrary"),
        name="in_proj",
    )(x, scale, shift, norm_w.reshape(1, d), w_main, w_gate)


def _pool_kernel(hist_ref, uprev_ref, u_ref, wbd_ref, ps_ref, o_ref, *, pos0, tp):
    i = pl.program_id(1)
    c = u_ref.shape[-1]
    grp = lax.broadcasted_iota(jnp.int32, (1, c), 1) // (c // len(POOL_WINDOWS))
    pos = pos0 + i * tp + lax.broadcasted_iota(jnp.int32, (tp, 1), 0)
    win = jnp.full((1, c), float(POOL_WINDOWS[-1]), F32)
    for gi in range(len(POOL_WINDOWS) - 2, -1, -1):
        win = jnp.where(grp == gi, float(POOL_WINDOWS[gi]), win)
    cnt = jnp.minimum(win, (pos + 1).astype(F32))
    for bi in range(u_ref.shape[0]):
        u = u_ref[bi]
        halo = jnp.where(i == 0, hist_ref[bi], uprev_ref[bi])
        sums = []
        s = jnp.concatenate([halo, u], axis=0)
        for w in POOL_WINDOWS:
            s = s + pltpu.roll(s, w // 2, 0)
            sums.append(s[POOL_HALO:])
        tot = sums[-1]
        for gi in range(len(POOL_WINDOWS) - 2, -1, -1):
            tot = jnp.where(grp == gi, sums[gi], tot)
        dlt = tot / cnt - u
        o_ref[bi] = _dot(dlt.astype(BF16), wbd_ref[...]) * ps_ref[...]


def _pool(hist, u, wbd, pool_scale, pos0, tp):
    nb, t, c = u.shape
    assert t % tp == 0
    if t >= POOL_HALO:
        assert tp % POOL_HALO == 0
        uprev, ratio, bb = u, tp // POOL_HALO, 1
        prev_spec = pl.BlockSpec((bb, POOL_HALO, c), lambda b, i: (b, jnp.maximum(i * ratio - 1, 0), 0))
    else:
        assert t == tp
        uprev, bb = hist, nb
        prev_spec = pl.BlockSpec((bb, POOL_HALO, c), lambda b, i: (b, 0, 0))
    return pl.pallas_call(
        functools.partial(_pool_kernel, pos0=pos0, tp=tp),
        grid=(nb // bb, t // tp),
        in_specs=[pl.BlockSpec((bb, POOL_HALO, c), lambda b, i: (b, 0, 0)), prev_spec,
                  pl.BlockSpec((bb, tp, c), lambda b, i: (b, i, 0)),
                  pl.BlockSpec((c, c), lambda b, i: (0, 0)),
                  pl.BlockSpec((1, c), lambda b, i: (0, 0))],
        out_specs=pl.BlockSpec((bb, tp, c), lambda b, i: (b, i, 0)),
        out_shape=jax.ShapeDtypeStruct((nb, t, c), F32),
        compiler_params=_params("arbitrary", "arbitrary"),
        name="pool",
    )(hist, uprev, u, wbd, pool_scale.reshape(1, c))


def _gelu_tanh(v):
    return 0.5 * v * (1.0 + jnp.tanh(np.sqrt(2.0 / np.pi).astype(np.float32) * (v + 0.044715 * (v * v * v))))


def _compress_kernel(pt_ref, *refs, n_pages_step, n_steps, has_new, token_major):
    del pt_ref
    p_cnt = n_pages_step
    page_refs = refs[:p_cnt]
    w1p_ref, w1_ref, pe_ref, w2_ref = refs[p_cnt:p_cnt + 4]
    k = p_cnt + 4
    a1new_ref = None
    if has_new:
        a1new_ref = refs[k]
        k += 1
    if token_major:
        out_ref = refs[k]
        k += 1
    else:
        outk_ref, outvt_ref = refs[k:k + 2]
        k += 2
    slab_scr, carry_scr, bias_scr = refs[k:k + 3]
    i = pl.program_id(1)
    rows = CHUNKS_PER_PAGE * p_cnt
    row_id = lax.broadcasted_iota(jnp.int32, (rows, 1), 0)

    @pl.when(i == 0)
    def _():
        carry_scr[...] = jnp.zeros_like(carry_scr)
        for kk in range(2):
            pb = _dot(pe_ref[kk].astype(BF16), w1_ref[kk])
            bias_scr[kk] = pb[0:1, :CMP_HID] + pb[1:2, CMP_HID:]

    def store(kg, comp):
        if token_major:
            out_ref[0, :, kg * HEAD_DIM:(kg + 1) * HEAD_DIM] = comp.astype(BF16)
        elif kg < KV_HEADS:
            outk_ref[0, kg] = comp.astype(BF16)
        else:
            wide = jnp.concatenate([comp, jnp.zeros((rows, LANES - HEAD_DIM), F32)], axis=1)
            outvt_ref[0, kg - KV_HEADS] = wide.T[:HEAD_DIM, :].astype(BF16)

    def finish(kg, a0_prev, a1):
        kk = kg // KV_HEADS
        pre = a0_prev + a1 + bias_scr[kk]
        return _dot(_gelu_tanh(pre).astype(BF16), w2_ref[kk])

    def main():
        for cb in range(KV_ROW // LANES):
            for p in range(p_cnt):
                rows_t = page_refs[p][0, cb].astype(BF16).T.astype(F32)
                for ch in range(CHUNKS_PER_PAGE):
                    r0 = (p * CHUNKS_PER_PAGE + ch) * CHUNK_PITCH
                    slab_scr[cb, r0:r0 + CMP_STRIDE, :] = rows_t[ch * CMP_STRIDE:(ch + 1) * CMP_STRIDE, :]
            x = jnp.concatenate([slab_scr[cb, pl.ds(j, rows, stride=CHUNK_PITCH), :] for j in range(CMP_STRIDE)],
                                axis=1).astype(BF16)
            a_both = _dot(x, w1p_ref[cb])
            for half in range(2):
                kg = 2 * cb + half
                a = a_both[:, half * 2 * CMP_HID:(half + 1) * 2 * CMP_HID]
                a0 = a[:, :CMP_HID]
                a0_prev = jnp.where(row_id == 0, carry_scr[kg], pltpu.roll(a0, 1, 0))
                carry_scr[kg] = a0[rows - 1:rows, :]
                store(kg, finish(kg, a0_prev, a[:, CMP_HID:]))

    if not has_new:
        main()
    else:
        pl.when(i < n_steps)(main)

        @pl.when(i == n_steps)
        def _():
            for kg in range(N_KV_SLABS):
                comp = finish(kg, carry_scr[kg], a1new_ref[0, kg:kg + 1, :])
                store(kg, jnp.where(row_id == 0, jnp.broadcast_to(comp, (rows, HEAD_DIM)), 0.0))


def _new_chunk_kernel(x_ref, w1_ref, o_ref):
    for kg in range(N_KV_SLABS):
        o_ref[kg] = _dot(x_ref[kg].astype(BF16), w1_ref[kg // KV_HEADS])[:, CMP_HID:]


def _new_chunk_first_layer(xnew, w1cat):
    n_slab, nb, xw = xnew.shape
    return pl.pallas_call(
        _new_chunk_kernel,
        out_shape=jax.ShapeDtypeStruct((n_slab, nb, CMP_HID), F32),
        compiler_params=pltpu.CompilerParams(vmem_limit_bytes=VMEM_LIMIT_BYTES),
        name="new_chunk",
    )(xnew, w1cat)


def _compress(pages, page_table, w1cat, pe2, w2, a1new, token_major, per_batch):
    nb, n_pages = page_table.shape
    p_cnt = min(PAGES_PER_STEP, n_pages)
    assert n_pages % p_cnt == 0
    n_steps = n_pages // p_cnt
    has_new = a1new is not None
    rows = CHUNKS_PER_PAGE * p_cnt
    grid_steps = n_steps + (1 if has_new else 0)
    tot_rows = rows * grid_steps
    xw = CMP_STRIDE * HEAD_DIM

    def page_spec(p):
        def imap(b, i, pt):
            step = jnp.minimum(i, n_steps - 1)
            page = pt[b * n_pages + step * p_cnt + p]
            return (b, 0, 0, page) if per_batch else (page, 0, 0, 0)
        return pl.BlockSpec((1, KV_ROW // LANES, LANES, PAGE_SIZE), imap)

    n_cb = KV_ROW // LANES
    w4 = w1cat.reshape(2, CMP_STRIDE, HEAD_DIM, 2 * CMP_HID)
    zero = jnp.zeros_like(w4[0])
    w1p = jnp.stack([jnp.concatenate([jnp.concatenate([w4[(2 * cb) // KV_HEADS], zero], axis=2),
                                      jnp.concatenate([zero, w4[(2 * cb + 1) // KV_HEADS]], axis=2)], axis=1)
                     for cb in range(n_cb)]).reshape(n_cb, CMP_STRIDE * LANES, 4 * CMP_HID)
    in_specs = [page_spec(p) for p in range(p_cnt)]
    in_specs += [pl.BlockSpec((n_cb, CMP_STRIDE * LANES, 4 * CMP_HID), lambda b, i, pt: (0, 0, 0)),
                 pl.BlockSpec((2, xw, 2 * CMP_HID), lambda b, i, pt: (0, 0, 0)),
                 pl.BlockSpec((2, SUBLANES, xw), lambda b, i, pt: (0, 0, 0)),
                 pl.BlockSpec((2, CMP_HID, HEAD_DIM), lambda b, i, pt: (0, 0, 0))]
    args = [pages] * p_cnt + [w1p, w1cat, pe2, w2]
    if has_new:
        in_specs.append(pl.BlockSpec((1, N_KV_SLABS, CMP_HID), lambda b, i, pt: (b, 0, 0)))
        args.append(a1new)
    if token_major:
        out_spec = pl.BlockSpec((1, rows, KV_ROW), lambda b, i, pt: (b, i, 0))
        out_shape = jax.ShapeDtypeStruct((nb, tot_rows, KV_ROW), BF16)
    else:
        assert rows % LANES == 0
        out_spec = [pl.BlockSpec((1, KV_HEADS, rows, HEAD_DIM), lambda b, i, pt: (b, 0, i, 0)),
                    pl.BlockSpec((1, KV_HEADS, HEAD_DIM, rows), lambda b, i, pt: (b, 0, 0, i))]
        out_shape = [jax.ShapeDtypeStruct((nb, KV_HEADS, tot_rows, HEAD_DIM), BF16),
                     jax.ShapeDtypeStruct((nb, KV_HEADS, HEAD_DIM, tot_rows), BF16)]
    return pl.pallas_call(
        functools.partial(_compress_kernel, n_pages_step=p_cnt, n_steps=n_steps, has_new=has_new,
                          token_major=token_major),
        grid_spec=pltpu.PrefetchScalarGridSpec(
            num_scalar_prefetch=1, grid=(nb, grid_steps), in_specs=in_specs, out_specs=out_spec,
            scratch_shapes=[pltpu.VMEM((n_cb, rows * CHUNK_PITCH, LANES), F32),
                            pltpu.VMEM((N_KV_SLABS, 1, CMP_HID), F32),
                            pltpu.VMEM((2, 1, CMP_HID), F32)]),
        out_shape=out_shape,
        compiler_params=_params("arbitrary", "arbitrary"),
        name="compress_new" if has_new else "compress",
    )(page_table.reshape(-1), *args)


def _softmax_parts(s, valid, axis):
    s = jnp.where(valid, s, NEG)
    m = jnp.max(s, axis=axis, keepdims=True)
    e = jnp.where(valid, jnp.exp2(s - m), 0.0)
    return e, jnp.sum(e, axis=axis, keepdims=True)


def _nsa_prompt_kernel(qt_ref, kc_ref, vct_ref, ks_ref, vst_ref, kw_ref, vwt_ref, gt_ref, mt_ref, o_ref, *,
                       tq, kb, n_cmp, k_top):
    qi = pl.program_id(2)
    qs = qi * tq
    r4 = GQA_GROUP
    cols = r4 * tq
    qt = jnp.concatenate([qt_ref[0, r] for r in range(r4)], axis=1)
    tpos = qs + lax.broadcasted_iota(jnp.int32, (1, tq), 1)

    def per_head(v):
        return jnp.concatenate([v] * r4, axis=1)

    def value_tiles(ref, k0, width):
        j0 = k0 // LANES
        return jnp.concatenate([ref[0, 0, j0 + j] for j in range(width // LANES)], axis=1)

    def normalise(acc):
        return acc[:HEAD_DIM] * (1.0 / acc[HEAD_DIM:HEAD_DIM + 1])

    nc = kc_ref.shape[2]
    s = _dot(kc_ref[0, 0], qt)
    cid = lax.broadcasted_iota(jnp.int32, (nc, 1), 0)
    valid = (cid >= 1) & (cid <= n_cmp) & (cid * CMP_STRIDE + (CMP_BLOCK - CMP_STRIDE - 1) <= tpos)
    s = s + per_head(jnp.where(valid, 0.0, NEG))
    e = jnp.exp2(s - jnp.max(s, axis=0, keepdims=True))
    den = jnp.sum(e, axis=0, keepdims=True)
    has_key = per_head(jnp.where(tpos >= CMP_BLOCK - 1, 1.0, 0.0)) if n_cmp >= 1 else 0.0
    p = e * (has_key / jnp.maximum(den, 1e-30))
    o_c = _dot(vct_ref[0, 0], p.astype(BF16))
    pg = p[:, 0:tq]
    for r in range(1, r4):
        pg = pg + p[:, r * tq:(r + 1) * tq]

    mt = mt_ref[...]
    imp = sum(_dot(mt, piece) for piece in _split3(pg))
    n_blk = imp.shape[0]
    blk = lax.broadcasted_iota(jnp.int32, (n_blk, 1), 0)
    cur = (qs + lax.broadcasted_iota(jnp.int32, (1, tq), 1)) // SEL_BLOCK
    score = jnp.where(blk <= cur, imp, -1.0)
    score = jnp.where((blk == 0) | (blk == cur) | (blk == cur - 1), FORCE, score)
    sub = lax.broadcasted_iota(jnp.int32, (SUBLANES, 1), 0)
    tiles = [score[v * SUBLANES:(v + 1) * SUBLANES, :] for v in range(n_blk // SUBLANES)]
    ranks = [jnp.zeros((SUBLANES, tq), F32) for _ in tiles]
    for j in range(n_blk):
        sj = score[j:j + 1, :]
        for v, sc in enumerate(tiles):
            if v > j // SUBLANES:
                ahead = sj >= sc
            elif v < j // SUBLANES:
                ahead = sj > sc
            else:
                ahead = (sj > sc) | ((sj == sc) & (sub > j % SUBLANES))
            ranks[v] = ranks[v] + jnp.where(ahead, 1.0, 0.0)
    rank = jnp.concatenate(ranks, axis=0)
    sel_bias = jnp.where((rank < float(k_top)) & (blk <= cur), 0.0, NEG).astype(BF16)
    q_aug = jnp.concatenate([qt, jnp.concatenate([sel_bias] * r4, axis=1)], axis=0)

    def scores(k0):
        return _dot(ks_ref[0, 0, pl.ds(k0, kb), :], q_aug)

    def absorb(sk, values, state):
        m_run, acc = state
        m_new = jnp.maximum(m_run, jnp.max(sk, axis=0, keepdims=True))
        pv = _dot(values, jnp.exp2(sk - m_new).astype(BF16))
        return m_new, jnp.exp2(m_run - m_new) * acc + pv

    def tile(kt, state):
        k0 = pl.multiple_of(kt * kb, kb)
        return absorb(scores(k0), value_tiles(vst_ref, k0, kb), state)

    init = (jnp.full((1, cols), NEG, F32), jnp.zeros((VT_ROWS, cols), F32))
    m_run, acc = lax.fori_loop(0, qi, tile, init)

    sub = min(BAND_SUB, tq)
    n_sub = tq // sub

    def sub_cols(v, j):
        return jnp.concatenate([v[:, r * tq + j * sub:r * tq + (j + 1) * sub] for r in range(r4)], axis=1)

    def join_subs(parts):
        return jnp.concatenate([parts[j][:, r * sub:(r + 1) * sub] for r in range(r4) for j in range(n_sub)],
                               axis=1)

    wl = min(kw_ref.shape[2], WINDOW + sub)
    sel_parts, win_parts = [], []
    for j in range(n_sub):
        tpos_j = tpos[:, j * sub:(j + 1) * sub]
        heads_j = lambda v: jnp.concatenate([v] * r4, axis=1)
        nk = (j + 1) * sub
        kpos = qs + lax.broadcasted_iota(jnp.int32, (nk, 1), 0)
        sk = _dot(ks_ref[0, 0, pl.ds(pl.multiple_of(qs, tq), nk), :], sub_cols(q_aug, j))
        sk = sk + heads_j(jnp.where(kpos <= tpos_j, 0.0, NEG))
        state_j = absorb(sk, value_tiles(vst_ref, qs, nk), (sub_cols(m_run, j), sub_cols(acc, j)))
        sel_parts.append(normalise(state_j[1]))
        w0 = pl.multiple_of(jnp.maximum(qs + (j + 1) * sub - wl, 0), LANES)
        dq = tpos_j - (w0 + lax.broadcasted_iota(jnp.int32, (wl, 1), 0))
        sw = _dot(kw_ref[0, 0, pl.ds(w0, wl), :], sub_cols(qt, j))
        sw = sw + heads_j(jnp.where((dq >= 0) & (dq < WINDOW), 0.0, NEG))
        ew = jnp.exp2(sw - jnp.max(sw, axis=0, keepdims=True))
        win_parts.append(normalise(_dot(value_tiles(vwt_ref, w0, wl), ew.astype(BF16))))
    o_s = join_subs(sel_parts)
    o_w = join_subs(win_parts)

    gt = gt_ref[0, 0]
    heads = []
    for r in range(r4):
        c0 = r * N_BRANCH
        sl = slice(r * tq, (r + 1) * tq)
        heads.append(gt[c0:c0 + 1] * o_c[:, sl] + gt[c0 + 1:c0 + 2] * o_s[:, sl] + gt[c0 + 2:c0 + 3] * o_w[:, sl])
    o_ref[0] = jnp.concatenate(heads, axis=0).T


def _importance_matrix(n_blk_pad, n_col):
    mt = np.zeros((n_blk_pad, n_col), np.float32)
    ratio = SEL_BLOCK // CMP_STRIDE
    for b in range(n_blk_pad):
        for c, w in zip(range(ratio * b, ratio * b + ratio + 1), (1.0,) + (2.0,) * (ratio - 1) + (1.0,)):
            if c < n_col:
                mt[b, c] = w
    return mt


def _nsa_prompt(q_t, comp_k, comp_vt, ks_aug, vs_t, kw, vw_t, gates_t, tq, kb):
    b_sz, _, _, s_len = q_t.shape
    nc = comp_k.shape[2]
    assert s_len % SEL_BLOCK == 0 and s_len % kb == 0 and kb % tq == 0 and s_len % tq == 0 and CMP_SPAN == 2
    assert tq % LANES == 0 and kb == tq
    n_blk = s_len // SEL_BLOCK
    assert n_blk <= MAX_SEL_BLOCKS
    n_cmp = s_len // CMP_STRIDE - CMP_SPAN + 1
    mt = jnp.asarray(_importance_matrix(MAX_SEL_BLOCKS, nc), BF16)

    def per_group(*shape):
        return pl.BlockSpec((1, 1) + shape, lambda b, g, i: (b, g) + (0,) * len(shape))

    return pl.pallas_call(
        functools.partial(_nsa_prompt_kernel, tq=tq, kb=kb, n_cmp=n_cmp, k_top=min(N_SEL, n_blk)),
        grid=(b_sz, KV_HEADS, s_len // tq),
        in_specs=[pl.BlockSpec((1, GQA_GROUP, HEAD_DIM, tq), lambda b, g, i: (b, g, 0, i)),
                  per_group(nc, HEAD_DIM), per_group(HEAD_DIM, nc),
                  per_group(s_len, HEAD_DIM + MAX_SEL_BLOCKS), per_group(s_len // LANES, VT_ROWS, LANES),
                  per_group(s_len, HEAD_DIM), per_group(s_len // LANES, VT_ROWS, LANES),
                  pl.BlockSpec((1, 1, GATE_ROWS, tq), lambda b, g, i: (b, g, 0, i)),
                  pl.BlockSpec((MAX_SEL_BLOCKS, nc), lambda b, g, i: (0, 0))],
        out_specs=pl.BlockSpec((1, tq, GQA_GROUP * HEAD_DIM), lambda b, g, i: (b, i, g)),
        out_shape=jax.ShapeDtypeStruct((b_sz, s_len, NSA_WIDTH), F32),
        compiler_params=_params("arbitrary", "arbitrary", "arbitrary"),
        name="nsa_prompt",
    )(q_t, comp_k, comp_vt, ks_aug, vs_t, kw, vw_t, gates_t, mt)


Q_COLS = LANES
COLS_PER_HEAD = Q_COLS // GQA_GROUP
V_OFF = LANES
V_COL0 = KV_HEADS * HEAD_DIM - V_OFF

def _nsa_sample_kernel(pt_ref, *refs, n_pages_step, n_steps, t_new, past, n_cmp, n_sel, k_top):
    del pt_ref
    p_cnt = n_pages_step
    page_refs = refs[:p_cnt]
    (qbd_ref, comp_ref, knew_ref, win_ref, wnew_ref, g_ref,
     o_ref, kv_scr, bias_scr, p_scr, m_scr, l_scr, acc_scr, oc_scr) = refs[p_cnt:]
    i = pl.program_id(1)
    qbd = qbd_ref[0]
    col = lax.broadcasted_iota(jnp.int32, (1, Q_COLS), 1)
    tok = col & (SUBLANES - 1)
    qpos = past + tok
    kwin = slice(0, 2 * LANES)
    vwin = slice(V_OFF, V_OFF + 2 * LANES)
    k_lanes = KV_HEADS * HEAD_DIM
    blocks_per_page = PAGE_SIZE // SEL_BLOCK
    nb_step = blocks_per_page * p_cnt
    keys_step = PAGE_SIZE * p_cnt

    def attend(rows_bf16, valid):
        s = _dot(rows_bf16[:, kwin], qbd)
        return jnp.where(valid, s, NEG)

    @pl.when((pl.program_id(0) == 0) & (i == 0))
    def _():
        key_blk = lax.broadcasted_iota(jnp.int32, (keys_step, 1), 0) // SEL_BLOCK
        lane_blk = lax.broadcasted_iota(jnp.int32, (1, 2 * LANES - k_lanes), 1)
        kv_scr[:, k_lanes:2 * LANES] = (key_blk == lane_blk).astype(BF16)

    @pl.when(i == 0)
    def _():
        comp = comp_ref[0]
        ncp = comp.shape[0]
        cid = lax.broadcasted_iota(jnp.int32, (ncp, 1), 0)
        valid = (cid >= 1) & (cid <= n_cmp) & (cid * CMP_STRIDE + (CMP_BLOCK - CMP_STRIDE - 1) <= qpos)
        s = attend(comp, valid)
        e, den = _softmax_parts(s, valid, 0)
        p = e * (1.0 / jnp.maximum(den, 1e-30))
        oc_scr[...] = _dot_tn(p.astype(BF16), comp[:, vwin])
        nbp = bias_scr.shape[0]
        ratio = SEL_BLOCK // CMP_STRIDE
        p_scr[0:ncp, :] = p
        p_scr[ncp:, :] = jnp.zeros((p_scr.shape[0] - ncp, Q_COLS), F32)
        taps = [p_scr[pl.ds(k, nbp, stride=ratio), :] for k in range(ratio + 1)]
        imp = taps[0] + taps[ratio]
        for k in range(1, ratio):
            imp = imp + 2.0 * taps[k]
        imp = imp + pltpu.roll(imp, COLS_PER_HEAD, 1)
        imp = imp + pltpu.roll(imp, 2 * COLS_PER_HEAD, 1)
        blk = lax.broadcasted_iota(jnp.int32, (nbp, 1), 0)
        cur = qpos // SEL_BLOCK
        score = jnp.where(blk <= cur, imp, -1.0)
        score = jnp.where((blk == 0) | (blk == cur) | (blk == cur - 1), FORCE, score)
        score = jnp.where(blk < n_sel, score, PAD_SCORE)
        blk_f = blk.astype(F32)
        bias = jnp.full((nbp, Q_COLS), NEG, F32)
        for _ in range(k_top):
            best = jnp.max(score, axis=0, keepdims=True)
            first = jnp.min(jnp.where(score == best, blk_f, float(nbp)), axis=0, keepdims=True)
            hit = blk_f == first
            bias = jnp.where(hit, 0.0, bias)
            score = jnp.where(hit, TAKEN_SCORE, score)
        bias_scr[...] = bias
        m_scr[...] = jnp.full_like(m_scr, NEG)
        l_scr[...] = jnp.zeros_like(l_scr)
        acc_scr[...] = jnp.zeros_like(acc_scr)

    def as_col(v):
        return jnp.broadcast_to(v, (SUBLANES, Q_COLS)).T[:, 0:1]

    def partial_softmax(s, v_rows):
        m_g = jnp.max(s, axis=0, keepdims=True)
        e = jnp.exp2(s - m_g)
        return m_g, jnp.sum(e, axis=0, keepdims=True), _dot_tn(e.astype(BF16), v_rows)

    def accumulate(*parts):
        m_run = m_scr[0:1, :]
        m_new = m_run
        for m_g, _, _ in parts:
            m_new = jnp.maximum(m_new, m_g)
        w_run = jnp.exp2(m_run - m_new)
        l_new = w_run * l_scr[0:1, :]
        acc_new = as_col(w_run) * acc_scr[...]
        for m_g, l_g, acc_g in parts:
            w_g = jnp.exp2(m_g - m_new)
            l_new = l_new + w_g * l_g
            acc_new = acc_new + as_col(w_g) * acc_g
        m_scr[0:1, :] = m_new
        l_scr[0:1, :] = l_new
        acc_scr[...] = acc_new

    bias_step = bias_scr[pl.ds(pl.multiple_of(i * nb_step, nb_step), nb_step), :].astype(BF16)
    pad = 2 * LANES - k_lanes - nb_step
    q_parts = [qbd[:k_lanes], bias_step] + ([jnp.zeros((pad, Q_COLS), BF16)] if pad else [])
    q_step = jnp.concatenate(q_parts, axis=0)
    for p in range(p_cnt):
        rows_p = slice(p * PAGE_SIZE, (p + 1) * PAGE_SIZE)
        tiles = [page_refs[p][0, cb].astype(BF16).T for cb in range(KV_ROW // LANES)]
        kv_scr[rows_p, 0:LANES] = tiles[0]
        kv_scr[rows_p, LANES:k_lanes] = tiles[1][:, :k_lanes - LANES]
        kv_scr[rows_p, 2 * LANES:3 * LANES] = tiles[1]
        kv_scr[rows_p, 3 * LANES:4 * LANES] = tiles[2]
    accumulate(partial_softmax(_dot(kv_scr[:, 0:2 * LANES], q_step), kv_scr[:, 2 * LANES:4 * LANES]))

    @pl.when(i == n_steps - 1)
    def _():
        knew = knew_ref[0].astype(BF16)
        krow = lax.broadcasted_iota(jnp.int32, (t_new, 1), 0)
        new_blk = past // SEL_BLOCK
        s_new = _dot(knew[:, kwin], qbd) + bias_scr[new_blk:new_blk + 1, :]
        accumulate(partial_softmax(jnp.where(krow <= tok, s_new, NEG), knew[:, vwin]))
        o_s = acc_scr[...] * as_col(1.0 / l_scr[0:1, :])

        wbuf = jnp.concatenate([win_ref[0, cb].astype(BF16).T for cb in range(KV_ROW // LANES)], axis=1)
        wnew = wnew_ref[0].astype(BF16)
        buf = wbuf.shape[0]
        pos_b = past - buf + lax.broadcasted_iota(jnp.int32, (buf, 1), 0)
        pos_n = past + krow
        ok_b = (qpos - pos_b >= 0) & (qpos - pos_b < WINDOW) & (pos_b >= 0)
        ok_n = (qpos - pos_n >= 0) & (qpos - pos_n < WINDOW)
        s_b = attend(wbuf, ok_b)
        s_n = attend(wnew, ok_n)
        m_w = jnp.maximum(jnp.max(s_b, axis=0, keepdims=True), jnp.max(s_n, axis=0, keepdims=True))
        e_b = jnp.where(ok_b, jnp.exp2(s_b - m_w), 0.0)
        e_n = jnp.where(ok_n, jnp.exp2(s_n - m_w), 0.0)
        den = jnp.sum(e_b, axis=0, keepdims=True) + jnp.sum(e_n, axis=0, keepdims=True)
        o_w = _dot_tn(e_b.astype(BF16), wbuf[:, vwin]) + _dot_tn(e_n.astype(BF16), wnew[:, vwin])
        o_w = o_w * jnp.broadcast_to(1.0 / jnp.maximum(den, 1e-30), (SUBLANES, Q_COLS)).T[:, 0:1]
        o_c = oc_scr[...]

        gt = g_ref[0]
        for g in range(KV_HEADS):
            for r in range(GQA_GROUP):
                r0 = r * COLS_PER_HEAD + g * SUBLANES
                c0 = V_COL0 + g * HEAD_DIM
                gc = (g * GQA_GROUP + r) * N_BRANCH
                blk_o = [o[r0:r0 + t_new, c0:c0 + HEAD_DIM] for o in (o_c, o_s, o_w)]
                h0 = (g * GQA_GROUP + r) * HEAD_DIM
                o_ref[0, :, h0:h0 + HEAD_DIM] = (gt[:, gc:gc + 1] * blk_o[0] + gt[:, gc + 1:gc + 2] * blk_o[1]
                                                 + gt[:, gc + 2:gc + 3] * blk_o[2])


def _nsa_sample(cache_slc, page_table, qbd, comp_tm, knew, win_t, wnew, gates, past):
    db, n_pages = page_table.shape
    t_new = knew.shape[1]
    assert t_new == SUBLANES and KV_HEADS * t_new <= COLS_PER_HEAD and past % SEL_BLOCK == 0 and t_new <= SEL_BLOCK
    p_cnt = min(PAGES_PER_STEP, n_pages)
    assert n_pages % p_cnt == 0
    n_steps = n_pages // p_cnt
    ncp = comp_tm.shape[1]
    n_str = -(-(past + t_new) // CMP_STRIDE)
    n_cmp = n_str - CMP_SPAN + 1
    n_sel = -(-(past + t_new) // SEL_BLOCK)
    blocks_step = (PAGE_SIZE // SEL_BLOCK) * p_cnt
    nbp = -(-max(n_sel, n_steps * blocks_step + 1) // SUBLANES) * SUBLANES
    keys_step = PAGE_SIZE * p_cnt
    assert blocks_step <= 2 * LANES - KV_HEADS * HEAD_DIM
    assert CMP_SPAN == 2
    p_rows = max(ncp, (SEL_BLOCK // CMP_STRIDE) * nbp) + SUBLANES
    buf = win_t.shape[-1]
    assert buf % LANES == 0

    def page_spec(p):
        return pl.BlockSpec((1, KV_ROW // LANES, LANES, PAGE_SIZE),
                            lambda b, i, pt: (pt[b * n_pages + i * p_cnt + p], 0, 0, 0))

    def per_b(*shape):
        return pl.BlockSpec((1,) + shape, lambda b, i, pt: (b,) + (0,) * len(shape))

    in_specs = [page_spec(p) for p in range(p_cnt)]
    in_specs += [per_b(2 * LANES, Q_COLS), per_b(ncp, KV_ROW),
                 per_b(t_new, KV_ROW), per_b(KV_ROW // LANES, LANES, buf), per_b(t_new, KV_ROW), per_b(t_new, LANES)]
    return pl.pallas_call(
        functools.partial(_nsa_sample_kernel, n_pages_step=p_cnt, n_steps=n_steps, t_new=t_new, past=past,
                          n_cmp=n_cmp, n_sel=n_sel, k_top=min(N_SEL, n_sel)),
        grid_spec=pltpu.PrefetchScalarGridSpec(
            num_scalar_prefetch=1, grid=(db, n_steps), in_specs=in_specs,
            out_specs=per_b(t_new, NSA_WIDTH),
            scratch_shapes=[pltpu.VMEM((keys_step, 4 * LANES), BF16),
                            pltpu.VMEM((nbp, Q_COLS), F32), pltpu.VMEM((p_rows, Q_COLS), F32),
                            pltpu.VMEM((SUBLANES, Q_COLS), F32), pltpu.VMEM((SUBLANES, Q_COLS), F32),
                            pltpu.VMEM((Q_COLS, 2 * LANES), F32), pltpu.VMEM((Q_COLS, 2 * LANES), F32)]),
        out_shape=jax.ShapeDtypeStruct((db, t_new, NSA_WIDTH), F32),
        compiler_params=_params("arbitrary", "arbitrary"),
        name="nsa_sample",
    )(page_table.reshape(-1), *([cache_slc] * p_cnt), qbd, comp_tm, knew, win_t, wnew, gates)


def _out_kernel(x_ref, po_ref, zp_ref, no_ref, zn_ref, gate_ref, gnp_ref, gnn_ref, w_ref, fn_ref, y_ref):
    zp = zp_ref[0]
    zn = zn_ref[0]
    mp = _rms(po_ref[0], gnp_ref[...]) * (zp * _sigmoid(zp))
    mn = _rms(no_ref[0], gnn_ref[...]) * (zn * _sigmoid(zn))
    m = jnp.concatenate([mp, mn], axis=-1).astype(BF16)
    xo = x_ref[0] + gate_ref[0] * _dot(m, w_ref[...])
    y_ref[0] = _rms(xo, fn_ref[...])


def _out(x, pool_o, zp, nsa_o, zn, gate, gn_pool, gn_nsa, w_out_bf16, final_norm, tm):
    nb, t, d = x.shape
    r = gate.shape[1]
    assert t % tm == 0 and r in (1, t)
    if r == 1:
        gate_spec = pl.BlockSpec((1, 1, d), lambda b, i: (b, 0, 0))
    else:
        gate_spec = pl.BlockSpec((1, tm, d), lambda b, i: (b, i, 0))

    def tok(width):
        return pl.BlockSpec((1, tm, width), lambda b, i: (b, i, 0))

    def const(*shape):
        return pl.BlockSpec(shape, lambda b, i: (0,) * len(shape))

    return pl.pallas_call(
        _out_kernel,
        grid=(nb, t // tm),
        in_specs=[tok(d), tok(POOL_WIDTH), tok(POOL_WIDTH), tok(NSA_WIDTH), tok(NSA_WIDTH), gate_spec,
                  const(1, POOL_WIDTH), const(1, NSA_WIDTH), const(d, d), const(1, d)],
        out_specs=tok(d),
        out_shape=jax.ShapeDtypeStruct((nb, t, d), F32),
        compiler_params=_params("arbitrary", "arbitrary"),
        name="out",
    )(x, pool_o, zp, nsa_o, zn, gate, gn_pool.reshape(1, -1), gn_nsa.reshape(1, -1), w_out_bf16,
      final_norm.reshape(1, d))


def _pages_t(pages):
    n = pages.shape[0]
    return pages.transpose(0, 2, 3, 4, 1).reshape(n, KV_ROW // LANES, LANES, PAGE_SIZE)


def _block_diag(pool_w):
    n, c, _ = pool_w.shape
    eye = jnp.eye(n, dtype=pool_w.dtype)
    return (eye[:, None, :, None] * pool_w[:, :, None, :]).reshape(n * c, n * c)


def _first_layer(phi_w1, phi_pe):
    w = phi_w1.reshape(CMP_SPAN, CMP_STRIDE, 2, HEAD_DIM, CMP_HID)
    w1cat = w.transpose(2, 1, 3, 0, 4).reshape(2, CMP_STRIDE * HEAD_DIM, CMP_SPAN * CMP_HID)
    pe = phi_pe.reshape(CMP_SPAN, CMP_STRIDE, 2, HEAD_DIM).transpose(2, 0, 1, 3).reshape(2, CMP_SPAN, -1)
    pe2 = jnp.concatenate([pe, jnp.zeros((2, SUBLANES - CMP_SPAN, pe.shape[-1]), pe.dtype)], axis=1)
    return w1cat.astype(BF16), pe2


def kernel(x_prompt, x_sample, c_prompt, c_sample, cache_cmp_kv, cache_slc_kv, state_win_kv, state_pool, page_table, norm_w, w_ada, b_ada, w_in, pool_w, pool_scale, phi_w1, phi_pe, phi_w2, gn_pool, gn_nsa, w_out, final_norm):
    b_sz, s_len, d = x_prompt.shape
    db, t_new, _ = x_sample.shape
    depth = norm_w.shape[0]
    assert depth == 1 and w_in.shape[-1] == IN_WIDTH and d // 4 == POOL_WIDTH
    n_phys = cache_cmp_kv.shape[1]
    n_pages = page_table.shape[1]
    past = n_pages * PAGE_SIZE
    lyr = 0

    n_c = b_sz + db
    c_all = jnp.concatenate([c_prompt, c_sample, jnp.zeros((-n_c % SUBLANES, d), F32)], axis=0)
    ada = _ada(c_all, w_ada[lyr], b_ada[lyr])
    shift, scale, gate = ada[:, :d], ada[:, d:2 * d], ada[:, 2 * d:]

    w_in_t = w_in[lyr].T
    w_main = _cast_rows(w_in_t, IN_MAIN, 5 * LANES, BF16)
    w_gate = jnp.pad(w_in_t[IN_MAIN:].astype(BF16), ((0, LANES - GATE_COLS), (0, 0)))
    w_out_bf16 = w_out[lyr].astype(BF16)
    wbd = _block_diag(pool_w[lyr]).astype(BF16)
    w1cat, pe2 = _first_layer(phi_w1[lyr], phi_pe[lyr])
    w2 = phi_w2[lyr].astype(BF16)
    n_tok_s = db * t_new

    def per_token(v):
        return jnp.repeat(v[b_sz:n_c], t_new, axis=0)[None]

    tm = min(ROW_TILE, s_len)
    tq = min(ATTN_TILE, s_len)
    (u_p, zp_p, zn_p, q_t, ks_aug, kw_p, vs_t, vw_t, g_t, kvc_t, kvs_t, kvw_t) = _in_proj(
        x_prompt, scale[:b_sz, None], shift[:b_sz, None], norm_w[lyr], w_main, w_gate, tm, True)
    pool_p = _pool(jnp.zeros((b_sz, POOL_HALO, POOL_WIDTH), F32), u_p, wbd, pool_scale[lyr], 0,
                   min(POOL_TILE, s_len))
    pages_p = s_len // PAGE_SIZE
    in_order = jnp.tile(jnp.arange(pages_p, dtype=jnp.int32), (b_sz, 1))
    comp_k, comp_vt = _compress(kvc_t, in_order, w1cat, pe2, w2, None, False, True)
    nsa_p = _nsa_prompt(q_t, comp_k, comp_vt, ks_aug, vs_t, kw_p, vw_t, g_t, tq=tq, kb=tq)
    y_p = _out(x_prompt, pool_p, zp_p, nsa_p, zn_p, gate[:b_sz, None], gn_pool[lyr], gn_nsa[lyr], w_out_bf16,
               final_norm, min(OUT_TILE, s_len))

    (u_s, zp_s, zn_s, kvc_s, kvs_s, kvw_s, q_s, g_s) = _in_proj(
        x_sample.reshape(1, n_tok_s, d), per_token(scale), per_token(shift), norm_w[lyr], w_main, w_gate, n_tok_s,
        False)
    u_s3 = u_s.reshape(db, t_new, POOL_WIDTH)
    hist = jnp.concatenate([jnp.zeros((db, POOL_HALO - state_pool.shape[2], POOL_WIDTH), F32), state_pool[lyr]], axis=1)
    pool_s = _pool(hist, u_s3, wbd, pool_scale[lyr], past, t_new)
    kvc_s3 = kvc_s.reshape(db, t_new, KV_ROW)
    kvs_s3 = kvs_s.reshape(db, t_new, KV_ROW)
    kvw_s3 = kvw_s.reshape(db, t_new, KV_ROW)
    xnew = jnp.pad(kvc_s3, ((0, 0), (0, CMP_STRIDE - t_new), (0, 0)))
    xnew = xnew.reshape(db, CMP_STRIDE, N_KV_SLABS, HEAD_DIM).transpose(2, 0, 1, 3).reshape(N_KV_SLABS, db, -1)
    a1new = _new_chunk_first_layer(xnew, w1cat).transpose(1, 0, 2)
    comp_s = _compress(_pages_t(cache_cmp_kv[lyr]), page_table, w1cat, pe2, w2, a1new, True, False)
    q5 = q_s.reshape(db, t_new, KV_HEADS, GQA_GROUP, HEAD_DIM)
    eye = jnp.eye(KV_HEADS, dtype=BF16)
    qbd = q5.transpose(0, 2, 4, 3, 1)[:, :, :, :, None, :] * eye[None, :, None, None, :, None]
    qbd = jnp.pad(qbd.reshape(db, KV_HEADS * HEAD_DIM, GQA_GROUP, KV_HEADS * t_new),
                  ((0, 0), (0, 2 * LANES - KV_HEADS * HEAD_DIM), (0, 0), (0, COLS_PER_HEAD - KV_HEADS * t_new)))
    qbd = qbd.reshape(db, 2 * LANES, Q_COLS)
    g_s3 = g_s.reshape(db, t_new, LANES)
    buf = state_win_kv.shape[2]
    win_t = state_win_kv[lyr].transpose(0, 2, 3, 4, 1).reshape(db, KV_ROW // LANES, LANES, buf)
    nsa_s = _nsa_sample(_pages_t(cache_slc_kv[lyr]), page_table, qbd, comp_s, kvs_s3,
                        win_t, kvw_s3, g_s3, past)
    y_s = _out(x_sample.reshape(1, n_tok_s, d), pool_s.reshape(1, n_tok_s, POOL_WIDTH), zp_s,
               nsa_s.reshape(1, n_tok_s, NSA_WIDTH), zn_s, per_token(gate), gn_pool[lyr], gn_nsa[lyr], w_out_bf16,
               final_norm, n_tok_s).reshape(db, t_new, d)

    kv_shape = (2, KV_HEADS, HEAD_DIM)
    win_len = min(WINDOW, s_len)
    new_win_s = jnp.concatenate([state_win_kv[lyr], kvw_s3.reshape((db, t_new) + kv_shape)], axis=1)[:, -buf:]
    hist_len = state_pool.shape[2]
    new_pool_s = jnp.concatenate([state_pool[lyr], u_s3], axis=1)[:, -hist_len:]

    def rows_from_t(v_t):
        return v_t.reshape((v_t.shape[0],) + kv_shape + (v_t.shape[-1],)).transpose(0, 4, 1, 2, 3)[None]

    return (y_p, y_s,
            rows_from_t(kvc_t), kvc_s3.reshape((1, db, t_new) + kv_shape),
            rows_from_t(kvs_t), kvs_s3.reshape((1, db, t_new) + kv_shape),
            rows_from_t(kvw_t[..., s_len - win_len:]), new_win_s[None],
            u_p[:, s_len - hist_len:][None], new_pool_s[None])
```

```python
import functools

import numpy as np
import jax
import jax.numpy as jnp
from jax import lax
from jax.experimental import pallas as pl
from jax.experimental.pallas import tpu as pltpu

F32 = jnp.float32
BF16 = jnp.bfloat16

HEAD_DIM = 64
GQA_GROUP = 4
KV_HEADS = 3
N_HEADS = KV_HEADS * GQA_GROUP
N_KV_SLABS = 2 * KV_HEADS
KV_ROW = N_KV_SLABS * HEAD_DIM
N_BRANCH = 3
POOL_WINDOWS = (2, 4, 8, 16)
POOL_HALO = 16
CMP_BLOCK = 32
CMP_STRIDE = 16
CMP_SPAN = CMP_BLOCK // CMP_STRIDE
CMP_HID = 2 * HEAD_DIM
SEL_BLOCK = 64
N_SEL = 16
WINDOW = 512
PAGE_SIZE = 128
CHUNKS_PER_PAGE = PAGE_SIZE // CMP_STRIDE
CHUNK_PITCH = 24
EPS = 1e-6
NEG = -1e30
FORCE = 1e4
LOG2_E = 1.4426950408889634
VT_ROWS = 80
PAD_SCORE = -1e38
TAKEN_SCORE = -3e38

LANES = 128
SUBLANES = 8
VMEM_LIMIT_BYTES = 56 * 1024 * 1024

ROW_TILE = 512
OUT_TILE = 1024
ATTN_TILE = 1024
PAGES_PER_STEP = 32
BAND_SUB = 256


def _dot(a, b):
    return jnp.dot(a, b, preferred_element_type=F32)


def _dot_nt(a, b):
    return lax.dot_general(a, b, (((1,), (1,)), ((), ())), preferred_element_type=F32)


def _dot_tn(a, b):
    return lax.dot_general(a, b, (((0,), (0,)), ((), ())), preferred_element_type=F32)


def _sigmoid(v):
    return 1.0 / (1.0 + jnp.exp(-v))


def _rms(v, g):
    return v * lax.rsqrt(jnp.mean(v * v, axis=-1, keepdims=True) + EPS) * g


def _split3(v):
    hi = v.astype(BF16)
    r1 = v - hi.astype(F32)
    mid = r1.astype(BF16)
    lo = (r1 - mid.astype(F32)).astype(BF16)
    return hi, mid, lo


def _params(*sem):
    return pltpu.CompilerParams(dimension_semantics=sem, vmem_limit_bytes=VMEM_LIMIT_BYTES)


def _ada_kernel(c_ref, w_ref, b_ref, o_ref):
    c = c_ref[...]
    a = (c * _sigmoid(c)).astype(BF16)
    o_ref[...] = _dot(a, w_ref[...].astype(BF16)) + b_ref[...]


def _ada(c, w_ada, b_ada):
    m, d = c.shape
    n = w_ada.shape[1]
    tn = 512
    return pl.pallas_call(
        _ada_kernel,
        grid=(n // tn,),
        in_specs=[pl.BlockSpec((m, d), lambda j: (0, 0)),
                  pl.BlockSpec((d, tn), lambda j: (0, j)),
                  pl.BlockSpec((1, tn), lambda j: (0, j))],
        out_specs=pl.BlockSpec((m, tn), lambda j: (0, j)),
        out_shape=jax.ShapeDtypeStruct((m, n), F32),
        compiler_params=_params("arbitrary"),
        name="ada",
    )(c, w_ada, b_ada.reshape(1, n))


POOL_WIDTH = 256
NSA_WIDTH = N_HEADS * HEAD_DIM
GATE_COLS = N_BRANCH * N_HEADS
MAX_SEL_BLOCKS = LANES - HEAD_DIM
_SEG = {}
_off = 0
for _name, _w in (("u", POOL_WIDTH), ("zp", POOL_WIDTH), ("q", NSA_WIDTH), ("zn", NSA_WIDTH),
                  ("kvc", KV_ROW), ("kvs", KV_ROW), ("kvw", KV_ROW)):
    _SEG[_name] = (_off, _w)
    _off += _w
IN_MAIN = _off
IN_WIDTH = IN_MAIN + GATE_COLS


GATE_ROWS = 16


def _in_proj_kernel(x_ref, sc_ref, sh_ref, nw_ref, w_ref, wg_ref, u_ref, zp_ref, zn_ref, *rest, attn_layouts):
    tm = x_ref.shape[1]
    h = (_rms(x_ref[0], nw_ref[...]) * (1.0 + sc_ref[0]) + sh_ref[0]).astype(BF16)

    def seg(name):
        s, w = _SEG[name]
        return _dot_nt(h, w_ref[s:s + w, :])

    u_ref[0] = seg("u")
    zp_ref[0] = seg("zp")
    zn_ref[0] = seg("zn")
    q = seg("q") * (HEAD_DIM ** -0.5 * LOG2_E)
    kvc = seg("kvc")
    kvs = seg("kvs")
    kvw = seg("kvw")
    sig = _sigmoid(_dot_nt(h, wg_ref[...]))
    if not attn_layouts:
        kvc_ref, kvs_ref, kvw_ref, q_ref, g_ref = rest
        kvc_ref[0] = kvc
        kvs_ref[0] = kvs
        kvw_ref[0] = kvw
        q_ref[0] = q.astype(BF16)
        g_ref[0] = sig
        return

    qt_ref, ks_aug_ref, kw_ref, vst_ref, vwt_ref, gt_ref, kvct_ref, kvst_ref, kvwt_ref = rest
    qt_ref[0] = q.T.astype(BF16).reshape(N_HEADS, HEAD_DIM, tm)
    kvs_t32 = kvs.T
    kvw_t32 = kvw.T
    kvct_ref[0] = kvc.T.reshape(KV_ROW // LANES, LANES, tm)
    kvst_ref[0] = kvs_t32.reshape(KV_ROW // LANES, LANES, tm)
    kvwt_ref[0] = kvw_t32.reshape(KV_ROW // LANES, LANES, tm)
    pos = pl.program_id(1) * tm + lax.broadcasted_iota(jnp.int32, (tm, 1), 0)
    onehot = (lax.broadcasted_iota(jnp.int32, (1, MAX_SEL_BLOCKS), 1) == pos // SEL_BLOCK).astype(BF16)
    kvs_t = kvs_t32.astype(BF16)
    kvw_t = kvw_t32.astype(BF16)
    v0 = KV_HEADS * HEAD_DIM
    tail = (lax.broadcasted_iota(jnp.int32, (VT_ROWS - HEAD_DIM, LANES), 0) == 0).astype(BF16)
    for g in range(KV_HEADS):
        ks_aug_ref[0, g] = jnp.concatenate([kvs[:, g * HEAD_DIM:(g + 1) * HEAD_DIM].astype(BF16), onehot], axis=1)
        kw_ref[0, g] = kvw[:, g * HEAD_DIM:(g + 1) * HEAD_DIM].astype(BF16)
        for c in range(tm // LANES):
            sl = (slice(v0 + g * HEAD_DIM, v0 + (g + 1) * HEAD_DIM), slice(c * LANES, (c + 1) * LANES))
            vst_ref[0, g, c] = jnp.concatenate([kvs_t[sl], tail], axis=0)
            vwt_ref[0, g, c] = jnp.concatenate([kvw_t[sl], tail], axis=0)
        per_group = GQA_GROUP * N_BRANCH
        rolled = sig if g == 0 else pltpu.roll(sig, LANES - per_group * g, 1)
        gt_ref[0, g] = rolled.T[:GATE_ROWS, :]


def _cast_kernel(w_ref, o_ref):
    o_ref[...] = w_ref[...].astype(o_ref.dtype)


def _cast_rows(w, n_rows, tile, dtype):
    cols = w.shape[1]
    assert n_rows % tile == 0 and tile % SUBLANES == 0 and n_rows <= w.shape[0] and cols % LANES == 0
    return pl.pallas_call(
        _cast_kernel,
        grid=(n_rows // tile,),
        in_specs=[pl.BlockSpec((tile, cols), lambda j: (j, 0))],
        out_specs=pl.BlockSpec((tile, cols), lambda j: (j, 0)),
        out_shape=jax.ShapeDtypeStruct((n_rows, cols), dtype),
        compiler_params=_params("arbitrary"),
        name="cast_rows",
    )(w)


def _in_proj(x, scale, shift, norm_w, w_main, w_gate, tm, attn_layouts):
    nb, t, d = x.shape
    r = scale.shape[1]
    assert t % tm == 0 and r in (1, t) and (tm % LANES == 0 or not attn_layouts)
    if r == 1:
        mod_spec = pl.BlockSpec((1, 1, d), lambda b, i: (b, 0, 0))
    else:
        mod_spec = pl.BlockSpec((1, tm, d), lambda b, i: (b, i, 0))

    def tok(width):
        return pl.BlockSpec((1, tm, width), lambda b, i: (b, i, 0))

    def hm(n, width=HEAD_DIM):
        return pl.BlockSpec((1, n, tm, width), lambda b, i: (b, 0, i, 0))

    def sd(*shape, dtype=F32):
        return jax.ShapeDtypeStruct(shape, dtype)

    out_specs = [tok(POOL_WIDTH), tok(POOL_WIDTH), tok(NSA_WIDTH)]
    out_shape = [sd(nb, t, POOL_WIDTH), sd(nb, t, POOL_WIDTH), sd(nb, t, NSA_WIDTH)]
    if attn_layouts:
        lane_tiles = tm // LANES
        vt_spec = pl.BlockSpec((1, KV_HEADS, lane_tiles, VT_ROWS, LANES), lambda b, i: (b, 0, i, 0, 0))
        vt_shape = sd(nb, KV_HEADS, t // LANES, VT_ROWS, LANES, dtype=BF16)
        rows_t_spec = pl.BlockSpec((1, KV_ROW // LANES, LANES, tm), lambda b, i: (b, 0, 0, i))
        rows_t_shape = sd(nb, KV_ROW // LANES, LANES, t)
        out_specs += [pl.BlockSpec((1, N_HEADS, HEAD_DIM, tm), lambda b, i: (b, 0, 0, i)),
                      hm(KV_HEADS, HEAD_DIM + MAX_SEL_BLOCKS), hm(KV_HEADS), vt_spec, vt_spec,
                      pl.BlockSpec((1, KV_HEADS, GATE_ROWS, tm), lambda b, i: (b, 0, 0, i)),
                      rows_t_spec, rows_t_spec, rows_t_spec]
        out_shape += [sd(nb, N_HEADS, HEAD_DIM, t, dtype=BF16),
                      sd(nb, KV_HEADS, t, HEAD_DIM + MAX_SEL_BLOCKS, dtype=BF16),
                      sd(nb, KV_HEADS, t, HEAD_DIM, dtype=BF16), vt_shape, vt_shape,
                      sd(nb, KV_HEADS, GATE_ROWS, t), rows_t_shape, rows_t_shape, rows_t_shape]
    else:
        out_specs += [tok(KV_ROW), tok(KV_ROW), tok(KV_ROW), tok(NSA_WIDTH), tok(LANES)]
        out_shape += [sd(nb, t, KV_ROW), sd(nb, t, KV_ROW), sd(nb, t, KV_ROW),
                      sd(nb, t, NSA_WIDTH, dtype=BF16), sd(nb, t, LANES)]
    return pl.pallas_call(
        functools.partial(_in_proj_kernel, attn_layouts=attn_layouts),
        grid=(nb, t // tm),
        in_specs=[tok(d), mod_spec, mod_spec,
                  pl.BlockSpec((1, d), lambda b, i: (0, 0)),
                  pl.BlockSpec((IN_MAIN, d), lambda b, i: (0, 0)),
                  pl.BlockSpec((LANES, d), lambda b, i: (0, 0))],
        out_specs=out_specs,
        out_shape=out_shape,
        compiler_params=_params("arbitrary", "arbitrary"),
        name="in_proj",
    )(x, scale, shift, norm_w.reshape(1, d), w_main, w_gate)


def _pool_rows(halo, u, pos, wbd, ps):
    c = u.shape[-1]
    grp = lax.broadcasted_iota(jnp.int32, (1, c), 1) // (c // len(POOL_WINDOWS))
    sums = []
    s = jnp.concatenate([halo, u], axis=0)
    for w in POOL_WINDOWS:
        s = s + pltpu.roll(s, w // 2, 0)
        sums.append(s[POOL_HALO:])
    tot = sums[-1]
    win = jnp.full((1, c), float(POOL_WINDOWS[-1]), F32)
    for gi in range(len(POOL_WINDOWS) - 2, -1, -1):
        tot = jnp.where(grp == gi, sums[gi], tot)
        win = jnp.where(grp == gi, float(POOL_WINDOWS[gi]), win)
    cnt = jnp.minimum(win, (pos + 1).astype(F32))
    return _dot((tot / cnt - u).astype(BF16), wbd) * ps


def _pool_kernel(hist_ref, uprev_ref, u_ref, wbd_ref, ps_ref, o_ref, *, pos0, tp):
    i = pl.program_id(1)
    pos = pos0 + i * tp + lax.broadcasted_iota(jnp.int32, (tp, 1), 0)
    for bi in range(u_ref.shape[0]):
        halo = jnp.where(i == 0, hist_ref[bi], uprev_ref[bi])
        o_ref[bi] = _pool_rows(halo, u_ref[bi], pos, wbd_ref[...], ps_ref[...])


def _pool(hist, u, wbd, pool_scale, pos0, tp):
    nb, t, c = u.shape
    assert t % tp == 0
    if t >= POOL_HALO:
        assert tp % POOL_HALO == 0
        uprev, ratio, bb = u, tp // POOL_HALO, 1
        prev_spec = pl.BlockSpec((bb, POOL_HALO, c), lambda b, i: (b, jnp.maximum(i * ratio - 1, 0), 0))
    else:
        assert t == tp
        uprev, bb = hist, nb
        prev_spec = pl.BlockSpec((bb, POOL_HALO, c), lambda b, i: (b, 0, 0))
    return pl.pallas_call(
        functools.partial(_pool_kernel, pos0=pos0, tp=tp),
        grid=(nb // bb, t // tp),
        in_specs=[pl.BlockSpec((bb, POOL_HALO, c), lambda b, i: (b, 0, 0)), prev_spec,
                  pl.BlockSpec((bb, tp, c), lambda b, i: (b, i, 0)),
                  pl.BlockSpec((c, c), lambda b, i: (0, 0)),
                  pl.BlockSpec((1, c), lambda b, i: (0, 0))],
        out_specs=pl.BlockSpec((bb, tp, c), lambda b, i: (b, i, 0)),
        out_shape=jax.ShapeDtypeStruct((nb, t, c), F32),
        compiler_params=_params("arbitrary", "arbitrary"),
        name="pool",
    )(hist, uprev, u, wbd, pool_scale.reshape(1, c))


def _gelu_tanh(v):
    return 0.5 * v * (1.0 + jnp.tanh(np.sqrt(2.0 / np.pi).astype(np.float32) * (v + 0.044715 * (v * v * v))))


def _compress_kernel(pt_ref, *refs, n_pages_step, n_steps, has_new, token_major):
    del pt_ref
    p_cnt = n_pages_step
    page_refs = refs[:p_cnt]
    w1p_ref, w1_ref, pe_ref, w2_ref = refs[p_cnt:p_cnt + 4]
    k = p_cnt + 4
    a1new_ref = None
    if has_new:
        a1new_ref = refs[k]
        k += 1
    if token_major:
        out_ref = refs[k]
        k += 1
    else:
        outk_ref, outvt_ref = refs[k:k + 2]
        k += 2
    slab_scr, carry_scr, bias_scr = refs[k:k + 3]
    i = pl.program_id(1)
    rows = CHUNKS_PER_PAGE * p_cnt
    row_id = lax.broadcasted_iota(jnp.int32, (rows, 1), 0)

    @pl.when(i == 0)
    def _():
        carry_scr[...] = jnp.zeros_like(carry_scr)
        for kk in range(2):
            pb = _dot(pe_ref[kk].astype(BF16), w1_ref[kk])
            bias_scr[kk] = pb[0:1, :CMP_HID] + pb[1:2, CMP_HID:]

    def store(kg, comp):
        if token_major:
            out_ref[0, :, kg * HEAD_DIM:(kg + 1) * HEAD_DIM] = comp.astype(BF16)
        elif kg < KV_HEADS:
            outk_ref[0, kg] = comp.astype(BF16)
        else:
            wide = jnp.concatenate([comp, jnp.zeros((rows, LANES - HEAD_DIM), F32)], axis=1)
            outvt_ref[0, kg - KV_HEADS] = wide.T[:HEAD_DIM, :].astype(BF16)

    def finish(kg, a0_prev, a1):
        kk = kg // KV_HEADS
        pre = a0_prev + a1 + bias_scr[kk]
        return _dot(_gelu_tanh(pre).astype(BF16), w2_ref[kk])

    def main():
        for cb in range(KV_ROW // LANES):
            for p in range(p_cnt):
                rows_t = page_refs[p][0, cb].astype(BF16).T.astype(F32)
                for ch in range(CHUNKS_PER_PAGE):
                    r0 = (p * CHUNKS_PER_PAGE + ch) * CHUNK_PITCH
                    slab_scr[cb, r0:r0 + CMP_STRIDE, :] = rows_t[ch * CMP_STRIDE:(ch + 1) * CMP_STRIDE, :]
            x = jnp.concatenate([slab_scr[cb, pl.ds(j, rows, stride=CHUNK_PITCH), :] for j in range(CMP_STRIDE)],
                                axis=1).astype(BF16)
            a_both = _dot(x, w1p_ref[cb])
            for half in range(2):
                kg = 2 * cb + half
                a = a_both[:, half * 2 * CMP_HID:(half + 1) * 2 * CMP_HID]
                a0 = a[:, :CMP_HID]
                a0_prev = jnp.where(row_id == 0, carry_scr[kg], pltpu.roll(a0, 1, 0))
                carry_scr[kg] = a0[rows - 1:rows, :]
                store(kg, finish(kg, a0_prev, a[:, CMP_HID:]))

    if not has_new:
        main()
    else:
        pl.when(i < n_steps)(main)

        @pl.when(i == n_steps)
        def _():
            for kg in range(N_KV_SLABS):
                comp = finish(kg, carry_scr[kg], a1new_ref[0, kg:kg + 1, :])
                store(kg, jnp.where(row_id == 0, jnp.broadcast_to(comp, (rows, HEAD_DIM)), 0.0))


def _new_chunk_kernel(x_ref, w1_ref, o_ref):
    for kg in range(N_KV_SLABS):
        o_ref[kg] = _dot(x_ref[kg].astype(BF16), w1_ref[kg // KV_HEADS])[:, CMP_HID:]


def _new_chunk_first_layer(xnew, w1cat):
    n_slab, nb, xw = xnew.shape
    return pl.pallas_call(
        _new_chunk_kernel,
        out_shape=jax.ShapeDtypeStruct((n_slab, nb, CMP_HID), F32),
        compiler_params=pltpu.CompilerParams(vmem_limit_bytes=VMEM_LIMIT_BYTES),
        name="new_chunk",
    )(xnew, w1cat)


def _compress(pages, page_table, w1cat, pe2, w2, a1new, token_major, per_batch):
    nb, n_pages = page_table.shape
    p_cnt = min(PAGES_PER_STEP, n_pages)
    assert n_pages % p_cnt == 0
    n_steps = n_pages // p_cnt
    has_new = a1new is not None
    rows = CHUNKS_PER_PAGE * p_cnt
    grid_steps = n_steps + (1 if has_new else 0)
    tot_rows = rows * grid_steps
    xw = CMP_STRIDE * HEAD_DIM

    def page_spec(p):
        def imap(b, i, pt):
            step = jnp.minimum(i, n_steps - 1)
            page = pt[b * n_pages + step * p_cnt + p]
            return (b, 0, 0, page) if per_batch else (page, 0, 0, 0)
        return pl.BlockSpec((1, KV_ROW // LANES, LANES, PAGE_SIZE), imap)

    n_cb = KV_ROW // LANES
    w4 = w1cat.reshape(2, CMP_STRIDE, HEAD_DIM, 2 * CMP_HID)
    zero = jnp.zeros_like(w4[0])
    w1p = jnp.stack([jnp.concatenate([jnp.concatenate([w4[(2 * cb) // KV_HEADS], zero], axis=2),
                                      jnp.concatenate([zero, w4[(2 * cb + 1) // KV_HEADS]], axis=2)], axis=1)
                     for cb in range(n_cb)]).reshape(n_cb, CMP_STRIDE * LANES, 4 * CMP_HID)
    in_specs = [page_spec(p) for p in range(p_cnt)]
    in_specs += [pl.BlockSpec((n_cb, CMP_STRIDE * LANES, 4 * CMP_HID), lambda b, i, pt: (0, 0, 0)),
                 pl.BlockSpec((2, xw, 2 * CMP_HID), lambda b, i, pt: (0, 0, 0)),
                 pl.BlockSpec((2, SUBLANES, xw), lambda b, i, pt: (0, 0, 0)),
                 pl.BlockSpec((2, CMP_HID, HEAD_DIM), lambda b, i, pt: (0, 0, 0))]
    args = [pages] * p_cnt + [w1p, w1cat, pe2, w2]
    if has_new:
        in_specs.append(pl.BlockSpec((1, N_KV_SLABS, CMP_HID), lambda b, i, pt: (b, 0, 0)))
        args.append(a1new)
    if token_major:
        out_spec = pl.BlockSpec((1, rows, KV_ROW), lambda b, i, pt: (b, i, 0))
        out_shape = jax.ShapeDtypeStruct((nb, tot_rows, KV_ROW), BF16)
    else:
        assert rows % LANES == 0
        out_spec = [pl.BlockSpec((1, KV_HEADS, rows, HEAD_DIM), lambda b, i, pt: (b, 0, i, 0)),
                    pl.BlockSpec((1, KV_HEADS, HEAD_DIM, rows), lambda b, i, pt: (b, 0, 0, i))]
        out_shape = [jax.ShapeDtypeStruct((nb, KV_HEADS, tot_rows, HEAD_DIM), BF16),
                     jax.ShapeDtypeStruct((nb, KV_HEADS, HEAD_DIM, tot_rows), BF16)]
    return pl.pallas_call(
        functools.partial(_compress_kernel, n_pages_step=p_cnt, n_steps=n_steps, has_new=has_new,
                          token_major=token_major),
        grid_spec=pltpu.PrefetchScalarGridSpec(
            num_scalar_prefetch=1, grid=(nb, grid_steps), in_specs=in_specs, out_specs=out_spec,
            scratch_shapes=[pltpu.VMEM((n_cb, rows * CHUNK_PITCH, LANES), F32),
                            pltpu.VMEM((N_KV_SLABS, 1, CMP_HID), F32),
                            pltpu.VMEM((2, 1, CMP_HID), F32)]),
        out_shape=out_shape,
        compiler_params=_params("arbitrary", "arbitrary"),
        name="compress_new" if has_new else "compress",
    )(page_table.reshape(-1), *args)


def _softmax_parts(s, valid, axis):
    s = jnp.where(valid, s, NEG)
    m = jnp.max(s, axis=axis, keepdims=True)
    e = jnp.where(valid, jnp.exp2(s - m), 0.0)
    return e, jnp.sum(e, axis=axis, keepdims=True)


def _nsa_prompt_kernel(qt_ref, kc_ref, vct_ref, ks_ref, vst_ref, kw_ref, vwt_ref, gt_ref, mt_ref, o_ref, *,
                       tq, kb, n_cmp, k_top):
    qi = pl.program_id(2)
    qs = qi * tq
    r4 = GQA_GROUP
    cols = r4 * tq
    qt = jnp.concatenate([qt_ref[0, r] for r in range(r4)], axis=1)
    tpos = qs + lax.broadcasted_iota(jnp.int32, (1, tq), 1)

    def per_head(v):
        return jnp.concatenate([v] * r4, axis=1)

    def value_tiles(ref, k0, width):
        j0 = k0 // LANES
        return jnp.concatenate([ref[0, 0, j0 + j] for j in range(width // LANES)], axis=1)

    def normalise(acc):
        return acc[:HEAD_DIM] * (1.0 / acc[HEAD_DIM:HEAD_DIM + 1])

    nc = kc_ref.shape[2]
    s = _dot(kc_ref[0, 0], qt)
    cid = lax.broadcasted_iota(jnp.int32, (nc, 1), 0)
    valid = (cid >= 1) & (cid <= n_cmp) & (cid * CMP_STRIDE + (CMP_BLOCK - CMP_STRIDE - 1) <= tpos)
    s = s + per_head(jnp.where(valid, 0.0, NEG))
    e = jnp.exp2(s - jnp.max(s, axis=0, keepdims=True))
    den = jnp.sum(e, axis=0, keepdims=True)
    has_key = per_head(jnp.where(tpos >= CMP_BLOCK - 1, 1.0, 0.0)) if n_cmp >= 1 else 0.0
    p = e * (has_key / jnp.maximum(den, 1e-30))
    o_c = _dot(vct_ref[0, 0], p.astype(BF16))
    pg = p[:, 0:tq]
    for r in range(1, r4):
        pg = pg + p[:, r * tq:(r + 1) * tq]

    mt = mt_ref[...]
    imp = sum(_dot(mt, piece) for piece in _split3(pg))
    n_blk = imp.shape[0]
    blk = lax.broadcasted_iota(jnp.int32, (n_blk, 1), 0)
    cur = (qs + lax.broadcasted_iota(jnp.int32, (1, tq), 1)) // SEL_BLOCK
    score = jnp.where(blk <= cur, imp, -1.0)
    score = jnp.where((blk == 0) | (blk == cur) | (blk == cur - 1), FORCE, score)
    sub = lax.broadcasted_iota(jnp.int32, (SUBLANES, 1), 0)
    tiles = [score[v * SUBLANES:(v + 1) * SUBLANES, :] for v in range(n_blk // SUBLANES)]
    ranks = [jnp.zeros((SUBLANES, tq), F32) for _ in tiles]
    for j in range(n_blk):
        sj = score[j:j + 1, :]
        for v, sc in enumerate(tiles):
            if v > j // SUBLANES:
                ahead = sj >= sc
            elif v < j // SUBLANES:
                ahead = sj > sc
            else:
                ahead = (sj > sc) | ((sj == sc) & (sub > j % SUBLANES))
            ranks[v] = ranks[v] + jnp.where(ahead, 1.0, 0.0)
    rank = jnp.concatenate(ranks, axis=0)
    sel_bias = jnp.where((rank < float(k_top)) & (blk <= cur), 0.0, NEG).astype(BF16)
    q_aug = jnp.concatenate([qt, jnp.concatenate([sel_bias] * r4, axis=1)], axis=0)

    def scores(k0):
        return _dot(ks_ref[0, 0, pl.ds(k0, kb), :], q_aug)

    def absorb(sk, values, state):
        m_run, acc = state
        m_new = jnp.maximum(m_run, jnp.max(sk, axis=0, keepdims=True))
        pv = _dot(values, jnp.exp2(sk - m_new).astype(BF16))
        return m_new, jnp.exp2(m_run - m_new) * acc + pv

    def tile(kt, state):
        k0 = pl.multiple_of(kt * kb, kb)
        return absorb(scores(k0), value_tiles(vst_ref, k0, kb), state)

    init = (jnp.full((1, cols), NEG, F32), jnp.zeros((VT_ROWS, cols), F32))
    m_run, acc = lax.fori_loop(0, qi, tile, init)

    sub = min(BAND_SUB, tq)
    n_sub = tq // sub

    def sub_cols(v, j):
        return jnp.concatenate([v[:, r * tq + j * sub:r * tq + (j + 1) * sub] for r in range(r4)], axis=1)

    def join_subs(parts):
        return jnp.concatenate([parts[j][:, r * sub:(r + 1) * sub] for r in range(r4) for j in range(n_sub)],
                               axis=1)

    wl = min(kw_ref.shape[2], WINDOW + sub)
    sel_parts, win_parts = [], []
    for j in range(n_sub):
        tpos_j = tpos[:, j * sub:(j + 1) * sub]
        heads_j = lambda v: jnp.concatenate([v] * r4, axis=1)
        nk = (j + 1) * sub
        kpos = qs + lax.broadcasted_iota(jnp.int32, (nk, 1), 0)
        sk = _dot(ks_ref[0, 0, pl.ds(pl.multiple_of(qs, tq), nk), :], sub_cols(q_aug, j))
        sk = sk + heads_j(jnp.where(kpos <= tpos_j, 0.0, NEG))
        state_j = absorb(sk, value_tiles(vst_ref, qs, nk), (sub_cols(m_run, j), sub_cols(acc, j)))
        sel_parts.append(normalise(state_j[1]))
        w0 = pl.multiple_of(jnp.maximum(qs + (j + 1) * sub - wl, 0), LANES)
        dq = tpos_j - (w0 + lax.broadcasted_iota(jnp.int32, (wl, 1), 0))
        sw = _dot(kw_ref[0, 0, pl.ds(w0, wl), :], sub_cols(qt, j))
        sw = sw + heads_j(jnp.where((dq >= 0) & (dq < WINDOW), 0.0, NEG))
        ew = jnp.exp2(sw - jnp.max(sw, axis=0, keepdims=True))
        win_parts.append(normalise(_dot(value_tiles(vwt_ref, w0, wl), ew.astype(BF16))))
    o_s = join_subs(sel_parts)
    o_w = join_subs(win_parts)

    gt = gt_ref[0, 0]
    heads = []
    for r in range(r4):
        c0 = r * N_BRANCH
        sl = slice(r * tq, (r + 1) * tq)
        heads.append(gt[c0:c0 + 1] * o_c[:, sl] + gt[c0 + 1:c0 + 2] * o_s[:, sl] + gt[c0 + 2:c0 + 3] * o_w[:, sl])
    o_ref[0] = jnp.concatenate(heads, axis=0).T


def _importance_matrix(n_blk_pad, n_col):
    mt = np.zeros((n_blk_pad, n_col), np.float32)
    ratio = SEL_BLOCK // CMP_STRIDE
    for b in range(n_blk_pad):
        for c, w in zip(range(ratio * b, ratio * b + ratio + 1), (1.0,) + (2.0,) * (ratio - 1) + (1.0,)):
            if c < n_col:
                mt[b, c] = w
    return mt


def _nsa_prompt(q_t, comp_k, comp_vt, ks_aug, vs_t, kw, vw_t, gates_t, tq, kb):
    b_sz, _, _, s_len = q_t.shape
    nc = comp_k.shape[2]
    assert s_len % SEL_BLOCK == 0 and s_len % kb == 0 and kb % tq == 0 and s_len % tq == 0 and CMP_SPAN == 2
    assert tq % LANES == 0 and kb == tq
    n_blk = s_len // SEL_BLOCK
    assert n_blk <= MAX_SEL_BLOCKS
    n_cmp = s_len // CMP_STRIDE - CMP_SPAN + 1
    mt = jnp.asarray(_importance_matrix(MAX_SEL_BLOCKS, nc), BF16)

    def per_group(*shape):
        return pl.BlockSpec((1, 1) + shape, lambda b, g, i: (b, g) + (0,) * len(shape))

    return pl.pallas_call(
        functools.partial(_nsa_prompt_kernel, tq=tq, kb=kb, n_cmp=n_cmp, k_top=min(N_SEL, n_blk)),
        grid=(b_sz, KV_HEADS, s_len // tq),
        in_specs=[pl.BlockSpec((1, GQA_GROUP, HEAD_DIM, tq), lambda b, g, i: (b, g, 0, i)),
                  per_group(nc, HEAD_DIM), per_group(HEAD_DIM, nc),
                  per_group(s_len, HEAD_DIM + MAX_SEL_BLOCKS), per_group(s_len // LANES, VT_ROWS, LANES),
                  per_group(s_len, HEAD_DIM), per_group(s_len // LANES, VT_ROWS, LANES),
                  pl.BlockSpec((1, 1, GATE_ROWS, tq), lambda b, g, i: (b, g, 0, i)),
                  pl.BlockSpec((MAX_SEL_BLOCKS, nc), lambda b, g, i: (0, 0))],
        out_specs=pl.BlockSpec((1, tq, GQA_GROUP * HEAD_DIM), lambda b, g, i: (b, i, g)),
        out_shape=jax.ShapeDtypeStruct((b_sz, s_len, NSA_WIDTH), F32),
        compiler_params=_params("arbitrary", "arbitrary", "arbitrary"),
        name="nsa_prompt",
    )(q_t, comp_k, comp_vt, ks_aug, vs_t, kw, vw_t, gates_t, mt)


Q_COLS = LANES
COLS_PER_HEAD = Q_COLS // GQA_GROUP
V_OFF = LANES
V_COL0 = KV_HEADS * HEAD_DIM - V_OFF

def _nsa_sample_kernel(pt_ref, *refs, n_pages_step, n_steps, t_new, past, n_cmp, n_sel, k_top):
    del pt_ref
    p_cnt = n_pages_step
    page_refs = refs[:p_cnt]
    (qbd_ref, comp_ref, knew_ref, win_ref, wnew_ref, g_ref,
     o_ref, kv_scr, bias_scr, p_scr, m_scr, l_scr, acc_scr, oc_scr) = refs[p_cnt:]
    i = pl.program_id(1)
    qbd = qbd_ref[0]
    col = lax.broadcasted_iota(jnp.int32, (1, Q_COLS), 1)
    tok = col & (SUBLANES - 1)
    qpos = past + tok
    kwin = slice(0, 2 * LANES)
    vwin = slice(V_OFF, V_OFF + 2 * LANES)
    k_lanes = KV_HEADS * HEAD_DIM
    blocks_per_page = PAGE_SIZE // SEL_BLOCK
    nb_step = blocks_per_page * p_cnt
    keys_step = PAGE_SIZE * p_cnt

    def attend(rows_bf16, valid):
        s = _dot(rows_bf16[:, kwin], qbd)
        return jnp.where(valid, s, NEG)

    @pl.when((pl.program_id(0) == 0) & (i == 0))
    def _():
        key_blk = lax.broadcasted_iota(jnp.int32, (keys_step, 1), 0) // SEL_BLOCK
        lane_blk = lax.broadcasted_iota(jnp.int32, (1, 2 * LANES - k_lanes), 1)
        kv_scr[:, k_lanes:2 * LANES] = (key_blk == lane_blk).astype(BF16)

    @pl.when(i == 0)
    def _():
        comp = comp_ref[0]
        ncp = comp.shape[0]
        cid = lax.broadcasted_iota(jnp.int32, (ncp, 1), 0)
        valid = (cid >= 1) & (cid <= n_cmp) & (cid * CMP_STRIDE + (CMP_BLOCK - CMP_STRIDE - 1) <= qpos)
        s = attend(comp, valid)
        e, den = _softmax_parts(s, valid, 0)
        p = e * (1.0 / jnp.maximum(den, 1e-30))
        oc_scr[...] = _dot_tn(p.astype(BF16), comp[:, vwin])
        nbp = bias_scr.shape[0]
        ratio = SEL_BLOCK // CMP_STRIDE
        p_scr[0:ncp, :] = p
        p_scr[ncp:, :] = jnp.zeros((p_scr.shape[0] - ncp, Q_COLS), F32)
        taps = [p_scr[pl.ds(k, nbp, stride=ratio), :] for k in range(ratio + 1)]
        imp = taps[0] + taps[ratio]
        for k in range(1, ratio):
            imp = imp + 2.0 * taps[k]
        imp = imp + pltpu.roll(imp, COLS_PER_HEAD, 1)
        imp = imp + pltpu.roll(imp, 2 * COLS_PER_HEAD, 1)
        blk = lax.broadcasted_iota(jnp.int32, (nbp, 1), 0)
        cur = qpos // SEL_BLOCK
        score = jnp.where(blk <= cur, imp, -1.0)
        score = jnp.where((blk == 0) | (blk == cur) | (blk == cur - 1), FORCE, score)
        score = jnp.where(blk < n_sel, score, PAD_SCORE)
        blk_f = blk.astype(F32)
        bias = jnp.full((nbp, Q_COLS), NEG, F32)
        for _ in range(k_top):
            best = jnp.max(score, axis=0, keepdims=True)
            first = jnp.min(jnp.where(score == best, blk_f, float(nbp)), axis=0, keepdims=True)
            hit = blk_f == first
            bias = jnp.where(hit, 0.0, bias)
            score = jnp.where(hit, TAKEN_SCORE, score)
        bias_scr[...] = bias
        m_scr[...] = jnp.full_like(m_scr, NEG)
        l_scr[...] = jnp.zeros_like(l_scr)
        acc_scr[...] = jnp.zeros_like(acc_scr)

    def as_col(v):
        return jnp.broadcast_to(v, (SUBLANES, Q_COLS)).T[:, 0:1]

    def partial_softmax(s, v_rows):
        m_g = jnp.max(s, axis=0, keepdims=True)
        e = jnp.exp2(s - m_g)
        return m_g, jnp.sum(e, axis=0, keepdims=True), _dot_tn(e.astype(BF16), v_rows)

    def accumulate(*parts):
        m_run = m_scr[0:1, :]
        m_new = m_run
        for m_g, _, _ in parts:
            m_new = jnp.maximum(m_new, m_g)
        w_run = jnp.exp2(m_run - m_new)
        l_new = w_run * l_scr[0:1, :]
        acc_new = as_col(w_run) * acc_scr[...]
        for m_g, l_g, acc_g in parts:
            w_g = jnp.exp2(m_g - m_new)
            l_new = l_new + w_g * l_g
            acc_new = acc_new + as_col(w_g) * acc_g
        m_scr[0:1, :] = m_new
        l_scr[0:1, :] = l_new
        acc_scr[...] = acc_new

    bias_step = bias_scr[pl.ds(pl.multiple_of(i * nb_step, nb_step), nb_step), :].astype(BF16)
    pad = 2 * LANES - k_lanes - nb_step
    q_parts = [qbd[:k_lanes], bias_step] + ([jnp.zeros((pad, Q_COLS), BF16)] if pad else [])
    q_step = jnp.concatenate(q_parts, axis=0)
    for p in range(p_cnt):
        rows_p = slice(p * PAGE_SIZE, (p + 1) * PAGE_SIZE)
        tiles = [page_refs[p][0, cb].astype(BF16).T for cb in range(KV_ROW // LANES)]
        kv_scr[rows_p, 0:LANES] = tiles[0]
        kv_scr[rows_p, LANES:k_lanes] = tiles[1][:, :k_lanes - LANES]
        kv_scr[rows_p, 2 * LANES:3 * LANES] = tiles[1]
        kv_scr[rows_p, 3 * LANES:4 * LANES] = tiles[2]
    accumulate(partial_softmax(_dot(kv_scr[:, 0:2 * LANES], q_step), kv_scr[:, 2 * LANES:4 * LANES]))

    @pl.when(i == n_steps - 1)
    def _():
        knew = knew_ref[0].astype(BF16)
        krow = lax.broadcasted_iota(jnp.int32, (t_new, 1), 0)
        new_blk = past // SEL_BLOCK
        s_new = _dot(knew[:, kwin], qbd) + bias_scr[new_blk:new_blk + 1, :]
        accumulate(partial_softmax(jnp.where(krow <= tok, s_new, NEG), knew[:, vwin]))
        o_s = acc_scr[...] * as_col(1.0 / l_scr[0:1, :])

        wbuf = jnp.concatenate([win_ref[0, cb].astype(BF16).T for cb in range(KV_ROW // LANES)], axis=1)
        wnew = wnew_ref[0].astype(BF16)
        buf = wbuf.shape[0]
        pos_b = past - buf + lax.broadcasted_iota(jnp.int32, (buf, 1), 0)
        pos_n = past + krow
        ok_b = (qpos - pos_b >= 0) & (qpos - pos_b < WINDOW) & (pos_b >= 0)
        ok_n = (qpos - pos_n >= 0) & (qpos - pos_n < WINDOW)
        s_b = attend(wbuf, ok_b)
        s_n = attend(wnew, ok_n)
        m_w = jnp.maximum(jnp.max(s_b, axis=0, keepdims=True), jnp.max(s_n, axis=0, keepdims=True))
        e_b = jnp.where(ok_b, jnp.exp2(s_b - m_w), 0.0)
        e_n = jnp.where(ok_n, jnp.exp2(s_n - m_w), 0.0)
        den = jnp.sum(e_b, axis=0, keepdims=True) + jnp.sum(e_n, axis=0, keepdims=True)
        o_w = _dot_tn(e_b.astype(BF16), wbuf[:, vwin]) + _dot_tn(e_n.astype(BF16), wnew[:, vwin])
        o_w = o_w * jnp.broadcast_to(1.0 / jnp.maximum(den, 1e-30), (SUBLANES, Q_COLS)).T[:, 0:1]
        o_c = oc_scr[...]

        gt = g_ref[0]
        for g in range(KV_HEADS):
            for r in range(GQA_GROUP):
                r0 = r * COLS_PER_HEAD + g * SUBLANES
                c0 = V_COL0 + g * HEAD_DIM
                gc = (g * GQA_GROUP + r) * N_BRANCH
                blk_o = [o[r0:r0 + t_new, c0:c0 + HEAD_DIM] for o in (o_c, o_s, o_w)]
                h0 = (g * GQA_GROUP + r) * HEAD_DIM
                o_ref[0, :, h0:h0 + HEAD_DIM] = (gt[:, gc:gc + 1] * blk_o[0] + gt[:, gc + 1:gc + 2] * blk_o[1]
                                                 + gt[:, gc + 2:gc + 3] * blk_o[2])


def _nsa_sample(cache_slc, page_table, qbd, comp_tm, knew, win_t, wnew, gates, past):
    db, n_pages = page_table.shape
    t_new = knew.shape[1]
    assert t_new == SUBLANES and KV_HEADS * t_new <= COLS_PER_HEAD and past % SEL_BLOCK == 0 and t_new <= SEL_BLOCK
    p_cnt = min(PAGES_PER_STEP, n_pages)
    assert n_pages % p_cnt == 0
    n_steps = n_pages // p_cnt
    ncp = comp_tm.shape[1]
    n_str = -(-(past + t_new) // CMP_STRIDE)
    n_cmp = n_str - CMP_SPAN + 1
    n_sel = -(-(past + t_new) // SEL_BLOCK)
    blocks_step = (PAGE_SIZE // SEL_BLOCK) * p_cnt
    nbp = -(-max(n_sel, n_steps * blocks_step + 1) // SUBLANES) * SUBLANES
    keys_step = PAGE_SIZE * p_cnt
    assert blocks_step <= 2 * LANES - KV_HEADS * HEAD_DIM
    assert CMP_SPAN == 2
    p_rows = max(ncp, (SEL_BLOCK // CMP_STRIDE) * nbp) + SUBLANES
    buf = win_t.shape[-1]
    assert buf % LANES == 0

    def page_spec(p):
        return pl.BlockSpec((1, KV_ROW // LANES, LANES, PAGE_SIZE),
                            lambda b, i, pt: (pt[b * n_pages + i * p_cnt + p], 0, 0, 0))

    def per_b(*shape):
        return pl.BlockSpec((1,) + shape, lambda b, i, pt: (b,) + (0,) * len(shape))

    in_specs = [page_spec(p) for p in range(p_cnt)]
    in_specs += [per_b(2 * LANES, Q_COLS), per_b(ncp, KV_ROW),
                 per_b(t_new, KV_ROW), per_b(KV_ROW // LANES, LANES, buf), per_b(t_new, KV_ROW), per_b(t_new, LANES)]
    return pl.pallas_call(
        functools.partial(_nsa_sample_kernel, n_pages_step=p_cnt, n_steps=n_steps, t_new=t_new, past=past,
                          n_cmp=n_cmp, n_sel=n_sel, k_top=min(N_SEL, n_sel)),
        grid_spec=pltpu.PrefetchScalarGridSpec(
            num_scalar_prefetch=1, grid=(db, n_steps), in_specs=in_specs,
            out_specs=per_b(t_new, NSA_WIDTH),
            scratch_shapes=[pltpu.VMEM((keys_step, 4 * LANES), BF16),
                            pltpu.VMEM((nbp, Q_COLS), F32), pltpu.VMEM((p_rows, Q_COLS), F32),
                            pltpu.VMEM((SUBLANES, Q_COLS), F32), pltpu.VMEM((SUBLANES, Q_COLS), F32),
                            pltpu.VMEM((Q_COLS, 2 * LANES), F32), pltpu.VMEM((Q_COLS, 2 * LANES), F32)]),
        out_shape=jax.ShapeDtypeStruct((db, t_new, NSA_WIDTH), F32),
        compiler_params=_params("arbitrary", "arbitrary"),
        name="nsa_sample",
    )(page_table.reshape(-1), *([cache_slc] * p_cnt), qbd, comp_tm, knew, win_t, wnew, gates)


def _out_kernel(x_ref, *refs, fused_pool):
    if fused_pool:
        uprev_ref, u_ref, wbd_ref, ps_ref = refs[:4]
        zp_ref, no_ref, zn_ref, gate_ref, gnp_ref, gnn_ref, w_ref, fn_ref, y_ref = refs[4:]
        i = pl.program_id(1)
        tm = u_ref.shape[1]
        halo = jnp.where(i == 0, 0.0, uprev_ref[0])
        pos = i * tm + lax.broadcasted_iota(jnp.int32, (tm, 1), 0)
        pool_o = _pool_rows(halo, u_ref[0], pos, wbd_ref[...], ps_ref[...])
    else:
        po_ref, zp_ref, no_ref, zn_ref, gate_ref, gnp_ref, gnn_ref, w_ref, fn_ref, y_ref = refs
        pool_o = po_ref[0]
    zp = zp_ref[0]
    zn = zn_ref[0]
    mp = _rms(pool_o, gnp_ref[...]) * (zp * _sigmoid(zp))
    mn = _rms(no_ref[0], gnn_ref[...]) * (zn * _sigmoid(zn))
    m = jnp.concatenate([mp, mn], axis=-1).astype(BF16)
    xo = x_ref[0] + gate_ref[0] * _dot(m, w_ref[...])
    y_ref[0] = _rms(xo, fn_ref[...])


def _out(x, pool, zp, nsa_o, zn, gate, gn_pool, gn_nsa, w_out_bf16, final_norm, tm):
    nb, t, d = x.shape
    r = gate.shape[1]
    assert t % tm == 0 and r in (1, t)
    fused_pool = isinstance(pool, tuple)
    if r == 1:
        gate_spec = pl.BlockSpec((1, 1, d), lambda b, i: (b, 0, 0))
    else:
        gate_spec = pl.BlockSpec((1, tm, d), lambda b, i: (b, i, 0))

    def tok(width):
        return pl.BlockSpec((1, tm, width), lambda b, i: (b, i, 0))

    def const(*shape):
        return pl.BlockSpec(shape, lambda b, i: (0,) * len(shape))

    if fused_pool:
        u, wbd, pool_scale = pool
        assert tm % POOL_HALO == 0
        ratio = tm // POOL_HALO
        pool_specs = [pl.BlockSpec((1, POOL_HALO, POOL_WIDTH), lambda b, i: (b, jnp.maximum(i * ratio - 1, 0), 0)),
                      tok(POOL_WIDTH), const(POOL_WIDTH, POOL_WIDTH), const(1, POOL_WIDTH)]
        pool_args = [u, u, wbd, pool_scale.reshape(1, -1)]
    else:
        pool_specs, pool_args = [tok(POOL_WIDTH)], [pool]
    return pl.pallas_call(
        functools.partial(_out_kernel, fused_pool=fused_pool),
        grid=(nb, t // tm),
        in_specs=[tok(d)] + pool_specs + [tok(POOL_WIDTH), tok(NSA_WIDTH), tok(NSA_WIDTH), gate_spec,
                                          const(1, POOL_WIDTH), const(1, NSA_WIDTH), const(d, d), const(1, d)],
        out_specs=tok(d),
        out_shape=jax.ShapeDtypeStruct((nb, t, d), F32),
        compiler_params=_params("arbitrary", "arbitrary"),
        name="out",
    )(x, *pool_args, zp, nsa_o, zn, gate, gn_pool.reshape(1, -1), gn_nsa.reshape(1, -1), w_out_bf16,
      final_norm.reshape(1, d))


def _pages_t(pages):
    n = pages.shape[0]
    return pages.transpose(0, 2, 3, 4, 1).reshape(n, KV_ROW // LANES, LANES, PAGE_SIZE)


def _block_diag(pool_w):
    n, c, _ = pool_w.shape
    eye = jnp.eye(n, dtype=pool_w.dtype)
    return (eye[:, None, :, None] * pool_w[:, :, None, :]).reshape(n * c, n * c)


def _first_layer(phi_w1, phi_pe):
    w = phi_w1.reshape(CMP_SPAN, CMP_STRIDE, 2, HEAD_DIM, CMP_HID)
    w1cat = w.transpose(2, 1, 3, 0, 4).reshape(2, CMP_STRIDE * HEAD_DIM, CMP_SPAN * CMP_HID)
    pe = phi_pe.reshape(CMP_SPAN, CMP_STRIDE, 2, HEAD_DIM).transpose(2, 0, 1, 3).reshape(2, CMP_SPAN, -1)
    pe2 = jnp.concatenate([pe, jnp.zeros((2, SUBLANES - CMP_SPAN, pe.shape[-1]), pe.dtype)], axis=1)
    return w1cat.astype(BF16), pe2


def kernel(x_prompt, x_sample, c_prompt, c_sample, cache_cmp_kv, cache_slc_kv, state_win_kv, state_pool, page_table, norm_w, w_ada, b_ada, w_in, pool_w, pool_scale, phi_w1, phi_pe, phi_w2, gn_pool, gn_nsa, w_out, final_norm):
    b_sz, s_len, d = x_prompt.shape
    db, t_new, _ = x_sample.shape
    depth = norm_w.shape[0]
    assert depth == 1 and w_in.shape[-1] == IN_WIDTH and d // 4 == POOL_WIDTH
    n_phys = cache_cmp_kv.shape[1]
    n_pages = page_table.shape[1]
    past = n_pages * PAGE_SIZE
    lyr = 0

    n_c = b_sz + db
    c_all = jnp.concatenate([c_prompt, c_sample, jnp.zeros((-n_c % SUBLANES, d), F32)], axis=0)
    ada = _ada(c_all, w_ada[lyr], b_ada[lyr])
    shift, scale, gate = ada[:, :d], ada[:, d:2 * d], ada[:, 2 * d:]

    w_in_t = w_in[lyr].T
    w_main = _cast_rows(w_in_t, IN_MAIN, 5 * LANES, BF16)
    w_gate = jnp.pad(w_in_t[IN_MAIN:].astype(BF16), ((0, LANES - GATE_COLS), (0, 0)))
    w_out_bf16 = w_out[lyr].astype(BF16)
    wbd = _block_diag(pool_w[lyr]).astype(BF16)
    w1cat, pe2 = _first_layer(phi_w1[lyr], phi_pe[lyr])
    w2 = phi_w2[lyr].astype(BF16)
    n_tok_s = db * t_new

    def per_token(v):
        return jnp.repeat(v[b_sz:n_c], t_new, axis=0)[None]

    tm = min(ROW_TILE, s_len)
    tq = min(ATTN_TILE, s_len)
    (u_p, zp_p, zn_p, q_t, ks_aug, kw_p, vs_t, vw_t, g_t, kvc_t, kvs_t, kvw_t) = _in_proj(
        x_prompt, scale[:b_sz, None], shift[:b_sz, None], norm_w[lyr], w_main, w_gate, tm, True)
    pages_p = s_len // PAGE_SIZE
    in_order = jnp.tile(jnp.arange(pages_p, dtype=jnp.int32), (b_sz, 1))
    comp_k, comp_vt = _compress(kvc_t, in_order, w1cat, pe2, w2, None, False, True)
    nsa_p = _nsa_prompt(q_t, comp_k, comp_vt, ks_aug, vs_t, kw_p, vw_t, g_t, tq=tq, kb=tq)
    y_p = _out(x_prompt, (u_p, wbd, pool_scale[lyr]), zp_p, nsa_p, zn_p, gate[:b_sz, None], gn_pool[lyr],
               gn_nsa[lyr], w_out_bf16,
               final_norm, min(OUT_TILE, s_len))

    (u_s, zp_s, zn_s, kvc_s, kvs_s, kvw_s, q_s, g_s) = _in_proj(
        x_sample.reshape(1, n_tok_s, d), per_token(scale), per_token(shift), norm_w[lyr], w_main, w_gate, n_tok_s,
        False)
    u_s3 = u_s.reshape(db, t_new, POOL_WIDTH)
    hist = jnp.concatenate([jnp.zeros((db, POOL_HALO - state_pool.shape[2], POOL_WIDTH), F32), state_pool[lyr]], axis=1)
    pool_s = _pool(hist, u_s3, wbd, pool_scale[lyr], past, t_new)
    kvc_s3 = kvc_s.reshape(db, t_new, KV_ROW)
    kvs_s3 = kvs_s.reshape(db, t_new, KV_ROW)
    kvw_s3 = kvw_s.reshape(db, t_new, KV_ROW)
    xnew = jnp.pad(kvc_s3, ((0, 0), (0, CMP_STRIDE - t_new), (0, 0)))
    xnew = xnew.reshape(db, CMP_STRIDE, N_KV_SLABS, HEAD_DIM).transpose(2, 0, 1, 3).reshape(N_KV_SLABS, db, -1)
    a1new = _new_chunk_first_layer(xnew, w1cat).transpose(1, 0, 2)
    comp_s = _compress(_pages_t(cache_cmp_kv[lyr]), page_table, w1cat, pe2, w2, a1new, True, False)
    q5 = q_s.reshape(db, t_new, KV_HEADS, GQA_GROUP, HEAD_DIM)
    eye = jnp.eye(KV_HEADS, dtype=BF16)
    qbd = q5.transpose(0, 2, 4, 3, 1)[:, :, :, :, None, :] * eye[None, :, None, None, :, None]
    qbd = jnp.pad(qbd.reshape(db, KV_HEADS * HEAD_DIM, GQA_GROUP, KV_HEADS * t_new),
                  ((0, 0), (0, 2 * LANES - KV_HEADS * HEAD_DIM), (0, 0), (0, COLS_PER_HEAD - KV_HEADS * t_new)))
    qbd = qbd.reshape(db, 2 * LANES, Q_COLS)
    g_s3 = g_s.reshape(db, t_new, LANES)
    buf = state_win_kv.shape[2]
    win_t = state_win_kv[lyr].transpose(0, 2, 3, 4, 1).reshape(db, KV_ROW // LANES, LANES, buf)
    nsa_s = _nsa_sample(_pages_t(cache_slc_kv[lyr]), page_table, qbd, comp_s, kvs_s3,
                        win_t, kvw_s3, g_s3, past)
    y_s = _out(x_sample.reshape(1, n_tok_s, d), pool_s.reshape(1, n_tok_s, POOL_WIDTH), zp_s,
               nsa_s.reshape(1, n_tok_s, NSA_WIDTH), zn_s, per_token(gate), gn_pool[lyr], gn_nsa[lyr], w_out_bf16,
               final_norm, n_tok_s).reshape(db, t_new, d)

    kv_shape = (2, KV_HEADS, HEAD_DIM)
    win_len = min(WINDOW, s_len)
    new_win_s = jnp.concatenate([state_win_kv[lyr], kvw_s3.reshape((db, t_new) + kv_shape)], axis=1)[:, -buf:]
    hist_len = state_pool.shape[2]
    new_pool_s = jnp.concatenate([state_pool[lyr], u_s3], axis=1)[:, -hist_len:]

    def rows_from_t(v_t):
        return v_t.reshape((v_t.shape[0],) + kv_shape + (v_t.shape[-1],)).transpose(0, 4, 1, 2, 3)[None]

    return (y_p, y_s,
            rows_from_t(kvc_t), kvc_s3.reshape((1, db, t_new) + kv_shape),
            rows_from_t(kvs_t), kvs_s3.reshape((1, db, t_new) + kv_shape),
            rows_from_t(kvw_t[..., s_len - win_len:]), new_win_s[None],
            u_p[:, s_len - hist_len:][None], new_pool_s[None])
```

```python
import functools

import numpy as np
import jax
import jax.numpy as jnp
from jax import lax
from jax.experimental import pallas as pl
from jax.experimental.pallas import tpu as pltpu

F32 = jnp.float32
BF16 = jnp.bfloat16

HEAD_DIM = 64
GQA_GROUP = 4
KV_HEADS = 3
N_HEADS = KV_HEADS * GQA_GROUP
N_KV_SLABS = 2 * KV_HEADS
KV_ROW = N_KV_SLABS * HEAD_DIM
N_BRANCH = 3
POOL_WINDOWS = (2, 4, 8, 16)
POOL_HALO = 16
CMP_BLOCK = 32
CMP_STRIDE = 16
CMP_SPAN = CMP_BLOCK // CMP_STRIDE
CMP_HID = 2 * HEAD_DIM
SEL_BLOCK = 64
N_SEL = 16
WINDOW = 512
PAGE_SIZE = 128
CHUNKS_PER_PAGE = PAGE_SIZE // CMP_STRIDE
CHUNK_PITCH = 24
EPS = 1e-6
NEG = -1e30
FORCE = 1e4
LOG2_E = 1.4426950408889634
VT_ROWS = 80
PAD_SCORE = -1e38
TAKEN_SCORE = -3e38

LANES = 128
SUBLANES = 8
VMEM_LIMIT_BYTES = 56 * 1024 * 1024

ROW_TILE = 512
OUT_TILE = 1024
ATTN_TILE = 1024
PAGES_PER_STEP = 32
BAND_SUB = 256


def _dot(a, b):
    return jnp.dot(a, b, preferred_element_type=F32)


def _dot_nt(a, b):
    return lax.dot_general(a, b, (((1,), (1,)), ((), ())), preferred_element_type=F32)


def _dot_tn(a, b):
    return lax.dot_general(a, b, (((0,), (0,)), ((), ())), preferred_element_type=F32)


def _sigmoid(v):
    return 1.0 / (1.0 + jnp.exp(-v))


def _rms(v, g):
    return v * lax.rsqrt(jnp.mean(v * v, axis=-1, keepdims=True) + EPS) * g


def _split3(v):
    hi = v.astype(BF16)
    r1 = v - hi.astype(F32)
    mid = r1.astype(BF16)
    lo = (r1 - mid.astype(F32)).astype(BF16)
    return hi, mid, lo


def _params(*sem):
    return pltpu.CompilerParams(dimension_semantics=sem, vmem_limit_bytes=VMEM_LIMIT_BYTES)


def _ada_kernel(c_ref, w_ref, b_ref, o_ref):
    c = c_ref[...]
    a = (c * _sigmoid(c)).astype(BF16)
    o_ref[...] = _dot(a, w_ref[...].astype(BF16)) + b_ref[...]


def _ada(c, w_ada, b_ada):
    m, d = c.shape
    n = w_ada.shape[1]
    tn = 512
    return pl.pallas_call(
        _ada_kernel,
        grid=(n // tn,),
        in_specs=[pl.BlockSpec((m, d), lambda j: (0, 0)),
                  pl.BlockSpec((d, tn), lambda j: (0, j)),
                  pl.BlockSpec((1, tn), lambda j: (0, j))],
        out_specs=pl.BlockSpec((m, tn), lambda j: (0, j)),
        out_shape=jax.ShapeDtypeStruct((m, n), F32),
        compiler_params=_params("arbitrary"),
        name="ada",
    )(c, w_ada, b_ada.reshape(1, n))


POOL_WIDTH = 256
NSA_WIDTH = N_HEADS * HEAD_DIM
GATE_COLS = N_BRANCH * N_HEADS
MAX_SEL_BLOCKS = LANES - HEAD_DIM
_SEG = {}
_off = 0
for _name, _w in (("u", POOL_WIDTH), ("zp", POOL_WIDTH), ("q", NSA_WIDTH), ("zn", NSA_WIDTH),
                  ("kvc", KV_ROW), ("kvs", KV_ROW), ("kvw", KV_ROW)):
    _SEG[_name] = (_off, _w)
    _off += _w
IN_MAIN = _off
IN_WIDTH = IN_MAIN + GATE_COLS


GATE_ROWS = 16


def _in_proj_kernel(x_ref, sc_ref, sh_ref, nw_ref, w_ref, wg_ref, u_ref, zp_ref, zn_ref, *rest, attn_layouts):
    tm = x_ref.shape[1]
    h = (_rms(x_ref[0], nw_ref[...]) * (1.0 + sc_ref[0]) + sh_ref[0]).astype(BF16)

    def seg(name):
        s, w = _SEG[name]
        return _dot_nt(h, w_ref[s:s + w, :])

    u_ref[0] = seg("u")
    zp_ref[0] = seg("zp")
    zn_ref[0] = seg("zn")
    q = seg("q") * (HEAD_DIM ** -0.5 * LOG2_E)
    kvc = seg("kvc")
    kvs = seg("kvs")
    kvw = seg("kvw")
    sig = _sigmoid(_dot_nt(h, wg_ref[...]))
    if not attn_layouts:
        kvc_ref, kvs_ref, kvw_ref, q_ref, g_ref = rest
        kvc_ref[0] = kvc
        kvs_ref[0] = kvs
        kvw_ref[0] = kvw
        q_ref[0] = q.astype(BF16)
        g_ref[0] = sig
        return

    qt_ref, ks_aug_ref, kw_ref, vst_ref, vwt_ref, gt_ref, kvct_ref, kvst_ref, kvwt_ref = rest
    qt_ref[0] = q.T.astype(BF16).reshape(N_HEADS, HEAD_DIM, tm)
    kvs_t32 = kvs.T
    kvw_t32 = kvw.T
    kvct_ref[0] = kvc.T.reshape(KV_ROW // LANES, LANES, tm)
    kvst_ref[0] = kvs_t32.reshape(KV_ROW // LANES, LANES, tm)
    kvwt_ref[0] = kvw_t32.reshape(KV_ROW // LANES, LANES, tm)
    pos = pl.program_id(1) * tm + lax.broadcasted_iota(jnp.int32, (tm, 1), 0)
    onehot = (lax.broadcasted_iota(jnp.int32, (1, MAX_SEL_BLOCKS), 1) == pos // SEL_BLOCK).astype(BF16)
    kvs_t = kvs_t32.astype(BF16)
    kvw_t = kvw_t32.astype(BF16)
    v0 = KV_HEADS * HEAD_DIM
    tail = (lax.broadcasted_iota(jnp.int32, (VT_ROWS - HEAD_DIM, LANES), 0) == 0).astype(BF16)
    for g in range(KV_HEADS):
        ks_aug_ref[0, g] = jnp.concatenate([kvs[:, g * HEAD_DIM:(g + 1) * HEAD_DIM].astype(BF16), onehot], axis=1)
        kw_ref[0, g] = kvw[:, g * HEAD_DIM:(g + 1) * HEAD_DIM].astype(BF16)
        for c in range(tm // LANES):
            sl = (slice(v0 + g * HEAD_DIM, v0 + (g + 1) * HEAD_DIM), slice(c * LANES, (c + 1) * LANES))
            vst_ref[0, g, c] = jnp.concatenate([kvs_t[sl], tail], axis=0)
            vwt_ref[0, g, c] = jnp.concatenate([kvw_t[sl], tail], axis=0)
        per_group = GQA_GROUP * N_BRANCH
        rolled = sig if g == 0 else pltpu.roll(sig, LANES - per_group * g, 1)
        gt_ref[0, g] = rolled.T[:GATE_ROWS, :]


def _cast_kernel(w_ref, o_ref):
    o_ref[...] = w_ref[...].astype(o_ref.dtype)


def _cast_rows(w, n_rows, tile, dtype):
    cols = w.shape[1]
    assert n_rows % tile == 0 and tile % SUBLANES == 0 and n_rows <= w.shape[0] and cols % LANES == 0
    return pl.pallas_call(
        _cast_kernel,
        grid=(n_rows // tile,),
        in_specs=[pl.BlockSpec((tile, cols), lambda j: (j, 0))],
        out_specs=pl.BlockSpec((tile, cols), lambda j: (j, 0)),
        out_shape=jax.ShapeDtypeStruct((n_rows, cols), dtype),
        compiler_params=_params("arbitrary"),
        name="cast_rows",
    )(w)


def _in_proj(x, scale, shift, norm_w, w_main, w_gate, tm, attn_layouts):
    nb, t, d = x.shape
    r = scale.shape[1]
    assert t % tm == 0 and r in (1, t) and (tm % LANES == 0 or not attn_layouts)
    if r == 1:
        mod_spec = pl.BlockSpec((1, 1, d), lambda b, i: (b, 0, 0))
    else:
        mod_spec = pl.BlockSpec((1, tm, d), lambda b, i: (b, i, 0))

    def tok(width):
        return pl.BlockSpec((1, tm, width), lambda b, i: (b, i, 0))

    def hm(n, width=HEAD_DIM):
        return pl.BlockSpec((1, n, tm, width), lambda b, i: (b, 0, i, 0))

    def sd(*shape, dtype=F32):
        return jax.ShapeDtypeStruct(shape, dtype)

    out_specs = [tok(POOL_WIDTH), tok(POOL_WIDTH), tok(NSA_WIDTH)]
    out_shape = [sd(nb, t, POOL_WIDTH), sd(nb, t, POOL_WIDTH), sd(nb, t, NSA_WIDTH)]
    if attn_layouts:
        lane_tiles = tm // LANES
        vt_spec = pl.BlockSpec((1, KV_HEADS, lane_tiles, VT_ROWS, LANES), lambda b, i: (b, 0, i, 0, 0))
        vt_shape = sd(nb, KV_HEADS, t // LANES, VT_ROWS, LANES, dtype=BF16)
        rows_t_spec = pl.BlockSpec((1, KV_ROW // LANES, LANES, tm), lambda b, i: (b, 0, 0, i))
        rows_t_shape = sd(nb, KV_ROW // LANES, LANES, t)
        out_specs += [pl.BlockSpec((1, N_HEADS, HEAD_DIM, tm), lambda b, i: (b, 0, 0, i)),
                      hm(KV_HEADS, HEAD_DIM + MAX_SEL_BLOCKS), hm(KV_HEADS), vt_spec, vt_spec,
                      pl.BlockSpec((1, KV_HEADS, GATE_ROWS, tm), lambda b, i: (b, 0, 0, i)),
                      rows_t_spec, rows_t_spec, rows_t_spec]
        out_shape += [sd(nb, N_HEADS, HEAD_DIM, t, dtype=BF16),
                      sd(nb, KV_HEADS, t, HEAD_DIM + MAX_SEL_BLOCKS, dtype=BF16),
                      sd(nb, KV_HEADS, t, HEAD_DIM, dtype=BF16), vt_shape, vt_shape,
                      sd(nb, KV_HEADS, GATE_ROWS, t), rows_t_shape, rows_t_shape, rows_t_shape]
    else:
        out_specs += [tok(KV_ROW), tok(KV_ROW), tok(KV_ROW), tok(NSA_WIDTH), tok(LANES)]
        out_shape += [sd(nb, t, KV_ROW), sd(nb, t, KV_ROW), sd(nb, t, KV_ROW),
                      sd(nb, t, NSA_WIDTH, dtype=BF16), sd(nb, t, LANES)]
    return pl.pallas_call(
        functools.partial(_in_proj_kernel, attn_layouts=attn_layouts),
        grid=(nb, t // tm),
        in_specs=[tok(d), mod_spec, mod_spec,
                  pl.BlockSpec((1, d), lambda b, i: (0, 0)),
                  pl.BlockSpec((IN_MAIN, d), lambda b, i: (0, 0)),
                  pl.BlockSpec((LANES, d), lambda b, i: (0, 0))],
        out_specs=out_specs,
        out_shape=out_shape,
        compiler_params=_params("arbitrary", "arbitrary"),
        name="in_proj",
    )(x, scale, shift, norm_w.reshape(1, d), w_main, w_gate)


def _pool_rows(halo, u, pos, wbd, ps):
    c = u.shape[-1]
    grp = lax.broadcasted_iota(jnp.int32, (1, c), 1) // (c // len(POOL_WINDOWS))
    sums = []
    s = jnp.concatenate([halo, u], axis=0)
    for w in POOL_WINDOWS:
        s = s + pltpu.roll(s, w // 2, 0)
        sums.append(s[POOL_HALO:])
    tot = sums[-1]
    win = jnp.full((1, c), float(POOL_WINDOWS[-1]), F32)
    for gi in range(len(POOL_WINDOWS) - 2, -1, -1):
        tot = jnp.where(grp == gi, sums[gi], tot)
        win = jnp.where(grp == gi, float(POOL_WINDOWS[gi]), win)
    cnt = jnp.minimum(win, (pos + 1).astype(F32))
    return _dot((tot / cnt - u).astype(BF16), wbd) * ps


def _pool_kernel(hist_ref, uprev_ref, u_ref, wbd_ref, ps_ref, o_ref, *, pos0, tp):
    i = pl.program_id(1)
    pos = pos0 + i * tp + lax.broadcasted_iota(jnp.int32, (tp, 1), 0)
    for bi in range(u_ref.shape[0]):
        halo = jnp.where(i == 0, hist_ref[bi], uprev_ref[bi])
        o_ref[bi] = _pool_rows(halo, u_ref[bi], pos, wbd_ref[...], ps_ref[...])


def _pool(hist, u, wbd, pool_scale, pos0, tp):
    nb, t, c = u.shape
    assert t % tp == 0
    if t >= POOL_HALO:
        assert tp % POOL_HALO == 0
        uprev, ratio, bb = u, tp // POOL_HALO, 1
        prev_spec = pl.BlockSpec((bb, POOL_HALO, c), lambda b, i: (b, jnp.maximum(i * ratio - 1, 0), 0))
    else:
        assert t == tp
        uprev, bb = hist, nb
        prev_spec = pl.BlockSpec((bb, POOL_HALO, c), lambda b, i: (b, 0, 0))
    return pl.pallas_call(
        functools.partial(_pool_kernel, pos0=pos0, tp=tp),
        grid=(nb // bb, t // tp),
        in_specs=[pl.BlockSpec((bb, POOL_HALO, c), lambda b, i: (b, 0, 0)), prev_spec,
                  pl.BlockSpec((bb, tp, c), lambda b, i: (b, i, 0)),
                  pl.BlockSpec((c, c), lambda b, i: (0, 0)),
                  pl.BlockSpec((1, c), lambda b, i: (0, 0))],
        out_specs=pl.BlockSpec((bb, tp, c), lambda b, i: (b, i, 0)),
        out_shape=jax.ShapeDtypeStruct((nb, t, c), F32),
        compiler_params=_params("arbitrary", "arbitrary"),
        name="pool",
    )(hist, uprev, u, wbd, pool_scale.reshape(1, c))


def _gelu_tanh(v):
    return 0.5 * v * (1.0 + jnp.tanh(np.sqrt(2.0 / np.pi).astype(np.float32) * (v + 0.044715 * (v * v * v))))


def _compress_kernel(pt_ref, *refs, n_pages_step, n_steps, has_new, token_major):
    del pt_ref
    p_cnt = n_pages_step
    page_refs = refs[:p_cnt]
    w1p_ref, w1_ref, pe_ref, w2_ref = refs[p_cnt:p_cnt + 4]
    k = p_cnt + 4
    a1new_ref = None
    if has_new:
        a1new_ref = refs[k]
        k += 1
    if token_major:
        out_ref = refs[k]
        k += 1
    else:
        outk_ref, outvt_ref = refs[k:k + 2]
        k += 2
    slab_scr, carry_scr, bias_scr = refs[k:k + 3]
    i = pl.program_id(1)
    rows = CHUNKS_PER_PAGE * p_cnt
    row_id = lax.broadcasted_iota(jnp.int32, (rows, 1), 0)

    @pl.when(i == 0)
    def _():
        carry_scr[...] = jnp.zeros_like(carry_scr)
        for kk in range(2):
            pb = _dot(pe_ref[kk].astype(BF16), w1_ref[kk])
            bias_scr[kk] = pb[0:1, :CMP_HID] + pb[1:2, CMP_HID:]

    def store(kg, comp):
        if token_major:
            out_ref[0, :, kg * HEAD_DIM:(kg + 1) * HEAD_DIM] = comp.astype(BF16)
        elif kg < KV_HEADS:
            outk_ref[0, kg] = comp.astype(BF16)
        else:
            wide = jnp.concatenate([comp, jnp.zeros((rows, LANES - HEAD_DIM), F32)], axis=1)
            outvt_ref[0, kg - KV_HEADS] = wide.T[:HEAD_DIM, :].astype(BF16)

    def finish(kg, a0_prev, a1):
        kk = kg // KV_HEADS
        pre = a0_prev + a1 + bias_scr[kk]
        return _dot(_gelu_tanh(pre).astype(BF16), w2_ref[kk])

    def main():
        for cb in range(KV_ROW // LANES):
            for p in range(p_cnt):
                rows_t = page_refs[p][0, cb].astype(BF16).T.astype(F32)
                for ch in range(CHUNKS_PER_PAGE):
                    r0 = (p * CHUNKS_PER_PAGE + ch) * CHUNK_PITCH
                    slab_scr[cb, r0:r0 + CMP_STRIDE, :] = rows_t[ch * CMP_STRIDE:(ch + 1) * CMP_STRIDE, :]
            x = jnp.concatenate([slab_scr[cb, pl.ds(j, rows, stride=CHUNK_PITCH), :] for j in range(CMP_STRIDE)],
                                axis=1).astype(BF16)
            a_both = _dot(x, w1p_ref[cb])
            for half in range(2):
                kg = 2 * cb + half
                a = a_both[:, half * 2 * CMP_HID:(half + 1) * 2 * CMP_HID]
                a0 = a[:, :CMP_HID]
                a0_prev = jnp.where(row_id == 0, carry_scr[kg], pltpu.roll(a0, 1, 0))
                carry_scr[kg] = a0[rows - 1:rows, :]
                store(kg, finish(kg, a0_prev, a[:, CMP_HID:]))

    if not has_new:
        main()
    else:
        pl.when(i < n_steps)(main)

        @pl.when(i == n_steps)
        def _():
            for kg in range(N_KV_SLABS):
                comp = finish(kg, carry_scr[kg], a1new_ref[0, kg:kg + 1, :])
                store(kg, jnp.where(row_id == 0, jnp.broadcast_to(comp, (rows, HEAD_DIM)), 0.0))


def _new_chunk_kernel(x_ref, w1_ref, o_ref):
    for kg in range(N_KV_SLABS):
        o_ref[kg] = _dot(x_ref[kg].astype(BF16), w1_ref[kg // KV_HEADS])[:, CMP_HID:]


def _new_chunk_first_layer(xnew, w1cat):
    n_slab, nb, xw = xnew.shape
    return pl.pallas_call(
        _new_chunk_kernel,
        out_shape=jax.ShapeDtypeStruct((n_slab, nb, CMP_HID), F32),
        compiler_params=pltpu.CompilerParams(vmem_limit_bytes=VMEM_LIMIT_BYTES),
        name="new_chunk",
    )(xnew, w1cat)


def _compress(pages, page_table, w1cat, pe2, w2, a1new, token_major, per_batch):
    nb, n_pages = page_table.shape
    p_cnt = min(PAGES_PER_STEP, n_pages)
    assert n_pages % p_cnt == 0
    n_steps = n_pages // p_cnt
    has_new = a1new is not None
    rows = CHUNKS_PER_PAGE * p_cnt
    grid_steps = n_steps + (1 if has_new else 0)
    tot_rows = rows * grid_steps
    xw = CMP_STRIDE * HEAD_DIM

    def page_spec(p):
        def imap(b, i, pt):
            step = jnp.minimum(i, n_steps - 1)
            page = pt[b * n_pages + step * p_cnt + p]
            return (b, 0, 0, page) if per_batch else (page, 0, 0, 0)
        return pl.BlockSpec((1, KV_ROW // LANES, LANES, PAGE_SIZE), imap)

    n_cb = KV_ROW // LANES
    w4 = w1cat.reshape(2, CMP_STRIDE, HEAD_DIM, 2 * CMP_HID)
    zero = jnp.zeros_like(w4[0])
    w1p = jnp.stack([jnp.concatenate([jnp.concatenate([w4[(2 * cb) // KV_HEADS], zero], axis=2),
                                      jnp.concatenate([zero, w4[(2 * cb + 1) // KV_HEADS]], axis=2)], axis=1)
                     for cb in range(n_cb)]).reshape(n_cb, CMP_STRIDE * LANES, 4 * CMP_HID)
    in_specs = [page_spec(p) for p in range(p_cnt)]
    in_specs += [pl.BlockSpec((n_cb, CMP_STRIDE * LANES, 4 * CMP_HID), lambda b, i, pt: (0, 0, 0)),
                 pl.BlockSpec((2, xw, 2 * CMP_HID), lambda b, i, pt: (0, 0, 0)),
                 pl.BlockSpec((2, SUBLANES, xw), lambda b, i, pt: (0, 0, 0)),
                 pl.BlockSpec((2, CMP_HID, HEAD_DIM), lambda b, i, pt: (0, 0, 0))]
    args = [pages] * p_cnt + [w1p, w1cat, pe2, w2]
    if has_new:
        in_specs.append(pl.BlockSpec((1, N_KV_SLABS, CMP_HID), lambda b, i, pt: (b, 0, 0)))
        args.append(a1new)
    if token_major:
        out_spec = pl.BlockSpec((1, rows, KV_ROW), lambda b, i, pt: (b, i, 0))
        out_shape = jax.ShapeDtypeStruct((nb, tot_rows, KV_ROW), BF16)
    else:
        assert rows % LANES == 0
        out_spec = [pl.BlockSpec((1, KV_HEADS, rows, HEAD_DIM), lambda b, i, pt: (b, 0, i, 0)),
                    pl.BlockSpec((1, KV_HEADS, HEAD_DIM, rows), lambda b, i, pt: (b, 0, 0, i))]
        out_shape = [jax.ShapeDtypeStruct((nb, KV_HEADS, tot_rows, HEAD_DIM), BF16),
                     jax.ShapeDtypeStruct((nb, KV_HEADS, HEAD_DIM, tot_rows), BF16)]
    return pl.pallas_call(
        functools.partial(_compress_kernel, n_pages_step=p_cnt, n_steps=n_steps, has_new=has_new,
                          token_major=token_major),
        grid_spec=pltpu.PrefetchScalarGridSpec(
            num_scalar_prefetch=1, grid=(nb, grid_steps), in_specs=in_specs, out_specs=out_spec,
            scratch_shapes=[pltpu.VMEM((n_cb, rows * CHUNK_PITCH, LANES), F32),
                            pltpu.VMEM((N_KV_SLABS, 1, CMP_HID), F32),
                            pltpu.VMEM((2, 1, CMP_HID), F32)]),
        out_shape=out_shape,
        compiler_params=_params("arbitrary", "arbitrary"),
        name="compress_new" if has_new else "compress",
    )(page_table.reshape(-1), *args)


def _softmax_parts(s, valid, axis):
    s = jnp.where(valid, s, NEG)
    m = jnp.max(s, axis=axis, keepdims=True)
    e = jnp.where(valid, jnp.exp2(s - m), 0.0)
    return e, jnp.sum(e, axis=axis, keepdims=True)


def _nsa_prompt_kernel(qt_ref, kc_ref, vct_ref, ks_ref, vst_ref, kw_ref, vwt_ref, gt_ref, mt_ref, o_ref, *,
                       tq, kb, n_cmp, k_top):
    qi = pl.program_id(2)
    qs = qi * tq
    r4 = GQA_GROUP
    cols = r4 * tq
    qt = jnp.concatenate([qt_ref[0, r] for r in range(r4)], axis=1)
    tpos = qs + lax.broadcasted_iota(jnp.int32, (1, tq), 1)

    def per_head(v):
        return jnp.concatenate([v] * r4, axis=1)

    def value_tiles(ref, k0, width):
        j0 = k0 // LANES
        return jnp.concatenate([ref[0, 0, j0 + j] for j in range(width // LANES)], axis=1)

    def normalise(acc):
        return acc[:HEAD_DIM] * (1.0 / acc[HEAD_DIM:HEAD_DIM + 1])

    nc = kc_ref.shape[2]
    s = _dot(kc_ref[0, 0], qt)
    cid = lax.broadcasted_iota(jnp.int32, (nc, 1), 0)
    valid = (cid >= 1) & (cid <= n_cmp) & (cid * CMP_STRIDE + (CMP_BLOCK - CMP_STRIDE - 1) <= tpos)
    s = s + per_head(jnp.where(valid, 0.0, NEG))
    e = jnp.exp2(s - jnp.max(s, axis=0, keepdims=True))
    den = jnp.sum(e, axis=0, keepdims=True)
    has_key = per_head(jnp.where(tpos >= CMP_BLOCK - 1, 1.0, 0.0)) if n_cmp >= 1 else 0.0
    p = e * (has_key / jnp.maximum(den, 1e-30))
    o_c = _dot(vct_ref[0, 0], p.astype(BF16))
    pg = p[:, 0:tq]
    for r in range(1, r4):
        pg = pg + p[:, r * tq:(r + 1) * tq]

    mt = mt_ref[...]
    imp = sum(_dot(mt, piece) for piece in _split3(pg))
    n_blk = imp.shape[0]
    blk = lax.broadcasted_iota(jnp.int32, (n_blk, 1), 0)
    cur = (qs + lax.broadcasted_iota(jnp.int32, (1, tq), 1)) // SEL_BLOCK
    score = jnp.where(blk <= cur, imp, -1.0)
    score = jnp.where((blk == 0) | (blk == cur) | (blk == cur - 1), FORCE, score)
    sub = lax.broadcasted_iota(jnp.int32, (SUBLANES, 1), 0)
    tiles = [score[v * SUBLANES:(v + 1) * SUBLANES, :] for v in range(n_blk // SUBLANES)]
    ranks = [jnp.zeros((SUBLANES, tq), F32) for _ in tiles]
    for j in range(n_blk):
        sj = score[j:j + 1, :]
        for v, sc in enumerate(tiles):
            if v > j // SUBLANES:
                ahead = sj >= sc
            elif v < j // SUBLANES:
                ahead = sj > sc
            else:
                ahead = (sj > sc) | ((sj == sc) & (sub > j % SUBLANES))
            ranks[v] = ranks[v] + jnp.where(ahead, 1.0, 0.0)
    rank = jnp.concatenate(ranks, axis=0)
    sel_bias = jnp.where((rank < float(k_top)) & (blk <= cur), 0.0, NEG).astype(BF16)
    q_aug = jnp.concatenate([qt, jnp.concatenate([sel_bias] * r4, axis=1)], axis=0)

    def scores(k0):
        return _dot(ks_ref[0, 0, pl.ds(k0, kb), :], q_aug)

    def absorb(sk, values, state):
        m_run, acc = state
        m_new = jnp.maximum(m_run, jnp.max(sk, axis=0, keepdims=True))
        pv = _dot(values, jnp.exp2(sk - m_new).astype(BF16))
        return m_new, jnp.exp2(m_run - m_new) * acc + pv

    def tile_parts(kt):
        k0 = pl.multiple_of(kt * kb, kb)
        sk = scores(k0)
        m_t = jnp.max(sk, axis=0, keepdims=True)
        return m_t, _dot(value_tiles(vst_ref, k0, kb), jnp.exp2(sk - m_t).astype(BF16))

    def merge(state, parts):
        m_old, acc_old = state
        m_new = m_old
        for m_t, _ in parts:
            m_new = jnp.maximum(m_new, m_t)
        acc_new = jnp.exp2(m_old - m_new) * acc_old
        for m_t, pv in parts:
            acc_new = acc_new + jnp.exp2(m_t - m_new) * pv
        return m_new, acc_new

    init = (jnp.full((1, cols), NEG, F32), jnp.zeros((VT_ROWS, cols), F32))
    state = lax.fori_loop(0, qi // 2, lambda kp, st: merge(st, [tile_parts(2 * kp), tile_parts(2 * kp + 1)]), init)
    m_run, acc = lax.cond(qi % 2 == 1, lambda st: merge(st, [tile_parts(qi - 1)]), lambda st: st, state)

    sub = min(BAND_SUB, tq)
    n_sub = tq // sub

    def sub_cols(v, j):
        return jnp.concatenate([v[:, r * tq + j * sub:r * tq + (j + 1) * sub] for r in range(r4)], axis=1)

    def join_subs(parts):
        return jnp.concatenate([parts[j][:, r * sub:(r + 1) * sub] for r in range(r4) for j in range(n_sub)],
                               axis=1)

    wl = min(kw_ref.shape[2], WINDOW + sub)
    sel_parts, win_parts = [], []
    for j in range(n_sub):
        tpos_j = tpos[:, j * sub:(j + 1) * sub]
        heads_j = lambda v: jnp.concatenate([v] * r4, axis=1)
        nk = (j + 1) * sub
        kpos = qs + lax.broadcasted_iota(jnp.int32, (nk, 1), 0)
        sk = _dot(ks_ref[0, 0, pl.ds(pl.multiple_of(qs, tq), nk), :], sub_cols(q_aug, j))
        sk = sk + heads_j(jnp.where(kpos <= tpos_j, 0.0, NEG))
        state_j = absorb(sk, value_tiles(vst_ref, qs, nk), (sub_cols(m_run, j), sub_cols(acc, j)))
        sel_parts.append(normalise(state_j[1]))
        w0 = pl.multiple_of(jnp.maximum(qs + (j + 1) * sub - wl, 0), LANES)
        dq = tpos_j - (w0 + lax.broadcasted_iota(jnp.int32, (wl, 1), 0))
        sw = _dot(kw_ref[0, 0, pl.ds(w0, wl), :], sub_cols(qt, j))
        sw = sw + heads_j(jnp.where((dq >= 0) & (dq < WINDOW), 0.0, NEG))
        ew = jnp.exp2(sw - jnp.max(sw, axis=0, keepdims=True))
        win_parts.append(normalise(_dot(value_tiles(vwt_ref, w0, wl), ew.astype(BF16))))
    o_s = join_subs(sel_parts)
    o_w = join_subs(win_parts)

    gt = gt_ref[0, 0]
    heads = []
    for r in range(r4):
        c0 = r * N_BRANCH
        sl = slice(r * tq, (r + 1) * tq)
        heads.append(gt[c0:c0 + 1] * o_c[:, sl] + gt[c0 + 1:c0 + 2] * o_s[:, sl] + gt[c0 + 2:c0 + 3] * o_w[:, sl])
    o_ref[0] = jnp.concatenate(heads, axis=0).T


def _importance_matrix(n_blk_pad, n_col):
    mt = np.zeros((n_blk_pad, n_col), np.float32)
    ratio = SEL_BLOCK // CMP_STRIDE
    for b in range(n_blk_pad):
        for c, w in zip(range(ratio * b, ratio * b + ratio + 1), (1.0,) + (2.0,) * (ratio - 1) + (1.0,)):
            if c < n_col:
                mt[b, c] = w
    return mt


def _nsa_prompt(q_t, comp_k, comp_vt, ks_aug, vs_t, kw, vw_t, gates_t, tq, kb):
    b_sz, _, _, s_len = q_t.shape
    nc = comp_k.shape[2]
    assert s_len % SEL_BLOCK == 0 and s_len % kb == 0 and kb % tq == 0 and s_len % tq == 0 and CMP_SPAN == 2
    assert tq % LANES == 0 and kb == tq
    n_blk = s_len // SEL_BLOCK
    assert n_blk <= MAX_SEL_BLOCKS
    n_cmp = s_len // CMP_STRIDE - CMP_SPAN + 1
    mt = jnp.asarray(_importance_matrix(MAX_SEL_BLOCKS, nc), BF16)

    def per_group(*shape):
        return pl.BlockSpec((1, 1) + shape, lambda b, g, i: (b, g) + (0,) * len(shape))

    return pl.pallas_call(
        functools.partial(_nsa_prompt_kernel, tq=tq, kb=kb, n_cmp=n_cmp, k_top=min(N_SEL, n_blk)),
        grid=(b_sz, KV_HEADS, s_len // tq),
        in_specs=[pl.BlockSpec((1, GQA_GROUP, HEAD_DIM, tq), lambda b, g, i: (b, g, 0, i)),
                  per_group(nc, HEAD_DIM), per_group(HEAD_DIM, nc),
                  per_group(s_len, HEAD_DIM + MAX_SEL_BLOCKS), per_group(s_len // LANES, VT_ROWS, LANES),
                  per_group(s_len, HEAD_DIM), per_group(s_len // LANES, VT_ROWS, LANES),
                  pl.BlockSpec((1, 1, GATE_ROWS, tq), lambda b, g, i: (b, g, 0, i)),
                  pl.BlockSpec((MAX_SEL_BLOCKS, nc), lambda b, g, i: (0, 0))],
        out_specs=pl.BlockSpec((1, tq, GQA_GROUP * HEAD_DIM), lambda b, g, i: (b, i, g)),
        out_shape=jax.ShapeDtypeStruct((b_sz, s_len, NSA_WIDTH), F32),
        compiler_params=_params("arbitrary", "arbitrary", "arbitrary"),
        name="nsa_prompt",
    )(q_t, comp_k, comp_vt, ks_aug, vs_t, kw, vw_t, gates_t, mt)


Q_COLS = LANES
COLS_PER_HEAD = Q_COLS // GQA_GROUP
V_OFF = LANES
V_COL0 = KV_HEADS * HEAD_DIM - V_OFF

def _nsa_sample_kernel(pt_ref, *refs, n_pages_step, n_steps, t_new, past, n_cmp, n_sel, k_top):
    del pt_ref
    p_cnt = n_pages_step
    page_refs = refs[:p_cnt]
    (qbd_ref, comp_ref, knew_ref, win_ref, wnew_ref, g_ref,
     o_ref, kv_scr, bias_scr, p_scr, m_scr, l_scr, acc_scr, oc_scr) = refs[p_cnt:]
    i = pl.program_id(1)
    qbd = qbd_ref[0]
    col = lax.broadcasted_iota(jnp.int32, (1, Q_COLS), 1)
    tok = col & (SUBLANES - 1)
    qpos = past + tok
    kwin = slice(0, 2 * LANES)
    vwin = slice(V_OFF, V_OFF + 2 * LANES)
    k_lanes = KV_HEADS * HEAD_DIM
    blocks_per_page = PAGE_SIZE // SEL_BLOCK
    nb_step = blocks_per_page * p_cnt
    keys_step = PAGE_SIZE * p_cnt

    def attend(rows_bf16, valid):
        s = _dot(rows_bf16[:, kwin], qbd)
        return jnp.where(valid, s, NEG)

    @pl.when((pl.program_id(0) == 0) & (i == 0))
    def _():
        key_blk = lax.broadcasted_iota(jnp.int32, (keys_step, 1), 0) // SEL_BLOCK
        lane_blk = lax.broadcasted_iota(jnp.int32, (1, 2 * LANES - k_lanes), 1)
        kv_scr[:, k_lanes:2 * LANES] = (key_blk == lane_blk).astype(BF16)

    @pl.when(i == 0)
    def _():
        comp = comp_ref[0]
        ncp = comp.shape[0]
        cid = lax.broadcasted_iota(jnp.int32, (ncp, 1), 0)
        valid = (cid >= 1) & (cid <= n_cmp) & (cid * CMP_STRIDE + (CMP_BLOCK - CMP_STRIDE - 1) <= qpos)
        s = attend(comp, valid)
        e, den = _softmax_parts(s, valid, 0)
        p = e * (1.0 / jnp.maximum(den, 1e-30))
        oc_scr[...] = _dot_tn(p.astype(BF16), comp[:, vwin])
        nbp = bias_scr.shape[0]
        ratio = SEL_BLOCK // CMP_STRIDE
        p_scr[0:ncp, :] = p
        p_scr[ncp:, :] = jnp.zeros((p_scr.shape[0] - ncp, Q_COLS), F32)
        taps = [p_scr[pl.ds(k, nbp, stride=ratio), :] for k in range(ratio + 1)]
        imp = taps[0] + taps[ratio]
        for k in range(1, ratio):
            imp = imp + 2.0 * taps[k]
        imp = imp + pltpu.roll(imp, COLS_PER_HEAD, 1)
        imp = imp + pltpu.roll(imp, 2 * COLS_PER_HEAD, 1)
        blk = lax.broadcasted_iota(jnp.int32, (nbp, 1), 0)
        cur = qpos // SEL_BLOCK
        score = jnp.where(blk <= cur, imp, -1.0)
        score = jnp.where((blk == 0) | (blk == cur) | (blk == cur - 1), FORCE, score)
        score = jnp.where(blk < n_sel, score, PAD_SCORE)
        blk_f = blk.astype(F32)
        bias = jnp.full((nbp, Q_COLS), NEG, F32)
        for _ in range(k_top):
            best = jnp.max(score, axis=0, keepdims=True)
            first = jnp.min(jnp.where(score == best, blk_f, float(nbp)), axis=0, keepdims=True)
            hit = blk_f == first
            bias = jnp.where(hit, 0.0, bias)
            score = jnp.where(hit, TAKEN_SCORE, score)
        bias_scr[...] = bias
        m_scr[...] = jnp.full_like(m_scr, NEG)
        l_scr[...] = jnp.zeros_like(l_scr)
        acc_scr[...] = jnp.zeros_like(acc_scr)

    def as_col(v):
        return jnp.broadcast_to(v, (SUBLANES, Q_COLS)).T[:, 0:1]

    def partial_softmax(s, v_rows):
        m_g = jnp.max(s, axis=0, keepdims=True)
        e = jnp.exp2(s - m_g)
        return m_g, jnp.sum(e, axis=0, keepdims=True), _dot_tn(e.astype(BF16), v_rows)

    def accumulate(*parts):
        m_run = m_scr[0:1, :]
        m_new = m_run
        for m_g, _, _ in parts:
            m_new = jnp.maximum(m_new, m_g)
        w_run = jnp.exp2(m_run - m_new)
        l_new = w_run * l_scr[0:1, :]
        acc_new = as_col(w_run) * acc_scr[...]
        for m_g, l_g, acc_g in parts:
            w_g = jnp.exp2(m_g - m_new)
            l_new = l_new + w_g * l_g
            acc_new = acc_new + as_col(w_g) * acc_g
        m_scr[0:1, :] = m_new
        l_scr[0:1, :] = l_new
        acc_scr[...] = acc_new

    bias_step = bias_scr[pl.ds(pl.multiple_of(i * nb_step, nb_step), nb_step), :].astype(BF16)
    pad = 2 * LANES - k_lanes - nb_step
    q_parts = [qbd[:k_lanes], bias_step] + ([jnp.zeros((pad, Q_COLS), BF16)] if pad else [])
    q_step = jnp.concatenate(q_parts, axis=0)
    for p in range(p_cnt):
        rows_p = slice(p * PAGE_SIZE, (p + 1) * PAGE_SIZE)
        tiles = [page_refs[p][0, cb].astype(BF16).T for cb in range(KV_ROW // LANES)]
        kv_scr[rows_p, 0:LANES] = tiles[0]
        kv_scr[rows_p, LANES:k_lanes] = tiles[1][:, :k_lanes - LANES]
        kv_scr[rows_p, 2 * LANES:3 * LANES] = tiles[1]
        kv_scr[rows_p, 3 * LANES:4 * LANES] = tiles[2]
    accumulate(partial_softmax(_dot(kv_scr[:, 0:2 * LANES], q_step), kv_scr[:, 2 * LANES:4 * LANES]))

    @pl.when(i == n_steps - 1)
    def _():
        knew = knew_ref[0].astype(BF16)
        krow = lax.broadcasted_iota(jnp.int32, (t_new, 1), 0)
        new_blk = past // SEL_BLOCK
        s_new = _dot(knew[:, kwin], qbd) + bias_scr[new_blk:new_blk + 1, :]
        accumulate(partial_softmax(jnp.where(krow <= tok, s_new, NEG), knew[:, vwin]))
        o_s = acc_scr[...] * as_col(1.0 / l_scr[0:1, :])

        wbuf = jnp.concatenate([win_ref[0, cb].astype(BF16).T for cb in range(KV_ROW // LANES)], axis=1)
        wnew = wnew_ref[0].astype(BF16)
        buf = wbuf.shape[0]
        pos_b = past - buf + lax.broadcasted_iota(jnp.int32, (buf, 1), 0)
        pos_n = past + krow
        ok_b = (qpos - pos_b >= 0) & (qpos - pos_b < WINDOW) & (pos_b >= 0)
        ok_n = (qpos - pos_n >= 0) & (qpos - pos_n < WINDOW)
        s_b = attend(wbuf, ok_b)
        s_n = attend(wnew, ok_n)
        m_w = jnp.maximum(jnp.max(s_b, axis=0, keepdims=True), jnp.max(s_n, axis=0, keepdims=True))
        e_b = jnp.where(ok_b, jnp.exp2(s_b - m_w), 0.0)
        e_n = jnp.where(ok_n, jnp.exp2(s_n - m_w), 0.0)
        den = jnp.sum(e_b, axis=0, keepdims=True) + jnp.sum(e_n, axis=0, keepdims=True)
        o_w = _dot_tn(e_b.astype(BF16), wbuf[:, vwin]) + _dot_tn(e_n.astype(BF16), wnew[:, vwin])
        o_w = o_w * jnp.broadcast_to(1.0 / jnp.maximum(den, 1e-30), (SUBLANES, Q_COLS)).T[:, 0:1]
        o_c = oc_scr[...]

        gt = g_ref[0]
        for g in range(KV_HEADS):
            for r in range(GQA_GROUP):
                r0 = r * COLS_PER_HEAD + g * SUBLANES
                c0 = V_COL0 + g * HEAD_DIM
                gc = (g * GQA_GROUP + r) * N_BRANCH
                blk_o = [o[r0:r0 + t_new, c0:c0 + HEAD_DIM] for o in (o_c, o_s, o_w)]
                h0 = (g * GQA_GROUP + r) * HEAD_DIM
                o_ref[0, :, h0:h0 + HEAD_DIM] = (gt[:, gc:gc + 1] * blk_o[0] + gt[:, gc + 1:gc + 2] * blk_o[1]
                                                 + gt[:, gc + 2:gc + 3] * blk_o[2])


def _nsa_sample(cache_slc, page_table, qbd, comp_tm, knew, win_t, wnew, gates, past):
    db, n_pages = page_table.shape
    t_new = knew.shape[1]
    assert t_new == SUBLANES and KV_HEADS * t_new <= COLS_PER_HEAD and past % SEL_BLOCK == 0 and t_new <= SEL_BLOCK
    p_cnt = min(PAGES_PER_STEP, n_pages)
    assert n_pages % p_cnt == 0
    n_steps = n_pages // p_cnt
    ncp = comp_tm.shape[1]
    n_str = -(-(past + t_new) // CMP_STRIDE)
    n_cmp = n_str - CMP_SPAN + 1
    n_sel = -(-(past + t_new) // SEL_BLOCK)
    blocks_step = (PAGE_SIZE // SEL_BLOCK) * p_cnt
    nbp = -(-max(n_sel, n_steps * blocks_step + 1) // SUBLANES) * SUBLANES
    keys_step = PAGE_SIZE * p_cnt
    assert blocks_step <= 2 * LANES - KV_HEADS * HEAD_DIM
    assert CMP_SPAN == 2
    p_rows = max(ncp, (SEL_BLOCK // CMP_STRIDE) * nbp) + SUBLANES
    buf = win_t.shape[-1]
    assert buf % LANES == 0

    def page_spec(p):
        return pl.BlockSpec((1, KV_ROW // LANES, LANES, PAGE_SIZE),
                            lambda b, i, pt: (pt[b * n_pages + i * p_cnt + p], 0, 0, 0))

    def per_b(*shape):
        return pl.BlockSpec((1,) + shape, lambda b, i, pt: (b,) + (0,) * len(shape))

    in_specs = [page_spec(p) for p in range(p_cnt)]
    in_specs += [per_b(2 * LANES, Q_COLS), per_b(ncp, KV_ROW),
                 per_b(t_new, KV_ROW), per_b(KV_ROW // LANES, LANES, buf), per_b(t_new, KV_ROW), per_b(t_new, LANES)]
    return pl.pallas_call(
        functools.partial(_nsa_sample_kernel, n_pages_step=p_cnt, n_steps=n_steps, t_new=t_new, past=past,
                          n_cmp=n_cmp, n_sel=n_sel, k_top=min(N_SEL, n_sel)),
        grid_spec=pltpu.PrefetchScalarGridSpec(
            num_scalar_prefetch=1, grid=(db, n_steps), in_specs=in_specs,
            out_specs=per_b(t_new, NSA_WIDTH),
            scratch_shapes=[pltpu.VMEM((keys_step, 4 * LANES), BF16),
                            pltpu.VMEM((nbp, Q_COLS), F32), pltpu.VMEM((p_rows, Q_COLS), F32),
                            pltpu.VMEM((SUBLANES, Q_COLS), F32), pltpu.VMEM((SUBLANES, Q_COLS), F32),
                            pltpu.VMEM((Q_COLS, 2 * LANES), F32), pltpu.VMEM((Q_COLS, 2 * LANES), F32)]),
        out_shape=jax.ShapeDtypeStruct((db, t_new, NSA_WIDTH), F32),
        compiler_params=_params("arbitrary", "arbitrary"),
        name="nsa_sample",
    )(page_table.reshape(-1), *([cache_slc] * p_cnt), qbd, comp_tm, knew, win_t, wnew, gates)


def _out_kernel(x_ref, *refs, fused_pool):
    if fused_pool:
        uprev_ref, u_ref, wbd_ref, ps_ref = refs[:4]
        zp_ref, no_ref, zn_ref, gate_ref, gnp_ref, gnn_ref, w_ref, fn_ref, y_ref = refs[4:]
        i = pl.program_id(1)
        tm = u_ref.shape[1]
        halo = jnp.where(i == 0, 0.0, uprev_ref[0])
        pos = i * tm + lax.broadcasted_iota(jnp.int32, (tm, 1), 0)
        pool_o = _pool_rows(halo, u_ref[0], pos, wbd_ref[...], ps_ref[...])
    else:
        po_ref, zp_ref, no_ref, zn_ref, gate_ref, gnp_ref, gnn_ref, w_ref, fn_ref, y_ref = refs
        pool_o = po_ref[0]
    zp = zp_ref[0]
    zn = zn_ref[0]
    mp = _rms(pool_o, gnp_ref[...]) * (zp * _sigmoid(zp))
    mn = _rms(no_ref[0], gnn_ref[...]) * (zn * _sigmoid(zn))
    m = jnp.concatenate([mp, mn], axis=-1).astype(BF16)
    xo = x_ref[0] + gate_ref[0] * _dot(m, w_ref[...])
    y_ref[0] = _rms(xo, fn_ref[...])


def _out(x, pool, zp, nsa_o, zn, gate, gn_pool, gn_nsa, w_out_bf16, final_norm, tm):
    nb, t, d = x.shape
    r = gate.shape[1]
    assert t % tm == 0 and r in (1, t)
    fused_pool = isinstance(pool, tuple)
    if r == 1:
        gate_spec = pl.BlockSpec((1, 1, d), lambda b, i: (b, 0, 0))
    else:
        gate_spec = pl.BlockSpec((1, tm, d), lambda b, i: (b, i, 0))

    def tok(width):
        return pl.BlockSpec((1, tm, width), lambda b, i: (b, i, 0))

    def const(*shape):
        return pl.BlockSpec(shape, lambda b, i: (0,) * len(shape))

    if fused_pool:
        u, wbd, pool_scale = pool
        assert tm % POOL_HALO == 0
        ratio = tm // POOL_HALO
        pool_specs = [pl.BlockSpec((1, POOL_HALO, POOL_WIDTH), lambda b, i: (b, jnp.maximum(i * ratio - 1, 0), 0)),
                      tok(POOL_WIDTH), const(POOL_WIDTH, POOL_WIDTH), const(1, POOL_WIDTH)]
        pool_args = [u, u, wbd, pool_scale.reshape(1, -1)]
    else:
        pool_specs, pool_args = [tok(POOL_WIDTH)], [pool]
    return pl.pallas_call(
        functools.partial(_out_kernel, fused_pool=fused_pool),
        grid=(nb, t // tm),
        in_specs=[tok(d)] + pool_specs + [tok(POOL_WIDTH), tok(NSA_WIDTH), tok(NSA_WIDTH), gate_spec,
                                          const(1, POOL_WIDTH), const(1, NSA_WIDTH), const(d, d), const(1, d)],
        out_specs=tok(d),
        out_shape=jax.ShapeDtypeStruct((nb, t, d), F32),
        compiler_params=_params("arbitrary", "arbitrary"),
        name="out",
    )(x, *pool_args, zp, nsa_o, zn, gate, gn_pool.reshape(1, -1), gn_nsa.reshape(1, -1), w_out_bf16,
      final_norm.reshape(1, d))


def _pages_t(pages):
    n = pages.shape[0]
    return pages.transpose(0, 2, 3, 4, 1).reshape(n, KV_ROW // LANES, LANES, PAGE_SIZE)


def _block_diag(pool_w):
    n, c, _ = pool_w.shape
    eye = jnp.eye(n, dtype=pool_w.dtype)
    return (eye[:, None, :, None] * pool_w[:, :, None, :]).reshape(n * c, n * c)


def _first_layer(phi_w1, phi_pe):
    w = phi_w1.reshape(CMP_SPAN, CMP_STRIDE, 2, HEAD_DIM, CMP_HID)
    w1cat = w.transpose(2, 1, 3, 0, 4).reshape(2, CMP_STRIDE * HEAD_DIM, CMP_SPAN * CMP_HID)
    pe = phi_pe.reshape(CMP_SPAN, CMP_STRIDE, 2, HEAD_DIM).transpose(2, 0, 1, 3).reshape(2, CMP_SPAN, -1)
    pe2 = jnp.concatenate([pe, jnp.zeros((2, SUBLANES - CMP_SPAN, pe.shape[-1]), pe.dtype)], axis=1)
    return w1cat.astype(BF16), pe2


def kernel(x_prompt, x_sample, c_prompt, c_sample, cache_cmp_kv, cache_slc_kv, state_win_kv, state_pool, page_table, norm_w, w_ada, b_ada, w_in, pool_w, pool_scale, phi_w1, phi_pe, phi_w2, gn_pool, gn_nsa, w_out, final_norm):
    b_sz, s_len, d = x_prompt.shape
    db, t_new, _ = x_sample.shape
    depth = norm_w.shape[0]
    assert depth == 1 and w_in.shape[-1] == IN_WIDTH and d // 4 == POOL_WIDTH
    n_phys = cache_cmp_kv.shape[1]
    n_pages = page_table.shape[1]
    past = n_pages * PAGE_SIZE
    lyr = 0

    n_c = b_sz + db
    c_all = jnp.concatenate([c_prompt, c_sample, jnp.zeros((-n_c % SUBLANES, d), F32)], axis=0)
    ada = _ada(c_all, w_ada[lyr], b_ada[lyr])
    shift, scale, gate = ada[:, :d], ada[:, d:2 * d], ada[:, 2 * d:]

    w_in_t = w_in[lyr].T
    w_main = _cast_rows(w_in_t, IN_MAIN, 5 * LANES, BF16)
    w_gate = jnp.pad(w_in_t[IN_MAIN:].astype(BF16), ((0, LANES - GATE_COLS), (0, 0)))
    w_out_bf16 = w_out[lyr].astype(BF16)
    wbd = _block_diag(pool_w[lyr]).astype(BF16)
    w1cat, pe2 = _first_layer(phi_w1[lyr], phi_pe[lyr])
    w2 = phi_w2[lyr].astype(BF16)
    n_tok_s = db * t_new

    def per_token(v):
        return jnp.repeat(v[b_sz:n_c], t_new, axis=0)[None]

    tm = min(ROW_TILE, s_len)
    tq = min(ATTN_TILE, s_len)
    (u_p, zp_p, zn_p, q_t, ks_aug, kw_p, vs_t, vw_t, g_t, kvc_t, kvs_t, kvw_t) = _in_proj(
        x_prompt, scale[:b_sz, None], shift[:b_sz, None], norm_w[lyr], w_main, w_gate, tm, True)
    pages_p = s_len // PAGE_SIZE
    in_order = jnp.tile(jnp.arange(pages_p, dtype=jnp.int32), (b_sz, 1))
    comp_k, comp_vt = _compress(kvc_t, in_order, w1cat, pe2, w2, None, False, True)
    nsa_p = _nsa_prompt(q_t, comp_k, comp_vt, ks_aug, vs_t, kw_p, vw_t, g_t, tq=tq, kb=tq)
    y_p = _out(x_prompt, (u_p, wbd, pool_scale[lyr]), zp_p, nsa_p, zn_p, gate[:b_sz, None], gn_pool[lyr],
               gn_nsa[lyr], w_out_bf16,
               final_norm, min(OUT_TILE, s_len))

    (u_s, zp_s, zn_s, kvc_s, kvs_s, kvw_s, q_s, g_s) = _in_proj(
        x_sample.reshape(1, n_tok_s, d), per_token(scale), per_token(shift), norm_w[lyr], w_main, w_gate, n_tok_s,
        False)
    u_s3 = u_s.reshape(db, t_new, POOL_WIDTH)
    hist = jnp.concatenate([jnp.zeros((db, POOL_HALO - state_pool.shape[2], POOL_WIDTH), F32), state_pool[lyr]], axis=1)
    pool_s = _pool(hist, u_s3, wbd, pool_scale[lyr], past, t_new)
    kvc_s3 = kvc_s.reshape(db, t_new, KV_ROW)
    kvs_s3 = kvs_s.reshape(db, t_new, KV_ROW)
    kvw_s3 = kvw_s.reshape(db, t_new, KV_ROW)
    xnew = jnp.pad(kvc_s3, ((0, 0), (0, CMP_STRIDE - t_new), (0, 0)))
    xnew = xnew.reshape(db, CMP_STRIDE, N_KV_SLABS, HEAD_DIM).transpose(2, 0, 1, 3).reshape(N_KV_SLABS, db, -1)
    a1new = _new_chunk_first_layer(xnew, w1cat).transpose(1, 0, 2)
    comp_s = _compress(_pages_t(cache_cmp_kv[lyr]), page_table, w1cat, pe2, w2, a1new, True, False)
    q5 = q_s.reshape(db, t_new, KV_HEADS, GQA_GROUP, HEAD_DIM)
    eye = jnp.eye(KV_HEADS, dtype=BF16)
    qbd = q5.transpose(0, 2, 4, 3, 1)[:, :, :, :, None, :] * eye[None, :, None, None, :, None]
    qbd = jnp.pad(qbd.reshape(db, KV_HEADS * HEAD_DIM, GQA_GROUP, KV_HEADS * t_new),
                  ((0, 0), (0, 2 * LANES - KV_HEADS * HEAD_DIM), (0, 0), (0, COLS_PER_HEAD - KV_HEADS * t_new)))
    qbd = qbd.reshape(db, 2 * LANES, Q_COLS)
    g_s3 = g_s.reshape(db, t_new, LANES)
    buf = state_win_kv.shape[2]
    win_t = state_win_kv[lyr].transpose(0, 2, 3, 4, 1).reshape(db, KV_ROW // LANES, LANES, buf)
    nsa_s = _nsa_sample(_pages_t(cache_slc_kv[lyr]), page_table, qbd, comp_s, kvs_s3,
                        win_t, kvw_s3, g_s3, past)
    y_s = _out(x_sample.reshape(1, n_tok_s, d), pool_s.reshape(1, n_tok_s, POOL_WIDTH), zp_s,
               nsa_s.reshape(1, n_tok_s, NSA_WIDTH), zn_s, per_token(gate), gn_pool[lyr], gn_nsa[lyr], w_out_bf16,
               final_norm, n_tok_s).reshape(db, t_new, d)

    kv_shape = (2, KV_HEADS, HEAD_DIM)
    win_len = min(WINDOW, s_len)
    new_win_s = jnp.concatenate([state_win_kv[lyr], kvw_s3.reshape((db, t_new) + kv_shape)], axis=1)[:, -buf:]
    hist_len = state_pool.shape[2]
    new_pool_s = jnp.concatenate([state_pool[lyr], u_s3], axis=1)[:, -hist_len:]

    def rows_from_t(v_t):
        return v_t.reshape((v_t.shape[0],) + kv_shape + (v_t.shape[-1],)).transpose(0, 4, 1, 2, 3)[None]

    return (y_p, y_s,
            rows_from_t(kvc_t), kvc_s3.reshape((1, db, t_new) + kv_shape),
            rows_from_t(kvs_t), kvs_s3.reshape((1, db, t_new) + kv_shape),
            rows_from_t(kvw_t[..., s_len - win_len:]), new_win_s[None],
            u_p[:, s_len - hist_len:][None], new_pool_s[None])
```
